```python
import math
import jax, jax.numpy as jnp
from jax import lax
import numpy as np

D_MODEL = 2048
BATCH = 8
SEQ = 8192
DEPTH = 4

HGRN_WIDTH = D_MODEL // 2
HGRN_HEAD_DIM = 128
HGRN_HEADS = HGRN_WIDTH // HGRN_HEAD_DIM
HGRN_CHUNK = 64
ATTN_HEAD_DIM = 64
ATTN_HEADS = (D_MODEL // 2) // ATTN_HEAD_DIM
ATTN_KV_HEADS = 4
ATTN_WIDTH = ATTN_HEADS * ATTN_HEAD_DIM
KV_WIDTH = ATTN_KV_HEADS * ATTN_HEAD_DIM
WINDOW = 128
CONV_WIDTH = D_MODEL // 2
CONV_K = 3
N_BUCKETS = 32
MAX_DISTANCE = 128
ALPHA = (2.0 * DEPTH) ** 0.25
BETA = (8.0 * DEPTH) ** -0.25
LN_EPS = 1e-5
RMS_EPS = 1e-6
SPLIT_SIZES = (
    HGRN_WIDTH, HGRN_WIDTH, HGRN_WIDTH, HGRN_WIDTH,
    ATTN_WIDTH, KV_WIDTH, KV_WIDTH, ATTN_WIDTH,
    CONV_WIDTH, CONV_WIDTH, CONV_WIDTH, CONV_WIDTH,
    D_MODEL, D_MODEL, D_MODEL,
)
N_IN = sum(SPLIT_SIZES)

kernel_name = "hybrid_hgrn2_swa_sink_shortconv_gated_merge"


def layer_norm(x, g, b):
    xf = x.astype(jnp.float32)
    mu = jnp.mean(xf, axis=-1, keepdims=True)
    var = jnp.mean(jnp.square(xf - mu), axis=-1, keepdims=True)
    return ((xf - mu) * lax.rsqrt(var + LN_EPS) * g.astype(jnp.float32) + b.astype(jnp.float32)).astype(x.dtype)


def t5_bucket(dist):
    max_exact = N_BUCKETS // 2
    is_small = dist < max_exact
    logd = jnp.log(jnp.maximum(dist, 1).astype(jnp.float32) / max_exact) / math.log(MAX_DISTANCE / max_exact)
    large = max_exact + (logd * (N_BUCKETS - max_exact)).astype(jnp.int32)
    large = jnp.minimum(large, N_BUCKETS - 1)
    return jnp.where(is_small, dist, large)


def band_relative_bias(rel_bias):
    i = jnp.arange(WINDOW)[:, None]
    j = jnp.arange(2 * WINDOW)[None, :]
    rel = jnp.clip(WINDOW + i - j, 0, WINDOW - 1)
    bucket = t5_bucket(rel)
    return jnp.transpose(rel_bias[bucket], (2, 0, 1)).astype(jnp.float32)


def hgrn2_mixer(q, f_logit, inp, lb):
    bsz, t, _ = q.shape
    h, d, c = HGRN_HEADS, HGRN_HEAD_DIM, HGRN_CHUNK
    nc = t // c
    qf = jax.nn.silu(q.astype(jnp.float32)) * (d ** -0.5)
    f = lb + (1.0 - lb) * jax.nn.sigmoid(f_logit.astype(jnp.float32))
    k = 1.0 - f
    g = jnp.log(f)
    v = inp.astype(jnp.float32)

    def to_chunks(a):
        return jnp.transpose(a.reshape(bsz, nc, c, h, d), (1, 0, 3, 2, 4))

    causal = jnp.tril(jnp.ones((c, c), dtype=bool))

    def step(state, chunk):
        qc, kc, vc, gc = chunk
        b = jnp.cumsum(gc, axis=2)
        inter = jnp.einsum('bhtk,bhkv->bhtv', qc * jnp.exp(b), state)
        diff = b[:, :, :, None, :] - b[:, :, None, :, :]
        decay = jnp.exp(jnp.where(causal[None, None, :, :, None], diff, -jnp.inf))
        scores = jnp.einsum('bhtk,bhsk,bhtsk->bhts', qc, kc, decay)
        intra = jnp.einsum('bhts,bhsv->bhtv', scores, vc)
        b_end = b[:, :, -1, :]
        k_to_end = kc * jnp.exp(b_end[:, :, None, :] - b)
        new_state = jnp.exp(b_end)[..., None] * state + jnp.einsum('bhsk,bhsv->bhkv', k_to_end, vc)
        return new_state, inter + intra

    s0 = jnp.zeros((bsz, h, d, d), jnp.float32)
    _, o = lax.scan(step, s0, (to_chunks(qf), to_chunks(k), to_chunks(v), to_chunks(g)))
    return jnp.transpose(o, (1, 0, 3, 2, 4)).reshape(bsz, t, h, d)


def swa_sink_attention(q, k, v, bias, sinks):
    bsz, t, _ = q.shape
    w, dh, kvh = WINDOW, ATTN_HEAD_DIM, ATTN_KV_HEADS
    grp = ATTN_HEADS // kvh
    nb = t // w
    qb = q.reshape(bsz, nb, w, kvh, grp, dh)
    kb = k.reshape(bsz, nb, w, kvh, dh)
    vb = v.reshape(bsz, nb, w, kvh, dh)

    def with_prev(a):
        prev = jnp.concatenate([jnp.zeros_like(a[:, :1]), a[:, :-1]], axis=1)
        return jnp.concatenate([prev, a], axis=2)

    kw, vw = with_prev(kb), with_prev(vb)
    s = jnp.einsum('bnqhgd,bnkhd->bhgnqk', qb, kw).astype(jnp.float32) * (dh ** -0.5)
    s = s + bias.reshape(kvh, grp, 1, w, 2 * w)
    i = jnp.arange(w)[:, None]
    j = jnp.arange(2 * w)[None, :]
    rel = w + i - j
    band = (rel >= 0) & (rel < w)
    key_pos = jnp.arange(nb)[:, None, None] * w - w + j[None]
    mask = band[None] & (key_pos >= 0)
    s = jnp.where(mask, s, -jnp.inf)
    sink = sinks.astype(jnp.float32).reshape(kvh, grp, 1, 1, 1)
    m = jnp.maximum(jnp.max(s, axis=-1, keepdims=True), sink)
    p = jnp.exp(s - m)
    p = p / (jnp.sum(p, axis=-1, keepdims=True) + jnp.exp(sink - m))
    o = jnp.einsum('bhgnqk,bnkhd->bnqhgd', p.astype(v.dtype), vw)
    return o.reshape(bsz, t, ATTN_WIDTH)


def short_gated_conv(b_gate, c_gate, xin, conv_w):
    h = c_gate * xin
    hp = jnp.pad(h, ((0, 0), (CONV_K - 1, 0), (0, 0)))
    t = h.shape[1]
    y = conv_w[0] * hp[:, 0:t] + conv_w[1] * hp[:, 1:t + 1] + conv_w[2] * hp[:, 2:t + 2]
    return b_gate * y


def _fwd_setup_inputs(seed: int = 0) -> dict:
    key = jax.random.key(seed)
    ks = jax.random.split(key, 13)
    nrm = jax.random.normal
    f32 = jnp.float32
    return {
        "x": nrm(ks[0], (BATCH, SEQ, D_MODEL), f32),
        "w_in": nrm(ks[1], (DEPTH, D_MODEL, N_IN), f32) * D_MODEL ** -0.5,
        "w_proj_hgrn": nrm(ks[2], (DEPTH, HGRN_WIDTH, D_MODEL), f32) * (HGRN_WIDTH ** -0.5 * BETA),
        "w_proj_attn": nrm(ks[3], (DEPTH, ATTN_WIDTH, D_MODEL), f32) * (ATTN_WIDTH ** -0.5 * BETA),
        "w_proj_conv": nrm(ks[4], (DEPTH, CONV_WIDTH, D_MODEL), f32) * (CONV_WIDTH ** -0.5 * BETA),
        "w_out": nrm(ks[5], (DEPTH, D_MODEL, D_MODEL), f32) * (D_MODEL ** -0.5 * BETA),
        "lb_param": nrm(ks[6], (DEPTH, HGRN_WIDTH), f32) * 0.5,
        "hgrn_norm_g": 1.0 + 0.02 * nrm(ks[7], (DEPTH, HGRN_WIDTH), f32),
        "attn_sinks": nrm(ks[8], (DEPTH, ATTN_HEADS), f32),
        "conv_w": nrm(ks[9], (DEPTH, CONV_K, CONV_WIDTH), f32) * CONV_K ** -0.5,
        "rel_bias": nrm(ks[10], (N_BUCKETS, ATTN_HEADS), f32) * 0.1,
        "ln_g": 1.0 + 0.02 * nrm(ks[11], (DEPTH, D_MODEL), f32),
        "ln_b": 0.02 * nrm(ks[12], (DEPTH, D_MODEL), f32),
    }


def _fwd_reference(x, w_in, w_proj_hgrn, w_proj_attn, w_proj_conv, w_out, lb_param, hgrn_norm_g,
              attn_sinks, conv_w, rel_bias, ln_g, ln_b):
    bsz, t, _ = x.shape
    split_idx = [int(s) for s in np.cumsum(SPLIT_SIZES)[:-1]]
    lb_soft = jax.nn.softmax(lb_param.astype(jnp.float32), axis=0)
    lower_bounds = jnp.cumsum(lb_soft, axis=0) - lb_soft[0:1]
    bias = band_relative_bias(rel_bias)
    for l in range(DEPTH):
        u = x @ w_in[l]
        (a_q, a_f, a_i, a_g, b_q, b_k, b_v, b_g,
         c_b, c_c, c_x, c_g, m_a, m_b, m_c) = jnp.split(u, split_idx, axis=-1)
        o_a = hgrn2_mixer(a_q, a_f, a_i, lower_bounds[l])
        o_a = o_a * lax.rsqrt(jnp.mean(jnp.square(o_a), axis=-1, keepdims=True) + RMS_EPS)
        o_a = o_a.reshape(bsz, t, HGRN_WIDTH) * hgrn_norm_g[l].astype(jnp.float32)
        y_a = (o_a.astype(x.dtype) * jax.nn.silu(a_g)) @ w_proj_hgrn[l]
        o_b = swa_sink_attention(b_q, b_k, b_v, bias, attn_sinks[l])
        y_b = (o_b * jax.nn.silu(b_g)) @ w_proj_attn[l]
        o_c = short_gated_conv(c_b, c_c, c_x, conv_w[l])
        y_c = (o_c * jax.nn.silu(c_g)) @ w_proj_conv[l]
        merged = jax.nn.sigmoid(m_a) * y_a + jax.nn.sigmoid(m_b) * y_b + jax.nn.sigmoid(m_c) * y_c
        y = merged @ w_out[l]
        x = layer_norm(ALPHA * x + y, ln_g[l], ln_b[l])
    return x


import jax as _jax
import jax.numpy as _jnp

TWIN_FORMAT = 'train_step'
FWD_PARAMS = ['x', 'w_in', 'w_proj_hgrn', 'w_proj_attn', 'w_proj_conv', 'w_out', 'lb_param', 'hgrn_norm_g', 'attn_sinks', 'conv_w', 'rel_bias', 'ln_g', 'ln_b']
TWIN_WEIGHTS = ['w_in', 'w_proj_hgrn', 'w_proj_attn', 'w_proj_conv', 'w_out', 'lb_param', 'hgrn_norm_g', 'attn_sinks', 'conv_w', 'rel_bias', 'ln_g', 'ln_b']
TWIN_DIFF_INPUT = 'x'
TWIN_INPUTS = ['x', 'w_in', 'w_proj_hgrn', 'w_proj_attn', 'w_proj_conv', 'w_out', 'lb_param', 'hgrn_norm_g', 'attn_sinks', 'conv_w', 'rel_bias', 'ln_g', 'ln_b', 'loss_target', 'm_w_in', 'm_w_proj_hgrn', 'm_w_proj_attn', 'm_w_proj_conv', 'm_w_out', 'm_lb_param', 'm_hgrn_norm_g', 'm_attn_sinks', 'm_conv_w', 'm_rel_bias', 'm_ln_g', 'm_ln_b', 'v_w_in', 'v_w_proj_hgrn', 'v_w_proj_attn', 'v_w_proj_conv', 'v_w_out', 'v_lb_param', 'v_hgrn_norm_g', 'v_attn_sinks', 'v_conv_w', 'v_rel_bias', 'v_ln_g', 'v_ln_b']
TWIN_OUTPUTS = ['loss', 'grad_x', 'grad_w_in', 'grad_w_proj_hgrn', 'grad_w_proj_attn', 'grad_w_proj_conv', 'grad_w_out', 'grad_lb_param', 'grad_hgrn_norm_g', 'grad_attn_sinks', 'grad_conv_w', 'grad_rel_bias', 'grad_ln_g', 'grad_ln_b', 'delta_w_in', 'delta_w_proj_hgrn', 'delta_w_proj_attn', 'delta_w_proj_conv', 'delta_w_out', 'delta_lb_param', 'delta_hgrn_norm_g', 'delta_attn_sinks', 'delta_conv_w', 'delta_rel_bias', 'delta_ln_g', 'delta_ln_b', 'new_m_w_in', 'new_m_w_proj_hgrn', 'new_m_w_proj_attn', 'new_m_w_proj_conv', 'new_m_w_out', 'new_m_lb_param', 'new_m_hgrn_norm_g', 'new_m_attn_sinks', 'new_m_conv_w', 'new_m_rel_bias', 'new_m_ln_g', 'new_m_ln_b', 'new_v_w_in', 'new_v_w_proj_hgrn', 'new_v_w_proj_attn', 'new_v_w_proj_conv', 'new_v_w_out', 'new_v_lb_param', 'new_v_hgrn_norm_g', 'new_v_attn_sinks', 'new_v_conv_w', 'new_v_rel_bias', 'new_v_ln_g', 'new_v_ln_b']
TWIN_LEAF_KINDS = {'loss': 'loss', 'grad_x': 'grad_x', 'grad_w_in': 'grad_w', 'grad_w_proj_hgrn': 'grad_w', 'grad_w_proj_attn': 'grad_w', 'grad_w_proj_conv': 'grad_w', 'grad_w_out': 'grad_w', 'grad_lb_param': 'grad_w', 'grad_hgrn_norm_g': 'grad_w', 'grad_attn_sinks': 'grad_w', 'grad_conv_w': 'grad_w', 'grad_rel_bias': 'grad_w', 'grad_ln_g': 'grad_w', 'grad_ln_b': 'grad_w', 'delta_w_in': 'delta_w', 'delta_w_proj_hgrn': 'delta_w', 'delta_w_proj_attn': 'delta_w', 'delta_w_proj_conv': 'delta_w', 'delta_w_out': 'delta_w', 'delta_lb_param': 'delta_w', 'delta_hgrn_norm_g': 'delta_w', 'delta_attn_sinks': 'delta_w', 'delta_conv_w': 'delta_w', 'delta_rel_bias': 'delta_w', 'delta_ln_g': 'delta_w', 'delta_ln_b': 'delta_w', 'new_m_w_in': 'new_m', 'new_m_w_proj_hgrn': 'new_m', 'new_m_w_proj_attn': 'new_m', 'new_m_w_proj_conv': 'new_m', 'new_m_w_out': 'new_m', 'new_m_lb_param': 'new_m', 'new_m_hgrn_norm_g': 'new_m', 'new_m_attn_sinks': 'new_m', 'new_m_conv_w': 'new_m', 'new_m_rel_bias': 'new_m', 'new_m_ln_g': 'new_m', 'new_m_ln_b': 'new_m', 'new_v_w_in': 'new_v', 'new_v_w_proj_hgrn': 'new_v', 'new_v_w_proj_attn': 'new_v', 'new_v_w_proj_conv': 'new_v', 'new_v_w_out': 'new_v', 'new_v_lb_param': 'new_v', 'new_v_hgrn_norm_g': 'new_v', 'new_v_attn_sinks': 'new_v', 'new_v_conv_w': 'new_v', 'new_v_rel_bias': 'new_v', 'new_v_ln_g': 'new_v', 'new_v_ln_b': 'new_v'}


def _forward(args):
    return _fwd_reference(*[args[k] for k in FWD_PARAMS])


def _output_shape():
    def fwd():
        inp = _fwd_setup_inputs(0)
        return _fwd_reference(*[inp[k] for k in FWD_PARAMS])
    out = _jax.eval_shape(fwd)
    return out.shape, out.dtype

N_MICROBATCH = 1
ADAM_LR = 0.001
ADAM_B1 = 0.9
ADAM_B2 = 0.999
ADAM_EPS = 1e-08
ADAM_WD = 0.01
ADAM_STEP = 10
PER_EXAMPLE_BATCH_AXIS = {'x': 0, 'loss_target': 0}
SHARED_INPUTS = []
_WEIGHT_DTYPES = {'w_in': _jnp.float32, 'w_proj_hgrn': _jnp.float32, 'w_proj_attn': _jnp.float32, 'w_proj_conv': _jnp.float32, 'w_out': _jnp.float32, 'lb_param': _jnp.float32, 'hgrn_norm_g': _jnp.float32, 'attn_sinks': _jnp.float32, 'conv_w': _jnp.float32, 'rel_bias': _jnp.float32, 'ln_g': _jnp.float32, 'ln_b': _jnp.float32}
MOMENT_SCALE = {'w_in': 3.803815e-03, 'w_proj_hgrn': 1.016057e-02, 'w_proj_attn': 2.103617e-03, 'w_proj_conv': 1.014570e-02, 'w_out': 1.445256e-02, 'lb_param': 6.432579e-04, 'hgrn_norm_g': 6.074680e-03, 'attn_sinks': 1.094177e-03, 'conv_w': 6.114874e-03, 'rel_bias': 3.019723e-03, 'ln_g': 1.603364e+01, 'ln_b': 7.096808e-01}


def _to_microbatches(a, axis):
    t = _jnp.moveaxis(a, axis, 0)
    t = t.reshape((N_MICROBATCH, t.shape[0] // N_MICROBATCH) + t.shape[1:])
    return _jnp.moveaxis(t, 1, axis + 1)


def setup_inputs(seed: int = 0) -> dict:
    inp = _fwd_setup_inputs(seed)
    key = _jax.random.fold_in(_jax.random.key(seed), 7919)
    shape, _ = _output_shape()
    out = dict(inp)
    out["loss_target"] = _jax.random.normal(_jax.random.fold_in(key, 0), shape, _jnp.float32)
    for i, name in enumerate(TWIN_WEIGHTS):
        w = inp[name].astype(_jnp.float32)
        if MOMENT_SCALE is None:
            s = _jnp.sqrt(_jnp.mean(_jnp.square(w)) + 1e-30)
        else:
            s = MOMENT_SCALE[name]
        km, kv = _jax.random.split(_jax.random.fold_in(key, i + 1))
        out[name] = w
        out["m_" + name] = s * _jax.random.normal(km, w.shape, _jnp.float32)
        out["v_" + name] = (s * s) * _jax.random.uniform(kv, w.shape, _jnp.float32, 0.5, 1.5)
    if N_MICROBATCH > 1:
        for name, axis in PER_EXAMPLE_BATCH_AXIS.items():
            out[name] = _to_microbatches(out[name], axis)
    return {'x': out['x'], 'w_in': out['w_in'], 'w_proj_hgrn': out['w_proj_hgrn'], 'w_proj_attn': out['w_proj_attn'], 'w_proj_conv': out['w_proj_conv'], 'w_out': out['w_out'], 'lb_param': out['lb_param'], 'hgrn_norm_g': out['hgrn_norm_g'], 'attn_sinks': out['attn_sinks'], 'conv_w': out['conv_w'], 'rel_bias': out['rel_bias'], 'ln_g': out['ln_g'], 'ln_b': out['ln_b'], 'loss_target': out['loss_target'], 'm_w_in': out['m_w_in'], 'm_w_proj_hgrn': out['m_w_proj_hgrn'], 'm_w_proj_attn': out['m_w_proj_attn'], 'm_w_proj_conv': out['m_w_proj_conv'], 'm_w_out': out['m_w_out'], 'm_lb_param': out['m_lb_param'], 'm_hgrn_norm_g': out['m_hgrn_norm_g'], 'm_attn_sinks': out['m_attn_sinks'], 'm_conv_w': out['m_conv_w'], 'm_rel_bias': out['m_rel_bias'], 'm_ln_g': out['m_ln_g'], 'm_ln_b': out['m_ln_b'], 'v_w_in': out['v_w_in'], 'v_w_proj_hgrn': out['v_w_proj_hgrn'], 'v_w_proj_attn': out['v_w_proj_attn'], 'v_w_proj_conv': out['v_w_proj_conv'], 'v_w_out': out['v_w_out'], 'v_lb_param': out['v_lb_param'], 'v_hgrn_norm_g': out['v_hgrn_norm_g'], 'v_attn_sinks': out['v_attn_sinks'], 'v_conv_w': out['v_conv_w'], 'v_rel_bias': out['v_rel_bias'], 'v_ln_g': out['v_ln_g'], 'v_ln_b': out['v_ln_b']}


def _loss(weights, diff, rest, loss_target):
    with _jax.named_scope("forward"):
        args = {**rest, TWIN_DIFF_INPUT: diff, **{k: w.astype(_WEIGHT_DTYPES[k]) for k, w in weights.items()}}
        y = _forward(args)
    with _jax.named_scope("loss_head"):
        err = _jnp.square(y.astype(_jnp.float32) - loss_target)
        return 0.5 * _jnp.sum(_jnp.mean(err, axis=-1)) if err.ndim else 0.5 * err


def _adamw(w, g, m, v):
    m = ADAM_B1 * m + (1.0 - ADAM_B1) * g
    v = ADAM_B2 * v + (1.0 - ADAM_B2) * _jnp.square(g)
    m_hat = m / (1.0 - ADAM_B1 ** ADAM_STEP)
    v_hat = v / (1.0 - ADAM_B2 ** ADAM_STEP)
    delta = -ADAM_LR * (m_hat / (_jnp.sqrt(v_hat) + ADAM_EPS) + ADAM_WD * w)
    return delta, m, v


def reference(x, w_in, w_proj_hgrn, w_proj_attn, w_proj_conv, w_out, lb_param, hgrn_norm_g, attn_sinks, conv_w, rel_bias, ln_g, ln_b, loss_target, m_w_in, m_w_proj_hgrn, m_w_proj_attn, m_w_proj_conv, m_w_out, m_lb_param, m_hgrn_norm_g, m_attn_sinks, m_conv_w, m_rel_bias, m_ln_g, m_ln_b, v_w_in, v_w_proj_hgrn, v_w_proj_attn, v_w_proj_conv, v_w_out, v_lb_param, v_hgrn_norm_g, v_attn_sinks, v_conv_w, v_rel_bias, v_ln_g, v_ln_b):
    given = dict(x=x, w_in=w_in, w_proj_hgrn=w_proj_hgrn, w_proj_attn=w_proj_attn, w_proj_conv=w_proj_conv, w_out=w_out, lb_param=lb_param, hgrn_norm_g=hgrn_norm_g, attn_sinks=attn_sinks, conv_w=conv_w, rel_bias=rel_bias, ln_g=ln_g, ln_b=ln_b, loss_target=loss_target, m_w_in=m_w_in, m_w_proj_hgrn=m_w_proj_hgrn, m_w_proj_attn=m_w_proj_attn, m_w_proj_conv=m_w_proj_conv, m_w_out=m_w_out, m_lb_param=m_lb_param, m_hgrn_norm_g=m_hgrn_norm_g, m_attn_sinks=m_attn_sinks, m_conv_w=m_conv_w, m_rel_bias=m_rel_bias, m_ln_g=m_ln_g, m_ln_b=m_ln_b, v_w_in=v_w_in, v_w_proj_hgrn=v_w_proj_hgrn, v_w_proj_attn=v_w_proj_attn, v_w_proj_conv=v_w_proj_conv, v_w_out=v_w_out, v_lb_param=v_lb_param, v_hgrn_norm_g=v_hgrn_norm_g, v_attn_sinks=v_attn_sinks, v_conv_w=v_conv_w, v_rel_bias=v_rel_bias, v_ln_g=v_ln_g, v_ln_b=v_ln_b)
    weights = {n: given[n] for n in TWIN_WEIGHTS}
    shared = {n: given[n] for n in SHARED_INPUTS}
    per_example = {n: given[n] for n in ['x']}
    grad_fn = _jax.value_and_grad(_loss, argnums=(0, 1))

    def one_microbatch(ex, loss_target):
        ex = dict(ex)
        diff = ex.pop(TWIN_DIFF_INPUT)
        return grad_fn(weights, diff, {**shared, **ex}, loss_target)

    if N_MICROBATCH == 1:
        loss, (grad_w, grad_x) = one_microbatch(per_example, given["loss_target"])
    else:
        def body(carry, xs):
            loss_sum, grad_sum = carry
            l_k, (gw_k, gx_k) = one_microbatch(xs[0], xs[1])
            with _jax.named_scope("update"):
                return (loss_sum + l_k, _jax.tree.map(_jnp.add, grad_sum, gw_k)), gx_k

        init = (_jnp.zeros((), _jnp.float32), _jax.tree.map(_jnp.zeros_like, weights))
        (loss, grad_w), grad_x = _jax.lax.scan(body, init, (per_example, given["loss_target"]))
    with _jax.named_scope("update"):
        delta_w, new_m, new_v = {}, {}, {}
        for n in TWIN_WEIGHTS:
            delta_w[n], new_m[n], new_v[n] = _adamw(weights[n], grad_w[n], given["m_" + n], given["v_" + n])
    return (loss, grad_x, *[grad_w[n] for n in TWIN_WEIGHTS], *[delta_w[n] for n in TWIN_WEIGHTS],
            *[new_m[n] for n in TWIN_WEIGHTS], *[new_v[n] for n in TWIN_WEIGHTS])
```

```python
import functools
import math

import jax
import jax.numpy as jnp
from jax import lax
from jax.experimental import pallas as pl
from jax.experimental.pallas import tpu as pltpu

F32 = jnp.float32
BF16 = jnp.bfloat16
MXU_DTYPE = BF16

D_MODEL = 2048
DEPTH = 4
HGRN_WIDTH = 1024
HGRN_HEAD_DIM = 128
HGRN_HEADS = 8
HGRN_CHUNK = 64
ATTN_HEAD_DIM = 64
ATTN_HEADS = 16
ATTN_KV_HEADS = 4
ATTN_GROUP = ATTN_HEADS // ATTN_KV_HEADS
ATTN_WIDTH = 1024
KV_WIDTH = 256
WINDOW = 128
CONV_WIDTH = 1024
CONV_K = 3
N_BUCKETS = 32
MAX_DISTANCE = 128
ALPHA = (2.0 * DEPTH) ** 0.25
LN_EPS = 1e-5
RMS_EPS = 1e-6
N_IN = 16896
OFF_AQ, OFF_AF, OFF_AI, OFF_AG = 0, 1024, 2048, 3072
OFF_BQ, OFF_BK, OFF_BV, OFF_BG = 4096, 5120, 5376, 5632
OFF_CB, OFF_CC, OFF_CX, OFF_CG = 6656, 7680, 8704, 9728
OFF_MA, OFF_MB, OFF_MC = 10752, 12800, 14848

ADAM_LR = 0.001
ADAM_B1 = 0.9
ADAM_B2 = 0.999
ADAM_EPS = 1e-08
ADAM_WD = 0.01
ADAM_STEP = 10

N_CHIPS = 4
VMEM_LIMIT_BYTES = 48 * 1024 * 1024
EXP_CLAMP = 80.0
MASK_VALUE = -1e30
MESH = pl.DeviceIdType.MESH


def _cp(sem=None):
    return pltpu.CompilerParams(dimension_semantics=sem, vmem_limit_bytes=VMEM_LIMIT_BYTES)


def _tile(dim, pref):
    return pref if dim % pref == 0 else dim


def _sds(shape, dtype):
    return jax.ShapeDtypeStruct(shape, dtype)


_DIMS = {
    "nn": (((1,), (0,)), ((), ())),
    "nt": (((1,), (1,)), ((), ())),
    "tn": (((0,), (0,)), ((), ())),
}


def _dot_raw(a, b, mode):
    return lax.dot_general(a.astype(MXU_DTYPE), b.astype(MXU_DTYPE), _DIMS[mode], preferred_element_type=F32)


@functools.partial(jax.custom_vjp, nondiff_argnums=(2,))
def _dot(a, b, mode):
    return _dot_raw(a, b, mode)


def _dot_fwd(a, b, mode):
    return _dot_raw(a, b, mode), (a, b)


def _dot_bwd(mode, res, g):
    a, b = res
    if mode == "nn":
        return _dot_raw(g, b, "nt"), _dot_raw(a, g, "tn")
    if mode == "nt":
        return _dot_raw(g, b, "nn"), _dot_raw(g, a, "tn")
    return _dot_raw(b, g, "nt"), _dot_raw(a, g, "nn")


_dot.defvjp(_dot_fwd, _dot_bwd)


def _matmul(a, b, *, mode, tm, tn, tk, name, a_idx=None, b_idx=None, out_dtype=F32, add=None, add_scale=1.0):
    a2, b2 = a.shape[-2:], b.shape[-2:]
    if mode == "nn":
        (M, K), (K2, N) = a2, b2
    elif mode == "nt":
        (M, K), (N, K2) = a2, b2
    else:
        (K, M), (K2, N) = a2, b2
    assert K == K2, (a.shape, b.shape, mode)
    tm, tn, tk = _tile(M, tm), _tile(N, tn), _tile(K, tk)
    nk = K // tk

    a_blk = (tk, tm) if mode == "tn" else (tm, tk)
    b_blk = (tn, tk) if mode == "nt" else (tk, tn)

    def a_map(i, j, k):
        ij = (k, i) if mode == "tn" else (i, k)
        return ij if a_idx is None else (a_idx,) + ij

    def b_map(i, j, k):
        ij = (j, k) if mode == "nt" else (k, j)
        return ij if b_idx is None else (b_idx,) + ij

    in_specs = [
        pl.BlockSpec(a_blk if a_idx is None else (None,) + a_blk, a_map),
        pl.BlockSpec(b_blk if b_idx is None else (None,) + b_blk, b_map),
    ]
    args = [a, b]
    if add is not None:
        in_specs.append(pl.BlockSpec((tm, tn), lambda i, j, k: (i, j)))
        args.append(add)
    n_in = len(args)

    def body(*refs):
        a_ref, b_ref = refs[0], refs[1]
        o_ref = refs[n_in]
        p = _dot_raw(a_ref[...], b_ref[...], mode)

        def finish(val):
            if add is not None:
                val = val + add_scale * refs[2][...]
            o_ref[...] = val.astype(out_dtype)

        if nk == 1:
            finish(p)
        else:
            acc_ref = refs[n_in + 1]
            k = pl.program_id(2)

            @pl.when(k == 0)
            def _():
                acc_ref[...] = p

            @pl.when(k > 0)
            def _():
                acc_ref[...] += p

            @pl.when(k == nk - 1)
            def _():
                finish(acc_ref[...])

    return pl.pallas_call(
        body,
        name=name,
        grid=(M // tm, N // tn, nk),
        in_specs=in_specs,
        out_specs=pl.BlockSpec((tm, tn), lambda i, j, k: (i, j)),
        out_shape=_sds((M, N), out_dtype),
        scratch_shapes=[pltpu.VMEM((tm, tn), F32)] if nk > 1 else [],
        compiler_params=_cp(("parallel", "parallel", "arbitrary")),
    )(*args)


def _scan_rows(x, reverse):
    n = x.shape[0]
    row = lax.broadcasted_iota(jnp.int32, x.shape, 0)
    s = 1
    while s < n:
        if reverse:
            x = x + jnp.where(row < n - s, pltpu.roll(x, n - s, 0), 0.0)
        else:
            x = x + jnp.where(row >= s, pltpu.roll(x, s, 0), 0.0)
        s *= 2
    return x


@jax.custom_vjp
def _cumsum_rows(x):
    return _scan_rows(x, False)


_cumsum_rows.defvjp(lambda x: (_scan_rows(x, False), None), lambda _, g: (_scan_rows(g, True),))


def _hgrn_chunk(state_t, qraw, fraw, v, lb):
    c = HGRN_CHUNK
    q = qraw * jax.nn.sigmoid(qraw) * (HGRN_HEAD_DIM ** -0.5)
    f = lb + (1.0 - lb) * jax.nn.sigmoid(fraw)
    k = 1.0 - f
    g = jnp.log(f)
    b = _cumsum_rows(g)
    row = lax.broadcasted_iota(jnp.int32, (c, HGRN_HEAD_DIM), 0)
    b_end = jnp.sum(g, axis=0, keepdims=True)
    b_mid = jnp.sum(jnp.where(row < c // 2, g, 0.0), axis=0, keepdims=True)
    inter = _dot(q * jnp.exp(b), state_t, "nt")
    qt = q * jnp.exp(jnp.minimum(b - b_mid, EXP_CLAMP))
    kt = k * jnp.exp(jnp.minimum(b_mid - b, EXP_CLAMP))
    s = _dot(qt, kt, "nt")
    ti = lax.broadcasted_iota(jnp.int32, (c, c), 0)
    si = lax.broadcasted_iota(jnp.int32, (c, c), 1)
    s = jnp.where(si <= ti, s, 0.0)
    intra = _dot(s, v, "nn")
    k_end = k * jnp.exp(b_end - b)
    new_state_t = state_t * jnp.exp(b_end) + _dot(v, k_end, "tn")
    return new_state_t, inter + intra


def _hgrn_specs(T):
    rows = _tile(T, 512)
    return rows, T // rows, rows // HGRN_CHUNK


def _hgrn_fwd(u, lower_l):
    T = u.shape[0]
    rows, nblk, ncr = _hgrn_specs(T)
    hb = HGRN_WIDTH // HGRN_HEAD_DIM

    def body(q_ref, f_ref, i_ref, lb_ref, o_ref, st_ref, state):
        @pl.when(pl.program_id(1) == 0)
        def _():
            state[...] = jnp.zeros_like(state)

        lb = lb_ref[...]
        for c in range(ncr):
            rs = pl.ds(c * HGRN_CHUNK, HGRN_CHUNK)
            st = state[...]
            st_ref[c] = st
            new, out = _hgrn_chunk(st, q_ref[rs, :], f_ref[rs, :], i_ref[rs, :], lb)
            state[...] = new
            o_ref[rs, :] = out

    blk = (rows, HGRN_HEAD_DIM)
    return pl.pallas_call(
        body,
        name="hgrn_fwd",
        grid=(HGRN_HEADS, nblk),
        in_specs=[
            pl.BlockSpec(blk, lambda h, r: (r, OFF_AQ // 128 + h)),
            pl.BlockSpec(blk, lambda h, r: (r, OFF_AF // 128 + h)),
            pl.BlockSpec(blk, lambda h, r: (r, OFF_AI // 128 + h)),
            pl.BlockSpec((1, HGRN_HEAD_DIM), lambda h, r: (0, h)),
        ],
        out_specs=[
            pl.BlockSpec(blk, lambda h, r: (r, h)),
            pl.BlockSpec((ncr, None, HGRN_HEAD_DIM, HGRN_HEAD_DIM), lambda h, r: (r, h, 0, 0)),
        ],
        out_shape=[_sds((T, HGRN_WIDTH), F32), _sds((T // HGRN_CHUNK, hb, HGRN_HEAD_DIM, HGRN_HEAD_DIM), F32)],
        scratch_shapes=[pltpu.VMEM((HGRN_HEAD_DIM, HGRN_HEAD_DIM), F32)],
        compiler_params=_cp(("parallel", "arbitrary")),
    )(u, u, u, lower_l)


def _hgrn_bwd(u, lower_l, states, do_raw):
    T = u.shape[0]
    rows, nblk, ncr = _hgrn_specs(T)

    def body(q_ref, f_ref, i_ref, lb_ref, st_ref, do_ref, dq_ref, df_ref, di_ref, dlb_ref, dstate):
        @pl.when(pl.program_id(1) == 0)
        def _():
            dstate[...] = jnp.zeros_like(dstate)
            dlb_ref[...] = jnp.zeros_like(dlb_ref)

        lb = lb_ref[...]
        for c in reversed(range(ncr)):
            rs = pl.ds(c * HGRN_CHUNK, HGRN_CHUNK)
            _, vjp = jax.vjp(_hgrn_chunk, st_ref[c], q_ref[rs, :], f_ref[rs, :], i_ref[rs, :], lb)
            dst, dq, df, dv, dlb = vjp((dstate[...], do_ref[rs, :]))
            dstate[...] = dst
            dq_ref[rs, :] = dq.astype(BF16)
            df_ref[rs, :] = df.astype(BF16)
            di_ref[rs, :] = dv.astype(BF16)
            dlb_ref[...] += dlb

    blk = (rows, HGRN_HEAD_DIM)
    last = nblk - 1
    out_blk = pl.BlockSpec(blk, lambda h, r: (last - r, h))
    return pl.pallas_call(
        body,
        name="hgrn_bwd",
        grid=(HGRN_HEADS, nblk),
        in_specs=[
            pl.BlockSpec(blk, lambda h, r: (last - r, OFF_AQ // 128 + h)),
            pl.BlockSpec(blk, lambda h, r: (last - r, OFF_AF // 128 + h)),
            pl.BlockSpec(blk, lambda h, r: (last - r, OFF_AI // 128 + h)),
            pl.BlockSpec((1, HGRN_HEAD_DIM), lambda h, r: (0, h)),
            pl.BlockSpec((ncr, None, HGRN_HEAD_DIM, HGRN_HEAD_DIM), lambda h, r: (last - r, h, 0, 0)),
            out_blk,
        ],
        out_specs=[out_blk, out_blk, out_blk, pl.BlockSpec((1, HGRN_HEAD_DIM), lambda h, r: (0, h))],
        out_shape=[_sds((T, HGRN_WIDTH), BF16)] * 3 + [_sds((1, HGRN_WIDTH), F32)],
        scratch_shapes=[pltpu.VMEM((HGRN_HEAD_DIM, HGRN_HEAD_DIM), F32)],
        compiler_params=_cp(("parallel", "arbitrary")),
    )(u, u, u, lower_l, states, do_raw)


QROWS = ATTN_GROUP * WINDOW


def _attn_block(q, kp, kc, vp, vc, bp, bc, sink, mp, mc):
    scale = ATTN_HEAD_DIM ** -0.5
    sp = jnp.where(mp, _dot(q, kp, "nt") * scale + bp, MASK_VALUE)
    sc = jnp.where(mc, _dot(q, kc, "nt") * scale + bc, MASK_VALUE)
    m = jnp.maximum(jnp.maximum(jnp.max(sp, axis=-1, keepdims=True), jnp.max(sc, axis=-1, keepdims=True)), sink)
    m = lax.stop_gradient(m)
    pp = jnp.exp(sp - m)
    pc = jnp.exp(sc - m)
    den = jnp.sum(pp, axis=-1, keepdims=True) + jnp.sum(pc, axis=-1, keepdims=True) + jnp.exp(sink - m)
    inv = 1.0 / den
    return _dot(pp * inv, vp, "nn") + _dot(pc * inv, vc, "nn")


def _attn_masks(first_block):
    i = lax.broadcasted_iota(jnp.int32, (QROWS, WINDOW), 0) % WINDOW
    j = lax.broadcasted_iota(jnp.int32, (QROWS, WINDOW), 1)
    return (j > i) & jnp.logical_not(first_block), j <= i


def _attn_in_specs():
    q_spec = pl.BlockSpec((None, None, QROWS, ATTN_HEAD_DIM), lambda h, n: (h, n, 0, 0))
    cur = pl.BlockSpec((None, WINDOW, ATTN_HEAD_DIM), lambda h, n: (h, n, 0))
    prev = pl.BlockSpec((None, WINDOW, ATTN_HEAD_DIM), lambda h, n: (h, jnp.maximum(n - 1, 0), 0))
    bias = pl.BlockSpec((None, QROWS, 2 * WINDOW), lambda h, n: (h, 0, 0))
    sink = pl.BlockSpec((None, QROWS, 1), lambda h, n: (h, 0, 0))
    return q_spec, cur, prev, bias, sink


def _attn_fwd(q4, k4, v4, bias4, sink4):
    kvh, nb = q4.shape[0], q4.shape[1]
    q_spec, cur, prev, bias, sink = _attn_in_specs()

    def body(q_ref, kp_ref, kc_ref, vp_ref, vc_ref, b_ref, s_ref, o_ref):
        mp, mc = _attn_masks(pl.program_id(1) == 0)
        o_ref[...] = _attn_block(q_ref[...], kp_ref[...], kc_ref[...], vp_ref[...], vc_ref[...],
                                 b_ref[:, :WINDOW], b_ref[:, WINDOW:], s_ref[...], mp, mc)

    return pl.pallas_call(
        body,
        name="attn_fwd",
        grid=(kvh, nb),
        in_specs=[q_spec, prev, cur, prev, cur, bias, sink],
        out_specs=q_spec,
        out_shape=_sds(q4.shape, F32),
        compiler_params=_cp(("parallel", "arbitrary")),
    )(q4, k4, k4, v4, v4, bias4, sink4)


def _attn_bwd(q4, k4, v4, bias4, sink4, do4):
    kvh, nb = q4.shape[0], q4.shape[1]
    T = k4.shape[1]
    q_spec, cur, prev, bias, sink = _attn_in_specs()

    def body(q_ref, kp_ref, kc_ref, vp_ref, vc_ref, b_ref, s_ref, do_ref,
             dq_ref, dkc_ref, dkp_ref, dvc_ref, dvp_ref, db_ref, ds_ref):
        n = pl.program_id(1)
        mp, mc = _attn_masks(n == 0)
        _, vjp = jax.vjp(
            functools.partial(_attn_block, mp=mp, mc=mc),
            q_ref[...], kp_ref[...], kc_ref[...], vp_ref[...], vc_ref[...],
            b_ref[:, :WINDOW], b_ref[:, WINDOW:], s_ref[...])
        dq, dkp, dkc, dvp, dvc, dbp, dbc, dsink = vjp(do_ref[...])
        dq_ref[...] = dq
        dkc_ref[...] = dkc
        dkp_ref[...] = dkp
        dvc_ref[...] = dvc
        dvp_ref[...] = dvp

        @pl.when(n == 0)
        def _():
            db_ref[...] = jnp.zeros_like(db_ref)
            ds_ref[...] = jnp.zeros_like(ds_ref)

        db_ref[:, :WINDOW] += dbp
        db_ref[:, WINDOW:] += dbc
        ds_ref[...] += dsink

    kv_sds = _sds((kvh, T, ATTN_HEAD_DIM), F32)
    return pl.pallas_call(
        body,
        name="attn_bwd",
        grid=(kvh, nb),
        in_specs=[q_spec, prev, cur, prev, cur, bias, sink, q_spec],
        out_specs=[q_spec, cur, cur, cur, cur, bias, sink],
        out_shape=[_sds(q4.shape, F32), kv_sds, kv_sds, kv_sds, kv_sds,
                   _sds((kvh, QROWS, 2 * WINDOW), F32), _sds((kvh, QROWS, 1), F32)],
        compiler_params=_cp(("parallel", "arbitrary")),
    )(q4, k4, k4, v4, v4, bias4, sink4, do4)


def _q_to_blocks(a):
    T = a.shape[0]
    a = a.reshape(T // WINDOW, WINDOW, ATTN_KV_HEADS, ATTN_GROUP, ATTN_HEAD_DIM)
    return jnp.transpose(a, (2, 0, 3, 1, 4)).reshape(ATTN_KV_HEADS, T // WINDOW, QROWS, ATTN_HEAD_DIM)


def _q_from_blocks(a):
    kvh, nb = a.shape[0], a.shape[1]
    a = a.reshape(kvh, nb, ATTN_GROUP, WINDOW, ATTN_HEAD_DIM)
    return jnp.transpose(a, (1, 3, 0, 2, 4)).reshape(nb * WINDOW, ATTN_WIDTH)


def _kv_to_heads(a):
    T = a.shape[0]
    return jnp.transpose(a.reshape(T, ATTN_KV_HEADS, ATTN_HEAD_DIM), (1, 0, 2))


def _kv_from_heads(a):
    return jnp.transpose(a, (1, 0, 2)).reshape(a.shape[1], KV_WIDTH)


def _shift_prev_contrib(cur, nxt):
    pad = jnp.zeros_like(nxt[:, :WINDOW])
    return cur + jnp.concatenate([nxt[:, WINDOW:], pad], axis=1)


MIX_COLS = 512


def _silu(x):
    return x * jax.nn.sigmoid(x)


def _silu_grad(x):
    s = jax.nn.sigmoid(x)
    return s * (1.0 + x * (1.0 - s))


def _shift_rows_down(h, first, second):
    n = h.shape[0]
    row = lax.broadcasted_iota(jnp.int32, h.shape, 0)
    s1 = jnp.where(row == 0, first, pltpu.roll(h, 1, 0))
    s2 = jnp.where(row == 0, second, jnp.where(row == 1, first, pltpu.roll(h, 2, 0)))
    del n
    return s1, s2


def _shift_rows_up(h, first, second):
    n = h.shape[0]
    row = lax.broadcasted_iota(jnp.int32, h.shape, 0)
    s1 = jnp.where(row == n - 1, first, pltpu.roll(h, n - 1, 0))
    s2 = jnp.where(row == n - 1, second, jnp.where(row == n - 2, first, pltpu.roll(h, n - 2, 0)))
    return s1, s2


def _mix_rows(T):
    return _tile(T, 256)


def _mix_fwd(u, o_raw, o_b, gn_l, cw_l):
    T = u.shape[0]
    tr = _mix_rows(T)
    nrow = T // tr
    hr = tr // 8

    def ucol(off):
        return pl.BlockSpec((tr, MIX_COLS), lambda i, j, off=off: (i, off // MIX_COLS + j))

    def uprev(off):
        return pl.BlockSpec((8, MIX_COLS), lambda i, j, off=off: (jnp.maximum(i * hr - 1, 0), off // MIX_COLS + j))

    act = pl.BlockSpec((tr, MIX_COLS), lambda i, j: (i, j))
    par = lambda rows: pl.BlockSpec((rows, MIX_COLS), lambda i, j: (0, j))

    def body(oraw_ref, ag_ref, ob_ref, bg_ref, cb_ref, cc_ref, cx_ref, cg_ref, ccp_ref, cxp_ref, gn_ref, cw_ref,
             ha_ref, hb_ref, hc_ref):
        ag = _silu(ag_ref[...])
        for h in range(MIX_COLS // HGRN_HEAD_DIM):
            cs = slice(h * HGRN_HEAD_DIM, (h + 1) * HGRN_HEAD_DIM)
            o = oraw_ref[:, cs]
            nrm = o * lax.rsqrt(jnp.mean(o * o, axis=-1, keepdims=True) + RMS_EPS)
            ha_ref[:, cs] = (nrm * gn_ref[:, cs] * ag[:, cs]).astype(BF16)
        hb_ref[...] = (ob_ref[...] * _silu(bg_ref[...])).astype(BF16)
        keep = (pl.program_id(0) > 0).astype(F32)
        hcur = cc_ref[...] * cx_ref[...]
        p1 = ccp_ref[7:8, :] * cxp_ref[7:8, :] * keep
        p2 = ccp_ref[6:7, :] * cxp_ref[6:7, :] * keep
        s1, s2 = _shift_rows_down(hcur, p1, p2)
        y = cw_ref[0:1, :] * s2 + cw_ref[1:2, :] * s1 + cw_ref[2:3, :] * hcur
        hc_ref[...] = (cb_ref[...] * y * _silu(cg_ref[...])).astype(BF16)

    out = _sds((T, HGRN_WIDTH), BF16)
    return pl.pallas_call(
        body,
        name="mix_fwd",
        grid=(nrow, HGRN_WIDTH // MIX_COLS),
        in_specs=[act, ucol(OFF_AG), act, ucol(OFF_BG), ucol(OFF_CB), ucol(OFF_CC), ucol(OFF_CX), ucol(OFF_CG),
                  uprev(OFF_CC), uprev(OFF_CX), par(1), par(CONV_K)],
        out_specs=[act, act, act],
        out_shape=[out, out, out],
        compiler_params=_cp(("parallel", "parallel")),
    )(o_raw, u, o_b, u, u, u, u, u, u, u, gn_l, cw_l)


def _mix_bwd(u, o_raw, o_b, gn_l, cw_l, dha, dhb, dhc):
    T = u.shape[0]
    tr = _mix_rows(T)
    nrow = T // tr
    hr = tr // 8
    last_halo = T // 8 - 1

    def ucol(off):
        return pl.BlockSpec((tr, MIX_COLS), lambda j, i, off=off: (i, off // MIX_COLS + j))

    def uprev(off):
        return pl.BlockSpec((8, MIX_COLS), lambda j, i, off=off: (jnp.maximum(i * hr - 1, 0), off // MIX_COLS + j))

    def unext(off):
        return pl.BlockSpec((8, MIX_COLS), lambda j, i, off=off: (jnp.minimum((i + 1) * hr, last_halo), off // MIX_COLS + j))

    act = pl.BlockSpec((tr, MIX_COLS), lambda j, i: (i, j))
    act_next = pl.BlockSpec((8, MIX_COLS), lambda j, i: (jnp.minimum((i + 1) * hr, last_halo), j))
    par = lambda rows: pl.BlockSpec((rows, MIX_COLS), lambda j, i: (0, j))

    def body(oraw_ref, ag_ref, ob_ref, bg_ref, cb_ref, cc_ref, cx_ref, cg_ref, ccp_ref, cxp_ref,
             cbn_ref, cgn_ref, dhcn_ref, gn_ref, cw_ref, dha_ref, dhb_ref, dhc_ref,
             doraw_ref, dob_ref, dag_ref, dbg_ref, dcb_ref, dcc_ref, dcx_ref, dcg_ref, dgn_ref, dcw_ref):
        i = pl.program_id(1)

        @pl.when(i == 0)
        def _():
            dgn_ref[...] = jnp.zeros_like(dgn_ref)
            dcw_ref[...] = jnp.zeros_like(dcw_ref)

        ag = ag_ref[...]
        sag = _silu(ag)
        dha = dha_ref[...]
        for h in range(MIX_COLS // HGRN_HEAD_DIM):
            cs = slice(h * HGRN_HEAD_DIM, (h + 1) * HGRN_HEAD_DIM)
            o = oraw_ref[:, cs]
            rs = lax.rsqrt(jnp.mean(o * o, axis=-1, keepdims=True) + RMS_EPS)
            nrm = o * rs
            gn = gn_ref[:, cs]
            d = dha[:, cs]
            dag_ref[:, cs] = (d * nrm * gn * _silu_grad(ag[:, cs])).astype(BF16)
            dgn_ref[:, cs] += jnp.sum(d * sag[:, cs] * nrm, axis=0, keepdims=True)
            dn = d * sag[:, cs] * gn
            doraw_ref[:, cs] = rs * (dn - nrm * jnp.mean(dn * nrm, axis=-1, keepdims=True))
        bg = bg_ref[...]
        dhb = dhb_ref[...]
        dob_ref[...] = dhb * _silu(bg)
        dbg_ref[...] = (dhb * ob_ref[...] * _silu_grad(bg)).astype(BF16)
        keep_prev = (i > 0).astype(F32)
        keep_next = (i < nrow - 1).astype(F32)
        cc, cx, cb, cg = cc_ref[...], cx_ref[...], cb_ref[...], cg_ref[...]
        hcur = cc * cx
        p1 = ccp_ref[7:8, :] * cxp_ref[7:8, :] * keep_prev
        p2 = ccp_ref[6:7, :] * cxp_ref[6:7, :] * keep_prev
        s1, s2 = _shift_rows_down(hcur, p1, p2)
        w0, w1, w2 = cw_ref[0:1, :], cw_ref[1:2, :], cw_ref[2:3, :]
        y = w0 * s2 + w1 * s1 + w2 * hcur
        dhc = dhc_ref[...]
        scg = _silu(cg)
        doc = dhc * scg
        dcg_ref[...] = (dhc * cb * y * _silu_grad(cg)).astype(BF16)
        dcb_ref[...] = (doc * y).astype(BF16)
        dy = doc * cb
        n1 = dhcn_ref[0:1, :] * _silu(cgn_ref[0:1, :]) * cbn_ref[0:1, :] * keep_next
        n2 = dhcn_ref[1:2, :] * _silu(cgn_ref[1:2, :]) * cbn_ref[1:2, :] * keep_next
        u1, u2 = _shift_rows_up(dy, n1, n2)
        dh = w2 * dy + w1 * u1 + w0 * u2
        dcc_ref[...] = (dh * cx).astype(BF16)
        dcx_ref[...] = (dh * cc).astype(BF16)
        dcw_ref[0:1, :] += jnp.sum(dy * s2, axis=0, keepdims=True)
        dcw_ref[1:2, :] += jnp.sum(dy * s1, axis=0, keepdims=True)
        dcw_ref[2:3, :] += jnp.sum(dy * hcur, axis=0, keepdims=True)

    f32o, bf = _sds((T, HGRN_WIDTH), F32), _sds((T, HGRN_WIDTH), BF16)
    return pl.pallas_call(
        body,
        name="mix_bwd",
        grid=(HGRN_WIDTH // MIX_COLS, nrow),
        in_specs=[act, ucol(OFF_AG), act, ucol(OFF_BG), ucol(OFF_CB), ucol(OFF_CC), ucol(OFF_CX), ucol(OFF_CG),
                  uprev(OFF_CC), uprev(OFF_CX), unext(OFF_CB), unext(OFF_CG), act_next, par(1), par(CONV_K),
                  act, act, act],
        out_specs=[act, act, act, act, act, act, act, act, par(1), par(8)],
        out_shape=[f32o, f32o, bf, bf, bf, bf, bf, bf, _sds((1, HGRN_WIDTH), F32), _sds((8, HGRN_WIDTH), F32)],
        compiler_params=_cp(("parallel", "arbitrary")),
    )(o_raw, u, o_b, u, u, u, u, u, u, u, u, u, dhc, gn_l, cw_l, dha, dhb, dhc)


def _merge_specs(T, order):
    tr = _tile(T, 256)

    def ucol(off):
        if order == "ij":
            return pl.BlockSpec((tr, MIX_COLS), lambda i, j, off=off: (i, off // MIX_COLS + j))
        return pl.BlockSpec((tr, MIX_COLS), lambda j, i, off=off: (i, off // MIX_COLS + j))

    act = pl.BlockSpec((tr, MIX_COLS), (lambda i, j: (i, j)) if order == "ij" else (lambda j, i: (i, j)))
    return tr, ucol, act


def _merge_fwd(u, ya, yb, yc):
    T = u.shape[0]
    tr, ucol, act = _merge_specs(T, "ij")

    def body(ma_ref, mb_ref, mc_ref, ya_ref, yb_ref, yc_ref, o_ref):
        o_ref[...] = (jax.nn.sigmoid(ma_ref[...]) * ya_ref[...] + jax.nn.sigmoid(mb_ref[...]) * yb_ref[...]
                      + jax.nn.sigmoid(mc_ref[...]) * yc_ref[...]).astype(BF16)

    return pl.pallas_call(
        body,
        name="merge_fwd",
        grid=(T // tr, D_MODEL // MIX_COLS),
        in_specs=[ucol(OFF_MA), ucol(OFF_MB), ucol(OFF_MC), act, act, act],
        out_specs=act,
        out_shape=_sds((T, D_MODEL), BF16),
        compiler_params=_cp(("parallel", "parallel")),
    )(u, u, u, ya, yb, yc)


def _merge_bwd(u, ya, yb, yc, dmerged):
    T = u.shape[0]
    tr, ucol, act = _merge_specs(T, "ij")

    def body(ma_ref, mb_ref, mc_ref, ya_ref, yb_ref, yc_ref, dm_ref, dya_ref, dyb_ref, dyc_ref, dma_ref, dmb_ref, dmc_ref):
        dm = dm_ref[...]
        for m_ref, y_ref, dy_ref, dg_ref in ((ma_ref, ya_ref, dya_ref, dma_ref), (mb_ref, yb_ref, dyb_ref, dmb_ref),
                                             (mc_ref, yc_ref, dyc_ref, dmc_ref)):
            s = jax.nn.sigmoid(m_ref[...])
            dy_ref[...] = (dm * s).astype(BF16)
            dg_ref[...] = (dm * y_ref[...] * s * (1.0 - s)).astype(BF16)

    out = _sds((T, D_MODEL), BF16)
    return pl.pallas_call(
        body,
        name="merge_bwd",
        grid=(T // tr, D_MODEL // MIX_COLS),
        in_specs=[ucol(OFF_MA), ucol(OFF_MB), ucol(OFF_MC), act, act, act, act],
        out_specs=[act] * 6,
        out_shape=[out] * 6,
        compiler_params=_cp(("parallel", "parallel")),
    )(u, u, u, ya, yb, yc, dmerged)


def _ln_fwd(x, y, g_l, b_l):
    T = x.shape[0]
    tr = _tile(T, 256)
    row = pl.BlockSpec((tr, D_MODEL), lambda i: (i, 0))
    par = pl.BlockSpec((1, D_MODEL), lambda i: (0, 0))

    def body(x_ref, y_ref, g_ref, b_ref, o_ref, z_ref):
        z = ALPHA * x_ref[...] + y_ref[...]
        z_ref[...] = z
        mu = jnp.mean(z, axis=-1, keepdims=True)
        zc = z - mu
        var = jnp.mean(zc * zc, axis=-1, keepdims=True)
        o_ref[...] = zc * lax.rsqrt(var + LN_EPS) * g_ref[...] + b_ref[...]

    return pl.pallas_call(
        body,
        name="ln_fwd",
        grid=(T // tr,),
        in_specs=[row, row, par, par],
        out_specs=[row, row],
        out_shape=[_sds((T, D_MODEL), F32)] * 2,
        compiler_params=_cp(("parallel",)),
    )(x, y, g_l, b_l)


def _ln_bwd(z, dxn, g_l):
    T = z.shape[0]
    tr = _tile(T, 256)
    row = pl.BlockSpec((tr, D_MODEL), lambda i: (i, 0))
    par = pl.BlockSpec((1, D_MODEL), lambda i: (0, 0))

    def body(z_ref, d_ref, g_ref, dz_ref, dg_ref, db_ref):
        @pl.when(pl.program_id(0) == 0)
        def _():
            dg_ref[...] = jnp.zeros_like(dg_ref)
            db_ref[...] = jnp.zeros_like(db_ref)

        z = z_ref[...]
        d = d_ref[...]
        mu = jnp.mean(z, axis=-1, keepdims=True)
        zc = z - mu
        rstd = lax.rsqrt(jnp.mean(zc * zc, axis=-1, keepdims=True) + LN_EPS)
        zh = zc * rstd
        dg_ref[...] += jnp.sum(d * zh, axis=0, keepdims=True)
        db_ref[...] += jnp.sum(d, axis=0, keepdims=True)
        dh = d * g_ref[...]
        dz_ref[...] = rstd * (dh - jnp.mean(dh, axis=-1, keepdims=True) - zh * jnp.mean(dh * zh, axis=-1, keepdims=True))

    return pl.pallas_call(
        body,
        name="ln_bwd",
        grid=(T // tr,),
        in_specs=[row, row, par],
        out_specs=[row, par, par],
        out_shape=[_sds((T, D_MODEL), F32), _sds((1, D_MODEL), F32), _sds((1, D_MODEL), F32)],
        compiler_params=_cp(("arbitrary",)),
    )(z, dxn, g_l)


def _loss_head(y, target):
    T = y.shape[0]
    tr = _tile(T, 256)
    row = pl.BlockSpec((tr, D_MODEL), lambda i: (i, 0))
    acc = pl.BlockSpec((8, 128), lambda i: (0, 0))

    def body(y_ref, t_ref, l_ref, d_ref):
        @pl.when(pl.program_id(0) == 0)
        def _():
            l_ref[...] = jnp.zeros_like(l_ref)

        err = y_ref[...] - t_ref[...]
        d_ref[...] = err * (1.0 / D_MODEL)
        part = 0.5 * jnp.sum(jnp.sum(err * err, axis=-1, keepdims=True) * (1.0 / D_MODEL), axis=0, keepdims=True)
        r = lax.broadcasted_iota(jnp.int32, (8, 128), 0)
        c = lax.broadcasted_iota(jnp.int32, (8, 128), 1)
        l_ref[...] += jnp.where((r == 0) & (c == 0), part, 0.0)

    return pl.pallas_call(
        body,
        name="loss_head",
        grid=(T // tr,),
        in_specs=[row, row],
        out_specs=[acc, row],
        out_shape=[_sds((8, 128), F32), _sds((T, D_MODEL), F32)],
        compiler_params=_cp(("arbitrary",)),
    )(y, target)


ADAMW_BLOCK_ELEMS = 256 * 1024


def _adamw(w, g, m, v, name):
    shape = w.shape
    cols = shape[-1]
    rows = math.prod(shape[:-1])
    flat = lambda a: a.reshape(rows, cols)
    if rows * cols <= ADAMW_BLOCK_ELEMS or rows % 8:
        tr = rows
    else:
        tr = 8
        while rows % (tr * 2) == 0 and tr * 2 * cols <= ADAMW_BLOCK_ELEMS:
            tr *= 2
    blk = pl.BlockSpec((tr, cols), lambda i: (i, 0))
    c1 = 1.0 - ADAM_B1 ** ADAM_STEP
    c2 = 1.0 - ADAM_B2 ** ADAM_STEP

    def body(w_ref, g_ref, m_ref, v_ref, d_ref, nm_ref, nv_ref):
        gg = g_ref[...]
        nm = ADAM_B1 * m_ref[...] + (1.0 - ADAM_B1) * gg
        nv = ADAM_B2 * v_ref[...] + (1.0 - ADAM_B2) * (gg * gg)
        nm_ref[...] = nm
        nv_ref[...] = nv
        d_ref[...] = -ADAM_LR * ((nm / c1) / (jnp.sqrt(nv / c2) + ADAM_EPS) + ADAM_WD * w_ref[...])

    outs = pl.pallas_call(
        body,
        name=name,
        grid=(rows // tr,),
        in_specs=[blk] * 4,
        out_specs=[blk] * 3,
        out_shape=[_sds((rows, cols), F32)] * 3,
        compiler_params=_cp(("parallel",)),
    )(flat(w), flat(g), flat(m), flat(v))
    return tuple(o.reshape(shape) for o in outs)


def _t5_bucket(dist):
    max_exact = N_BUCKETS // 2
    logd = jnp.log(jnp.maximum(dist, 1).astype(F32) / max_exact) / math.log(MAX_DISTANCE / max_exact)
    large = jnp.minimum(max_exact + (logd * (N_BUCKETS - max_exact)).astype(jnp.int32), N_BUCKETS - 1)
    return jnp.where(dist < max_exact, dist, large)


def _band_bias(rel_bias):
    i = jnp.arange(WINDOW)[:, None]
    j = jnp.arange(2 * WINDOW)[None, :]
    bucket = _t5_bucket(jnp.clip(WINDOW + i - j, 0, WINDOW - 1))
    return jnp.transpose(rel_bias[bucket], (2, 0, 1)).astype(F32)


def _lower_bounds(lb_param):
    soft = jax.nn.softmax(lb_param.astype(F32), axis=0)
    return jnp.cumsum(soft, axis=0) - soft[0:1]


def _mm_rows(T):
    return _tile(T, 1024)


def _layer_fwd(x, w_in_l, w_proj_l, w_out_l, lower_l, bias4, sink4_l, gn_l, cw_l, lng_l, lnb_l):
    T = x.shape[0]
    tm = _mm_rows(T)
    u = _matmul(x, w_in_l, mode="nn", tm=tm, tn=768, tk=D_MODEL, name="mm_u")
    o_raw, states = _hgrn_fwd(u, lower_l)
    q4 = _q_to_blocks(u[:, OFF_BQ:OFF_BQ + ATTN_WIDTH])
    k4 = _kv_to_heads(u[:, OFF_BK:OFF_BK + KV_WIDTH])
    v4 = _kv_to_heads(u[:, OFF_BV:OFF_BV + KV_WIDTH])
    o_b = _q_from_blocks(_attn_fwd(q4, k4, v4, bias4, sink4_l))
    ha, hb, hc = _mix_fwd(u, o_raw, o_b, gn_l, cw_l)
    ys = [_matmul(h, w_proj_l, mode="nn", tm=tm, tn=1024, tk=HGRN_WIDTH, b_idx=i, name="mm_proj")
          for i, h in enumerate((ha, hb, hc))]
    merged = _merge_fwd(u, *ys)
    y = _matmul(merged, w_out_l, mode="nn", tm=tm, tn=1024, tk=D_MODEL, name="mm_out")
    xn, z = _ln_fwd(x, y, lng_l, lnb_l)
    saved = dict(x=x, u=u, o_raw=o_raw, states=states, q4=q4, k4=k4, v4=v4, o_b=o_b, hs=(ha, hb, hc), ys=ys,
                 merged=merged, z=z)
    return xn, saved


def _layer_bwd(dxn, s, w_in_l, w_proj_l, w_out_l, lower_l, bias4, sink4_l, gn_l, cw_l, lng_l):
    T = dxn.shape[0]
    tm = _mm_rows(T)
    u = s["u"]
    dz, d_lng, d_lnb = _ln_bwd(s["z"], dxn, lng_l)
    dmerged = _matmul(dz, w_out_l, mode="nt", tm=tm, tn=1024, tk=D_MODEL, name="mm_dmerged")
    g_w_out = _matmul(s["merged"], dz, mode="tn", tm=1024, tn=1024, tk=1024, name="mm_gw_out")
    *dys, dma, dmb, dmc = _merge_bwd(u, *s["ys"], dmerged)
    dhs = [_matmul(dy, w_proj_l, mode="nt", tm=tm, tn=1024, tk=D_MODEL, b_idx=i, name="mm_dh") for i, dy in enumerate(dys)]
    g_w_proj = jnp.stack([_matmul(h, dy, mode="tn", tm=1024, tn=1024, tk=1024, name="mm_gw_proj")
                          for h, dy in zip(s["hs"], dys)])
    do_raw, do_b, dag, dbg, dcb, dcc, dcx, dcg, d_gn, d_cw = _mix_bwd(u, s["o_raw"], s["o_b"], gn_l, cw_l, *dhs)
    daq, daf, dai, d_lower = _hgrn_bwd(u, lower_l, s["states"], do_raw)
    dq4, dkc, dkp, dvc, dvp, d_bias4, d_sink4 = _attn_bwd(s["q4"], s["k4"], s["v4"], bias4, sink4_l, _q_to_blocks(do_b))
    dbq = _q_from_blocks(dq4).astype(BF16)
    dbk = _kv_from_heads(_shift_prev_contrib(dkc, dkp)).astype(BF16)
    dbv = _kv_from_heads(_shift_prev_contrib(dvc, dvp)).astype(BF16)
    du = jnp.concatenate([daq, daf, dai, dag, dbq, dbk, dbv, dbg, dcb, dcc, dcx, dcg, dma, dmb, dmc], axis=1)
    dx = _matmul(du, w_in_l, mode="nt", tm=tm, tn=1024, tk=1536, name="mm_dx", add=dz, add_scale=ALPHA)
    g_w_in = _matmul(s["x"], du, mode="tn", tm=1024, tn=768, tk=1024, name="mm_gw_in")
    d_sinks = jnp.sum(d_sink4.reshape(ATTN_HEADS, WINDOW), axis=-1)
    small = dict(lower=d_lower[0], gn=d_gn[0], sinks=d_sinks, cw=d_cw[:CONV_K], bias=d_bias4.reshape(ATTN_HEADS, WINDOW, 2 * WINDOW),
                 lng=d_lng[0], lnb=d_lnb[0])
    return dx, g_w_in, g_w_proj, g_w_out, small


def _local_step(x, target, w_in_full, w_proj_full, w_out_full, lb_param, hgrn_norm_g, attn_sinks, conv_w_full, rel_bias, ln_g, ln_b):
    lower, lower_vjp = jax.vjp(_lower_bounds, lb_param)
    bias, bias_vjp = jax.vjp(_band_bias, rel_bias)
    bias4 = bias.reshape(ATTN_KV_HEADS, QROWS, 2 * WINDOW)
    sink4 = jnp.broadcast_to(attn_sinks.reshape(DEPTH, ATTN_HEADS, 1, 1), (DEPTH, ATTN_HEADS, WINDOW, 1)).reshape(
        DEPTH, ATTN_KV_HEADS, QROWS, 1)
    row = lambda a, l: a[l:l + 1]
    saved = []
    h = x
    for l in range(DEPTH):
        h, s = _layer_fwd(h, w_in_full[l], w_proj_full[l], w_out_full[l], row(lower, l), bias4, sink4[l],
                          row(hgrn_norm_g, l), conv_w_full[l], row(ln_g, l), row(ln_b, l))
        saved.append(s)
    loss_blk, dh = _loss_head(h, target)
    g_in, g_proj, g_out, smalls = [None] * DEPTH, [None] * DEPTH, [None] * DEPTH, [None] * DEPTH
    for l in reversed(range(DEPTH)):
        dh, g_in[l], g_proj[l], g_out[l], smalls[l] = _layer_bwd(
            dh, saved[l], w_in_full[l], w_proj_full[l], w_out_full[l], row(lower, l), bias4, sink4[l],
            row(hgrn_norm_g, l), conv_w_full[l], row(ln_g, l))
    stack = lambda k: jnp.stack([sm[k] for sm in smalls])
    d_bias = smalls[0]["bias"] + smalls[1]["bias"] + smalls[2]["bias"] + smalls[3]["bias"]
    small = dict(
        lb_param=lower_vjp(stack("lower"))[0], hgrn_norm_g=stack("gn"), attn_sinks=stack("sinks"), conv_w=stack("cw"),
        rel_bias=bias_vjp(d_bias)[0], ln_g=stack("lng"), ln_b=stack("lnb"))
    return loss_blk, dh, g_in, g_proj, g_out, small


ANY = pl.BlockSpec(memory_space=pl.ANY)
DMA_SEM = pltpu.SemaphoreType.DMA


def _coords():
    return lax.axis_index("x"), lax.axis_index("y"), lax.axis_index("c")


def _other_chips(x, y):
    return [(1 - x, y), (x, 1 - y), (1 - x, 1 - y)]


def _remote(src, dst, send_sem, recv_sem, device):
    return pltpu.make_async_remote_copy(src_ref=src, dst_ref=dst, send_sem=send_sem, recv_sem=recv_sem,
                                        device_id=device, device_id_type=MESH)


def _sub(ref, axis, index, size):
    idx = [slice(None)] * len(ref.shape)
    idx[axis] = pl.ds(pl.multiple_of(index * size, size), size)
    return ref.at[tuple(idx)]


def _gather_shards(shards, layer, sax, name):
    shp = shards.shape[1:]
    hax = 3 - sax
    w, hw = shp[sax], shp[hax] // 2
    out_shape = list(shp)
    out_shape[sax] = w * N_CHIPS

    def body(src_ref, out_ref, s_ici, r_ici, s_d2d, r_d2d, loc_sem):
        x, y, c = _coords()
        me, sib, j = (x, y, c), (x, y, 1 - c), 2 * x + y
        src = src_ref.at[layer]
        region = lambda slab, half: _sub(_sub(out_ref, sax, slab, w), hax, half, hw)
        loc = pltpu.make_async_copy(src, _sub(out_ref, sax, j, w), loc_sem)
        loc.start()
        chips = _other_chips(x, y)
        first = [_remote(_sub(src, hax, c, hw), region(j, c), s_ici.at[k], r_ici.at[k], (px, py, c))
                 for k, (px, py) in enumerate(chips)]
        for cp in first:
            cp.start()
        passed = []
        for k, (px, py) in enumerate(chips):
            reg = region(2 * px + py, c)
            _remote(reg, reg, s_ici.at[k], r_ici.at[k], me).wait_recv()
            cp = _remote(reg, reg, s_d2d.at[k], r_d2d.at[k], sib)
            cp.start()
            passed.append(cp)
        for k, (px, py) in enumerate(chips):
            reg = region(2 * px + py, 1 - c)
            _remote(reg, reg, s_d2d.at[k], r_d2d.at[k], me).wait_recv()
        for cp in first + passed:
            cp.wait_send()
        loc.wait()

    return pl.pallas_call(
        body,
        name=name,
        in_specs=[ANY],
        out_specs=ANY,
        out_shape=_sds(tuple(out_shape), shards.dtype),
        scratch_shapes=[DMA_SEM((3,)), DMA_SEM((3,)), DMA_SEM((3,)), DMA_SEM((3,)), DMA_SEM(())],
    )(shards)


def _pair_exchange(g, hax, name):
    hw = g.shape[hax] // 2
    out_shape = list(g.shape)
    out_shape[hax] = hw

    def body(g_ref, recv_ref, send_sem, recv_sem):
        x, y, c = _coords()
        cp = _remote(_sub(g_ref, hax, 1 - c, hw), recv_ref, send_sem, recv_sem, (x, y, 1 - c))
        cp.start()
        cp.wait()

    return pl.pallas_call(
        body, name=name, in_specs=[ANY], out_specs=ANY, out_shape=_sds(tuple(out_shape), g.dtype),
        scratch_shapes=[DMA_SEM(()), DMA_SEM(())],
    )(g)


def _add_own_half(place, g, recv, hax, blk, name):
    L, ah, bh = recv.shape
    tr, tc = blk
    nr, nc = ah // tr, bh // tc

    def g_map(l, i, jc, p):
        return (l, i + p[0] * nr, jc) if hax == 1 else (l, i, jc + p[0] * nc)

    def body(p_ref, g_ref, r_ref, o_ref):
        o_ref[...] = g_ref[...] + r_ref[...]

    same = pl.BlockSpec((None, tr, tc), lambda l, i, jc, p: (l, i, jc))
    return pl.pallas_call(
        body,
        name=name,
        grid_spec=pltpu.PrefetchScalarGridSpec(
            num_scalar_prefetch=1, grid=(L, nr, nc),
            in_specs=[pl.BlockSpec((None, tr, tc), g_map), same], out_specs=same),
        out_shape=_sds(recv.shape, F32),
        compiler_params=_cp(("parallel", "parallel", "parallel")),
    )(place, g, recv)


def _slab_exchange(p, sax, name):
    w = p.shape[sax] // N_CHIPS
    slab_shape = list(p.shape)
    slab_shape[sax] = w

    def body(p_ref, recv_ref, send_sems, recv_sems):
        x, y, c = _coords()
        copies = [_remote(_sub(p_ref, sax, 2 * px + py, w), recv_ref.at[k], send_sems.at[k], recv_sems.at[k], (px, py, c))
                  for k, (px, py) in enumerate(_other_chips(x, y))]
        for cp in copies:
            cp.start()
        for cp in copies:
            cp.wait()

    return pl.pallas_call(
        body, name=name, in_specs=[ANY], out_specs=ANY, out_shape=_sds((3, *slab_shape), p.dtype),
        scratch_shapes=[DMA_SEM((3,)), DMA_SEM((3,))],
    )(p)


def _add_slabs(place, p, recv, sax, blk, name):
    _, L, a, b = recv.shape
    tr, tc = blk
    nr, nc = a // tr, b // tc

    def p_map(l, i, jc, pr):
        return (l, i + pr[1] * nr, jc) if sax == 1 else (l, i, jc + pr[1] * nc)

    def body(p_ref, own_ref, r0_ref, r1_ref, r2_ref, o_ref):
        o_ref[...] = ((own_ref[...] + r0_ref[...]) + r1_ref[...]) + r2_ref[...]

    def rk(k):
        return pl.BlockSpec((None, None, tr, tc), lambda l, i, jc, pr, k=k: (k, l, i, jc))

    return pl.pallas_call(
        body,
        name=name,
        grid_spec=pltpu.PrefetchScalarGridSpec(
            num_scalar_prefetch=1, grid=(L, nr, nc),
            in_specs=[pl.BlockSpec((None, tr, tc), p_map), rk(0), rk(1), rk(2)],
            out_specs=pl.BlockSpec((None, tr, tc), lambda l, i, jc, pr: (l, i, jc))),
        out_shape=_sds((L, a, b), F32),
        compiler_params=_cp(("parallel", "parallel", "parallel")),
    )(place, p, recv, recv, recv)


def _pair_assemble(r, hax, name):
    hw = r.shape[hax]
    out_shape = list(r.shape)
    out_shape[hax] = 2 * hw

    def body(r_ref, out_ref, send_sem, recv_sem, loc_sem):
        x, y, c = _coords()
        mine, other = _sub(out_ref, hax, c, hw), _sub(out_ref, hax, 1 - c, hw)
        loc = pltpu.make_async_copy(r_ref, mine, loc_sem)
        loc.start()
        cp = _remote(r_ref, mine, send_sem, recv_sem, (x, y, 1 - c))
        cp.start()
        _remote(other, other, send_sem, recv_sem, (x, y, c)).wait_recv()
        cp.wait_send()
        loc.wait()

    return pl.pallas_call(
        body, name=name, in_specs=[ANY], out_specs=ANY, out_shape=_sds(tuple(out_shape), r.dtype),
        scratch_shapes=[DMA_SEM(()), DMA_SEM(()), DMA_SEM(())],
    )(r)


def _reduce_scatter(place, g, sax, blk_pair, blk_slab, tag):
    hax = 3 - sax
    recv = _pair_exchange(g, hax, "rs_pair_" + tag)
    p = _add_own_half(place, g, recv, hax, blk_pair, "rs_pair_add_" + tag)
    slabs = _slab_exchange(p, sax, "rs_slab_" + tag)
    r = _add_slabs(place, p, slabs, sax, blk_slab, "rs_slab_add_" + tag)
    return _pair_assemble(r, hax, "rs_assemble_" + tag)


N_DEV = 8


def _all_reduce_small(v, name):
    rows = v.shape[0]

    def body(v_ref, gath_ref, sum_ref, send_sems, recv_sems, local_sem):
        x, y, c = _coords()
        me, sib = (x, y, c), (x, y, 1 - c)
        chips = _other_chips(x, y)

        def slot(px, py, pc):
            return gath_ref.at[pl.ds(pl.multiple_of((4 * px + 2 * py + pc) * rows, rows), rows), :]

        def copy(k, block, to, src=None):
            return _remote(slot(*block) if src is None else src, slot(*block), send_sems.at[k], recv_sems.at[k], to)

        mine = pltpu.make_async_copy(v_ref, slot(*me), local_sem)
        mine.start()
        first = [copy(0, me, sib, src=v_ref)] + [copy(1 + k, me, (*chip, c), src=v_ref) for k, chip in enumerate(chips)]
        for cp in first:
            cp.start()
        passed = [copy(4 + k, (*chip, c), sib) for k, chip in enumerate(chips)]
        for k, chip in enumerate(chips):
            copy(1 + k, (*chip, c), me).wait_recv()
            passed[k].start()
        copy(0, sib, me).wait_recv()
        for k, chip in enumerate(chips):
            copy(4 + k, (*chip, 1 - c), me).wait_recv()
        for cp in first + passed:
            cp.wait_send()
        mine.wait()
        acc = gath_ref[0:rows, :]
        for d in range(1, N_DEV):
            acc = acc + gath_ref[d * rows:(d + 1) * rows, :]
        sum_ref[...] = acc

    vm = pl.BlockSpec(memory_space=pltpu.VMEM)
    return pl.pallas_call(
        body, name=name, in_specs=[vm], out_specs=[vm, vm],
        out_shape=[_sds((N_DEV * rows, 128), F32), _sds((rows, 128), F32)],
        scratch_shapes=[DMA_SEM((7,)), DMA_SEM((7,)), DMA_SEM(())],
    )(v)[1]


def _pad_rows(a):
    flat = a.reshape(-1).astype(F32)
    rows = -(-flat.shape[0] // (8 * 128)) * 8
    return jnp.pad(flat, (0, rows * 128 - flat.shape[0])).reshape(rows, 128)


def _sum_over_devices(parts, name):
    blocks = [_pad_rows(a) for a in parts.values()]
    total = _all_reduce_small(jnp.concatenate(blocks, axis=0), name)
    out, r0 = {}, 0
    for (key, a), blk in zip(parts.items(), blocks):
        out[key] = total[r0:r0 + blk.shape[0]].reshape(-1)[:a.size].reshape(a.shape)
        r0 += blk.shape[0]
    return out


def kernel(x, w_in, w_proj_hgrn, w_proj_attn, w_proj_conv, w_out, lb_param, hgrn_norm_g, attn_sinks, conv_w, rel_bias, ln_g, ln_b, loss_target, m_w_in, m_w_proj_hgrn, m_w_proj_attn, m_w_proj_conv, m_w_out, m_lb_param, m_hgrn_norm_g, m_attn_sinks, m_conv_w, m_rel_bias, m_ln_g, m_ln_b, v_w_in, v_w_proj_hgrn, v_w_proj_attn, v_w_proj_conv, v_w_out, v_lb_param, v_hgrn_norm_g, v_attn_sinks, v_conv_w, v_rel_bias, v_ln_g, v_ln_b):
    xi, yi, ci = _coords()
    slab = 2 * xi + yi
    place = jnp.stack([ci, slab]).astype(jnp.int32)
    conv_cols = conv_w.shape[-1]

    w_in_b = w_in.astype(BF16)[:, None]
    w_proj_b = jnp.stack([w_proj_hgrn, w_proj_attn, w_proj_conv], axis=1).astype(BF16)
    w_out_b = w_out.astype(BF16)[:, None]
    w_in_full = [_gather_shards(w_in_b, l, 2, f"gather_w_in_{l}")[0] for l in range(DEPTH)]
    w_proj_full = [_gather_shards(w_proj_b, l, 2, f"gather_w_proj_{l}") for l in range(DEPTH)]
    w_out_full = [_gather_shards(w_out_b, l, 1, f"gather_w_out_{l}")[0] for l in range(DEPTH)]
    conv_spread = lax.dynamic_update_slice(jnp.zeros((DEPTH, CONV_K, CONV_WIDTH), F32), conv_w, (0, 0, slab * conv_cols))
    conv_full = 0.5 * _sum_over_devices({"conv_w": conv_spread}, "gather_conv_w")["conv_w"]

    loss_blk, dx, g_in, g_proj, g_out, small = _local_step(
        x[0], loss_target[0], w_in_full, w_proj_full, w_out_full, lb_param, hgrn_norm_g, attn_sinks, conv_full,
        rel_bias, ln_g, ln_b)

    gs_in, gs_proj, gs_out = [], [], []
    for l in range(DEPTH):
        gs_in.append(_reduce_scatter(place, g_in[l][None], 2, (128, 4224), (128, 4224), f"w_in_{l}")[0])
        gs_proj.append(_reduce_scatter(place, g_proj[l], 2, (256, 2048), (512, 512), f"w_proj_{l}"))
        gs_out.append(_reduce_scatter(place, g_out[l][None], 1, (256, 1024), (256, 1024), f"w_out_{l}")[0])
    g_w_in, g_w_out = jnp.stack(gs_in), jnp.stack(gs_out)
    g_w_proj = jnp.stack(gs_proj)
    small = dict(small, loss=loss_blk[0:1, 0:1])
    small = _sum_over_devices(small, "sum_small")
    loss = small["loss"][0, 0]
    g_conv = lax.dynamic_slice(small["conv_w"], (0, 0, slab * conv_cols), (DEPTH, CONV_K, conv_cols))

    grads = [g_w_in, g_w_proj[:, 0], g_w_proj[:, 1], g_w_proj[:, 2], g_w_out, small["lb_param"], small["hgrn_norm_g"],
             small["attn_sinks"], g_conv, small["rel_bias"], small["ln_g"], small["ln_b"]]
    names = ["w_in", "w_proj_hgrn", "w_proj_attn", "w_proj_conv", "w_out", "lb_param", "hgrn_norm_g", "attn_sinks",
             "conv_w", "rel_bias", "ln_g", "ln_b"]
    ws = [w_in, w_proj_hgrn, w_proj_attn, w_proj_conv, w_out, lb_param, hgrn_norm_g, attn_sinks, conv_w, rel_bias, ln_g, ln_b]
    ms = [m_w_in, m_w_proj_hgrn, m_w_proj_attn, m_w_proj_conv, m_w_out, m_lb_param, m_hgrn_norm_g, m_attn_sinks, m_conv_w,
          m_rel_bias, m_ln_g, m_ln_b]
    vs = [v_w_in, v_w_proj_hgrn, v_w_proj_attn, v_w_proj_conv, v_w_out, v_lb_param, v_hgrn_norm_g, v_attn_sinks, v_conv_w,
          v_rel_bias, v_ln_g, v_ln_b]
    upd = [_adamw(w, g, m, v, "adamw_" + n) for n, w, g, m, v in zip(names, ws, grads, ms, vs)]
    deltas, new_ms, new_vs = zip(*upd)
    return (loss, dx[None], *grads, *deltas, *new_ms, *new_vs)
```

```python
import functools
import math

import jax
import jax.numpy as jnp
from jax import lax
from jax.experimental import pallas as pl
from jax.experimental.pallas import tpu as pltpu

F32 = jnp.float32
BF16 = jnp.bfloat16
MXU_DTYPE = BF16

D_MODEL = 2048
DEPTH = 4
HGRN_WIDTH = 1024
HGRN_HEAD_DIM = 128
HGRN_HEADS = 8
HGRN_CHUNK = 64
ATTN_HEAD_DIM = 64
ATTN_HEADS = 16
ATTN_KV_HEADS = 4
ATTN_GROUP = ATTN_HEADS // ATTN_KV_HEADS
ATTN_WIDTH = 1024
KV_WIDTH = 256
WINDOW = 128
CONV_WIDTH = 1024
CONV_K = 3
N_BUCKETS = 32
MAX_DISTANCE = 128
ALPHA = (2.0 * DEPTH) ** 0.25
LN_EPS = 1e-5
RMS_EPS = 1e-6
N_IN = 16896
OFF_AQ, OFF_AF, OFF_AI, OFF_AG = 0, 1024, 2048, 3072
OFF_BQ, OFF_BK, OFF_BV, OFF_BG = 4096, 5120, 5376, 5632
OFF_CB, OFF_CC, OFF_CX, OFF_CG = 6656, 7680, 8704, 9728
OFF_MA, OFF_MB, OFF_MC = 10752, 12800, 14848

ADAM_LR = 0.001
ADAM_B1 = 0.9
ADAM_B2 = 0.999
ADAM_EPS = 1e-08
ADAM_WD = 0.01
ADAM_STEP = 10

N_CHIPS = 4
VMEM_LIMIT_BYTES = 48 * 1024 * 1024
EXP_CLAMP = 80.0
MASK_VALUE = -1e30
MESH = pl.DeviceIdType.MESH


def _cp(sem=None):
    return pltpu.CompilerParams(dimension_semantics=sem, vmem_limit_bytes=VMEM_LIMIT_BYTES)


def _tile(dim, pref):
    return pref if dim % pref == 0 else dim


def _sds(shape, dtype):
    return jax.ShapeDtypeStruct(shape, dtype)


_DIMS = {
    "nn": (((1,), (0,)), ((), ())),
    "nt": (((1,), (1,)), ((), ())),
    "tn": (((0,), (0,)), ((), ())),
}


def _dot_raw(a, b, mode):
    return lax.dot_general(a.astype(MXU_DTYPE), b.astype(MXU_DTYPE), _DIMS[mode], preferred_element_type=F32)


@functools.partial(jax.custom_vjp, nondiff_argnums=(2,))
def _dot(a, b, mode):
    return _dot_raw(a, b, mode)


def _dot_fwd(a, b, mode):
    return _dot_raw(a, b, mode), (a, b)


def _dot_bwd(mode, res, g):
    a, b = res
    if mode == "nn":
        return _dot_raw(g, b, "nt"), _dot_raw(a, g, "tn")
    if mode == "nt":
        return _dot_raw(g, b, "nn"), _dot_raw(g, a, "tn")
    return _dot_raw(b, g, "nt"), _dot_raw(a, g, "nn")


_dot.defvjp(_dot_fwd, _dot_bwd)


def _matmul(a, b, *, mode, tm, tn, tk, name, a_idx=None, b_idx=None, out_dtype=F32, add=None, add_scale=1.0):
    a2, b2 = a.shape[-2:], b.shape[-2:]
    if mode == "nn":
        (M, K), (K2, N) = a2, b2
    elif mode == "nt":
        (M, K), (N, K2) = a2, b2
    else:
        (K, M), (K2, N) = a2, b2
    assert K == K2, (a.shape, b.shape, mode)
    tm, tn, tk = _tile(M, tm), _tile(N, tn), _tile(K, tk)
    nk = K // tk

    a_blk = (tk, tm) if mode == "tn" else (tm, tk)
    b_blk = (tn, tk) if mode == "nt" else (tk, tn)

    def a_map(i, j, k):
        ij = (k, i) if mode == "tn" else (i, k)
        return ij if a_idx is None else (a_idx,) + ij

    def b_map(i, j, k):
        ij = (j, k) if mode == "nt" else (k, j)
        return ij if b_idx is None else (b_idx,) + ij

    in_specs = [
        pl.BlockSpec(a_blk if a_idx is None else (None,) + a_blk, a_map),
        pl.BlockSpec(b_blk if b_idx is None else (None,) + b_blk, b_map),
    ]
    args = [a, b]
    if add is not None:
        in_specs.append(pl.BlockSpec((tm, tn), lambda i, j, k: (i, j)))
        args.append(add)
    n_in = len(args)

    def body(*refs):
        a_ref, b_ref = refs[0], refs[1]
        o_ref = refs[n_in]
        p = _dot_raw(a_ref[...], b_ref[...], mode)

        def finish(val):
            if add is not None:
                val = val + add_scale * refs[2][...]
            o_ref[...] = val.astype(out_dtype)

        if nk == 1:
            finish(p)
        else:
            acc_ref = refs[n_in + 1]
            k = pl.program_id(2)

            @pl.when(k == 0)
            def _():
                acc_ref[...] = p

            @pl.when(k > 0)
            def _():
                acc_ref[...] += p

            @pl.when(k == nk - 1)
            def _():
                finish(acc_ref[...])

    return pl.pallas_call(
        body,
        name=name,
        grid=(M // tm, N // tn, nk),
        in_specs=in_specs,
        out_specs=pl.BlockSpec((tm, tn), lambda i, j, k: (i, j)),
        out_shape=_sds((M, N), out_dtype),
        scratch_shapes=[pltpu.VMEM((tm, tn), F32)] if nk > 1 else [],
        compiler_params=_cp(("parallel", "parallel", "arbitrary")),
    )(*args)


def _scan_rows(x, reverse):
    n = x.shape[0]
    row = lax.broadcasted_iota(jnp.int32, x.shape, 0)
    s = 1
    while s < n:
        if reverse:
            x = x + jnp.where(row < n - s, pltpu.roll(x, n - s, 0), 0.0)
        else:
            x = x + jnp.where(row >= s, pltpu.roll(x, s, 0), 0.0)
        s *= 2
    return x


@jax.custom_vjp
def _cumsum_rows(x):
    return _scan_rows(x, False)


_cumsum_rows.defvjp(lambda x: (_scan_rows(x, False), None), lambda _, g: (_scan_rows(g, True),))


def _hgrn_chunk(state_t, qraw, fraw, v, lb):
    c = HGRN_CHUNK
    q = qraw * jax.nn.sigmoid(qraw) * (HGRN_HEAD_DIM ** -0.5)
    f = lb + (1.0 - lb) * jax.nn.sigmoid(fraw)
    k = 1.0 - f
    g = jnp.log(f)
    b = _cumsum_rows(g)
    row = lax.broadcasted_iota(jnp.int32, (c, HGRN_HEAD_DIM), 0)
    b_end = jnp.sum(g, axis=0, keepdims=True)
    b_mid = jnp.sum(jnp.where(row < c // 2, g, 0.0), axis=0, keepdims=True)
    inter = _dot(q * jnp.exp(b), state_t, "nt")
    qt = q * jnp.exp(jnp.minimum(b - b_mid, EXP_CLAMP))
    kt = k * jnp.exp(jnp.minimum(b_mid - b, EXP_CLAMP))
    s = _dot(qt, kt, "nt")
    ti = lax.broadcasted_iota(jnp.int32, (c, c), 0)
    si = lax.broadcasted_iota(jnp.int32, (c, c), 1)
    s = jnp.where(si <= ti, s, 0.0)
    intra = _dot(s, v, "nn")
    k_end = k * jnp.exp(b_end - b)
    new_state_t = state_t * jnp.exp(b_end) + _dot(v, k_end, "tn")
    return new_state_t, inter + intra


def _hgrn_specs(T):
    rows = _tile(T, 512)
    return rows, T // rows, rows // HGRN_CHUNK


def _hgrn_fwd(u, lower_l):
    T = u.shape[0]
    rows, nblk, ncr = _hgrn_specs(T)
    hb = HGRN_WIDTH // HGRN_HEAD_DIM

    def body(q_ref, f_ref, i_ref, lb_ref, o_ref, st_ref, state):
        @pl.when(pl.program_id(1) == 0)
        def _():
            state[...] = jnp.zeros_like(state)

        lb = lb_ref[...]
        for c in range(ncr):
            rs = pl.ds(c * HGRN_CHUNK, HGRN_CHUNK)
            st = state[...]
            st_ref[c] = st
            new, out = _hgrn_chunk(st, q_ref[rs, :], f_ref[rs, :], i_ref[rs, :], lb)
            state[...] = new
            o_ref[rs, :] = out

    blk = (rows, HGRN_HEAD_DIM)
    return pl.pallas_call(
        body,
        name="hgrn_fwd",
        grid=(HGRN_HEADS, nblk),
        in_specs=[
            pl.BlockSpec(blk, lambda h, r: (r, OFF_AQ // 128 + h)),
            pl.BlockSpec(blk, lambda h, r: (r, OFF_AF // 128 + h)),
            pl.BlockSpec(blk, lambda h, r: (r, OFF_AI // 128 + h)),
            pl.BlockSpec((1, HGRN_HEAD_DIM), lambda h, r: (0, h)),
        ],
        out_specs=[
            pl.BlockSpec(blk, lambda h, r: (r, h)),
            pl.BlockSpec((ncr, None, HGRN_HEAD_DIM, HGRN_HEAD_DIM), lambda h, r: (r, h, 0, 0)),
        ],
        out_shape=[_sds((T, HGRN_WIDTH), F32), _sds((T // HGRN_CHUNK, hb, HGRN_HEAD_DIM, HGRN_HEAD_DIM), F32)],
        scratch_shapes=[pltpu.VMEM((HGRN_HEAD_DIM, HGRN_HEAD_DIM), F32)],
        compiler_params=_cp(("parallel", "arbitrary")),
    )(u, u, u, lower_l)


def _hgrn_bwd(u, lower_l, states, do_raw):
    T = u.shape[0]
    rows, nblk, ncr = _hgrn_specs(T)

    def body(q_ref, f_ref, i_ref, lb_ref, st_ref, do_ref, dq_ref, df_ref, di_ref, dlb_ref, dstate):
        @pl.when(pl.program_id(1) == 0)
        def _():
            dstate[...] = jnp.zeros_like(dstate)
            dlb_ref[...] = jnp.zeros_like(dlb_ref)

        lb = lb_ref[...]
        for c in reversed(range(ncr)):
            rs = pl.ds(c * HGRN_CHUNK, HGRN_CHUNK)
            _, vjp = jax.vjp(_hgrn_chunk, st_ref[c], q_ref[rs, :], f_ref[rs, :], i_ref[rs, :], lb)
            dst, dq, df, dv, dlb = vjp((dstate[...], do_ref[rs, :]))
            dstate[...] = dst
            dq_ref[rs, :] = dq.astype(BF16)
            df_ref[rs, :] = df.astype(BF16)
            di_ref[rs, :] = dv.astype(BF16)
            dlb_ref[...] += dlb

    blk = (rows, HGRN_HEAD_DIM)
    last = nblk - 1
    out_blk = pl.BlockSpec(blk, lambda h, r: (last - r, h))
    return pl.pallas_call(
        body,
        name="hgrn_bwd",
        grid=(HGRN_HEADS, nblk),
        in_specs=[
            pl.BlockSpec(blk, lambda h, r: (last - r, OFF_AQ // 128 + h)),
            pl.BlockSpec(blk, lambda h, r: (last - r, OFF_AF // 128 + h)),
            pl.BlockSpec(blk, lambda h, r: (last - r, OFF_AI // 128 + h)),
            pl.BlockSpec((1, HGRN_HEAD_DIM), lambda h, r: (0, h)),
            pl.BlockSpec((ncr, None, HGRN_HEAD_DIM, HGRN_HEAD_DIM), lambda h, r: (last - r, h, 0, 0)),
            out_blk,
        ],
        out_specs=[out_blk, out_blk, out_blk, pl.BlockSpec((1, HGRN_HEAD_DIM), lambda h, r: (0, h))],
        out_shape=[_sds((T, HGRN_WIDTH), BF16)] * 3 + [_sds((1, HGRN_WIDTH), F32)],
        scratch_shapes=[pltpu.VMEM((HGRN_HEAD_DIM, HGRN_HEAD_DIM), F32)],
        compiler_params=_cp(("parallel", "arbitrary")),
    )(u, u, u, lower_l, states, do_raw)


QROWS = ATTN_GROUP * WINDOW


def _attn_block(q, kp, kc, vp, vc, bp, bc, sink, mp, mc):
    scale = ATTN_HEAD_DIM ** -0.5
    sp = jnp.where(mp, _dot(q, kp, "nt") * scale + bp, MASK_VALUE)
    sc = jnp.where(mc, _dot(q, kc, "nt") * scale + bc, MASK_VALUE)
    m = jnp.maximum(jnp.maximum(jnp.max(sp, axis=-1, keepdims=True), jnp.max(sc, axis=-1, keepdims=True)), sink)
    m = lax.stop_gradient(m)
    pp = jnp.exp(sp - m)
    pc = jnp.exp(sc - m)
    den = jnp.sum(pp, axis=-1, keepdims=True) + jnp.sum(pc, axis=-1, keepdims=True) + jnp.exp(sink - m)
    inv = 1.0 / den
    return _dot(pp * inv, vp, "nn") + _dot(pc * inv, vc, "nn")


def _attn_masks(first_block):
    i = lax.broadcasted_iota(jnp.int32, (QROWS, WINDOW), 0) % WINDOW
    j = lax.broadcasted_iota(jnp.int32, (QROWS, WINDOW), 1)
    return (j > i) & jnp.logical_not(first_block), j <= i


def _attn_in_specs():
    q_spec = pl.BlockSpec((None, None, QROWS, ATTN_HEAD_DIM), lambda h, n: (h, n, 0, 0))
    cur = pl.BlockSpec((None, WINDOW, ATTN_HEAD_DIM), lambda h, n: (h, n, 0))
    prev = pl.BlockSpec((None, WINDOW, ATTN_HEAD_DIM), lambda h, n: (h, jnp.maximum(n - 1, 0), 0))
    bias = pl.BlockSpec((None, QROWS, 2 * WINDOW), lambda h, n: (h, 0, 0))
    sink = pl.BlockSpec((None, QROWS, 1), lambda h, n: (h, 0, 0))
    return q_spec, cur, prev, bias, sink


def _attn_fwd(q4, k4, v4, bias4, sink4):
    kvh, nb = q4.shape[0], q4.shape[1]
    q_spec, cur, prev, bias, sink = _attn_in_specs()

    def body(q_ref, kp_ref, kc_ref, vp_ref, vc_ref, b_ref, s_ref, o_ref):
        mp, mc = _attn_masks(pl.program_id(1) == 0)
        o_ref[...] = _attn_block(q_ref[...], kp_ref[...], kc_ref[...], vp_ref[...], vc_ref[...],
                                 b_ref[:, :WINDOW], b_ref[:, WINDOW:], s_ref[...], mp, mc)

    return pl.pallas_call(
        body,
        name="attn_fwd",
        grid=(kvh, nb),
        in_specs=[q_spec, prev, cur, prev, cur, bias, sink],
        out_specs=q_spec,
        out_shape=_sds(q4.shape, F32),
        compiler_params=_cp(("parallel", "arbitrary")),
    )(q4, k4, k4, v4, v4, bias4, sink4)


def _attn_bwd(q4, k4, v4, bias4, sink4, do4):
    kvh, nb = q4.shape[0], q4.shape[1]
    T = k4.shape[1]
    q_spec, cur, prev, bias, sink = _attn_in_specs()

    def body(q_ref, kp_ref, kc_ref, vp_ref, vc_ref, b_ref, s_ref, do_ref,
             dq_ref, dkc_ref, dkp_ref, dvc_ref, dvp_ref, db_ref, ds_ref):
        n = pl.program_id(1)
        mp, mc = _attn_masks(n == 0)
        _, vjp = jax.vjp(
            functools.partial(_attn_block, mp=mp, mc=mc),
            q_ref[...], kp_ref[...], kc_ref[...], vp_ref[...], vc_ref[...],
            b_ref[:, :WINDOW], b_ref[:, WINDOW:], s_ref[...])
        dq, dkp, dkc, dvp, dvc, dbp, dbc, dsink = vjp(do_ref[...])
        dq_ref[...] = dq
        dkc_ref[...] = dkc
        dkp_ref[...] = dkp
        dvc_ref[...] = dvc
        dvp_ref[...] = dvp

        @pl.when(n == 0)
        def _():
            db_ref[...] = jnp.zeros_like(db_ref)
            ds_ref[...] = jnp.zeros_like(ds_ref)

        db_ref[:, :WINDOW] += dbp
        db_ref[:, WINDOW:] += dbc
        ds_ref[...] += dsink

    kv_sds = _sds((kvh, T, ATTN_HEAD_DIM), F32)
    return pl.pallas_call(
        body,
        name="attn_bwd",
        grid=(kvh, nb),
        in_specs=[q_spec, prev, cur, prev, cur, bias, sink, q_spec],
        out_specs=[q_spec, cur, cur, cur, cur, bias, sink],
        out_shape=[_sds(q4.shape, F32), kv_sds, kv_sds, kv_sds, kv_sds,
                   _sds((kvh, QROWS, 2 * WINDOW), F32), _sds((kvh, QROWS, 1), F32)],
        compiler_params=_cp(("parallel", "arbitrary")),
    )(q4, k4, k4, v4, v4, bias4, sink4, do4)


def _q_to_blocks(a):
    T = a.shape[0]
    a = a.reshape(T // WINDOW, WINDOW, ATTN_KV_HEADS, ATTN_GROUP, ATTN_HEAD_DIM)
    return jnp.transpose(a, (2, 0, 3, 1, 4)).reshape(ATTN_KV_HEADS, T // WINDOW, QROWS, ATTN_HEAD_DIM)


def _q_from_blocks(a):
    kvh, nb = a.shape[0], a.shape[1]
    a = a.reshape(kvh, nb, ATTN_GROUP, WINDOW, ATTN_HEAD_DIM)
    return jnp.transpose(a, (1, 3, 0, 2, 4)).reshape(nb * WINDOW, ATTN_WIDTH)


def _kv_to_heads(a):
    T = a.shape[0]
    return jnp.transpose(a.reshape(T, ATTN_KV_HEADS, ATTN_HEAD_DIM), (1, 0, 2))


def _kv_from_heads(a):
    return jnp.transpose(a, (1, 0, 2)).reshape(a.shape[1], KV_WIDTH)


def _shift_prev_contrib(cur, nxt):
    pad = jnp.zeros_like(nxt[:, :WINDOW])
    return cur + jnp.concatenate([nxt[:, WINDOW:], pad], axis=1)


MIX_COLS = 512


def _silu(x):
    return x * jax.nn.sigmoid(x)


def _silu_grad(x):
    s = jax.nn.sigmoid(x)
    return s * (1.0 + x * (1.0 - s))


def _shift_rows_down(h, first, second):
    n = h.shape[0]
    row = lax.broadcasted_iota(jnp.int32, h.shape, 0)
    s1 = jnp.where(row == 0, first, pltpu.roll(h, 1, 0))
    s2 = jnp.where(row == 0, second, jnp.where(row == 1, first, pltpu.roll(h, 2, 0)))
    del n
    return s1, s2


def _shift_rows_up(h, first, second):
    n = h.shape[0]
    row = lax.broadcasted_iota(jnp.int32, h.shape, 0)
    s1 = jnp.where(row == n - 1, first, pltpu.roll(h, n - 1, 0))
    s2 = jnp.where(row == n - 1, second, jnp.where(row == n - 2, first, pltpu.roll(h, n - 2, 0)))
    return s1, s2


def _mix_rows(T):
    return _tile(T, 256)


def _mix_fwd(u, o_raw, o_b, gn_l, cw_l):
    T = u.shape[0]
    tr = _mix_rows(T)
    nrow = T // tr
    hr = tr // 8

    def ucol(off):
        return pl.BlockSpec((tr, MIX_COLS), lambda i, j, off=off: (i, off // MIX_COLS + j))

    def uprev(off):
        return pl.BlockSpec((8, MIX_COLS), lambda i, j, off=off: (jnp.maximum(i * hr - 1, 0), off // MIX_COLS + j))

    act = pl.BlockSpec((tr, MIX_COLS), lambda i, j: (i, j))
    par = lambda rows: pl.BlockSpec((rows, MIX_COLS), lambda i, j: (0, j))

    def body(oraw_ref, ag_ref, ob_ref, bg_ref, cb_ref, cc_ref, cx_ref, cg_ref, ccp_ref, cxp_ref, gn_ref, cw_ref,
             ha_ref, hb_ref, hc_ref, hat_ref, hbt_ref, hct_ref):
        ag = _silu(ag_ref[...])
        for h in range(MIX_COLS // HGRN_HEAD_DIM):
            cs = slice(h * HGRN_HEAD_DIM, (h + 1) * HGRN_HEAD_DIM)
            o = oraw_ref[:, cs]
            nrm = o * lax.rsqrt(jnp.mean(o * o, axis=-1, keepdims=True) + RMS_EPS)
            ha = nrm * gn_ref[:, cs] * ag[:, cs]
            ha_ref[:, cs] = ha.astype(BF16)
            hat_ref[cs, :] = ha.T.astype(BF16)
        hb = ob_ref[...] * _silu(bg_ref[...])
        hb_ref[...] = hb.astype(BF16)
        hbt_ref[...] = hb.T.astype(BF16)
        keep = (pl.program_id(0) > 0).astype(F32)
        hcur = cc_ref[...] * cx_ref[...]
        p1 = ccp_ref[7:8, :] * cxp_ref[7:8, :] * keep
        p2 = ccp_ref[6:7, :] * cxp_ref[6:7, :] * keep
        s1, s2 = _shift_rows_down(hcur, p1, p2)
        y = cw_ref[0:1, :] * s2 + cw_ref[1:2, :] * s1 + cw_ref[2:3, :] * hcur
        hc = cb_ref[...] * y * _silu(cg_ref[...])
        hc_ref[...] = hc.astype(BF16)
        hct_ref[...] = hc.T.astype(BF16)

    out = _sds((T, HGRN_WIDTH), BF16)
    out_t = _sds((HGRN_WIDTH, T), BF16)
    act_t = pl.BlockSpec((MIX_COLS, tr), lambda i, j: (j, i))
    return pl.pallas_call(
        body,
        name="mix_fwd",
        grid=(nrow, HGRN_WIDTH // MIX_COLS),
        in_specs=[act, ucol(OFF_AG), act, ucol(OFF_BG), ucol(OFF_CB), ucol(OFF_CC), ucol(OFF_CX), ucol(OFF_CG),
                  uprev(OFF_CC), uprev(OFF_CX), par(1), par(CONV_K)],
        out_specs=[act, act, act, act_t, act_t, act_t],
        out_shape=[out, out, out, out_t, out_t, out_t],
        compiler_params=_cp(("parallel", "parallel")),
    )(o_raw, u, o_b, u, u, u, u, u, u, u, gn_l, cw_l)


def _mix_bwd(u, o_raw, o_b, gn_l, cw_l, dha, dhb, dhc):
    T = u.shape[0]
    tr = _mix_rows(T)
    nrow = T // tr
    hr = tr // 8
    last_halo = T // 8 - 1

    def ucol(off):
        return pl.BlockSpec((tr, MIX_COLS), lambda j, i, off=off: (i, off // MIX_COLS + j))

    def uprev(off):
        return pl.BlockSpec((8, MIX_COLS), lambda j, i, off=off: (jnp.maximum(i * hr - 1, 0), off // MIX_COLS + j))

    def unext(off):
        return pl.BlockSpec((8, MIX_COLS), lambda j, i, off=off: (jnp.minimum((i + 1) * hr, last_halo), off // MIX_COLS + j))

    act = pl.BlockSpec((tr, MIX_COLS), lambda j, i: (i, j))
    act_next = pl.BlockSpec((8, MIX_COLS), lambda j, i: (jnp.minimum((i + 1) * hr, last_halo), j))
    par = lambda rows: pl.BlockSpec((rows, MIX_COLS), lambda j, i: (0, j))

    def body(oraw_ref, ag_ref, ob_ref, bg_ref, cb_ref, cc_ref, cx_ref, cg_ref, ccp_ref, cxp_ref,
             cbn_ref, cgn_ref, dhcn_ref, gn_ref, cw_ref, dha_ref, dhb_ref, dhc_ref,
             doraw_ref, dob_ref, dag_ref, dbg_ref, dcb_ref, dcc_ref, dcx_ref, dcg_ref, dgn_ref, dcw_ref):
        i = pl.program_id(1)

        @pl.when(i == 0)
        def _():
            dgn_ref[...] = jnp.zeros_like(dgn_ref)
            dcw_ref[...] = jnp.zeros_like(dcw_ref)

        ag = ag_ref[...]
        sag = _silu(ag)
        dha = dha_ref[...]
        for h in range(MIX_COLS // HGRN_HEAD_DIM):
            cs = slice(h * HGRN_HEAD_DIM, (h + 1) * HGRN_HEAD_DIM)
            o = oraw_ref[:, cs]
            rs = lax.rsqrt(jnp.mean(o * o, axis=-1, keepdims=True) + RMS_EPS)
            nrm = o * rs
            gn = gn_ref[:, cs]
            d = dha[:, cs]
            dag_ref[:, cs] = (d * nrm * gn * _silu_grad(ag[:, cs])).astype(BF16)
            dgn_ref[:, cs] += jnp.sum(d * sag[:, cs] * nrm, axis=0, keepdims=True)
            dn = d * sag[:, cs] * gn
            doraw_ref[:, cs] = rs * (dn - nrm * jnp.mean(dn * nrm, axis=-1, keepdims=True))
        bg = bg_ref[...]
        dhb = dhb_ref[...]
        dob_ref[...] = dhb * _silu(bg)
        dbg_ref[...] = (dhb * ob_ref[...] * _silu_grad(bg)).astype(BF16)
        keep_prev = (i > 0).astype(F32)
        keep_next = (i < nrow - 1).astype(F32)
        cc, cx, cb, cg = cc_ref[...], cx_ref[...], cb_ref[...], cg_ref[...]
        hcur = cc * cx
        p1 = ccp_ref[7:8, :] * cxp_ref[7:8, :] * keep_prev
        p2 = ccp_ref[6:7, :] * cxp_ref[6:7, :] * keep_prev
        s1, s2 = _shift_rows_down(hcur, p1, p2)
        w0, w1, w2 = cw_ref[0:1, :], cw_ref[1:2, :], cw_ref[2:3, :]
        y = w0 * s2 + w1 * s1 + w2 * hcur
        dhc = dhc_ref[...]
        scg = _silu(cg)
        doc = dhc * scg
        dcg_ref[...] = (dhc * cb * y * _silu_grad(cg)).astype(BF16)
        dcb_ref[...] = (doc * y).astype(BF16)
        dy = doc * cb
        n1 = dhcn_ref[0:1, :] * _silu(cgn_ref[0:1, :]) * cbn_ref[0:1, :] * keep_next
        n2 = dhcn_ref[1:2, :] * _silu(cgn_ref[1:2, :]) * cbn_ref[1:2, :] * keep_next
        u1, u2 = _shift_rows_up(dy, n1, n2)
        dh = w2 * dy + w1 * u1 + w0 * u2
        dcc_ref[...] = (dh * cx).astype(BF16)
        dcx_ref[...] = (dh * cc).astype(BF16)
        dcw_ref[0:1, :] += jnp.sum(dy * s2, axis=0, keepdims=True)
        dcw_ref[1:2, :] += jnp.sum(dy * s1, axis=0, keepdims=True)
        dcw_ref[2:3, :] += jnp.sum(dy * hcur, axis=0, keepdims=True)

    f32o, bf = _sds((T, HGRN_WIDTH), F32), _sds((T, HGRN_WIDTH), BF16)
    return pl.pallas_call(
        body,
        name="mix_bwd",
        grid=(HGRN_WIDTH // MIX_COLS, nrow),
        in_specs=[act, ucol(OFF_AG), act, ucol(OFF_BG), ucol(OFF_CB), ucol(OFF_CC), ucol(OFF_CX), ucol(OFF_CG),
                  uprev(OFF_CC), uprev(OFF_CX), unext(OFF_CB), unext(OFF_CG), act_next, par(1), par(CONV_K),
                  act, act, act],
        out_specs=[act, act, act, act, act, act, act, act, par(1), par(8)],
        out_shape=[f32o, f32o, bf, bf, bf, bf, bf, bf, _sds((1, HGRN_WIDTH), F32), _sds((8, HGRN_WIDTH), F32)],
        compiler_params=_cp(("parallel", "arbitrary")),
    )(o_raw, u, o_b, u, u, u, u, u, u, u, u, u, dhc, gn_l, cw_l, dha, dhb, dhc)


def _merge_specs(T, order):
    tr = _tile(T, 256)

    def ucol(off):
        if order == "ij":
            return pl.BlockSpec((tr, MIX_COLS), lambda i, j, off=off: (i, off // MIX_COLS + j))
        return pl.BlockSpec((tr, MIX_COLS), lambda j, i, off=off: (i, off // MIX_COLS + j))

    act = pl.BlockSpec((tr, MIX_COLS), (lambda i, j: (i, j)) if order == "ij" else (lambda j, i: (i, j)))
    return tr, ucol, act


def _merge_fwd(u, ya, yb, yc):
    T = u.shape[0]
    tr, ucol, act = _merge_specs(T, "ij")

    def body(ma_ref, mb_ref, mc_ref, ya_ref, yb_ref, yc_ref, o_ref, ot_ref):
        merged = (jax.nn.sigmoid(ma_ref[...]) * ya_ref[...] + jax.nn.sigmoid(mb_ref[...]) * yb_ref[...]
                  + jax.nn.sigmoid(mc_ref[...]) * yc_ref[...])
        o_ref[...] = merged.astype(BF16)
        ot_ref[...] = merged.T.astype(BF16)

    return pl.pallas_call(
        body,
        name="merge_fwd",
        grid=(T // tr, D_MODEL // MIX_COLS),
        in_specs=[ucol(OFF_MA), ucol(OFF_MB), ucol(OFF_MC), act, act, act],
        out_specs=[act, pl.BlockSpec((MIX_COLS, tr), lambda i, j: (j, i))],
        out_shape=[_sds((T, D_MODEL), BF16), _sds((D_MODEL, T), BF16)],
        compiler_params=_cp(("parallel", "parallel")),
    )(u, u, u, ya, yb, yc)


def _merge_bwd(u, ya, yb, yc, dmerged):
    T = u.shape[0]
    tr, ucol, act = _merge_specs(T, "ij")

    def body(ma_ref, mb_ref, mc_ref, ya_ref, yb_ref, yc_ref, dm_ref, dya_ref, dyb_ref, dyc_ref, dma_ref, dmb_ref, dmc_ref):
        dm = dm_ref[...]
        for m_ref, y_ref, dy_ref, dg_ref in ((ma_ref, ya_ref, dya_ref, dma_ref), (mb_ref, yb_ref, dyb_ref, dmb_ref),
                                             (mc_ref, yc_ref, dyc_ref, dmc_ref)):
            s = jax.nn.sigmoid(m_ref[...])
            dy_ref[...] = (dm * s).astype(BF16)
            dg_ref[...] = (dm * y_ref[...] * s * (1.0 - s)).astype(BF16)

    out = _sds((T, D_MODEL), BF16)
    return pl.pallas_call(
        body,
        name="merge_bwd",
        grid=(T // tr, D_MODEL // MIX_COLS),
        in_specs=[ucol(OFF_MA), ucol(OFF_MB), ucol(OFF_MC), act, act, act, act],
        out_specs=[act] * 6,
        out_shape=[out] * 6,
        compiler_params=_cp(("parallel", "parallel")),
    )(u, u, u, ya, yb, yc, dmerged)


def _ln_fwd(x, y, g_l, b_l):
    T = x.shape[0]
    tr = _tile(T, 256)
    row = pl.BlockSpec((tr, D_MODEL), lambda i: (i, 0))
    col = pl.BlockSpec((D_MODEL, tr), lambda i: (0, i))
    par = pl.BlockSpec((1, D_MODEL), lambda i: (0, 0))

    def body(x_ref, y_ref, g_ref, b_ref, o_ref, z_ref, ob_ref, ot_ref):
        z = ALPHA * x_ref[...] + y_ref[...]
        z_ref[...] = z
        mu = jnp.mean(z, axis=-1, keepdims=True)
        zc = z - mu
        var = jnp.mean(zc * zc, axis=-1, keepdims=True)
        o = zc * lax.rsqrt(var + LN_EPS) * g_ref[...] + b_ref[...]
        o_ref[...] = o
        ob_ref[...] = o.astype(BF16)
        ot_ref[...] = o.T.astype(BF16)

    return pl.pallas_call(
        body,
        name="ln_fwd",
        grid=(T // tr,),
        in_specs=[row, row, par, par],
        out_specs=[row, row, row, col],
        out_shape=[_sds((T, D_MODEL), F32)] * 2 + [_sds((T, D_MODEL), BF16), _sds((D_MODEL, T), BF16)],
        compiler_params=_cp(("parallel",)),
    )(x, y, g_l, b_l)


def _operand_forms(x):
    T = x.shape[0]
    tr = _tile(T, 256)
    row = pl.BlockSpec((tr, D_MODEL), lambda i: (i, 0))
    col = pl.BlockSpec((D_MODEL, tr), lambda i: (0, i))

    def body(x_ref, xb_ref, xt_ref):
        xv = x_ref[...]
        xb_ref[...] = xv.astype(BF16)
        xt_ref[...] = xv.T.astype(BF16)

    return pl.pallas_call(
        body,
        name="operand_forms",
        grid=(T // tr,),
        in_specs=[row],
        out_specs=[row, col],
        out_shape=[_sds((T, D_MODEL), BF16), _sds((D_MODEL, T), BF16)],
        compiler_params=_cp(("parallel",)),
    )(x)


def _ln_bwd(z, dxn, g_l):
    T = z.shape[0]
    tr = _tile(T, 256)
    row = pl.BlockSpec((tr, D_MODEL), lambda i: (i, 0))
    par = pl.BlockSpec((1, D_MODEL), lambda i: (0, 0))

    def body(z_ref, d_ref, g_ref, dz_ref, dzb_ref, dg_ref, db_ref):
        @pl.when(pl.program_id(0) == 0)
        def _():
            dg_ref[...] = jnp.zeros_like(dg_ref)
            db_ref[...] = jnp.zeros_like(db_ref)

        z = z_ref[...]
        d = d_ref[...]
        mu = jnp.mean(z, axis=-1, keepdims=True)
        zc = z - mu
        rstd = lax.rsqrt(jnp.mean(zc * zc, axis=-1, keepdims=True) + LN_EPS)
        zh = zc * rstd
        dg_ref[...] += jnp.sum(d * zh, axis=0, keepdims=True)
        db_ref[...] += jnp.sum(d, axis=0, keepdims=True)
        dh = d * g_ref[...]
        dz = rstd * (dh - jnp.mean(dh, axis=-1, keepdims=True) - zh * jnp.mean(dh * zh, axis=-1, keepdims=True))
        dz_ref[...] = dz
        dzb_ref[...] = dz.astype(BF16)

    return pl.pallas_call(
        body,
        name="ln_bwd",
        grid=(T // tr,),
        in_specs=[row, row, par],
        out_specs=[row, row, par, par],
        out_shape=[_sds((T, D_MODEL), F32), _sds((T, D_MODEL), BF16), _sds((1, D_MODEL), F32), _sds((1, D_MODEL), F32)],
        compiler_params=_cp(("arbitrary",)),
    )(z, dxn, g_l)


def _loss_head(y, target):
    T = y.shape[0]
    tr = _tile(T, 256)
    row = pl.BlockSpec((tr, D_MODEL), lambda i: (i, 0))
    acc = pl.BlockSpec((8, 128), lambda i: (0, 0))

    def body(y_ref, t_ref, l_ref, d_ref):
        @pl.when(pl.program_id(0) == 0)
        def _():
            l_ref[...] = jnp.zeros_like(l_ref)

        err = y_ref[...] - t_ref[...]
        d_ref[...] = err * (1.0 / D_MODEL)
        part = 0.5 * jnp.sum(jnp.sum(err * err, axis=-1, keepdims=True) * (1.0 / D_MODEL), axis=0, keepdims=True)
        r = lax.broadcasted_iota(jnp.int32, (8, 128), 0)
        c = lax.broadcasted_iota(jnp.int32, (8, 128), 1)
        l_ref[...] += jnp.where((r == 0) & (c == 0), part, 0.0)

    return pl.pallas_call(
        body,
        name="loss_head",
        grid=(T // tr,),
        in_specs=[row, row],
        out_specs=[acc, row],
        out_shape=[_sds((8, 128), F32), _sds((T, D_MODEL), F32)],
        compiler_params=_cp(("arbitrary",)),
    )(y, target)


ADAMW_BLOCK_ELEMS = 256 * 1024


def _adamw(w, g, m, v, name):
    shape = w.shape
    cols = shape[-1]
    rows = math.prod(shape[:-1])
    flat = lambda a: a.reshape(rows, cols)
    if rows * cols <= ADAMW_BLOCK_ELEMS or rows % 8:
        tr = rows
    else:
        tr = 8
        while rows % (tr * 2) == 0 and tr * 2 * cols <= ADAMW_BLOCK_ELEMS:
            tr *= 2
    blk = pl.BlockSpec((tr, cols), lambda i: (i, 0))
    c1 = 1.0 - ADAM_B1 ** ADAM_STEP
    c2 = 1.0 - ADAM_B2 ** ADAM_STEP

    def body(w_ref, g_ref, m_ref, v_ref, d_ref, nm_ref, nv_ref):
        gg = g_ref[...]
        nm = ADAM_B1 * m_ref[...] + (1.0 - ADAM_B1) * gg
        nv = ADAM_B2 * v_ref[...] + (1.0 - ADAM_B2) * (gg * gg)
        nm_ref[...] = nm
        nv_ref[...] = nv
        d_ref[...] = -ADAM_LR * ((nm / c1) / (jnp.sqrt(nv / c2) + ADAM_EPS) + ADAM_WD * w_ref[...])

    outs = pl.pallas_call(
        body,
        name=name,
        grid=(rows // tr,),
        in_specs=[blk] * 4,
        out_specs=[blk] * 3,
        out_shape=[_sds((rows, cols), F32)] * 3,
        compiler_params=_cp(("parallel",)),
    )(flat(w), flat(g), flat(m), flat(v))
    return tuple(o.reshape(shape) for o in outs)


def _t5_bucket(dist):
    max_exact = N_BUCKETS // 2
    logd = jnp.log(jnp.maximum(dist, 1).astype(F32) / max_exact) / math.log(MAX_DISTANCE / max_exact)
    large = jnp.minimum(max_exact + (logd * (N_BUCKETS - max_exact)).astype(jnp.int32), N_BUCKETS - 1)
    return jnp.where(dist < max_exact, dist, large)


def _band_bias(rel_bias):
    i = jnp.arange(WINDOW)[:, None]
    j = jnp.arange(2 * WINDOW)[None, :]
    bucket = _t5_bucket(jnp.clip(WINDOW + i - j, 0, WINDOW - 1))
    onehot = (bucket[:, :, None] == jnp.arange(N_BUCKETS)[None, None, :]).astype(F32)
    return jnp.einsum("ijb,bh->hij", onehot, rel_bias.astype(F32), precision=lax.Precision.HIGHEST)


def _lower_bounds(lb_param):
    soft = jax.nn.softmax(lb_param.astype(F32), axis=0)
    return jnp.cumsum(soft, axis=0) - soft[0:1]


def _mm_rows(T):
    return _tile(T, 1024)


def _layer_fwd(xs, w_in_l, w_proj_l, w_out_l, lower_l, bias4, sink4_l, gn_l, cw_l, lng_l, lnb_l):
    x, xb, xt = xs
    T = x.shape[0]
    tm = _mm_rows(T)
    u = _matmul(xb, w_in_l, mode="nn", tm=tm, tn=768, tk=D_MODEL, name="mm_u")
    o_raw, states = _hgrn_fwd(u, lower_l)
    q4 = _q_to_blocks(u[:, OFF_BQ:OFF_BQ + ATTN_WIDTH])
    k4 = _kv_to_heads(u[:, OFF_BK:OFF_BK + KV_WIDTH])
    v4 = _kv_to_heads(u[:, OFF_BV:OFF_BV + KV_WIDTH])
    o_b = _q_from_blocks(_attn_fwd(q4, k4, v4, bias4, sink4_l))
    ha, hb, hc, hat, hbt, hct = _mix_fwd(u, o_raw, o_b, gn_l, cw_l)
    ys = [_matmul(h, w_proj_l, mode="nn", tm=tm, tn=1024, tk=HGRN_WIDTH, b_idx=i, name="mm_proj")
          for i, h in enumerate((ha, hb, hc))]
    merged, merged_t = _merge_fwd(u, *ys)
    y = _matmul(merged, w_out_l, mode="nn", tm=tm, tn=1024, tk=D_MODEL, name="mm_out")
    xn, z, xnb, xnt = _ln_fwd(x, y, lng_l, lnb_l)
    saved = dict(xt=xt, u=u, o_raw=o_raw, states=states, q4=q4, k4=k4, v4=v4, o_b=o_b, hts=(hat, hbt, hct), ys=ys,
                 merged_t=merged_t, z=z)
    return (xn, xnb, xnt), saved


def _layer_bwd(dxn, s, w_in_l, w_proj_l, w_out_l, lower_l, bias4, sink4_l, gn_l, cw_l, lng_l):
    T = dxn.shape[0]
    tm = _mm_rows(T)
    u = s["u"]
    dz, dzb, d_lng, d_lnb = _ln_bwd(s["z"], dxn, lng_l)
    dmerged = _matmul(dzb, w_out_l, mode="nt", tm=tm, tn=1024, tk=D_MODEL, name="mm_dmerged")
    g_w_out = _matmul(s["merged_t"], dzb, mode="nn", tm=1024, tn=1024, tk=2048, name="mm_gw_out")
    *dys, dma, dmb, dmc = _merge_bwd(u, *s["ys"], dmerged)
    dhs = [_matmul(dy, w_proj_l, mode="nt", tm=tm, tn=1024, tk=D_MODEL, b_idx=i, name="mm_dh") for i, dy in enumerate(dys)]
    g_w_proj = jnp.stack([_matmul(ht, dy, mode="nn", tm=1024, tn=1024, tk=2048, name="mm_gw_proj")
                          for ht, dy in zip(s["hts"], dys)])
    do_raw, do_b, dag, dbg, dcb, dcc, dcx, dcg, d_gn, d_cw = _mix_bwd(u, s["o_raw"], s["o_b"], gn_l, cw_l, *dhs)
    daq, daf, dai, d_lower = _hgrn_bwd(u, lower_l, s["states"], do_raw)
    dq4, dkc, dkp, dvc, dvp, d_bias4, d_sink4 = _attn_bwd(s["q4"], s["k4"], s["v4"], bias4, sink4_l, _q_to_blocks(do_b))
    dbq = _q_from_blocks(dq4).astype(BF16)
    dbk = _kv_from_heads(_shift_prev_contrib(dkc, dkp)).astype(BF16)
    dbv = _kv_from_heads(_shift_prev_contrib(dvc, dvp)).astype(BF16)
    du = jnp.concatenate([daq, daf, dai, dag, dbq, dbk, dbv, dbg, dcb, dcc, dcx, dcg, dma, dmb, dmc], axis=1)
    dx = _matmul(du, w_in_l, mode="nt", tm=tm, tn=1024, tk=1536, name="mm_dx", add=dz, add_scale=ALPHA)
    g_w_in = _matmul(s["xt"], du, mode="nn", tm=1024, tn=768, tk=2048, name="mm_gw_in")
    d_sinks = jnp.sum(d_sink4.reshape(ATTN_HEADS, WINDOW), axis=-1)
    small = dict(lower=d_lower[0], gn=d_gn[0], sinks=d_sinks, cw=d_cw[:CONV_K], bias=d_bias4.reshape(ATTN_HEADS, WINDOW, 2 * WINDOW),
                 lng=d_lng[0], lnb=d_lnb[0])
    return dx, g_w_in, g_w_proj, g_w_out, small


def _local_step(x, target, w_in_full, w_proj_full, w_out_full, lb_param, hgrn_norm_g, attn_sinks, conv_w_full, rel_bias, ln_g, ln_b):
    lower, lower_vjp = jax.vjp(_lower_bounds, lb_param)
    bias, bias_vjp = jax.vjp(_band_bias, rel_bias)
    bias4 = bias.reshape(ATTN_KV_HEADS, QROWS, 2 * WINDOW)
    sink4 = jnp.broadcast_to(attn_sinks.reshape(DEPTH, ATTN_HEADS, 1, 1), (DEPTH, ATTN_HEADS, WINDOW, 1)).reshape(
        DEPTH, ATTN_KV_HEADS, QROWS, 1)
    row = lambda a, l: a[l:l + 1]
    saved = []
    hs = (x, *_operand_forms(x))
    for l in range(DEPTH):
        hs, s = _layer_fwd(hs, w_in_full[l], w_proj_full[l], w_out_full[l], row(lower, l), bias4, sink4[l],
                           row(hgrn_norm_g, l), conv_w_full[l], row(ln_g, l), row(ln_b, l))
        saved.append(s)
    loss_blk, dh = _loss_head(hs[0], target)
    g_in, g_proj, g_out, smalls = [None] * DEPTH, [None] * DEPTH, [None] * DEPTH, [None] * DEPTH
    for l in reversed(range(DEPTH)):
        dh, g_in[l], g_proj[l], g_out[l], smalls[l] = _layer_bwd(
            dh, saved[l], w_in_full[l], w_proj_full[l], w_out_full[l], row(lower, l), bias4, sink4[l],
            row(hgrn_norm_g, l), conv_w_full[l], row(ln_g, l))
    stack = lambda k: jnp.stack([sm[k] for sm in smalls])
    d_bias = smalls[0]["bias"] + smalls[1]["bias"] + smalls[2]["bias"] + smalls[3]["bias"]
    small = dict(
        lb_param=lower_vjp(stack("lower"))[0], hgrn_norm_g=stack("gn"), attn_sinks=stack("sinks"), conv_w=stack("cw"),
        rel_bias=bias_vjp(d_bias)[0], ln_g=stack("lng"), ln_b=stack("lnb"))
    return loss_blk, dh, g_in, g_proj, g_out, small


ANY = pl.BlockSpec(memory_space=pl.ANY)
DMA_SEM = pltpu.SemaphoreType.DMA


def _coords():
    return lax.axis_index("x"), lax.axis_index("y"), lax.axis_index("c")


def _other_chips(x, y):
    return [(1 - x, y), (x, 1 - y), (1 - x, 1 - y)]


def _remote(src, dst, send_sem, recv_sem, device):
    return pltpu.make_async_remote_copy(src_ref=src, dst_ref=dst, send_sem=send_sem, recv_sem=recv_sem,
                                        device_id=device, device_id_type=MESH)


def _sub(ref, axis, index, size):
    idx = [slice(None)] * len(ref.shape)
    idx[axis] = pl.ds(pl.multiple_of(index * size, size), size)
    return ref.at[tuple(idx)]


def _gather_shards(shards, layer, sax, name):
    shp = shards.shape[1:]
    hax = 3 - sax
    w, hw = shp[sax], shp[hax] // 2
    out_shape = list(shp)
    out_shape[sax] = w * N_CHIPS

    def body(src_ref, out_ref, s_ici, r_ici, s_d2d, r_d2d, loc_sem):
        x, y, c = _coords()
        me, sib, j = (x, y, c), (x, y, 1 - c), 2 * x + y
        src = src_ref.at[layer]
        region = lambda slab, half: _sub(_sub(out_ref, sax, slab, w), hax, half, hw)
        loc = pltpu.make_async_copy(src, _sub(out_ref, sax, j, w), loc_sem)
        loc.start()
        chips = _other_chips(x, y)
        first = [_remote(_sub(src, hax, c, hw), region(j, c), s_ici.at[k], r_ici.at[k], (px, py, c))
                 for k, (px, py) in enumerate(chips)]
        for cp in first:
            cp.start()
        passed = []
        for k, (px, py) in enumerate(chips):
            reg = region(2 * px + py, c)
            _remote(reg, reg, s_ici.at[k], r_ici.at[k], me).wait_recv()
            cp = _remote(reg, reg, s_d2d.at[k], r_d2d.at[k], sib)
            cp.start()
            passed.append(cp)
        for k, (px, py) in enumerate(chips):
            reg = region(2 * px + py, 1 - c)
            _remote(reg, reg, s_d2d.at[k], r_d2d.at[k], me).wait_recv()
        for cp in first + passed:
            cp.wait_send()
        loc.wait()

    return pl.pallas_call(
        body,
        name=name,
        in_specs=[ANY],
        out_specs=ANY,
        out_shape=_sds(tuple(out_shape), shards.dtype),
        scratch_shapes=[DMA_SEM((3,)), DMA_SEM((3,)), DMA_SEM((3,)), DMA_SEM((3,)), DMA_SEM(())],
    )(shards)


def _pair_exchange(g, hax, name):
    hw = g.shape[hax] // 2
    out_shape = list(g.shape)
    out_shape[hax] = hw

    def body(g_ref, recv_ref, send_sem, recv_sem):
        x, y, c = _coords()
        cp = _remote(_sub(g_ref, hax, 1 - c, hw), recv_ref, send_sem, recv_sem, (x, y, 1 - c))
        cp.start()
        cp.wait()

    return pl.pallas_call(
        body, name=name, in_specs=[ANY], out_specs=ANY, out_shape=_sds(tuple(out_shape), g.dtype),
        scratch_shapes=[DMA_SEM(()), DMA_SEM(())],
    )(g)


def _add_own_half(place, g, recv, hax, blk, name):
    L, ah, bh = recv.shape
    tr, tc = blk
    nr, nc = ah // tr, bh // tc

    def g_map(l, i, jc, p):
        return (l, i + p[0] * nr, jc) if hax == 1 else (l, i, jc + p[0] * nc)

    def body(p_ref, g_ref, r_ref, o_ref):
        o_ref[...] = (g_ref[...] + r_ref[...]).astype(BF16)

    same = pl.BlockSpec((None, tr, tc), lambda l, i, jc, p: (l, i, jc))
    return pl.pallas_call(
        body,
        name=name,
        grid_spec=pltpu.PrefetchScalarGridSpec(
            num_scalar_prefetch=1, grid=(L, nr, nc),
            in_specs=[pl.BlockSpec((None, tr, tc), g_map), same], out_specs=same),
        out_shape=_sds(recv.shape, BF16),
        compiler_params=_cp(("parallel", "parallel", "parallel")),
    )(place, g, recv)


def _slab_exchange(p, sax, name):
    w = p.shape[sax] // N_CHIPS
    slab_shape = list(p.shape)
    slab_shape[sax] = w

    def body(p_ref, recv_ref, send_sems, recv_sems):
        x, y, c = _coords()
        copies = [_remote(_sub(p_ref, sax, 2 * px + py, w), recv_ref.at[k], send_sems.at[k], recv_sems.at[k], (px, py, c))
                  for k, (px, py) in enumerate(_other_chips(x, y))]
        for cp in copies:
            cp.start()
        for cp in copies:
            cp.wait()

    return pl.pallas_call(
        body, name=name, in_specs=[ANY], out_specs=ANY, out_shape=_sds((3, *slab_shape), p.dtype),
        scratch_shapes=[DMA_SEM((3,)), DMA_SEM((3,))],
    )(p)


def _add_slabs(place, g, pair, recv, sax, blk, name):
    hax = 3 - sax
    _, L, a, b = recv.shape
    tr, tc = blk
    nr, nc = a // tr, b // tc

    def g_map(l, i, jc, p):
        return (l, p[0] * nr + i, p[1] * nc + jc) if hax == 1 else (l, p[1] * nr + i, p[0] * nc + jc)

    def pair_map(l, i, jc, p):
        return (l, i, p[1] * nc + jc) if hax == 1 else (l, p[1] * nr + i, jc)

    def out_map(l, i, jc, p):
        return (l, p[0] * nr + i, jc) if hax == 1 else (l, i, p[0] * nc + jc)

    def body(p_ref, g_ref, pair_ref, r0_ref, r1_ref, r2_ref, o_ref):
        own = g_ref[...] + pair_ref[...]
        o_ref[...] = ((own + r0_ref[...].astype(F32)) + r1_ref[...].astype(F32)) + r2_ref[...].astype(F32)

    def rk(k):
        return pl.BlockSpec((None, None, tr, tc), lambda l, i, jc, p, k=k: (k, l, i, jc))

    out_shape = [L, a, b]
    out_shape[hax] *= 2
    blk3 = (None, tr, tc)
    return pl.pallas_call(
        body,
        name=name,
        grid_spec=pltpu.PrefetchScalarGridSpec(
            num_scalar_prefetch=1, grid=(L, nr, nc),
            in_specs=[pl.BlockSpec(blk3, g_map), pl.BlockSpec(blk3, pair_map), rk(0), rk(1), rk(2)],
            out_specs=pl.BlockSpec(blk3, out_map)),
        out_shape=_sds(tuple(out_shape), F32),
        compiler_params=_cp(("parallel", "parallel", "parallel")),
    )(place, g, pair, recv, recv, recv)


def _pair_assemble(r, hax, name):
    hw = r.shape[hax] // 2

    def body(r_ref, out_ref, send_sem, recv_sem):
        del r_ref
        x, y, c = _coords()
        mine, other = _sub(out_ref, hax, c, hw), _sub(out_ref, hax, 1 - c, hw)
        cp = _remote(mine, mine, send_sem, recv_sem, (x, y, 1 - c))
        cp.start()
        _remote(other, other, send_sem, recv_sem, (x, y, c)).wait_recv()
        cp.wait_send()

    return pl.pallas_call(
        body, name=name, in_specs=[ANY], out_specs=ANY, out_shape=_sds(r.shape, r.dtype),
        input_output_aliases={0: 0}, scratch_shapes=[DMA_SEM(()), DMA_SEM(())],
    )(r)


def _reduce_scatter(place, g, sax, blk_pair, blk_slab, tag):
    hax = 3 - sax
    pair = _pair_exchange(g, hax, "rs_pair_" + tag)
    p = _add_own_half(place, g, pair, hax, blk_pair, "rs_pair_add_" + tag)
    slabs = _slab_exchange(p, sax, "rs_slab_" + tag)
    r = _add_slabs(place, g, pair, slabs, sax, blk_slab, "rs_slab_add_" + tag)
    return _pair_assemble(r, hax, "rs_assemble_" + tag)


N_DEV = 8


def _all_reduce_small(v, name):
    rows = v.shape[0]

    def body(v_ref, gath_ref, sum_ref, send_sems, recv_sems, local_sem):
        x, y, c = _coords()
        me, sib = (x, y, c), (x, y, 1 - c)
        chips = _other_chips(x, y)

        def slot(px, py, pc):
            return gath_ref.at[pl.ds(pl.multiple_of((4 * px + 2 * py + pc) * rows, rows), rows), :]

        def copy(k, block, to, src=None):
            return _remote(slot(*block) if src is None else src, slot(*block), send_sems.at[k], recv_sems.at[k], to)

        mine = pltpu.make_async_copy(v_ref, slot(*me), local_sem)
        mine.start()
        first = [copy(0, me, sib, src=v_ref)] + [copy(1 + k, me, (*chip, c), src=v_ref) for k, chip in enumerate(chips)]
        for cp in first:
            cp.start()
        passed = [copy(4 + k, (*chip, c), sib) for k, chip in enumerate(chips)]
        for k, chip in enumerate(chips):
            copy(1 + k, (*chip, c), me).wait_recv()
            passed[k].start()
        copy(0, sib, me).wait_recv()
        for k, chip in enumerate(chips):
            copy(4 + k, (*chip, 1 - c), me).wait_recv()
        for cp in first + passed:
            cp.wait_send()
        mine.wait()
        acc = gath_ref[0:rows, :]
        for d in range(1, N_DEV):
            acc = acc + gath_ref[d * rows:(d + 1) * rows, :]
        sum_ref[...] = acc

    vm = pl.BlockSpec(memory_space=pltpu.VMEM)
    return pl.pallas_call(
        body, name=name, in_specs=[vm], out_specs=[vm, vm],
        out_shape=[_sds((N_DEV * rows, 128), F32), _sds((rows, 128), F32)],
        scratch_shapes=[DMA_SEM((7,)), DMA_SEM((7,)), DMA_SEM(())],
    )(v)[1]


def _pad_rows(a):
    flat = a.reshape(-1).astype(F32)
    rows = -(-flat.shape[0] // (8 * 128)) * 8
    return jnp.pad(flat, (0, rows * 128 - flat.shape[0])).reshape(rows, 128)


def _sum_over_devices(parts, name):
    blocks = [_pad_rows(a) for a in parts.values()]
    total = _all_reduce_small(jnp.concatenate(blocks, axis=0), name)
    out, r0 = {}, 0
    for (key, a), blk in zip(parts.items(), blocks):
        out[key] = total[r0:r0 + blk.shape[0]].reshape(-1)[:a.size].reshape(a.shape)
        r0 += blk.shape[0]
    return out


def kernel(x, w_in, w_proj_hgrn, w_proj_attn, w_proj_conv, w_out, lb_param, hgrn_norm_g, attn_sinks, conv_w, rel_bias, ln_g, ln_b, loss_target, m_w_in, m_w_proj_hgrn, m_w_proj_attn, m_w_proj_conv, m_w_out, m_lb_param, m_hgrn_norm_g, m_attn_sinks, m_conv_w, m_rel_bias, m_ln_g, m_ln_b, v_w_in, v_w_proj_hgrn, v_w_proj_attn, v_w_proj_conv, v_w_out, v_lb_param, v_hgrn_norm_g, v_attn_sinks, v_conv_w, v_rel_bias, v_ln_g, v_ln_b):
    xi, yi, ci = _coords()
    slab = 2 * xi + yi
    place = jnp.stack([ci, slab]).astype(jnp.int32)
    conv_cols = conv_w.shape[-1]

    w_in_b = w_in.astype(BF16)[:, None]
    w_proj_b = jnp.stack([w_proj_hgrn, w_proj_attn, w_proj_conv], axis=1).astype(BF16)
    w_out_b = w_out.astype(BF16)[:, None]
    w_in_full = [_gather_shards(w_in_b, l, 2, f"gather_w_in_{l}")[0] for l in range(DEPTH)]
    w_proj_full = [_gather_shards(w_proj_b, l, 2, f"gather_w_proj_{l}") for l in range(DEPTH)]
    w_out_full = [_gather_shards(w_out_b, l, 1, f"gather_w_out_{l}")[0] for l in range(DEPTH)]
    conv_spread = lax.dynamic_update_slice(jnp.zeros((DEPTH, CONV_K, CONV_WIDTH), F32), conv_w, (0, 0, slab * conv_cols))
    conv_full = 0.5 * _sum_over_devices({"conv_w": conv_spread}, "gather_conv_w")["conv_w"]

    loss_blk, dx, g_in, g_proj, g_out, small = _local_step(
        x[0], loss_target[0], w_in_full, w_proj_full, w_out_full, lb_param, hgrn_norm_g, attn_sinks, conv_full,
        rel_bias, ln_g, ln_b)

    gs_in, gs_proj, gs_out = [], [], []
    for l in range(DEPTH):
        gs_in.append(_reduce_scatter(place, g_in[l][None], 2, (128, 4224), (128, 4224), f"w_in_{l}")[0])
        gs_proj.append(_reduce_scatter(place, g_proj[l], 2, (256, 2048), (512, 512), f"w_proj_{l}"))
        gs_out.append(_reduce_scatter(place, g_out[l][None], 1, (256, 1024), (256, 1024), f"w_out_{l}")[0])
    g_w_in, g_w_out = jnp.stack(gs_in), jnp.stack(gs_out)
    g_w_proj = jnp.stack(gs_proj)
    small = dict(small, loss=loss_blk[0:1, 0:1])
    small = _sum_over_devices(small, "sum_small")
    loss = small["loss"][0, 0]
    g_conv = lax.dynamic_slice(small["conv_w"], (0, 0, slab * conv_cols), (DEPTH, CONV_K, conv_cols))

    grads = [g_w_in, g_w_proj[:, 0], g_w_proj[:, 1], g_w_proj[:, 2], g_w_out, small["lb_param"], small["hgrn_norm_g"],
             small["attn_sinks"], g_conv, small["rel_bias"], small["ln_g"], small["ln_b"]]
    names = ["w_in", "w_proj_hgrn", "w_proj_attn", "w_proj_conv", "w_out", "lb_param", "hgrn_norm_g", "attn_sinks",
             "conv_w", "rel_bias", "ln_g", "ln_b"]
    ws = [w_in, w_proj_hgrn, w_proj_attn, w_proj_conv, w_out, lb_param, hgrn_norm_g, attn_sinks, conv_w, rel_bias, ln_g, ln_b]
    ms = [m_w_in, m_w_proj_hgrn, m_w_proj_attn, m_w_proj_conv, m_w_out, m_lb_param, m_hgrn_norm_g, m_attn_sinks, m_conv_w,
          m_rel_bias, m_ln_g, m_ln_b]
    vs = [v_w_in, v_w_proj_hgrn, v_w_proj_attn, v_w_proj_conv, v_w_out, v_lb_param, v_hgrn_norm_g, v_attn_sinks, v_conv_w,
          v_rel_bias, v_ln_g, v_ln_b]
    upd = [_adamw(w, g, m, v, "adamw_" + n) for n, w, g, m, v in zip(names, ws, grads, ms, vs)]
    deltas, new_ms, new_vs = zip(*upd)
    return (loss, dx[None], *grads, *deltas, *new_ms, *new_vs)
```

```python
import functools
import math

import jax
import jax.numpy as jnp
from jax import lax
from jax.experimental import pallas as pl
from jax.experimental.pallas import tpu as pltpu

F32 = jnp.float32
BF16 = jnp.bfloat16
MXU_DTYPE = BF16

D_MODEL = 2048
DEPTH = 4
HGRN_WIDTH = 1024
HGRN_HEAD_DIM = 128
HGRN_HEADS = 8
HGRN_CHUNK = 64
ATTN_HEAD_DIM = 64
ATTN_HEADS = 16
ATTN_KV_HEADS = 4
ATTN_GROUP = ATTN_HEADS // ATTN_KV_HEADS
ATTN_WIDTH = 1024
KV_WIDTH = 256
WINDOW = 128
CONV_WIDTH = 1024
CONV_K = 3
N_BUCKETS = 32
MAX_DISTANCE = 128
ALPHA = (2.0 * DEPTH) ** 0.25
LN_EPS = 1e-5
RMS_EPS = 1e-6
N_IN = 16896
OFF_AQ, OFF_AF, OFF_AI, OFF_AG = 0, 1024, 2048, 3072
OFF_BQ, OFF_BK, OFF_BV, OFF_BG = 4096, 5120, 5376, 5632
OFF_CB, OFF_CC, OFF_CX, OFF_CG = 6656, 7680, 8704, 9728
OFF_MA, OFF_MB, OFF_MC = 10752, 12800, 14848

ADAM_LR = 0.001
ADAM_B1 = 0.9
ADAM_B2 = 0.999
ADAM_EPS = 1e-08
ADAM_WD = 0.01
ADAM_STEP = 10

N_CHIPS = 4
VMEM_LIMIT_BYTES = 48 * 1024 * 1024
EXP_CLAMP = 80.0
MASK_VALUE = -1e30
MESH = pl.DeviceIdType.MESH


def _cp(sem=None):
    return pltpu.CompilerParams(dimension_semantics=sem, vmem_limit_bytes=VMEM_LIMIT_BYTES)


def _tile(dim, pref):
    return pref if dim % pref == 0 else dim


def _sds(shape, dtype):
    return jax.ShapeDtypeStruct(shape, dtype)


def _call(body, *, name, grid, in_specs, out_specs, out_shape, args, scratch_shapes=(), sem=None, plan=None):
    in_specs, out_specs, out_shape = list(in_specs), list(out_specs), list(out_shape)
    if plan is None:
        outs = pl.pallas_call(body, name=name, grid=grid, in_specs=in_specs, out_specs=out_specs, out_shape=out_shape,
                              scratch_shapes=list(scratch_shapes), compiler_params=_cp(sem))(*args)
        return list(outs), []
    ni, no, ns = len(in_specs), len(out_specs), len(scratch_shapes)
    pi, po = len(plan.inputs), len(plan.out_shapes)

    def carrier(*refs):
        c_in, p_in = refs[:ni], refs[ni:ni + pi]
        c_out, p_out = refs[ni + pi:ni + pi + no], refs[ni + pi + no:ni + pi + no + po]
        c_scr = refs[ni + pi + no + po:ni + pi + no + po + ns]
        send_sems, recv_sems = refs[-2], refs[-1]
        ids = [pl.program_id(a) for a in range(len(grid))]
        first = functools.reduce(jnp.logical_and, [i == 0 for i in ids])
        last = functools.reduce(jnp.logical_and, [i == n - 1 for i, n in zip(ids, grid)])

        @pl.when(first)
        def _():
            for cp in plan.make(p_in, p_out, send_sems, recv_sems, 0):
                cp.start()

        body(*c_in, *c_out, *c_scr)

        @pl.when(last)
        def _():
            for cp in plan.make(p_in, p_out, send_sems, recv_sems, 0):
                cp.wait()

    any_spec = pl.BlockSpec(memory_space=pl.ANY)
    outs = pl.pallas_call(
        carrier, name=name, grid=grid,
        in_specs=in_specs + [any_spec] * pi, out_specs=out_specs + [any_spec] * po,
        out_shape=out_shape + list(plan.out_shapes),
        scratch_shapes=list(scratch_shapes) + [pltpu.SemaphoreType.DMA((plan.n,)), pltpu.SemaphoreType.DMA((plan.n,))],
        input_output_aliases={ni + k: no + v for k, v in plan.aliases.items()},
        compiler_params=_cp(tuple("arbitrary" for _ in grid)),
    )(*args, *plan.inputs)
    return list(outs[:no]), list(outs[no:])


_DIMS = {
    "nn": (((1,), (0,)), ((), ())),
    "nt": (((1,), (1,)), ((), ())),
    "tn": (((0,), (0,)), ((), ())),
}


def _dot_raw(a, b, mode):
    return lax.dot_general(a.astype(MXU_DTYPE), b.astype(MXU_DTYPE), _DIMS[mode], preferred_element_type=F32)


@functools.partial(jax.custom_vjp, nondiff_argnums=(2,))
def _dot(a, b, mode):
    return _dot_raw(a, b, mode)


def _dot_fwd(a, b, mode):
    return _dot_raw(a, b, mode), (a, b)


def _dot_bwd(mode, res, g):
    a, b = res
    if mode == "nn":
        return _dot_raw(g, b, "nt"), _dot_raw(a, g, "tn")
    if mode == "nt":
        return _dot_raw(g, b, "nn"), _dot_raw(g, a, "tn")
    return _dot_raw(b, g, "nt"), _dot_raw(a, g, "nn")


_dot.defvjp(_dot_fwd, _dot_bwd)


def _matmul(a, b, *, mode, tm, tn, tk, name, a_idx=None, b_idx=None, out_dtype=F32, add=None, add_scale=1.0, plan=None):
    a2, b2 = a.shape[-2:], b.shape[-2:]
    if mode == "nn":
        (M, K), (K2, N) = a2, b2
    elif mode == "nt":
        (M, K), (N, K2) = a2, b2
    else:
        (K, M), (K2, N) = a2, b2
    assert K == K2, (a.shape, b.shape, mode)
    tm, tn, tk = _tile(M, tm), _tile(N, tn), _tile(K, tk)
    nk = K // tk

    a_blk = (tk, tm) if mode == "tn" else (tm, tk)
    b_blk = (tn, tk) if mode == "nt" else (tk, tn)

    def a_map(i, j, k):
        ij = (k, i) if mode == "tn" else (i, k)
        return ij if a_idx is None else (a_idx,) + ij

    def b_map(i, j, k):
        ij = (j, k) if mode == "nt" else (k, j)
        return ij if b_idx is None else (b_idx,) + ij

    in_specs = [
        pl.BlockSpec(a_blk if a_idx is None else (None,) + a_blk, a_map),
        pl.BlockSpec(b_blk if b_idx is None else (None,) + b_blk, b_map),
    ]
    args = [a, b]
    if add is not None:
        in_specs.append(pl.BlockSpec((tm, tn), lambda i, j, k: (i, j)))
        args.append(add)
    n_in = len(args)

    def body(*refs):
        a_ref, b_ref = refs[0], refs[1]
        o_ref = refs[n_in]
        p = _dot_raw(a_ref[...], b_ref[...], mode)

        def finish(val):
            if add is not None:
                val = val + add_scale * refs[2][...]
            o_ref[...] = val.astype(out_dtype)

        if nk == 1:
            finish(p)
        else:
            acc_ref = refs[n_in + 1]
            k = pl.program_id(2)

            @pl.when(k == 0)
            def _():
                acc_ref[...] = p

            @pl.when(k > 0)
            def _():
                acc_ref[...] += p

            @pl.when(k == nk - 1)
            def _():
                finish(acc_ref[...])

    (out,), plan_outs = _call(
        body,
        name=name,
        grid=(M // tm, N // tn, nk),
        in_specs=in_specs,
        out_specs=[pl.BlockSpec((tm, tn), lambda i, j, k: (i, j))],
        out_shape=[_sds((M, N), out_dtype)],
        scratch_shapes=[pltpu.VMEM((tm, tn), F32)] if nk > 1 else [],
        sem=("parallel", "parallel", "arbitrary"),
        args=args,
        plan=plan,
    )
    return out if plan is None else (out, plan_outs)


def _scan_rows(x, reverse):
    n = x.shape[0]
    row = lax.broadcasted_iota(jnp.int32, x.shape, 0)
    s = 1
    while s < n:
        if reverse:
            x = x + jnp.where(row < n - s, pltpu.roll(x, n - s, 0), 0.0)
        else:
            x = x + jnp.where(row >= s, pltpu.roll(x, s, 0), 0.0)
        s *= 2
    return x


@jax.custom_vjp
def _cumsum_rows(x):
    return _scan_rows(x, False)


_cumsum_rows.defvjp(lambda x: (_scan_rows(x, False), None), lambda _, g: (_scan_rows(g, True),))


def _hgrn_chunk(state_t, qraw, fraw, v, lb):
    c = HGRN_CHUNK
    q = qraw * jax.nn.sigmoid(qraw) * (HGRN_HEAD_DIM ** -0.5)
    f = lb + (1.0 - lb) * jax.nn.sigmoid(fraw)
    k = 1.0 - f
    g = jnp.log(f)
    b = _cumsum_rows(g)
    row = lax.broadcasted_iota(jnp.int32, (c, HGRN_HEAD_DIM), 0)
    b_end = jnp.sum(g, axis=0, keepdims=True)
    b_mid = jnp.sum(jnp.where(row < c // 2, g, 0.0), axis=0, keepdims=True)
    inter = _dot(q * jnp.exp(b), state_t, "nt")
    qt = q * jnp.exp(jnp.minimum(b - b_mid, EXP_CLAMP))
    kt = k * jnp.exp(jnp.minimum(b_mid - b, EXP_CLAMP))
    s = _dot(qt, kt, "nt")
    ti = lax.broadcasted_iota(jnp.int32, (c, c), 0)
    si = lax.broadcasted_iota(jnp.int32, (c, c), 1)
    s = jnp.where(si <= ti, s, 0.0)
    intra = _dot(s, v, "nn")
    k_end = k * jnp.exp(b_end - b)
    new_state_t = state_t * jnp.exp(b_end) + _dot(v, k_end, "tn")
    return new_state_t, inter + intra


def _hgrn_specs(T):
    rows = _tile(T, 512)
    return rows, T // rows, rows // HGRN_CHUNK


def _hgrn_fwd(u, lower_l):
    T = u.shape[0]
    rows, nblk, ncr = _hgrn_specs(T)
    hb = HGRN_WIDTH // HGRN_HEAD_DIM

    def body(q_ref, f_ref, i_ref, lb_ref, o_ref, st_ref, state):
        @pl.when(pl.program_id(1) == 0)
        def _():
            state[...] = jnp.zeros_like(state)

        lb = lb_ref[...]
        for c in range(ncr):
            rs = pl.ds(c * HGRN_CHUNK, HGRN_CHUNK)
            st = state[...]
            st_ref[c] = st
            new, out = _hgrn_chunk(st, q_ref[rs, :], f_ref[rs, :], i_ref[rs, :], lb)
            state[...] = new
            o_ref[rs, :] = out

    blk = (rows, HGRN_HEAD_DIM)
    return pl.pallas_call(
        body,
        name="hgrn_fwd",
        grid=(HGRN_HEADS, nblk),
        in_specs=[
            pl.BlockSpec(blk, lambda h, r: (r, OFF_AQ // 128 + h)),
            pl.BlockSpec(blk, lambda h, r: (r, OFF_AF // 128 + h)),
            pl.BlockSpec(blk, lambda h, r: (r, OFF_AI // 128 + h)),
            pl.BlockSpec((1, HGRN_HEAD_DIM), lambda h, r: (0, h)),
        ],
        out_specs=[
            pl.BlockSpec(blk, lambda h, r: (r, h)),
            pl.BlockSpec((ncr, None, HGRN_HEAD_DIM, HGRN_HEAD_DIM), lambda h, r: (r, h, 0, 0)),
        ],
        out_shape=[_sds((T, HGRN_WIDTH), F32), _sds((T // HGRN_CHUNK, hb, HGRN_HEAD_DIM, HGRN_HEAD_DIM), F32)],
        scratch_shapes=[pltpu.VMEM((HGRN_HEAD_DIM, HGRN_HEAD_DIM), F32)],
        compiler_params=_cp(("parallel", "arbitrary")),
    )(u, u, u, lower_l)


def _hgrn_bwd(u, lower_l, states, do_raw):
    T = u.shape[0]
    rows, nblk, ncr = _hgrn_specs(T)

    def body(q_ref, f_ref, i_ref, lb_ref, st_ref, do_ref, dq_ref, df_ref, di_ref, dlb_ref, dstate):
        @pl.when(pl.program_id(1) == 0)
        def _():
            dstate[...] = jnp.zeros_like(dstate)
            dlb_ref[...] = jnp.zeros_like(dlb_ref)

        lb = lb_ref[...]
        for c in reversed(range(ncr)):
            rs = pl.ds(c * HGRN_CHUNK, HGRN_CHUNK)
            _, vjp = jax.vjp(_hgrn_chunk, st_ref[c], q_ref[rs, :], f_ref[rs, :], i_ref[rs, :], lb)
            dst, dq, df, dv, dlb = vjp((dstate[...], do_ref[rs, :]))
            dstate[...] = dst
            dq_ref[rs, :] = dq.astype(BF16)
            df_ref[rs, :] = df.astype(BF16)
            di_ref[rs, :] = dv.astype(BF16)
            dlb_ref[...] += dlb

    blk = (rows, HGRN_HEAD_DIM)
    last = nblk - 1
    out_blk = pl.BlockSpec(blk, lambda h, r: (last - r, h))
    return pl.pallas_call(
        body,
        name="hgrn_bwd",
        grid=(HGRN_HEADS, nblk),
        in_specs=[
            pl.BlockSpec(blk, lambda h, r: (last - r, OFF_AQ // 128 + h)),
            pl.BlockSpec(blk, lambda h, r: (last - r, OFF_AF // 128 + h)),
            pl.BlockSpec(blk, lambda h, r: (last - r, OFF_AI // 128 + h)),
            pl.BlockSpec((1, HGRN_HEAD_DIM), lambda h, r: (0, h)),
            pl.BlockSpec((ncr, None, HGRN_HEAD_DIM, HGRN_HEAD_DIM), lambda h, r: (last - r, h, 0, 0)),
            out_blk,
        ],
        out_specs=[out_blk, out_blk, out_blk, pl.BlockSpec((1, HGRN_HEAD_DIM), lambda h, r: (0, h))],
        out_shape=[_sds((T, HGRN_WIDTH), BF16)] * 3 + [_sds((1, HGRN_WIDTH), F32)],
        scratch_shapes=[pltpu.VMEM((HGRN_HEAD_DIM, HGRN_HEAD_DIM), F32)],
        compiler_params=_cp(("parallel", "arbitrary")),
    )(u, u, u, lower_l, states, do_raw)


QROWS = ATTN_GROUP * WINDOW


def _attn_block(q, kp, kc, vp, vc, bp, bc, sink, mp, mc):
    scale = ATTN_HEAD_DIM ** -0.5
    sp = jnp.where(mp, _dot(q, kp, "nt") * scale + bp, MASK_VALUE)
    sc = jnp.where(mc, _dot(q, kc, "nt") * scale + bc, MASK_VALUE)
    m = jnp.maximum(jnp.maximum(jnp.max(sp, axis=-1, keepdims=True), jnp.max(sc, axis=-1, keepdims=True)), sink)
    m = lax.stop_gradient(m)
    pp = jnp.exp(sp - m)
    pc = jnp.exp(sc - m)
    den = jnp.sum(pp, axis=-1, keepdims=True) + jnp.sum(pc, axis=-1, keepdims=True) + jnp.exp(sink - m)
    inv = 1.0 / den
    return _dot(pp * inv, vp, "nn") + _dot(pc * inv, vc, "nn")


def _attn_masks(first_block):
    i = lax.broadcasted_iota(jnp.int32, (QROWS, WINDOW), 0) % WINDOW
    j = lax.broadcasted_iota(jnp.int32, (QROWS, WINDOW), 1)
    return (j > i) & jnp.logical_not(first_block), j <= i


def _attn_in_specs():
    q_spec = pl.BlockSpec((None, None, QROWS, ATTN_HEAD_DIM), lambda h, n: (h, n, 0, 0))
    cur = pl.BlockSpec((None, WINDOW, ATTN_HEAD_DIM), lambda h, n: (h, n, 0))
    prev = pl.BlockSpec((None, WINDOW, ATTN_HEAD_DIM), lambda h, n: (h, jnp.maximum(n - 1, 0), 0))
    bias = pl.BlockSpec((None, QROWS, 2 * WINDOW), lambda h, n: (h, 0, 0))
    sink = pl.BlockSpec((None, QROWS, 1), lambda h, n: (h, 0, 0))
    return q_spec, cur, prev, bias, sink


def _attn_fwd(q4, k4, v4, bias4, sink4):
    kvh, nb = q4.shape[0], q4.shape[1]
    q_spec, cur, prev, bias, sink = _attn_in_specs()

    def body(q_ref, kp_ref, kc_ref, vp_ref, vc_ref, b_ref, s_ref, o_ref):
        mp, mc = _attn_masks(pl.program_id(1) == 0)
        o_ref[...] = _attn_block(q_ref[...], kp_ref[...], kc_ref[...], vp_ref[...], vc_ref[...],
                                 b_ref[:, :WINDOW], b_ref[:, WINDOW:], s_ref[...], mp, mc)

    return pl.pallas_call(
        body,
        name="attn_fwd",
        grid=(kvh, nb),
        in_specs=[q_spec, prev, cur, prev, cur, bias, sink],
        out_specs=q_spec,
        out_shape=_sds(q4.shape, F32),
        compiler_params=_cp(("parallel", "arbitrary")),
    )(q4, k4, k4, v4, v4, bias4, sink4)


def _attn_bwd(q4, k4, v4, bias4, sink4, do4, plan=None):
    kvh, nb = q4.shape[0], q4.shape[1]
    T = k4.shape[1]
    q_spec, cur, prev, bias, sink = _attn_in_specs()

    def body(q_ref, kp_ref, kc_ref, vp_ref, vc_ref, b_ref, s_ref, do_ref,
             dq_ref, dkc_ref, dkp_ref, dvc_ref, dvp_ref, db_ref, ds_ref):
        n = pl.program_id(1)
        mp, mc = _attn_masks(n == 0)
        _, vjp = jax.vjp(
            functools.partial(_attn_block, mp=mp, mc=mc),
            q_ref[...], kp_ref[...], kc_ref[...], vp_ref[...], vc_ref[...],
            b_ref[:, :WINDOW], b_ref[:, WINDOW:], s_ref[...])
        dq, dkp, dkc, dvp, dvc, dbp, dbc, dsink = vjp(do_ref[...])
        dq_ref[...] = dq
        dkc_ref[...] = dkc
        dkp_ref[...] = dkp
        dvc_ref[...] = dvc
        dvp_ref[...] = dvp

        @pl.when(n == 0)
        def _():
            db_ref[...] = jnp.zeros_like(db_ref)
            ds_ref[...] = jnp.zeros_like(ds_ref)

        db_ref[:, :WINDOW] += dbp
        db_ref[:, WINDOW:] += dbc
        ds_ref[...] += dsink

    kv_sds = _sds((kvh, T, ATTN_HEAD_DIM), F32)
    outs, plan_outs = _call(
        body,
        name="attn_bwd",
        grid=(kvh, nb),
        in_specs=[q_spec, prev, cur, prev, cur, bias, sink, q_spec],
        out_specs=[q_spec, cur, cur, cur, cur, bias, sink],
        out_shape=[_sds(q4.shape, F32), kv_sds, kv_sds, kv_sds, kv_sds,
                   _sds((kvh, QROWS, 2 * WINDOW), F32), _sds((kvh, QROWS, 1), F32)],
        sem=("parallel", "arbitrary"),
        args=(q4, k4, k4, v4, v4, bias4, sink4, do4),
        plan=plan,
    )
    return outs, plan_outs


def _q_to_blocks(a):
    T = a.shape[0]
    a = a.reshape(T // WINDOW, WINDOW, ATTN_KV_HEADS, ATTN_GROUP, ATTN_HEAD_DIM)
    return jnp.transpose(a, (2, 0, 3, 1, 4)).reshape(ATTN_KV_HEADS, T // WINDOW, QROWS, ATTN_HEAD_DIM)


def _q_from_blocks(a):
    kvh, nb = a.shape[0], a.shape[1]
    a = a.reshape(kvh, nb, ATTN_GROUP, WINDOW, ATTN_HEAD_DIM)
    return jnp.transpose(a, (1, 3, 0, 2, 4)).reshape(nb * WINDOW, ATTN_WIDTH)


def _kv_to_heads(a):
    T = a.shape[0]
    return jnp.transpose(a.reshape(T, ATTN_KV_HEADS, ATTN_HEAD_DIM), (1, 0, 2))


def _kv_from_heads(a):
    return jnp.transpose(a, (1, 0, 2)).reshape(a.shape[1], KV_WIDTH)


def _shift_prev_contrib(cur, nxt):
    pad = jnp.zeros_like(nxt[:, :WINDOW])
    return cur + jnp.concatenate([nxt[:, WINDOW:], pad], axis=1)


MIX_COLS = 512


def _silu(x):
    return x * jax.nn.sigmoid(x)


def _silu_grad(x):
    s = jax.nn.sigmoid(x)
    return s * (1.0 + x * (1.0 - s))


def _shift_rows_down(h, first, second):
    n = h.shape[0]
    row = lax.broadcasted_iota(jnp.int32, h.shape, 0)
    s1 = jnp.where(row == 0, first, pltpu.roll(h, 1, 0))
    s2 = jnp.where(row == 0, second, jnp.where(row == 1, first, pltpu.roll(h, 2, 0)))
    del n
    return s1, s2


def _shift_rows_up(h, first, second):
    n = h.shape[0]
    row = lax.broadcasted_iota(jnp.int32, h.shape, 0)
    s1 = jnp.where(row == n - 1, first, pltpu.roll(h, n - 1, 0))
    s2 = jnp.where(row == n - 1, second, jnp.where(row == n - 2, first, pltpu.roll(h, n - 2, 0)))
    return s1, s2


def _mix_rows(T):
    return _tile(T, 256)


def _mix_fwd(u, o_raw, o_b, gn_l, cw_l):
    T = u.shape[0]
    tr = _mix_rows(T)
    nrow = T // tr
    hr = tr // 8

    def ucol(off):
        return pl.BlockSpec((tr, MIX_COLS), lambda i, j, off=off: (i, off // MIX_COLS + j))

    def uprev(off):
        return pl.BlockSpec((8, MIX_COLS), lambda i, j, off=off: (jnp.maximum(i * hr - 1, 0), off // MIX_COLS + j))

    act = pl.BlockSpec((tr, MIX_COLS), lambda i, j: (i, j))
    par = lambda rows: pl.BlockSpec((rows, MIX_COLS), lambda i, j: (0, j))

    def body(oraw_ref, ag_ref, ob_ref, bg_ref, cb_ref, cc_ref, cx_ref, cg_ref, ccp_ref, cxp_ref, gn_ref, cw_ref,
             ha_ref, hb_ref, hc_ref, hat_ref, hbt_ref, hct_ref):
        ag = _silu(ag_ref[...])
        for h in range(MIX_COLS // HGRN_HEAD_DIM):
            cs = slice(h * HGRN_HEAD_DIM, (h + 1) * HGRN_HEAD_DIM)
            o = oraw_ref[:, cs]
            nrm = o * lax.rsqrt(jnp.mean(o * o, axis=-1, keepdims=True) + RMS_EPS)
            ha = nrm * gn_ref[:, cs] * ag[:, cs]
            ha_ref[:, cs] = ha.astype(BF16)
            hat_ref[cs, :] = ha.T.astype(BF16)
        hb = ob_ref[...] * _silu(bg_ref[...])
        hb_ref[...] = hb.astype(BF16)
        hbt_ref[...] = hb.T.astype(BF16)
        keep = (pl.program_id(0) > 0).astype(F32)
        hcur = cc_ref[...] * cx_ref[...]
        p1 = ccp_ref[7:8, :] * cxp_ref[7:8, :] * keep
        p2 = ccp_ref[6:7, :] * cxp_ref[6:7, :] * keep
        s1, s2 = _shift_rows_down(hcur, p1, p2)
        y = cw_ref[0:1, :] * s2 + cw_ref[1:2, :] * s1 + cw_ref[2:3, :] * hcur
        hc = cb_ref[...] * y * _silu(cg_ref[...])
        hc_ref[...] = hc.astype(BF16)
        hct_ref[...] = hc.T.astype(BF16)

    out = _sds((T, HGRN_WIDTH), BF16)
    out_t = _sds((HGRN_WIDTH, T), BF16)
    act_t = pl.BlockSpec((MIX_COLS, tr), lambda i, j: (j, i))
    return pl.pallas_call(
        body,
        name="mix_fwd",
        grid=(nrow, HGRN_WIDTH // MIX_COLS),
        in_specs=[act, ucol(OFF_AG), act, ucol(OFF_BG), ucol(OFF_CB), ucol(OFF_CC), ucol(OFF_CX), ucol(OFF_CG),
                  uprev(OFF_CC), uprev(OFF_CX), par(1), par(CONV_K)],
        out_specs=[act, act, act, act_t, act_t, act_t],
        out_shape=[out, out, out, out_t, out_t, out_t],
        compiler_params=_cp(("parallel", "parallel")),
    )(o_raw, u, o_b, u, u, u, u, u, u, u, gn_l, cw_l)


def _mix_bwd(u, o_raw, o_b, gn_l, cw_l, dha, dhb, dhc):
    T = u.shape[0]
    tr = _mix_rows(T)
    nrow = T // tr
    hr = tr // 8
    last_halo = T // 8 - 1

    def ucol(off):
        return pl.BlockSpec((tr, MIX_COLS), lambda j, i, off=off: (i, off // MIX_COLS + j))

    def uprev(off):
        return pl.BlockSpec((8, MIX_COLS), lambda j, i, off=off: (jnp.maximum(i * hr - 1, 0), off // MIX_COLS + j))

    def unext(off):
        return pl.BlockSpec((8, MIX_COLS), lambda j, i, off=off: (jnp.minimum((i + 1) * hr, last_halo), off // MIX_COLS + j))

    act = pl.BlockSpec((tr, MIX_COLS), lambda j, i: (i, j))
    act_next = pl.BlockSpec((8, MIX_COLS), lambda j, i: (jnp.minimum((i + 1) * hr, last_halo), j))
    par = lambda rows: pl.BlockSpec((rows, MIX_COLS), lambda j, i: (0, j))

    def body(oraw_ref, ag_ref, ob_ref, bg_ref, cb_ref, cc_ref, cx_ref, cg_ref, ccp_ref, cxp_ref,
             cbn_ref, cgn_ref, dhcn_ref, gn_ref, cw_ref, dha_ref, dhb_ref, dhc_ref,
             doraw_ref, dob_ref, dag_ref, dbg_ref, dcb_ref, dcc_ref, dcx_ref, dcg_ref, dgn_ref, dcw_ref):
        i = pl.program_id(1)

        @pl.when(i == 0)
        def _():
            dgn_ref[...] = jnp.zeros_like(dgn_ref)
            dcw_ref[...] = jnp.zeros_like(dcw_ref)

        ag = ag_ref[...]
        sag = _silu(ag)
        dha = dha_ref[...]
        for h in range(MIX_COLS // HGRN_HEAD_DIM):
            cs = slice(h * HGRN_HEAD_DIM, (h + 1) * HGRN_HEAD_DIM)
            o = oraw_ref[:, cs]
            rs = lax.rsqrt(jnp.mean(o * o, axis=-1, keepdims=True) + RMS_EPS)
            nrm = o * rs
            gn = gn_ref[:, cs]
            d = dha[:, cs]
            dag_ref[:, cs] = (d * nrm * gn * _silu_grad(ag[:, cs])).astype(BF16)
            dgn_ref[:, cs] += jnp.sum(d * sag[:, cs] * nrm, axis=0, keepdims=True)
            dn = d * sag[:, cs] * gn
            doraw_ref[:, cs] = rs * (dn - nrm * jnp.mean(dn * nrm, axis=-1, keepdims=True))
        bg = bg_ref[...]
        dhb = dhb_ref[...]
        dob_ref[...] = dhb * _silu(bg)
        dbg_ref[...] = (dhb * ob_ref[...] * _silu_grad(bg)).astype(BF16)
        keep_prev = (i > 0).astype(F32)
        keep_next = (i < nrow - 1).astype(F32)
        cc, cx, cb, cg = cc_ref[...], cx_ref[...], cb_ref[...], cg_ref[...]
        hcur = cc * cx
        p1 = ccp_ref[7:8, :] * cxp_ref[7:8, :] * keep_prev
        p2 = ccp_ref[6:7, :] * cxp_ref[6:7, :] * keep_prev
        s1, s2 = _shift_rows_down(hcur, p1, p2)
        w0, w1, w2 = cw_ref[0:1, :], cw_ref[1:2, :], cw_ref[2:3, :]
        y = w0 * s2 + w1 * s1 + w2 * hcur
        dhc = dhc_ref[...]
        scg = _silu(cg)
        doc = dhc * scg
        dcg_ref[...] = (dhc * cb * y * _silu_grad(cg)).astype(BF16)
        dcb_ref[...] = (doc * y).astype(BF16)
        dy = doc * cb
        n1 = dhcn_ref[0:1, :] * _silu(cgn_ref[0:1, :]) * cbn_ref[0:1, :] * keep_next
        n2 = dhcn_ref[1:2, :] * _silu(cgn_ref[1:2, :]) * cbn_ref[1:2, :] * keep_next
        u1, u2 = _shift_rows_up(dy, n1, n2)
        dh = w2 * dy + w1 * u1 + w0 * u2
        dcc_ref[...] = (dh * cx).astype(BF16)
        dcx_ref[...] = (dh * cc).astype(BF16)
        dcw_ref[0:1, :] += jnp.sum(dy * s2, axis=0, keepdims=True)
        dcw_ref[1:2, :] += jnp.sum(dy * s1, axis=0, keepdims=True)
        dcw_ref[2:3, :] += jnp.sum(dy * hcur, axis=0, keepdims=True)

    f32o, bf = _sds((T, HGRN_WIDTH), F32), _sds((T, HGRN_WIDTH), BF16)
    return pl.pallas_call(
        body,
        name="mix_bwd",
        grid=(HGRN_WIDTH // MIX_COLS, nrow),
        in_specs=[act, ucol(OFF_AG), act, ucol(OFF_BG), ucol(OFF_CB), ucol(OFF_CC), ucol(OFF_CX), ucol(OFF_CG),
                  uprev(OFF_CC), uprev(OFF_CX), unext(OFF_CB), unext(OFF_CG), act_next, par(1), par(CONV_K),
                  act, act, act],
        out_specs=[act, act, act, act, act, act, act, act, par(1), par(8)],
        out_shape=[f32o, f32o, bf, bf, bf, bf, bf, bf, _sds((1, HGRN_WIDTH), F32), _sds((8, HGRN_WIDTH), F32)],
        compiler_params=_cp(("parallel", "arbitrary")),
    )(o_raw, u, o_b, u, u, u, u, u, u, u, u, u, dhc, gn_l, cw_l, dha, dhb, dhc)


def _merge_specs(T, order):
    tr = _tile(T, 256)

    def ucol(off):
        if order == "ij":
            return pl.BlockSpec((tr, MIX_COLS), lambda i, j, off=off: (i, off // MIX_COLS + j))
        return pl.BlockSpec((tr, MIX_COLS), lambda j, i, off=off: (i, off // MIX_COLS + j))

    act = pl.BlockSpec((tr, MIX_COLS), (lambda i, j: (i, j)) if order == "ij" else (lambda j, i: (i, j)))
    return tr, ucol, act


def _merge_fwd(u, ya, yb, yc):
    T = u.shape[0]
    tr, ucol, act = _merge_specs(T, "ij")

    def body(ma_ref, mb_ref, mc_ref, ya_ref, yb_ref, yc_ref, o_ref, ot_ref):
        merged = (jax.nn.sigmoid(ma_ref[...]) * ya_ref[...] + jax.nn.sigmoid(mb_ref[...]) * yb_ref[...]
                  + jax.nn.sigmoid(mc_ref[...]) * yc_ref[...])
        o_ref[...] = merged.astype(BF16)
        ot_ref[...] = merged.T.astype(BF16)

    return pl.pallas_call(
        body,
        name="merge_fwd",
        grid=(T // tr, D_MODEL // MIX_COLS),
        in_specs=[ucol(OFF_MA), ucol(OFF_MB), ucol(OFF_MC), act, act, act],
        out_specs=[act, pl.BlockSpec((MIX_COLS, tr), lambda i, j: (j, i))],
        out_shape=[_sds((T, D_MODEL), BF16), _sds((D_MODEL, T), BF16)],
        compiler_params=_cp(("parallel", "parallel")),
    )(u, u, u, ya, yb, yc)


def _merge_bwd(u, ya, yb, yc, dmerged, plan=None):
    T = u.shape[0]
    tr, ucol, act = _merge_specs(T, "ij")

    def body(ma_ref, mb_ref, mc_ref, ya_ref, yb_ref, yc_ref, dm_ref, dya_ref, dyb_ref, dyc_ref, dma_ref, dmb_ref, dmc_ref):
        dm = dm_ref[...]
        for m_ref, y_ref, dy_ref, dg_ref in ((ma_ref, ya_ref, dya_ref, dma_ref), (mb_ref, yb_ref, dyb_ref, dmb_ref),
                                             (mc_ref, yc_ref, dyc_ref, dmc_ref)):
            s = jax.nn.sigmoid(m_ref[...])
            dy_ref[...] = (dm * s).astype(BF16)
            dg_ref[...] = (dm * y_ref[...] * s * (1.0 - s)).astype(BF16)

    out = _sds((T, D_MODEL), BF16)
    return _call(
        body,
        name="merge_bwd",
        grid=(T // tr, D_MODEL // MIX_COLS),
        in_specs=[ucol(OFF_MA), ucol(OFF_MB), ucol(OFF_MC), act, act, act, act],
        out_specs=[act] * 6,
        out_shape=[out] * 6,
        sem=("parallel", "parallel"),
        args=(u, u, u, ya, yb, yc, dmerged),
        plan=plan,
    )


def _ln_fwd(x, y, g_l, b_l):
    T = x.shape[0]
    tr = _tile(T, 256)
    row = pl.BlockSpec((tr, D_MODEL), lambda i: (i, 0))
    col = pl.BlockSpec((D_MODEL, tr), lambda i: (0, i))
    par = pl.BlockSpec((1, D_MODEL), lambda i: (0, 0))

    def body(x_ref, y_ref, g_ref, b_ref, o_ref, z_ref, ob_ref, ot_ref):
        z = ALPHA * x_ref[...] + y_ref[...]
        z_ref[...] = z
        mu = jnp.mean(z, axis=-1, keepdims=True)
        zc = z - mu
        var = jnp.mean(zc * zc, axis=-1, keepdims=True)
        o = zc * lax.rsqrt(var + LN_EPS) * g_ref[...] + b_ref[...]
        o_ref[...] = o
        ob_ref[...] = o.astype(BF16)
        ot_ref[...] = o.T.astype(BF16)

    return pl.pallas_call(
        body,
        name="ln_fwd",
        grid=(T // tr,),
        in_specs=[row, row, par, par],
        out_specs=[row, row, row, col],
        out_shape=[_sds((T, D_MODEL), F32)] * 2 + [_sds((T, D_MODEL), BF16), _sds((D_MODEL, T), BF16)],
        compiler_params=_cp(("parallel",)),
    )(x, y, g_l, b_l)


def _operand_forms(x):
    T = x.shape[0]
    tr = _tile(T, 256)
    row = pl.BlockSpec((tr, D_MODEL), lambda i: (i, 0))
    col = pl.BlockSpec((D_MODEL, tr), lambda i: (0, i))

    def body(x_ref, xb_ref, xt_ref):
        xv = x_ref[...]
        xb_ref[...] = xv.astype(BF16)
        xt_ref[...] = xv.T.astype(BF16)

    return pl.pallas_call(
        body,
        name="operand_forms",
        grid=(T // tr,),
        in_specs=[row],
        out_specs=[row, col],
        out_shape=[_sds((T, D_MODEL), BF16), _sds((D_MODEL, T), BF16)],
        compiler_params=_cp(("parallel",)),
    )(x)


def _ln_bwd(z, dxn, g_l):
    T = z.shape[0]
    tr = _tile(T, 256)
    row = pl.BlockSpec((tr, D_MODEL), lambda i: (i, 0))
    par = pl.BlockSpec((1, D_MODEL), lambda i: (0, 0))

    def body(z_ref, d_ref, g_ref, dz_ref, dzb_ref, dg_ref, db_ref):
        @pl.when(pl.program_id(0) == 0)
        def _():
            dg_ref[...] = jnp.zeros_like(dg_ref)
            db_ref[...] = jnp.zeros_like(db_ref)

        z = z_ref[...]
        d = d_ref[...]
        mu = jnp.mean(z, axis=-1, keepdims=True)
        zc = z - mu
        rstd = lax.rsqrt(jnp.mean(zc * zc, axis=-1, keepdims=True) + LN_EPS)
        zh = zc * rstd
        dg_ref[...] += jnp.sum(d * zh, axis=0, keepdims=True)
        db_ref[...] += jnp.sum(d, axis=0, keepdims=True)
        dh = d * g_ref[...]
        dz = rstd * (dh - jnp.mean(dh, axis=-1, keepdims=True) - zh * jnp.mean(dh * zh, axis=-1, keepdims=True))
        dz_ref[...] = dz
        dzb_ref[...] = dz.astype(BF16)

    return pl.pallas_call(
        body,
        name="ln_bwd",
        grid=(T // tr,),
        in_specs=[row, row, par],
        out_specs=[row, row, par, par],
        out_shape=[_sds((T, D_MODEL), F32), _sds((T, D_MODEL), BF16), _sds((1, D_MODEL), F32), _sds((1, D_MODEL), F32)],
        compiler_params=_cp(("arbitrary",)),
    )(z, dxn, g_l)


def _loss_head(y, target):
    T = y.shape[0]
    tr = _tile(T, 256)
    row = pl.BlockSpec((tr, D_MODEL), lambda i: (i, 0))
    acc = pl.BlockSpec((8, 128), lambda i: (0, 0))

    def body(y_ref, t_ref, l_ref, d_ref):
        @pl.when(pl.program_id(0) == 0)
        def _():
            l_ref[...] = jnp.zeros_like(l_ref)

        err = y_ref[...] - t_ref[...]
        d_ref[...] = err * (1.0 / D_MODEL)
        part = 0.5 * jnp.sum(jnp.sum(err * err, axis=-1, keepdims=True) * (1.0 / D_MODEL), axis=0, keepdims=True)
        r = lax.broadcasted_iota(jnp.int32, (8, 128), 0)
        c = lax.broadcasted_iota(jnp.int32, (8, 128), 1)
        l_ref[...] += jnp.where((r == 0) & (c == 0), part, 0.0)

    return pl.pallas_call(
        body,
        name="loss_head",
        grid=(T // tr,),
        in_specs=[row, row],
        out_specs=[acc, row],
        out_shape=[_sds((8, 128), F32), _sds((T, D_MODEL), F32)],
        compiler_params=_cp(("arbitrary",)),
    )(y, target)


ADAMW_BLOCK_ELEMS = 256 * 1024


def _adamw(w, g, m, v, name):
    shape = w.shape
    cols = shape[-1]
    rows = math.prod(shape[:-1])
    flat = lambda a: a.reshape(rows, cols)
    if rows * cols <= ADAMW_BLOCK_ELEMS or rows % 8:
        tr = rows
    else:
        tr = 8
        while rows % (tr * 2) == 0 and tr * 2 * cols <= ADAMW_BLOCK_ELEMS:
            tr *= 2
    blk = pl.BlockSpec((tr, cols), lambda i: (i, 0))
    c1 = 1.0 - ADAM_B1 ** ADAM_STEP
    c2 = 1.0 - ADAM_B2 ** ADAM_STEP

    def body(w_ref, g_ref, m_ref, v_ref, d_ref, nm_ref, nv_ref):
        gg = g_ref[...]
        nm = ADAM_B1 * m_ref[...] + (1.0 - ADAM_B1) * gg
        nv = ADAM_B2 * v_ref[...] + (1.0 - ADAM_B2) * (gg * gg)
        nm_ref[...] = nm
        nv_ref[...] = nv
        d_ref[...] = -ADAM_LR * ((nm / c1) / (jnp.sqrt(nv / c2) + ADAM_EPS) + ADAM_WD * w_ref[...])

    outs = pl.pallas_call(
        body,
        name=name,
        grid=(rows // tr,),
        in_specs=[blk] * 4,
        out_specs=[blk] * 3,
        out_shape=[_sds((rows, cols), F32)] * 3,
        compiler_params=_cp(("parallel",)),
    )(flat(w), flat(g), flat(m), flat(v))
    return tuple(o.reshape(shape) for o in outs)


def _t5_bucket(dist):
    max_exact = N_BUCKETS // 2
    logd = jnp.log(jnp.maximum(dist, 1).astype(F32) / max_exact) / math.log(MAX_DISTANCE / max_exact)
    large = jnp.minimum(max_exact + (logd * (N_BUCKETS - max_exact)).astype(jnp.int32), N_BUCKETS - 1)
    return jnp.where(dist < max_exact, dist, large)


def _band_bias(rel_bias):
    i = jnp.arange(WINDOW)[:, None]
    j = jnp.arange(2 * WINDOW)[None, :]
    bucket = _t5_bucket(jnp.clip(WINDOW + i - j, 0, WINDOW - 1))
    onehot = (bucket[:, :, None] == jnp.arange(N_BUCKETS)[None, None, :]).astype(F32)
    return jnp.einsum("ijb,bh->hij", onehot, rel_bias.astype(F32), precision=lax.Precision.HIGHEST)


def _lower_bounds(lb_param):
    soft = jax.nn.softmax(lb_param.astype(F32), axis=0)
    return jnp.cumsum(soft, axis=0) - soft[0:1]


def _mm_rows(T):
    return _tile(T, 1024)


def _layer_fwd(xs, ex, layer, lower_l, bias4, sink4_l, gn_l, cw_l, lng_l, lnb_l):
    x, xb, xt = xs
    T = x.shape[0]
    tm = _mm_rows(T)
    w = ex.weights[layer]
    w_in_l, w_proj_l, w_out_l = w["w_in"][0], w["w_proj"], w["w_out"][0]
    plan = ex.fwd_plan(layer)
    u = _matmul(xb, w_in_l, mode="nn", tm=tm, tn=768, tk=D_MODEL, name="mm_u", plan=plan)
    if plan is not None:
        u, landed = u
        ex.fwd_landed(layer, landed)
    o_raw, states = _hgrn_fwd(u, lower_l)
    q4 = _q_to_blocks(u[:, OFF_BQ:OFF_BQ + ATTN_WIDTH])
    k4 = _kv_to_heads(u[:, OFF_BK:OFF_BK + KV_WIDTH])
    v4 = _kv_to_heads(u[:, OFF_BV:OFF_BV + KV_WIDTH])
    o_b = _q_from_blocks(_attn_fwd(q4, k4, v4, bias4, sink4_l))
    ha, hb, hc, hat, hbt, hct = _mix_fwd(u, o_raw, o_b, gn_l, cw_l)
    ys = [_matmul(h, w_proj_l, mode="nn", tm=tm, tn=1024, tk=HGRN_WIDTH, b_idx=i, name="mm_proj")
          for i, h in enumerate((ha, hb, hc))]
    merged, merged_t = _merge_fwd(u, *ys)
    y = _matmul(merged, w_out_l, mode="nn", tm=tm, tn=1024, tk=D_MODEL, name="mm_out")
    xn, z, xnb, xnt = _ln_fwd(x, y, lng_l, lnb_l)
    saved = dict(xt=xt, u=u, o_raw=o_raw, states=states, q4=q4, k4=k4, v4=v4, o_b=o_b, hts=(hat, hbt, hct), ys=ys,
                 merged_t=merged_t, z=z)
    return (xn, xnb, xnt), saved


def _layer_bwd(dxn, s, ex, layer, lower_l, bias4, sink4_l, gn_l, cw_l, lng_l):
    T = dxn.shape[0]
    tm = _mm_rows(T)
    u = s["u"]
    w = ex.weights[layer]
    w_in_l, w_proj_l, w_out_l = w["w_in"][0], w["w_proj"], w["w_out"][0]
    dz, dzb, d_lng, d_lnb = _ln_bwd(s["z"], dxn, lng_l)
    dmerged = _matmul(dzb, w_out_l, mode="nt", tm=tm, tn=1024, tk=D_MODEL, name="mm_dmerged")
    g_w_out = _matmul(s["merged_t"], dzb, mode="nn", tm=1024, tn=1024, tk=2048, name="mm_gw_out")
    plan = ex.pair_plan(layer)
    (*dys, dma, dmb, dmc), landed = _merge_bwd(u, *s["ys"], dmerged, plan=plan)
    if plan is not None:
        ex.pair_landed(layer, landed)
    dhs =[_matmul(dy, w_proj_l, mode="nt", tm=tm, tn=1024, tk=D_MODEL, b_idx=i, name="mm_dh") for i, dy in enumerate(dys)]
    g_w_proj = jnp.stack([_matmul(ht, dy, mode="nn", tm=1024, tn=1024, tk=2048, name="mm_gw_proj")
                          for ht, dy in zip(s["hts"], dys)])
    do_raw, do_b, dag, dbg, dcb, dcc, dcx, dcg, d_gn, d_cw = _mix_bwd(u, s["o_raw"], s["o_b"], gn_l, cw_l, *dhs)
    daq, daf, dai, d_lower = _hgrn_bwd(u, lower_l, s["states"], do_raw)
    plan = ex.slab_plan(layer)
    (dq4, dkc, dkp, dvc, dvp, d_bias4, d_sink4), landed = _attn_bwd(
        s["q4"], s["k4"], s["v4"], bias4, sink4_l, _q_to_blocks(do_b), plan=plan)
    if plan is not None:
        ex.slab_landed(layer, landed)
    dbq = _q_from_blocks(dq4).astype(BF16)
    dbk = _kv_from_heads(_shift_prev_contrib(dkc, dkp)).astype(BF16)
    dbv = _kv_from_heads(_shift_prev_contrib(dvc, dvp)).astype(BF16)
    du = jnp.concatenate([daq, daf, dai, dag, dbq, dbk, dbv, dbg, dcb, dcc, dcx, dcg, dma, dmb, dmc], axis=1)
    dx = _matmul(du, w_in_l, mode="nt", tm=tm, tn=1024, tk=1536, name="mm_dx", add=dz, add_scale=ALPHA)
    g_w_in = _matmul(s["xt"], du, mode="nn", tm=1024, tn=768, tk=2048, name="mm_gw_in")
    d_sinks = jnp.sum(d_sink4.reshape(ATTN_HEADS, WINDOW), axis=-1)
    small = dict(lower=d_lower[0], gn=d_gn[0], sinks=d_sinks, cw=d_cw[:CONV_K], bias=d_bias4.reshape(ATTN_HEADS, WINDOW, 2 * WINDOW),
                 lng=d_lng[0], lnb=d_lnb[0])
    ex.grads_ready(layer, dict(w_in=g_w_in[None], w_proj=g_w_proj, w_out=g_w_out[None]))
    return dx, small


def _local_step(x, target, ex, lb_param, hgrn_norm_g, attn_sinks, conv_w_full, rel_bias, ln_g, ln_b):
    lower, lower_vjp = jax.vjp(_lower_bounds, lb_param)
    bias, bias_vjp = jax.vjp(_band_bias, rel_bias)
    bias4 = bias.reshape(ATTN_KV_HEADS, QROWS, 2 * WINDOW)
    sink4 = jnp.broadcast_to(attn_sinks.reshape(DEPTH, ATTN_HEADS, 1, 1), (DEPTH, ATTN_HEADS, WINDOW, 1)).reshape(
        DEPTH, ATTN_KV_HEADS, QROWS, 1)
    row = lambda a, l: a[l:l + 1]
    saved = []
    hs = (x, *_operand_forms(x))
    for l in range(DEPTH):
        hs, s = _layer_fwd(hs, ex, l, row(lower, l), bias4, sink4[l], row(hgrn_norm_g, l), conv_w_full[l], row(ln_g, l), row(ln_b, l))
        saved.append(s)
    loss_blk, dh = _loss_head(hs[0], target)
    smalls = [None] * DEPTH
    for l in reversed(range(DEPTH)):
        dh, smalls[l] = _layer_bwd(dh, saved[l], ex, l, row(lower, l), bias4, sink4[l], row(hgrn_norm_g, l), conv_w_full[l], row(ln_g, l))
    stack = lambda k: jnp.stack([sm[k] for sm in smalls])
    d_bias = smalls[0]["bias"] + smalls[1]["bias"] + smalls[2]["bias"] + smalls[3]["bias"]
    small = dict(
        lb_param=lower_vjp(stack("lower"))[0], hgrn_norm_g=stack("gn"), attn_sinks=stack("sinks"), conv_w=stack("cw"),
        rel_bias=bias_vjp(d_bias)[0], ln_g=stack("lng"), ln_b=stack("lnb"))
    return loss_blk, dh, small


ANY = pl.BlockSpec(memory_space=pl.ANY)
DMA_SEM = pltpu.SemaphoreType.DMA


def _coords():
    return lax.axis_index("x"), lax.axis_index("y"), lax.axis_index("c")


def _other_chips(x, y):
    return [(1 - x, y), (x, 1 - y), (1 - x, 1 - y)]


def _remote(src, dst, send_sem, recv_sem, device):
    return pltpu.make_async_remote_copy(src_ref=src, dst_ref=dst, send_sem=send_sem, recv_sem=recv_sem,
                                        device_id=device, device_id_type=MESH)


def _sub(ref, axis, index, size):
    idx = [slice(None)] * len(ref.shape)
    idx[axis] = pl.ds(pl.multiple_of(index * size, size), size)
    return ref.at[tuple(idx)]


class _Plan:
    def __init__(self, inputs, out_shapes, aliases, n, make):
        self.inputs, self.out_shapes, self.aliases, self.n, self.make = tuple(inputs), tuple(out_shapes), dict(aliases), n, make


class _Xfer:
    def __init__(self, send, recv=None):
        self.send, self.recv = send, send if recv is None else recv

    def start(self):
        self.send.start()

    def wait(self):
        self.send.wait_send()
        self.recv.wait_recv()


def _merge_plans(plans):
    def make(in_refs, out_refs, send_sems, recv_sems, base):
        out, i0, o0, b0 = [], 0, 0, base
        for p in plans:
            out += p.make(in_refs[i0:i0 + len(p.inputs)], out_refs[o0:o0 + len(p.out_shapes)], send_sems, recv_sems, b0)
            i0, o0, b0 = i0 + len(p.inputs), o0 + len(p.out_shapes), b0 + p.n
        return out

    inputs, out_shapes, aliases = [], [], {}
    for p in plans:
        aliases.update({len(inputs) + k: len(out_shapes) + v for k, v in p.aliases.items()})
        inputs += p.inputs
        out_shapes += p.out_shapes
    return _Plan(inputs, out_shapes, aliases, sum(p.n for p in plans), make)


def _run_plan(plan, name):
    ni, no = len(plan.inputs), len(plan.out_shapes)

    def body(*refs):
        transfers = plan.make(refs[:ni], refs[ni:ni + no], refs[ni + no], refs[ni + no + 1], 0)
        for t in transfers:
            t.start()
        for t in transfers:
            t.wait()

    outs = pl.pallas_call(
        body, name=name, in_specs=[ANY] * ni, out_specs=[ANY] * no, out_shape=list(plan.out_shapes),
        input_output_aliases=plan.aliases, scratch_shapes=[DMA_SEM((plan.n,)), DMA_SEM((plan.n,))],
    )(*plan.inputs)
    return list(outs)


def _gather_send_plan(shards, layer, sax):
    shp = shards.shape[1:]
    hax = 3 - sax
    w, hw = shp[sax], shp[hax] // 2
    out_shape = list(shp)
    out_shape[sax] = w * N_CHIPS

    def make(in_refs, out_refs, send_sems, recv_sems, base):
        (src_ref,), (out_ref,) = in_refs, out_refs
        x, y, c = _coords()
        j = 2 * x + y
        src = src_ref.at[layer]
        own = pltpu.make_async_copy(src, _sub(out_ref, sax, j, w), send_sems.at[base])
        dst = _sub(_sub(out_ref, sax, j, w), hax, c, hw)
        return [own] + [
            _Xfer(_remote(_sub(src, hax, c, hw), dst, send_sems.at[base + 1 + k], recv_sems.at[base + 1 + k], (px, py, c)))
            for k, (px, py) in enumerate(_other_chips(x, y))]

    return _Plan([shards], [_sds(tuple(out_shape), shards.dtype)], {}, 4, make)


def _gather_pass_plan(full, sax):
    hax = 3 - sax
    w, hw = full.shape[sax] // N_CHIPS, full.shape[hax] // 2

    def make(in_refs, out_refs, send_sems, recv_sems, base):
        (out_ref,) = out_refs
        x, y, c = _coords()
        region = lambda slab, half: _sub(_sub(out_ref, sax, slab, w), hax, half, hw)
        out = []
        for k, (px, py) in enumerate(_other_chips(x, y)):
            mine, theirs = region(2 * px + py, c), region(2 * px + py, 1 - c)
            sems = send_sems.at[base + k], recv_sems.at[base + k]
            out.append(_Xfer(_remote(mine, mine, *sems, (x, y, 1 - c)), _remote(theirs, theirs, *sems, (x, y, c))))
        return out

    return _Plan([full], [_sds(full.shape, full.dtype)], {0: 0}, 3, make)


def _pair_exchange_plan(g, hax):
    hw = g.shape[hax] // 2
    out_shape = list(g.shape)
    out_shape[hax] = hw

    def make(in_refs, out_refs, send_sems, recv_sems, base):
        x, y, c = _coords()
        return [_Xfer(_remote(_sub(in_refs[0], hax, 1 - c, hw), out_refs[0], send_sems.at[base], recv_sems.at[base], (x, y, 1 - c)))]

    return _Plan([g], [_sds(tuple(out_shape), g.dtype)], {}, 1, make)


def _add_own_half(place, g, recv, hax, blk, name):
    L, ah, bh = recv.shape
    tr, tc = blk
    nr, nc = ah // tr, bh // tc

    def g_map(l, i, jc, p):
        return (l, i + p[0] * nr, jc) if hax == 1 else (l, i, jc + p[0] * nc)

    def body(p_ref, g_ref, r_ref, o_ref):
        o_ref[...] = (g_ref[...] + r_ref[...]).astype(BF16)

    same = pl.BlockSpec((None, tr, tc), lambda l, i, jc, p: (l, i, jc))
    return pl.pallas_call(
        body,
        name=name,
        grid_spec=pltpu.PrefetchScalarGridSpec(
            num_scalar_prefetch=1, grid=(L, nr, nc),
            in_specs=[pl.BlockSpec((None, tr, tc), g_map), same], out_specs=same),
        out_shape=_sds(recv.shape, BF16),
        compiler_params=_cp(("parallel", "parallel", "parallel")),
    )(place, g, recv)


def _slab_exchange_plan(p, sax):
    w = p.shape[sax] // N_CHIPS
    slab_shape = list(p.shape)
    slab_shape[sax] = w

    def make(in_refs, out_refs, send_sems, recv_sems, base):
        x, y, c = _coords()
        return [_Xfer(_remote(_sub(in_refs[0], sax, 2 * px + py, w), out_refs[0].at[k], send_sems.at[base + k],
                              recv_sems.at[base + k], (px, py, c)))
                for k, (px, py) in enumerate(_other_chips(x, y))]

    return _Plan([p], [_sds((3, *slab_shape), p.dtype)], {}, 3, make)


def _add_slabs(place, g, pair, recv, sax, blk, name):
    hax = 3 - sax
    _, L, a, b = recv.shape
    tr, tc = blk
    nr, nc = a // tr, b // tc

    def g_map(l, i, jc, p):
        return (l, p[0] * nr + i, p[1] * nc + jc) if hax == 1 else (l, p[1] * nr + i, p[0] * nc + jc)

    def pair_map(l, i, jc, p):
        return (l, i, p[1] * nc + jc) if hax == 1 else (l, p[1] * nr + i, jc)

    def out_map(l, i, jc, p):
        return (l, p[0] * nr + i, jc) if hax == 1 else (l, i, p[0] * nc + jc)

    def body(p_ref, g_ref, pair_ref, r0_ref, r1_ref, r2_ref, o_ref):
        own = g_ref[...] + pair_ref[...]
        o_ref[...] = ((own + r0_ref[...].astype(F32)) + r1_ref[...].astype(F32)) + r2_ref[...].astype(F32)

    def rk(k):
        return pl.BlockSpec((None, None, tr, tc), lambda l, i, jc, p, k=k: (k, l, i, jc))

    out_shape = [L, a, b]
    out_shape[hax] *= 2
    blk3 = (None, tr, tc)
    return pl.pallas_call(
        body,
        name=name,
        grid_spec=pltpu.PrefetchScalarGridSpec(
            num_scalar_prefetch=1, grid=(L, nr, nc),
            in_specs=[pl.BlockSpec(blk3, g_map), pl.BlockSpec(blk3, pair_map), rk(0), rk(1), rk(2)],
            out_specs=pl.BlockSpec(blk3, out_map)),
        out_shape=_sds(tuple(out_shape), F32),
        compiler_params=_cp(("parallel", "parallel", "parallel")),
    )(place, g, pair, recv, recv, recv)


def _pair_assemble_plan(r, hax):
    hw = r.shape[hax] // 2

    def make(in_refs, out_refs, send_sems, recv_sems, base):
        x, y, c = _coords()
        mine, other = _sub(out_refs[0], hax, c, hw), _sub(out_refs[0], hax, 1 - c, hw)
        sems = send_sems.at[base], recv_sems.at[base]
        return [_Xfer(_remote(mine, mine, *sems, (x, y, 1 - c)), _remote(other, other, *sems, (x, y, c)))]

    return _Plan([r], [_sds(r.shape, r.dtype)], {0: 0}, 1, make)


CLASSES = dict(w_in=(2, (128, 4224), (128, 4224)), w_proj=(2, (256, 2048), (512, 512)), w_out=(1, (256, 1024), (256, 1024)))


class _Exchanges:
    def __init__(self, place, shards):
        self.place, self.shards = place, shards
        self.weights, self.pending, self.pair, self.reduced = {}, {}, {}, {}

    def _pass_on(self, bufs, tag):
        plans = [_gather_pass_plan(b, CLASSES[k][0]) for k, b in zip(CLASSES, bufs)]
        return dict(zip(CLASSES, _run_plan(_merge_plans(plans), "gather_pass_" + tag)))

    def gather_send_plan(self, layer):
        return _merge_plans([_gather_send_plan(self.shards[k], layer, CLASSES[k][0]) for k in CLASSES])

    def first_weights(self):
        self.weights[0] = self._pass_on(_run_plan(self.gather_send_plan(0), "gather_send_0"), "0")

    def fwd_plan(self, layer):
        return self.gather_send_plan(layer + 1) if layer + 1 < DEPTH else None

    def fwd_landed(self, layer, bufs):
        self.weights[layer + 1] = self._pass_on(bufs, str(layer + 1))

    def grads_ready(self, layer, grads):
        self.pending[layer] = grads

    def pair_plan(self, layer):
        if layer + 1 not in self.pending:
            return None
        g = self.pending[layer + 1]
        return _merge_plans([_pair_exchange_plan(g[k], 3 - CLASSES[k][0]) for k in CLASSES])

    def pair_landed(self, layer, bufs):
        self.pair[layer + 1] = dict(zip(CLASSES, bufs))

    def slab_plan(self, layer):
        src = layer + 1
        if src not in self.pair:
            return None
        g, pair = self.pending[src], self.pair[src]
        sums = [_add_own_half(self.place, g[k], pair[k], 3 - CLASSES[k][0], CLASSES[k][1], f"rs_pair_add_{k}_{src}") for k in CLASSES]
        return _merge_plans([_slab_exchange_plan(p, CLASSES[k][0]) for k, p in zip(CLASSES, sums)])

    def slab_landed(self, layer, bufs):
        src = layer + 1
        g, pair = self.pending.pop(src), self.pair.pop(src)
        halves = [_add_slabs(self.place, g[k], pair[k], r, CLASSES[k][0], CLASSES[k][2], f"rs_slab_add_{k}_{src}")
                  for k, r in zip(CLASSES, bufs)]
        plans = [_pair_assemble_plan(h, 3 - CLASSES[k][0]) for k, h in zip(CLASSES, halves)]
        self.reduced[src] = dict(zip(CLASSES, _run_plan(_merge_plans(plans), f"rs_assemble_{src}")))

    def finish(self):
        self.pair_landed(-1, _run_plan(self.pair_plan(-1), "rs_pair_0"))
        self.slab_landed(-1, _run_plan(self.slab_plan(-1), "rs_slab_0"))
        return [self.reduced[l] for l in range(DEPTH)]


N_DEV = 8


def _all_reduce_small(v, name):
    rows = v.shape[0]

    def body(v_ref, gath_ref, sum_ref, send_sems, recv_sems, local_sem):
        x, y, c = _coords()
        me, sib = (x, y, c), (x, y, 1 - c)
        chips = _other_chips(x, y)

        def slot(px, py, pc):
            return gath_ref.at[pl.ds(pl.multiple_of((4 * px + 2 * py + pc) * rows, rows), rows), :]

        def copy(k, block, to, src=None):
            return _remote(slot(*block) if src is None else src, slot(*block), send_sems.at[k], recv_sems.at[k], to)

        mine = pltpu.make_async_copy(v_ref, slot(*me), local_sem)
        mine.start()
        first = [copy(0, me, sib, src=v_ref)] + [copy(1 + k, me, (*chip, c), src=v_ref) for k, chip in enumerate(chips)]
        for cp in first:
            cp.start()
        passed = [copy(4 + k, (*chip, c), sib) for k, chip in enumerate(chips)]
        for k, chip in enumerate(chips):
            copy(1 + k, (*chip, c), me).wait_recv()
            passed[k].start()
        copy(0, sib, me).wait_recv()
        for k, chip in enumerate(chips):
            copy(4 + k, (*chip, 1 - c), me).wait_recv()
        for cp in first + passed:
            cp.wait_send()
        mine.wait()
        acc = gath_ref[0:rows, :]
        for d in range(1, N_DEV):
            acc = acc + gath_ref[d * rows:(d + 1) * rows, :]
        sum_ref[...] = acc

    vm = pl.BlockSpec(memory_space=pltpu.VMEM)
    return pl.pallas_call(
        body, name=name, in_specs=[vm], out_specs=[vm, vm],
        out_shape=[_sds((N_DEV * rows, 128), F32), _sds((rows, 128), F32)],
        scratch_shapes=[DMA_SEM((7,)), DMA_SEM((7,)), DMA_SEM(())],
    )(v)[1]


def _pad_rows(a):
    flat = a.reshape(-1).astype(F32)
    rows = -(-flat.shape[0] // (8 * 128)) * 8
    return jnp.pad(flat, (0, rows * 128 - flat.shape[0])).reshape(rows, 128)


def _sum_over_devices(parts, name):
    blocks = [_pad_rows(a) for a in parts.values()]
    total = _all_reduce_small(jnp.concatenate(blocks, axis=0), name)
    out, r0 = {}, 0
    for (key, a), blk in zip(parts.items(), blocks):
        out[key] = total[r0:r0 + blk.shape[0]].reshape(-1)[:a.size].reshape(a.shape)
        r0 += blk.shape[0]
    return out


def kernel(x, w_in, w_proj_hgrn, w_proj_attn, w_proj_conv, w_out, lb_param, hgrn_norm_g, attn_sinks, conv_w, rel_bias, ln_g, ln_b, loss_target, m_w_in, m_w_proj_hgrn, m_w_proj_attn, m_w_proj_conv, m_w_out, m_lb_param, m_hgrn_norm_g, m_attn_sinks, m_conv_w, m_rel_bias, m_ln_g, m_ln_b, v_w_in, v_w_proj_hgrn, v_w_proj_attn, v_w_proj_conv, v_w_out, v_lb_param, v_hgrn_norm_g, v_attn_sinks, v_conv_w, v_rel_bias, v_ln_g, v_ln_b):
    xi, yi, ci = _coords()
    slab = 2 * xi + yi
    place = jnp.stack([ci, slab]).astype(jnp.int32)
    conv_cols = conv_w.shape[-1]

    w_in_b = w_in.astype(BF16)[:, None]
    w_proj_b = jnp.stack([w_proj_hgrn, w_proj_attn, w_proj_conv], axis=1).astype(BF16)
    w_out_b = w_out.astype(BF16)[:, None]
    ex = _Exchanges(place, dict(w_in=w_in_b, w_proj=w_proj_b, w_out=w_out_b))
    ex.first_weights()
    conv_spread = lax.dynamic_update_slice(jnp.zeros((DEPTH, CONV_K, CONV_WIDTH), F32), conv_w, (0, 0, slab * conv_cols))
    conv_full = 0.5 * _sum_over_devices({"conv_w": conv_spread}, "gather_conv_w")["conv_w"]

    loss_blk, dx, small = _local_step(x[0], loss_target[0], ex, lb_param, hgrn_norm_g, attn_sinks, conv_full, rel_bias, ln_g, ln_b)

    reduced = ex.finish()
    g_w_in = jnp.stack([r["w_in"][0] for r in reduced])
    g_w_proj = jnp.stack([r["w_proj"] for r in reduced])
    g_w_out = jnp.stack([r["w_out"][0] for r in reduced])
    small = dict(small, loss=loss_blk[0:1, 0:1])
    small = _sum_over_devices(small, "sum_small")
    loss = small["loss"][0, 0]
    g_conv = lax.dynamic_slice(small["conv_w"], (0, 0, slab * conv_cols), (DEPTH, CONV_K, conv_cols))

    grads = [g_w_in, g_w_proj[:, 0], g_w_proj[:, 1], g_w_proj[:, 2], g_w_out, small["lb_param"], small["hgrn_norm_g"],
             small["attn_sinks"], g_conv, small["rel_bias"], small["ln_g"], small["ln_b"]]
    names = ["w_in", "w_proj_hgrn", "w_proj_attn", "w_proj_conv", "w_out", "lb_param", "hgrn_norm_g", "attn_sinks",
             "conv_w", "rel_bias", "ln_g", "ln_b"]
    ws = [w_in, w_proj_hgrn, w_proj_attn, w_proj_conv, w_out, lb_param, hgrn_norm_g, attn_sinks, conv_w, rel_bias, ln_g, ln_b]
    ms = [m_w_in, m_w_proj_hgrn, m_w_proj_attn, m_w_proj_conv, m_w_out, m_lb_param, m_hgrn_norm_g, m_attn_sinks, m_conv_w,
          m_rel_bias, m_ln_g, m_ln_b]
    vs = [v_w_in, v_w_proj_hgrn, v_w_proj_attn, v_w_proj_conv, v_w_out, v_lb_param, v_hgrn_norm_g, v_attn_sinks, v_conv_w,
          v_rel_bias, v_ln_g, v_ln_b]
    upd = [_adamw(w, g, m, v, "adamw_" + n) for n, w, g, m, v in zip(names, ws, grads, ms, vs)]
    deltas, new_ms, new_vs = zip(*upd)
    return (loss, dx[None], *grads, *deltas, *new_ms, *new_vs)
```

```python
import functools
import math

import jax
import jax.numpy as jnp
from jax import lax
from jax.experimental import pallas as pl
from jax.experimental.pallas import tpu as pltpu

F32 = jnp.float32
BF16 = jnp.bfloat16
MXU_DTYPE = BF16

D_MODEL = 2048
DEPTH = 4
HGRN_WIDTH = 1024
HGRN_HEAD_DIM = 128
HGRN_HEADS = 8
HGRN_CHUNK = 64
ATTN_HEAD_DIM = 64
ATTN_HEADS = 16
ATTN_KV_HEADS = 4
ATTN_GROUP = ATTN_HEADS // ATTN_KV_HEADS
ATTN_WIDTH = 1024
KV_WIDTH = 256
WINDOW = 128
CONV_WIDTH = 1024
CONV_K = 3
N_BUCKETS = 32
MAX_DISTANCE = 128
ALPHA = (2.0 * DEPTH) ** 0.25
LN_EPS = 1e-5
RMS_EPS = 1e-6
N_IN = 16896
OFF_AQ, OFF_AF, OFF_AI, OFF_AG = 0, 1024, 2048, 3072
OFF_BQ, OFF_BK, OFF_BV, OFF_BG = 4096, 5120, 5376, 5632
OFF_CB, OFF_CC, OFF_CX, OFF_CG = 6656, 7680, 8704, 9728
OFF_MA, OFF_MB, OFF_MC = 10752, 12800, 14848

ADAM_LR = 0.001
ADAM_B1 = 0.9
ADAM_B2 = 0.999
ADAM_EPS = 1e-08
ADAM_WD = 0.01
ADAM_STEP = 10

N_CHIPS = 4
VMEM_LIMIT_BYTES = 48 * 1024 * 1024
EXP_CLAMP = 80.0
MASK_VALUE = -1e30
MESH = pl.DeviceIdType.MESH


def _cp(sem=None):
    return pltpu.CompilerParams(dimension_semantics=sem, vmem_limit_bytes=VMEM_LIMIT_BYTES)


def _tile(dim, pref):
    return pref if dim % pref == 0 else dim


def _sds(shape, dtype):
    return jax.ShapeDtypeStruct(shape, dtype)


def _call(body, *, name, grid, in_specs, out_specs, out_shape, args, scratch_shapes=(), sem=None, plan=None):
    in_specs, out_specs, out_shape = list(in_specs), list(out_specs), list(out_shape)
    if plan is None:
        outs = pl.pallas_call(body, name=name, grid=grid, in_specs=in_specs, out_specs=out_specs, out_shape=out_shape,
                              scratch_shapes=list(scratch_shapes), compiler_params=_cp(sem))(*args)
        return list(outs), []
    ni, no, ns = len(in_specs), len(out_specs), len(scratch_shapes)
    pi, po = len(plan.inputs), len(plan.out_shapes)

    def carrier(*refs):
        c_in, p_in = refs[:ni], refs[ni:ni + pi]
        c_out, p_out = refs[ni + pi:ni + pi + no], refs[ni + pi + no:ni + pi + no + po]
        c_scr = refs[ni + pi + no + po:ni + pi + no + po + ns]
        send_sems, recv_sems = refs[-2], refs[-1]
        ids = [pl.program_id(a) for a in range(len(grid))]
        first = functools.reduce(jnp.logical_and, [i == 0 for i in ids])
        last = functools.reduce(jnp.logical_and, [i == n - 1 for i, n in zip(ids, grid)])

        @pl.when(first)
        def _():
            for cp in plan.make(p_in, p_out, send_sems, recv_sems, 0):
                cp.start()

        body(*c_in, *c_out, *c_scr)

        @pl.when(last)
        def _():
            for cp in plan.make(p_in, p_out, send_sems, recv_sems, 0):
                cp.wait()

    any_spec = pl.BlockSpec(memory_space=pl.ANY)
    outs = pl.pallas_call(
        carrier, name=name, grid=grid,
        in_specs=in_specs + [any_spec] * pi, out_specs=out_specs + [any_spec] * po,
        out_shape=out_shape + list(plan.out_shapes),
        scratch_shapes=list(scratch_shapes) + [pltpu.SemaphoreType.DMA((plan.n,)), pltpu.SemaphoreType.DMA((plan.n,))],
        input_output_aliases={ni + k: no + v for k, v in plan.aliases.items()},
        compiler_params=_cp(tuple("arbitrary" for _ in grid)),
    )(*args, *plan.inputs)
    return list(outs[:no]), list(outs[no:])


_DIMS = {
    "nn": (((1,), (0,)), ((), ())),
    "nt": (((1,), (1,)), ((), ())),
    "tn": (((0,), (0,)), ((), ())),
}


def _dot_raw(a, b, mode):
    return lax.dot_general(a.astype(MXU_DTYPE), b.astype(MXU_DTYPE), _DIMS[mode], preferred_element_type=F32)


@functools.partial(jax.custom_vjp, nondiff_argnums=(2,))
def _dot(a, b, mode):
    return _dot_raw(a, b, mode)


def _dot_fwd(a, b, mode):
    return _dot_raw(a, b, mode), (a, b)


def _dot_bwd(mode, res, g):
    a, b = res
    if mode == "nn":
        return _dot_raw(g, b, "nt"), _dot_raw(a, g, "tn")
    if mode == "nt":
        return _dot_raw(g, b, "nn"), _dot_raw(g, a, "tn")
    return _dot_raw(b, g, "nt"), _dot_raw(a, g, "nn")


_dot.defvjp(_dot_fwd, _dot_bwd)


def _matmul(a, b, *, mode, tm, tn, tk, name, a_idx=None, b_idx=None, out_dtype=F32, add=None, add_scale=1.0, plan=None):
    a2, b2 = a.shape[-2:], b.shape[-2:]
    if mode == "nn":
        (M, K), (K2, N) = a2, b2
    elif mode == "nt":
        (M, K), (N, K2) = a2, b2
    else:
        (K, M), (K2, N) = a2, b2
    assert K == K2, (a.shape, b.shape, mode)
    tm, tn, tk = _tile(M, tm), _tile(N, tn), _tile(K, tk)
    nk = K // tk

    a_blk = (tk, tm) if mode == "tn" else (tm, tk)
    b_blk = (tn, tk) if mode == "nt" else (tk, tn)

    def a_map(i, j, k):
        ij = (k, i) if mode == "tn" else (i, k)
        return ij if a_idx is None else (a_idx,) + ij

    def b_map(i, j, k):
        ij = (j, k) if mode == "nt" else (k, j)
        return ij if b_idx is None else (b_idx,) + ij

    in_specs = [
        pl.BlockSpec(a_blk if a_idx is None else (None,) + a_blk, a_map),
        pl.BlockSpec(b_blk if b_idx is None else (None,) + b_blk, b_map),
    ]
    args = [a, b]
    if add is not None:
        in_specs.append(pl.BlockSpec((tm, tn), lambda i, j, k: (i, j)))
        args.append(add)
    n_in = len(args)

    def body(*refs):
        a_ref, b_ref = refs[0], refs[1]
        o_ref = refs[n_in]
        p = _dot_raw(a_ref[...], b_ref[...], mode)

        def finish(val):
            if add is not None:
                val = val + add_scale * refs[2][...]
            o_ref[...] = val.astype(out_dtype)

        if nk == 1:
            finish(p)
        else:
            acc_ref = refs[n_in + 1]
            k = pl.program_id(2)

            @pl.when(k == 0)
            def _():
                acc_ref[...] = p

            @pl.when(k > 0)
            def _():
                acc_ref[...] += p

            @pl.when(k == nk - 1)
            def _():
                finish(acc_ref[...])

    (out,), plan_outs = _call(
        body,
        name=name,
        grid=(M // tm, N // tn, nk),
        in_specs=in_specs,
        out_specs=[pl.BlockSpec((tm, tn), lambda i, j, k: (i, j))],
        out_shape=[_sds((M, N), out_dtype)],
        scratch_shapes=[pltpu.VMEM((tm, tn), F32)] if nk > 1 else [],
        sem=("parallel", "parallel", "arbitrary"),
        args=args,
        plan=plan,
    )
    return out if plan is None else (out, plan_outs)


def _scan_rows(x, reverse):
    n = x.shape[0]
    row = lax.broadcasted_iota(jnp.int32, x.shape, 0)
    s = 1
    while s < n:
        if reverse:
            x = x + jnp.where(row < n - s, pltpu.roll(x, n - s, 0), 0.0)
        else:
            x = x + jnp.where(row >= s, pltpu.roll(x, s, 0), 0.0)
        s *= 2
    return x


@jax.custom_vjp
def _cumsum_rows(x):
    return _scan_rows(x, False)


_cumsum_rows.defvjp(lambda x: (_scan_rows(x, False), None), lambda _, g: (_scan_rows(g, True),))


def _hgrn_chunk(state_t, qraw, fraw, v, lb):
    c = HGRN_CHUNK
    q = qraw * jax.nn.sigmoid(qraw) * (HGRN_HEAD_DIM ** -0.5)
    f = lb + (1.0 - lb) * jax.nn.sigmoid(fraw)
    k = 1.0 - f
    g = jnp.log(f)
    b = _cumsum_rows(g)
    row = lax.broadcasted_iota(jnp.int32, (c, HGRN_HEAD_DIM), 0)
    b_end = jnp.sum(g, axis=0, keepdims=True)
    b_mid = jnp.sum(jnp.where(row < c // 2, g, 0.0), axis=0, keepdims=True)
    inter = _dot(q * jnp.exp(b), state_t, "nt")
    qt = q * jnp.exp(jnp.minimum(b - b_mid, EXP_CLAMP))
    kt = k * jnp.exp(jnp.minimum(b_mid - b, EXP_CLAMP))
    s = _dot(qt, kt, "nt")
    ti = lax.broadcasted_iota(jnp.int32, (c, c), 0)
    si = lax.broadcasted_iota(jnp.int32, (c, c), 1)
    s = jnp.where(si <= ti, s, 0.0)
    intra = _dot(s, v, "nn")
    k_end = k * jnp.exp(b_end - b)
    new_state_t = state_t * jnp.exp(b_end) + _dot(v, k_end, "tn")
    return new_state_t, inter + intra


def _hgrn_specs(T):
    rows = _tile(T, 512)
    return rows, T // rows, rows // HGRN_CHUNK


def _hgrn_fwd(u, lower_l):
    T = u.shape[0]
    rows, nblk, ncr = _hgrn_specs(T)
    hb = HGRN_WIDTH // HGRN_HEAD_DIM

    def body(q_ref, f_ref, i_ref, lb_ref, o_ref, st_ref, state):
        @pl.when(pl.program_id(1) == 0)
        def _():
            state[...] = jnp.zeros_like(state)

        lb = lb_ref[...]
        for c in range(ncr):
            rs = pl.ds(c * HGRN_CHUNK, HGRN_CHUNK)
            st = state[...]
            st_ref[c] = st
            new, out = _hgrn_chunk(st, q_ref[rs, :], f_ref[rs, :], i_ref[rs, :], lb)
            state[...] = new
            o_ref[rs, :] = out

    blk = (rows, HGRN_HEAD_DIM)
    return pl.pallas_call(
        body,
        name="hgrn_fwd",
        grid=(HGRN_HEADS, nblk),
        in_specs=[
            pl.BlockSpec(blk, lambda h, r: (r, OFF_AQ // 128 + h)),
            pl.BlockSpec(blk, lambda h, r: (r, OFF_AF // 128 + h)),
            pl.BlockSpec(blk, lambda h, r: (r, OFF_AI // 128 + h)),
            pl.BlockSpec((1, HGRN_HEAD_DIM), lambda h, r: (0, h)),
        ],
        out_specs=[
            pl.BlockSpec(blk, lambda h, r: (r, h)),
            pl.BlockSpec((ncr, None, HGRN_HEAD_DIM, HGRN_HEAD_DIM), lambda h, r: (r, h, 0, 0)),
        ],
        out_shape=[_sds((T, HGRN_WIDTH), F32), _sds((T // HGRN_CHUNK, hb, HGRN_HEAD_DIM, HGRN_HEAD_DIM), F32)],
        scratch_shapes=[pltpu.VMEM((HGRN_HEAD_DIM, HGRN_HEAD_DIM), F32)],
        compiler_params=_cp(("parallel", "arbitrary")),
    )(u, u, u, lower_l)


def _hgrn_bwd(u, lower_l, states, do_raw):
    T = u.shape[0]
    rows, nblk, ncr = _hgrn_specs(T)

    def body(q_ref, f_ref, i_ref, lb_ref, st_ref, do_ref, dq_ref, df_ref, di_ref, dlb_ref, dstate):
        @pl.when(pl.program_id(1) == 0)
        def _():
            dstate[...] = jnp.zeros_like(dstate)
            dlb_ref[...] = jnp.zeros_like(dlb_ref)

        lb = lb_ref[...]
        for c in reversed(range(ncr)):
            rs = pl.ds(c * HGRN_CHUNK, HGRN_CHUNK)
            _, vjp = jax.vjp(_hgrn_chunk, st_ref[c], q_ref[rs, :], f_ref[rs, :], i_ref[rs, :], lb)
            dst, dq, df, dv, dlb = vjp((dstate[...], do_ref[rs, :]))
            dstate[...] = dst
            dq_ref[rs, :] = dq.astype(BF16)
            df_ref[rs, :] = df.astype(BF16)
            di_ref[rs, :] = dv.astype(BF16)
            dlb_ref[...] += dlb

    blk = (rows, HGRN_HEAD_DIM)
    last = nblk - 1
    out_blk = pl.BlockSpec(blk, lambda h, r: (last - r, h))
    return pl.pallas_call(
        body,
        name="hgrn_bwd",
        grid=(HGRN_HEADS, nblk),
        in_specs=[
            pl.BlockSpec(blk, lambda h, r: (last - r, OFF_AQ // 128 + h)),
            pl.BlockSpec(blk, lambda h, r: (last - r, OFF_AF // 128 + h)),
            pl.BlockSpec(blk, lambda h, r: (last - r, OFF_AI // 128 + h)),
            pl.BlockSpec((1, HGRN_HEAD_DIM), lambda h, r: (0, h)),
            pl.BlockSpec((ncr, None, HGRN_HEAD_DIM, HGRN_HEAD_DIM), lambda h, r: (last - r, h, 0, 0)),
            out_blk,
        ],
        out_specs=[out_blk, out_blk, out_blk, pl.BlockSpec((1, HGRN_HEAD_DIM), lambda h, r: (0, h))],
        out_shape=[_sds((T, HGRN_WIDTH), BF16)] * 3 + [_sds((1, HGRN_WIDTH), F32)],
        scratch_shapes=[pltpu.VMEM((HGRN_HEAD_DIM, HGRN_HEAD_DIM), F32)],
        compiler_params=_cp(("parallel", "arbitrary")),
    )(u, u, u, lower_l, states, do_raw)


QROWS = ATTN_GROUP * WINDOW


def _attn_block(q, kp, kc, vp, vc, bp, bc, sink, mp, mc):
    scale = ATTN_HEAD_DIM ** -0.5
    sp = jnp.where(mp, _dot(q, kp, "nt") * scale + bp, MASK_VALUE)
    sc = jnp.where(mc, _dot(q, kc, "nt") * scale + bc, MASK_VALUE)
    m = jnp.maximum(jnp.maximum(jnp.max(sp, axis=-1, keepdims=True), jnp.max(sc, axis=-1, keepdims=True)), sink)
    m = lax.stop_gradient(m)
    pp = jnp.exp(sp - m)
    pc = jnp.exp(sc - m)
    den = jnp.sum(pp, axis=-1, keepdims=True) + jnp.sum(pc, axis=-1, keepdims=True) + jnp.exp(sink - m)
    inv = 1.0 / den
    return _dot(pp * inv, vp, "nn") + _dot(pc * inv, vc, "nn")


def _attn_masks(first_block):
    i = lax.broadcasted_iota(jnp.int32, (QROWS, WINDOW), 0) % WINDOW
    j = lax.broadcasted_iota(jnp.int32, (QROWS, WINDOW), 1)
    return (j > i) & jnp.logical_not(first_block), j <= i


def _attn_in_specs():
    q_spec = pl.BlockSpec((WINDOW, ATTN_WIDTH), lambda n: (n, OFF_BQ // ATTN_WIDTH))
    k_cur = pl.BlockSpec((WINDOW, KV_WIDTH), lambda n: (n, OFF_BK // KV_WIDTH))
    k_prev = pl.BlockSpec((WINDOW, KV_WIDTH), lambda n: (jnp.maximum(n - 1, 0), OFF_BK // KV_WIDTH))
    v_cur = pl.BlockSpec((WINDOW, KV_WIDTH), lambda n: (n, OFF_BV // KV_WIDTH))
    v_prev = pl.BlockSpec((WINDOW, KV_WIDTH), lambda n: (jnp.maximum(n - 1, 0), OFF_BV // KV_WIDTH))
    bias = pl.BlockSpec((ATTN_KV_HEADS, QROWS, 2 * WINDOW), lambda n: (0, 0, 0))
    sink = pl.BlockSpec((ATTN_KV_HEADS, QROWS, 1), lambda n: (0, 0, 0))
    return [q_spec, k_prev, k_cur, v_prev, v_cur, bias, sink]


def _head_cols(a):
    return slice(a * ATTN_HEAD_DIM, (a + 1) * ATTN_HEAD_DIM)


def _group_rows(ref, h):
    return jnp.concatenate([ref[:, _head_cols(ATTN_GROUP * h + g)] for g in range(ATTN_GROUP)], axis=0)


def _attn_fwd(u, bias4, sink4):
    T = u.shape[0]

    def body(q_ref, kp_ref, kc_ref, vp_ref, vc_ref, b_ref, s_ref, o_ref):
        mp, mc = _attn_masks(pl.program_id(0) == 0)
        for h in range(ATTN_KV_HEADS):
            hs = _head_cols(h)
            o = _attn_block(_group_rows(q_ref, h), kp_ref[:, hs], kc_ref[:, hs], vp_ref[:, hs], vc_ref[:, hs],
                            b_ref[h, :, :WINDOW], b_ref[h, :, WINDOW:], s_ref[h], mp, mc)
            for g in range(ATTN_GROUP):
                o_ref[:, _head_cols(ATTN_GROUP * h + g)] = o[g * WINDOW:(g + 1) * WINDOW]

    return pl.pallas_call(
        body,
        name="attn_fwd",
        grid=(T // WINDOW,),
        in_specs=_attn_in_specs(),
        out_specs=pl.BlockSpec((WINDOW, ATTN_WIDTH), lambda n: (n, 0)),
        out_shape=_sds((T, ATTN_WIDTH), F32),
        compiler_params=_cp(("parallel",)),
    )(u, u, u, u, u, bias4, sink4)


def _attn_bwd(u, bias4, sink4, do, plan=None):
    T = u.shape[0]
    act = pl.BlockSpec((WINDOW, ATTN_WIDTH), lambda n: (n, 0))
    kv = pl.BlockSpec((WINDOW, KV_WIDTH), lambda n: (n, 0))
    in_specs = _attn_in_specs()
    bias, sink = in_specs[5], in_specs[6]

    def body(q_ref, kp_ref, kc_ref, vp_ref, vc_ref, b_ref, s_ref, do_ref,
             dq_ref, dkc_ref, dkp_ref, dvc_ref, dvp_ref, db_ref, ds_ref):
        n = pl.program_id(0)
        mp, mc = _attn_masks(n == 0)

        @pl.when(n == 0)
        def _():
            db_ref[...] = jnp.zeros_like(db_ref)
            ds_ref[...] = jnp.zeros_like(ds_ref)

        dqs, dkps, dkcs, dvps, dvcs = [], [], [], [], []
        for h in range(ATTN_KV_HEADS):
            hs = _head_cols(h)
            _, vjp = jax.vjp(
                functools.partial(_attn_block, mp=mp, mc=mc),
                _group_rows(q_ref, h), kp_ref[:, hs], kc_ref[:, hs], vp_ref[:, hs], vc_ref[:, hs],
                b_ref[h, :, :WINDOW], b_ref[h, :, WINDOW:], s_ref[h])
            dq, dkp, dkc, dvp, dvc, dbp, dbc, dsink = vjp(_group_rows(do_ref, h))
            dqs += [dq[g * WINDOW:(g + 1) * WINDOW] for g in range(ATTN_GROUP)]
            dkps.append(dkp)
            dkcs.append(dkc)
            dvps.append(dvp)
            dvcs.append(dvc)
            db_ref[h, :, :WINDOW] += dbp
            db_ref[h, :, WINDOW:] += dbc
            ds_ref[h] += dsink
        dq_ref[...] = jnp.concatenate(dqs, axis=1).astype(BF16)
        dkc_ref[...] = jnp.concatenate(dkcs, axis=1)
        dkp_ref[...] = jnp.concatenate(dkps, axis=1)
        dvc_ref[...] = jnp.concatenate(dvcs, axis=1)
        dvp_ref[...] = jnp.concatenate(dvps, axis=1)

    kv_sds = _sds((T, KV_WIDTH), F32)
    return _call(
        body,
        name="attn_bwd",
        grid=(T // WINDOW,),
        in_specs=in_specs + [act],
        out_specs=[act, kv, kv, kv, kv, bias, sink],
        out_shape=[_sds((T, ATTN_WIDTH), BF16), kv_sds, kv_sds, kv_sds, kv_sds,
                   _sds((ATTN_KV_HEADS, QROWS, 2 * WINDOW), F32), _sds((ATTN_KV_HEADS, QROWS, 1), F32)],
        sem=("arbitrary",),
        args=(u, u, u, u, u, bias4, sink4, do),
        plan=plan,
    )


def _with_next_block_part(cur, nxt):
    pad = jnp.zeros_like(nxt[:WINDOW])
    return (cur + jnp.concatenate([nxt[WINDOW:], pad], axis=0)).astype(BF16)


MIX_COLS = 512


def _silu(x):
    return x * jax.nn.sigmoid(x)


def _silu_grad(x):
    s = jax.nn.sigmoid(x)
    return s * (1.0 + x * (1.0 - s))


def _shift_rows_down(h, first, second):
    n = h.shape[0]
    row = lax.broadcasted_iota(jnp.int32, h.shape, 0)
    s1 = jnp.where(row == 0, first, pltpu.roll(h, 1, 0))
    s2 = jnp.where(row == 0, second, jnp.where(row == 1, first, pltpu.roll(h, 2, 0)))
    del n
    return s1, s2


def _shift_rows_up(h, first, second):
    n = h.shape[0]
    row = lax.broadcasted_iota(jnp.int32, h.shape, 0)
    s1 = jnp.where(row == n - 1, first, pltpu.roll(h, n - 1, 0))
    s2 = jnp.where(row == n - 1, second, jnp.where(row == n - 2, first, pltpu.roll(h, n - 2, 0)))
    return s1, s2


def _mix_rows(T):
    return _tile(T, 256)


def _mix_fwd(u, o_raw, o_b, gn_l, cw_l):
    T = u.shape[0]
    tr = _mix_rows(T)
    nrow = T // tr
    hr = tr // 8

    def ucol(off):
        return pl.BlockSpec((tr, MIX_COLS), lambda i, j, off=off: (i, off // MIX_COLS + j))

    def uprev(off):
        return pl.BlockSpec((8, MIX_COLS), lambda i, j, off=off: (jnp.maximum(i * hr - 1, 0), off // MIX_COLS + j))

    act = pl.BlockSpec((tr, MIX_COLS), lambda i, j: (i, j))
    par = lambda rows: pl.BlockSpec((rows, MIX_COLS), lambda i, j: (0, j))

    def body(oraw_ref, ag_ref, ob_ref, bg_ref, cb_ref, cc_ref, cx_ref, cg_ref, ccp_ref, cxp_ref, gn_ref, cw_ref,
             ha_ref, hb_ref, hc_ref, hat_ref, hbt_ref, hct_ref):
        ag = _silu(ag_ref[...])
        for h in range(MIX_COLS // HGRN_HEAD_DIM):
            cs = slice(h * HGRN_HEAD_DIM, (h + 1) * HGRN_HEAD_DIM)
            o = oraw_ref[:, cs]
            nrm = o * lax.rsqrt(jnp.mean(o * o, axis=-1, keepdims=True) + RMS_EPS)
            ha = nrm * gn_ref[:, cs] * ag[:, cs]
            ha_ref[:, cs] = ha.astype(BF16)
            hat_ref[cs, :] = ha.T.astype(BF16)
        hb = ob_ref[...] * _silu(bg_ref[...])
        hb_ref[...] = hb.astype(BF16)
        hbt_ref[...] = hb.T.astype(BF16)
        keep = (pl.program_id(0) > 0).astype(F32)
        hcur = cc_ref[...] * cx_ref[...]
        p1 = ccp_ref[7:8, :] * cxp_ref[7:8, :] * keep
        p2 = ccp_ref[6:7, :] * cxp_ref[6:7, :] * keep
        s1, s2 = _shift_rows_down(hcur, p1, p2)
        y = cw_ref[0:1, :] * s2 + cw_ref[1:2, :] * s1 + cw_ref[2:3, :] * hcur
        hc = cb_ref[...] * y * _silu(cg_ref[...])
        hc_ref[...] = hc.astype(BF16)
        hct_ref[...] = hc.T.astype(BF16)

    out = _sds((T, HGRN_WIDTH), BF16)
    out_t = _sds((HGRN_WIDTH, T), BF16)
    act_t = pl.BlockSpec((MIX_COLS, tr), lambda i, j: (j, i))
    return pl.pallas_call(
        body,
        name="mix_fwd",
        grid=(nrow, HGRN_WIDTH // MIX_COLS),
        in_specs=[act, ucol(OFF_AG), act, ucol(OFF_BG), ucol(OFF_CB), ucol(OFF_CC), ucol(OFF_CX), ucol(OFF_CG),
                  uprev(OFF_CC), uprev(OFF_CX), par(1), par(CONV_K)],
        out_specs=[act, act, act, act_t, act_t, act_t],
        out_shape=[out, out, out, out_t, out_t, out_t],
        compiler_params=_cp(("parallel", "parallel")),
    )(o_raw, u, o_b, u, u, u, u, u, u, u, gn_l, cw_l)


def _mix_bwd(u, o_raw, o_b, gn_l, cw_l, dha, dhb, dhc):
    T = u.shape[0]
    tr = _mix_rows(T)
    nrow = T // tr
    hr = tr // 8
    last_halo = T // 8 - 1

    def ucol(off):
        return pl.BlockSpec((tr, MIX_COLS), lambda j, i, off=off: (i, off // MIX_COLS + j))

    def uprev(off):
        return pl.BlockSpec((8, MIX_COLS), lambda j, i, off=off: (jnp.maximum(i * hr - 1, 0), off // MIX_COLS + j))

    def unext(off):
        return pl.BlockSpec((8, MIX_COLS), lambda j, i, off=off: (jnp.minimum((i + 1) * hr, last_halo), off // MIX_COLS + j))

    act = pl.BlockSpec((tr, MIX_COLS), lambda j, i: (i, j))
    act_next = pl.BlockSpec((8, MIX_COLS), lambda j, i: (jnp.minimum((i + 1) * hr, last_halo), j))
    par = lambda rows: pl.BlockSpec((rows, MIX_COLS), lambda j, i: (0, j))

    def body(oraw_ref, ag_ref, ob_ref, bg_ref, cb_ref, cc_ref, cx_ref, cg_ref, ccp_ref, cxp_ref,
             cbn_ref, cgn_ref, dhcn_ref, gn_ref, cw_ref, dha_ref, dhb_ref, dhc_ref,
             doraw_ref, dob_ref, dag_ref, dbg_ref, dcb_ref, dcc_ref, dcx_ref, dcg_ref, dgn_ref, dcw_ref):
        i = pl.program_id(1)

        @pl.when(i == 0)
        def _():
            dgn_ref[...] = jnp.zeros_like(dgn_ref)
            dcw_ref[...] = jnp.zeros_like(dcw_ref)

        ag = ag_ref[...]
        sag = _silu(ag)
        dha = dha_ref[...]
        for h in range(MIX_COLS // HGRN_HEAD_DIM):
            cs = slice(h * HGRN_HEAD_DIM, (h + 1) * HGRN_HEAD_DIM)
            o = oraw_ref[:, cs]
            rs = lax.rsqrt(jnp.mean(o * o, axis=-1, keepdims=True) + RMS_EPS)
            nrm = o * rs
            gn = gn_ref[:, cs]
            d = dha[:, cs]
            dag_ref[:, cs] = (d * nrm * gn * _silu_grad(ag[:, cs])).astype(BF16)
            dgn_ref[:, cs] += jnp.sum(d * sag[:, cs] * nrm, axis=0, keepdims=True)
            dn = d * sag[:, cs] * gn
            doraw_ref[:, cs] = rs * (dn - nrm * jnp.mean(dn * nrm, axis=-1, keepdims=True))
        bg = bg_ref[...]
        dhb = dhb_ref[...]
        dob_ref[...] = dhb * _silu(bg)
        dbg_ref[...] = (dhb * ob_ref[...] * _silu_grad(bg)).astype(BF16)
        keep_prev = (i > 0).astype(F32)
        keep_next = (i < nrow - 1).astype(F32)
        cc, cx, cb, cg = cc_ref[...], cx_ref[...], cb_ref[...], cg_ref[...]
        hcur = cc * cx
        p1 = ccp_ref[7:8, :] * cxp_ref[7:8, :] * keep_prev
        p2 = ccp_ref[6:7, :] * cxp_ref[6:7, :] * keep_prev
        s1, s2 = _shift_rows_down(hcur, p1, p2)
        w0, w1, w2 = cw_ref[0:1, :], cw_ref[1:2, :], cw_ref[2:3, :]
        y = w0 * s2 + w1 * s1 + w2 * hcur
        dhc = dhc_ref[...]
        scg = _silu(cg)
        doc = dhc * scg
        dcg_ref[...] = (dhc * cb * y * _silu_grad(cg)).astype(BF16)
        dcb_ref[...] = (doc * y).astype(BF16)
        dy = doc * cb
        n1 = dhcn_ref[0:1, :] * _silu(cgn_ref[0:1, :]) * cbn_ref[0:1, :] * keep_next
        n2 = dhcn_ref[1:2, :] * _silu(cgn_ref[1:2, :]) * cbn_ref[1:2, :] * keep_next
        u1, u2 = _shift_rows_up(dy, n1, n2)
        dh = w2 * dy + w1 * u1 + w0 * u2
        dcc_ref[...] = (dh * cx).astype(BF16)
        dcx_ref[...] = (dh * cc).astype(BF16)
        dcw_ref[0:1, :] += jnp.sum(dy * s2, axis=0, keepdims=True)
        dcw_ref[1:2, :] += jnp.sum(dy * s1, axis=0, keepdims=True)
        dcw_ref[2:3, :] += jnp.sum(dy * hcur, axis=0, keepdims=True)

    f32o, bf = _sds((T, HGRN_WIDTH), F32), _sds((T, HGRN_WIDTH), BF16)
    return pl.pallas_call(
        body,
        name="mix_bwd",
        grid=(HGRN_WIDTH // MIX_COLS, nrow),
        in_specs=[act, ucol(OFF_AG), act, ucol(OFF_BG), ucol(OFF_CB), ucol(OFF_CC), ucol(OFF_CX), ucol(OFF_CG),
                  uprev(OFF_CC), uprev(OFF_CX), unext(OFF_CB), unext(OFF_CG), act_next, par(1), par(CONV_K),
                  act, act, act],
        out_specs=[act, act, act, act, act, act, act, act, par(1), par(8)],
        out_shape=[f32o, f32o, bf, bf, bf, bf, bf, bf, _sds((1, HGRN_WIDTH), F32), _sds((8, HGRN_WIDTH), F32)],
        compiler_params=_cp(("parallel", "arbitrary")),
    )(o_raw, u, o_b, u, u, u, u, u, u, u, u, u, dhc, gn_l, cw_l, dha, dhb, dhc)


def _merge_specs(T, order):
    tr = _tile(T, 256)

    def ucol(off):
        if order == "ij":
            return pl.BlockSpec((tr, MIX_COLS), lambda i, j, off=off: (i, off // MIX_COLS + j))
        return pl.BlockSpec((tr, MIX_COLS), lambda j, i, off=off: (i, off // MIX_COLS + j))

    act = pl.BlockSpec((tr, MIX_COLS), (lambda i, j: (i, j)) if order == "ij" else (lambda j, i: (i, j)))
    return tr, ucol, act


def _merge_fwd(u, ya, yb, yc):
    T = u.shape[0]
    tr, ucol, act = _merge_specs(T, "ij")

    def body(ma_ref, mb_ref, mc_ref, ya_ref, yb_ref, yc_ref, o_ref, ot_ref):
        merged = (jax.nn.sigmoid(ma_ref[...]) * ya_ref[...] + jax.nn.sigmoid(mb_ref[...]) * yb_ref[...]
                  + jax.nn.sigmoid(mc_ref[...]) * yc_ref[...])
        o_ref[...] = merged.astype(BF16)
        ot_ref[...] = merged.T.astype(BF16)

    return pl.pallas_call(
        body,
        name="merge_fwd",
        grid=(T // tr, D_MODEL // MIX_COLS),
        in_specs=[ucol(OFF_MA), ucol(OFF_MB), ucol(OFF_MC), act, act, act],
        out_specs=[act, pl.BlockSpec((MIX_COLS, tr), lambda i, j: (j, i))],
        out_shape=[_sds((T, D_MODEL), BF16), _sds((D_MODEL, T), BF16)],
        compiler_params=_cp(("parallel", "parallel")),
    )(u, u, u, ya, yb, yc)


def _merge_bwd(u, ya, yb, yc, dmerged, plan=None):
    T = u.shape[0]
    tr, ucol, act = _merge_specs(T, "ij")

    def body(ma_ref, mb_ref, mc_ref, ya_ref, yb_ref, yc_ref, dm_ref, dya_ref, dyb_ref, dyc_ref, dma_ref, dmb_ref, dmc_ref):
        dm = dm_ref[...]
        for m_ref, y_ref, dy_ref, dg_ref in ((ma_ref, ya_ref, dya_ref, dma_ref), (mb_ref, yb_ref, dyb_ref, dmb_ref),
                                             (mc_ref, yc_ref, dyc_ref, dmc_ref)):
            s = jax.nn.sigmoid(m_ref[...])
            dy_ref[...] = (dm * s).astype(BF16)
            dg_ref[...] = (dm * y_ref[...] * s * (1.0 - s)).astype(BF16)

    out = _sds((T, D_MODEL), BF16)
    return _call(
        body,
        name="merge_bwd",
        grid=(T // tr, D_MODEL // MIX_COLS),
        in_specs=[ucol(OFF_MA), ucol(OFF_MB), ucol(OFF_MC), act, act, act, act],
        out_specs=[act] * 6,
        out_shape=[out] * 6,
        sem=("parallel", "parallel"),
        args=(u, u, u, ya, yb, yc, dmerged),
        plan=plan,
    )


def _ln_fwd(x, y, g_l, b_l):
    T = x.shape[0]
    tr = _tile(T, 256)
    row = pl.BlockSpec((tr, D_MODEL), lambda i: (i, 0))
    col = pl.BlockSpec((D_MODEL, tr), lambda i: (0, i))
    par = pl.BlockSpec((1, D_MODEL), lambda i: (0, 0))

    def body(x_ref, y_ref, g_ref, b_ref, o_ref, z_ref, ob_ref, ot_ref):
        z = ALPHA * x_ref[...] + y_ref[...]
        z_ref[...] = z
        mu = jnp.mean(z, axis=-1, keepdims=True)
        zc = z - mu
        var = jnp.mean(zc * zc, axis=-1, keepdims=True)
        o = zc * lax.rsqrt(var + LN_EPS) * g_ref[...] + b_ref[...]
        o_ref[...] = o
        ob_ref[...] = o.astype(BF16)
        ot_ref[...] = o.T.astype(BF16)

    return pl.pallas_call(
        body,
        name="ln_fwd",
        grid=(T // tr,),
        in_specs=[row, row, par, par],
        out_specs=[row, row, row, col],
        out_shape=[_sds((T, D_MODEL), F32)] * 2 + [_sds((T, D_MODEL), BF16), _sds((D_MODEL, T), BF16)],
        compiler_params=_cp(("parallel",)),
    )(x, y, g_l, b_l)


def _operand_forms(x):
    T = x.shape[0]
    tr = _tile(T, 256)
    row = pl.BlockSpec((tr, D_MODEL), lambda i: (i, 0))
    col = pl.BlockSpec((D_MODEL, tr), lambda i: (0, i))

    def body(x_ref, xb_ref, xt_ref):
        xv = x_ref[...]
        xb_ref[...] = xv.astype(BF16)
        xt_ref[...] = xv.T.astype(BF16)

    return pl.pallas_call(
        body,
        name="operand_forms",
        grid=(T // tr,),
        in_specs=[row],
        out_specs=[row, col],
        out_shape=[_sds((T, D_MODEL), BF16), _sds((D_MODEL, T), BF16)],
        compiler_params=_cp(("parallel",)),
    )(x)


def _ln_bwd(z, dxn, g_l):
    T = z.shape[0]
    tr = _tile(T, 256)
    row = pl.BlockSpec((tr, D_MODEL), lambda i: (i, 0))
    par = pl.BlockSpec((1, D_MODEL), lambda i: (0, 0))

    def body(z_ref, d_ref, g_ref, dz_ref, dzb_ref, dg_ref, db_ref):
        @pl.when(pl.program_id(0) == 0)
        def _():
            dg_ref[...] = jnp.zeros_like(dg_ref)
            db_ref[...] = jnp.zeros_like(db_ref)

        z = z_ref[...]
        d = d_ref[...]
        mu = jnp.mean(z, axis=-1, keepdims=True)
        zc = z - mu
        rstd = lax.rsqrt(jnp.mean(zc * zc, axis=-1, keepdims=True) + LN_EPS)
        zh = zc * rstd
        dg_ref[...] += jnp.sum(d * zh, axis=0, keepdims=True)
        db_ref[...] += jnp.sum(d, axis=0, keepdims=True)
        dh = d * g_ref[...]
        dz = rstd * (dh - jnp.mean(dh, axis=-1, keepdims=True) - zh * jnp.mean(dh * zh, axis=-1, keepdims=True))
        dz_ref[...] = dz
        dzb_ref[...] = dz.astype(BF16)

    return pl.pallas_call(
        body,
        name="ln_bwd",
        grid=(T // tr,),
        in_specs=[row, row, par],
        out_specs=[row, row, par, par],
        out_shape=[_sds((T, D_MODEL), F32), _sds((T, D_MODEL), BF16), _sds((1, D_MODEL), F32), _sds((1, D_MODEL), F32)],
        compiler_params=_cp(("arbitrary",)),
    )(z, dxn, g_l)


def _loss_head(y, target):
    T = y.shape[0]
    tr = _tile(T, 256)
    row = pl.BlockSpec((tr, D_MODEL), lambda i: (i, 0))
    acc = pl.BlockSpec((8, 128), lambda i: (0, 0))

    def body(y_ref, t_ref, l_ref, d_ref):
        @pl.when(pl.program_id(0) == 0)
        def _():
            l_ref[...] = jnp.zeros_like(l_ref)

        err = y_ref[...] - t_ref[...]
        d_ref[...] = err * (1.0 / D_MODEL)
        part = 0.5 * jnp.sum(jnp.sum(err * err, axis=-1, keepdims=True) * (1.0 / D_MODEL), axis=0, keepdims=True)
        r = lax.broadcasted_iota(jnp.int32, (8, 128), 0)
        c = lax.broadcasted_iota(jnp.int32, (8, 128), 1)
        l_ref[...] += jnp.where((r == 0) & (c == 0), part, 0.0)

    return pl.pallas_call(
        body,
        name="loss_head",
        grid=(T // tr,),
        in_specs=[row, row],
        out_specs=[acc, row],
        out_shape=[_sds((8, 128), F32), _sds((T, D_MODEL), F32)],
        compiler_params=_cp(("arbitrary",)),
    )(y, target)


ADAMW_BLOCK_ELEMS = 256 * 1024


def _adamw(w, g, m, v, name):
    shape = w.shape
    cols = shape[-1]
    rows = math.prod(shape[:-1])
    flat = lambda a: a.reshape(rows, cols)
    if rows * cols <= ADAMW_BLOCK_ELEMS or rows % 8:
        tr = rows
    else:
        tr = 8
        while rows % (tr * 2) == 0 and tr * 2 * cols <= ADAMW_BLOCK_ELEMS:
            tr *= 2
    blk = pl.BlockSpec((tr, cols), lambda i: (i, 0))
    c1 = 1.0 - ADAM_B1 ** ADAM_STEP
    c2 = 1.0 - ADAM_B2 ** ADAM_STEP

    def body(w_ref, g_ref, m_ref, v_ref, d_ref, nm_ref, nv_ref):
        gg = g_ref[...]
        nm = ADAM_B1 * m_ref[...] + (1.0 - ADAM_B1) * gg
        nv = ADAM_B2 * v_ref[...] + (1.0 - ADAM_B2) * (gg * gg)
        nm_ref[...] = nm
        nv_ref[...] = nv
        d_ref[...] = -ADAM_LR * ((nm / c1) / (jnp.sqrt(nv / c2) + ADAM_EPS) + ADAM_WD * w_ref[...])

    outs = pl.pallas_call(
        body,
        name=name,
        grid=(rows // tr,),
        in_specs=[blk] * 4,
        out_specs=[blk] * 3,
        out_shape=[_sds((rows, cols), F32)] * 3,
        compiler_params=_cp(("parallel",)),
    )(flat(w), flat(g), flat(m), flat(v))
    return tuple(o.reshape(shape) for o in outs)


def _t5_bucket(dist):
    max_exact = N_BUCKETS // 2
    logd = jnp.log(jnp.maximum(dist, 1).astype(F32) / max_exact) / math.log(MAX_DISTANCE / max_exact)
    large = jnp.minimum(max_exact + (logd * (N_BUCKETS - max_exact)).astype(jnp.int32), N_BUCKETS - 1)
    return jnp.where(dist < max_exact, dist, large)


def _band_bias(rel_bias):
    i = jnp.arange(WINDOW)[:, None]
    j = jnp.arange(2 * WINDOW)[None, :]
    bucket = _t5_bucket(jnp.clip(WINDOW + i - j, 0, WINDOW - 1))
    onehot = (bucket[:, :, None] == jnp.arange(N_BUCKETS)[None, None, :]).astype(F32)
    return jnp.einsum("ijb,bh->hij", onehot, rel_bias.astype(F32), precision=lax.Precision.HIGHEST)


def _lower_bounds(lb_param):
    soft = jax.nn.softmax(lb_param.astype(F32), axis=0)
    return jnp.cumsum(soft, axis=0) - soft[0:1]


def _mm_rows(T):
    return _tile(T, 1024)


def _layer_fwd(xs, ex, layer, lower_l, bias4, sink4_l, gn_l, cw_l, lng_l, lnb_l):
    x, xb, xt = xs
    T = x.shape[0]
    tm = _mm_rows(T)
    w = ex.weights[layer]
    w_in_l, w_proj_l, w_out_l = w["w_in"][0], w["w_proj"], w["w_out"][0]
    plan = ex.fwd_plan(layer)
    u = _matmul(xb, w_in_l, mode="nn", tm=tm, tn=1536, tk=D_MODEL, name="mm_u", plan=plan)
    if plan is not None:
        u, landed = u
        ex.fwd_landed(layer, landed)
    o_raw, states = _hgrn_fwd(u, lower_l)
    o_b = _attn_fwd(u, bias4, sink4_l)
    ha, hb, hc, hat, hbt, hct = _mix_fwd(u, o_raw, o_b, gn_l, cw_l)
    ys = [_matmul(h, w_proj_l, mode="nn", tm=tm, tn=1024, tk=HGRN_WIDTH, b_idx=i, name="mm_proj")
          for i, h in enumerate((ha, hb, hc))]
    merged, merged_t = _merge_fwd(u, *ys)
    y = _matmul(merged, w_out_l, mode="nn", tm=tm, tn=1024, tk=D_MODEL, name="mm_out")
    xn, z, xnb, xnt = _ln_fwd(x, y, lng_l, lnb_l)
    saved = dict(xt=xt, u=u, o_raw=o_raw, states=states, o_b=o_b, hts=(hat, hbt, hct), ys=ys,
                 merged_t=merged_t, z=z)
    return (xn, xnb, xnt), saved


def _layer_bwd(dxn, s, ex, layer, lower_l, bias4, sink4_l, gn_l, cw_l, lng_l):
    T = dxn.shape[0]
    tm = _mm_rows(T)
    u = s["u"]
    w = ex.weights[layer]
    w_in_l, w_proj_l, w_out_l = w["w_in"][0], w["w_proj"], w["w_out"][0]
    dz, dzb, d_lng, d_lnb = _ln_bwd(s["z"], dxn, lng_l)
    dmerged = _matmul(dzb, w_out_l, mode="nt", tm=tm, tn=1024, tk=D_MODEL, name="mm_dmerged")
    g_w_out = _matmul(s["merged_t"], dzb, mode="nn", tm=1024, tn=1024, tk=2048, name="mm_gw_out")
    plan = ex.pair_plan(layer)
    (*dys, dma, dmb, dmc), landed = _merge_bwd(u, *s["ys"], dmerged, plan=plan)
    if plan is not None:
        ex.pair_landed(layer, landed)
    dhs =[_matmul(dy, w_proj_l, mode="nt", tm=tm, tn=1024, tk=D_MODEL, b_idx=i, name="mm_dh") for i, dy in enumerate(dys)]
    g_w_proj = jnp.stack([_matmul(ht, dy, mode="nn", tm=1024, tn=1024, tk=2048, name="mm_gw_proj")
                          for ht, dy in zip(s["hts"], dys)])
    do_raw, do_b, dag, dbg, dcb, dcc, dcx, dcg, d_gn, d_cw = _mix_bwd(u, s["o_raw"], s["o_b"], gn_l, cw_l, *dhs)
    daq, daf, dai, d_lower = _hgrn_bwd(u, lower_l, s["states"], do_raw)
    plan = ex.slab_plan(layer)
    (dbq, dkc, dkp, dvc, dvp, d_bias4, d_sink4), landed = _attn_bwd(u, bias4, sink4_l, do_b, plan=plan)
    if plan is not None:
        ex.slab_landed(layer, landed)
    dbk = _with_next_block_part(dkc, dkp)
    dbv = _with_next_block_part(dvc, dvp)
    du = jnp.concatenate([daq, daf, dai, dag, dbq, dbk, dbv, dbg, dcb, dcc, dcx, dcg, dma, dmb, dmc], axis=1)
    dx = _matmul(du, w_in_l, mode="nt", tm=_tile(T, 512), tn=1024, tk=5632, name="mm_dx", add=dz, add_scale=ALPHA)
    g_w_in = _matmul(s["xt"], du, mode="nn", tm=1024, tn=768, tk=4096, name="mm_gw_in")
    d_sinks = jnp.sum(d_sink4.reshape(ATTN_HEADS, WINDOW), axis=-1)
    small = dict(lower=d_lower[0], gn=d_gn[0], sinks=d_sinks, cw=d_cw[:CONV_K], bias=d_bias4.reshape(ATTN_HEADS, WINDOW, 2 * WINDOW),
                 lng=d_lng[0], lnb=d_lnb[0])
    ex.grads_ready(layer, dict(w_in=g_w_in[None], w_proj=g_w_proj, w_out=g_w_out[None]))
    return dx, small


def _local_step(x, target, ex, lb_param, hgrn_norm_g, attn_sinks, conv_w_full, rel_bias, ln_g, ln_b):
    lower, lower_vjp = jax.vjp(_lower_bounds, lb_param)
    bias, bias_vjp = jax.vjp(_band_bias, rel_bias)
    bias4 = bias.reshape(ATTN_KV_HEADS, QROWS, 2 * WINDOW)
    sink4 = jnp.broadcast_to(attn_sinks.reshape(DEPTH, ATTN_HEADS, 1, 1), (DEPTH, ATTN_HEADS, WINDOW, 1)).reshape(
        DEPTH, ATTN_KV_HEADS, QROWS, 1)
    row = lambda a, l: a[l:l + 1]
    saved = []
    hs = (x, *_operand_forms(x))
    for l in range(DEPTH):
        hs, s = _layer_fwd(hs, ex, l, row(lower, l), bias4, sink4[l], row(hgrn_norm_g, l), conv_w_full[l], row(ln_g, l), row(ln_b, l))
        saved.append(s)
    loss_blk, dh = _loss_head(hs[0], target)
    smalls = [None] * DEPTH
    for l in reversed(range(DEPTH)):
        dh, smalls[l] = _layer_bwd(dh, saved[l], ex, l, row(lower, l), bias4, sink4[l], row(hgrn_norm_g, l), conv_w_full[l], row(ln_g, l))
    stack = lambda k: jnp.stack([sm[k] for sm in smalls])
    d_bias = smalls[0]["bias"] + smalls[1]["bias"] + smalls[2]["bias"] + smalls[3]["bias"]
    small = dict(
        lb_param=lower_vjp(stack("lower"))[0], hgrn_norm_g=stack("gn"), attn_sinks=stack("sinks"), conv_w=stack("cw"),
        rel_bias=bias_vjp(d_bias)[0], ln_g=stack("lng"), ln_b=stack("lnb"))
    return loss_blk, dh, small


ANY = pl.BlockSpec(memory_space=pl.ANY)
DMA_SEM = pltpu.SemaphoreType.DMA


def _coords():
    return lax.axis_index("x"), lax.axis_index("y"), lax.axis_index("c")


def _other_chips(x, y):
    return [(1 - x, y), (x, 1 - y), (1 - x, 1 - y)]


def _remote(src, dst, send_sem, recv_sem, device):
    return pltpu.make_async_remote_copy(src_ref=src, dst_ref=dst, send_sem=send_sem, recv_sem=recv_sem,
                                        device_id=device, device_id_type=MESH)


def _sub(ref, axis, index, size):
    idx = [slice(None)] * len(ref.shape)
    idx[axis] = pl.ds(pl.multiple_of(index * size, size), size)
    return ref.at[tuple(idx)]


class _Plan:
    def __init__(self, inputs, out_shapes, aliases, n, make):
        self.inputs, self.out_shapes, self.aliases, self.n, self.make = tuple(inputs), tuple(out_shapes), dict(aliases), n, make


class _Xfer:
    def __init__(self, send, recv=None):
        self.send, self.recv = send, send if recv is None else recv

    def start(self):
        self.send.start()

    def wait(self):
        self.send.wait_send()
        self.recv.wait_recv()


def _merge_plans(plans):
    def make(in_refs, out_refs, send_sems, recv_sems, base):
        out, i0, o0, b0 = [], 0, 0, base
        for p in plans:
            out += p.make(in_refs[i0:i0 + len(p.inputs)], out_refs[o0:o0 + len(p.out_shapes)], send_sems, recv_sems, b0)
            i0, o0, b0 = i0 + len(p.inputs), o0 + len(p.out_shapes), b0 + p.n
        return out

    inputs, out_shapes, aliases = [], [], {}
    for p in plans:
        aliases.update({len(inputs) + k: len(out_shapes) + v for k, v in p.aliases.items()})
        inputs += p.inputs
        out_shapes += p.out_shapes
    return _Plan(inputs, out_shapes, aliases, sum(p.n for p in plans), make)


def _run_plan(plan, name):
    ni, no = len(plan.inputs), len(plan.out_shapes)

    def body(*refs):
        transfers = plan.make(refs[:ni], refs[ni:ni + no], refs[ni + no], refs[ni + no + 1], 0)
        for t in transfers:
            t.start()
        for t in transfers:
            t.wait()

    outs = pl.pallas_call(
        body, name=name, in_specs=[ANY] * ni, out_specs=[ANY] * no, out_shape=list(plan.out_shapes),
        input_output_aliases=plan.aliases, scratch_shapes=[DMA_SEM((plan.n,)), DMA_SEM((plan.n,))],
    )(*plan.inputs)
    return list(outs)


def _gather_send_plan(shards, layer, sax):
    shp = shards.shape[1:]
    hax = 3 - sax
    w, hw = shp[sax], shp[hax] // 2
    out_shape = list(shp)
    out_shape[sax] = w * N_CHIPS

    def make(in_refs, out_refs, send_sems, recv_sems, base):
        (src_ref,), (out_ref,) = in_refs, out_refs
        x, y, c = _coords()
        j = 2 * x + y
        src = src_ref.at[layer]
        own = pltpu.make_async_copy(src, _sub(out_ref, sax, j, w), send_sems.at[base])
        dst = _sub(_sub(out_ref, sax, j, w), hax, c, hw)
        return [own] + [
            _Xfer(_remote(_sub(src, hax, c, hw), dst, send_sems.at[base + 1 + k], recv_sems.at[base + 1 + k], (px, py, c)))
            for k, (px, py) in enumerate(_other_chips(x, y))]

    return _Plan([shards], [_sds(tuple(out_shape), shards.dtype)], {}, 4, make)


def _gather_pass_plan(full, sax):
    hax = 3 - sax
    w, hw = full.shape[sax] // N_CHIPS, full.shape[hax] // 2

    def make(in_refs, out_refs, send_sems, recv_sems, base):
        (out_ref,) = out_refs
        x, y, c = _coords()
        region = lambda slab, half: _sub(_sub(out_ref, sax, slab, w), hax, half, hw)
        out = []
        for k, (px, py) in enumerate(_other_chips(x, y)):
            mine, theirs = region(2 * px + py, c), region(2 * px + py, 1 - c)
            sems = send_sems.at[base + k], recv_sems.at[base + k]
            out.append(_Xfer(_remote(mine, mine, *sems, (x, y, 1 - c)), _remote(theirs, theirs, *sems, (x, y, c))))
        return out

    return _Plan([full], [_sds(full.shape, full.dtype)], {0: 0}, 3, make)


def _pair_exchange_plan(g, hax):
    hw = g.shape[hax] // 2
    out_shape = list(g.shape)
    out_shape[hax] = hw

    def make(in_refs, out_refs, send_sems, recv_sems, base):
        x, y, c = _coords()
        return [_Xfer(_remote(_sub(in_refs[0], hax, 1 - c, hw), out_refs[0], send_sems.at[base], recv_sems.at[base], (x, y, 1 - c)))]

    return _Plan([g], [_sds(tuple(out_shape), g.dtype)], {}, 1, make)


def _add_own_half(place, g, recv, hax, blk, name):
    L, ah, bh = recv.shape
    tr, tc = blk
    nr, nc = ah // tr, bh // tc

    def g_map(l, i, jc, p):
        return (l, i + p[0] * nr, jc) if hax == 1 else (l, i, jc + p[0] * nc)

    def body(p_ref, g_ref, r_ref, o_ref):
        o_ref[...] = (g_ref[...] + r_ref[...]).astype(BF16)

    same = pl.BlockSpec((None, tr, tc), lambda l, i, jc, p: (l, i, jc))
    return pl.pallas_call(
        body,
        name=name,
        grid_spec=pltpu.PrefetchScalarGridSpec(
            num_scalar_prefetch=1, grid=(L, nr, nc),
            in_specs=[pl.BlockSpec((None, tr, tc), g_map), same], out_specs=same),
        out_shape=_sds(recv.shape, BF16),
        compiler_params=_cp(("parallel", "parallel", "parallel")),
    )(place, g, recv)


def _slab_exchange_plan(p, sax):
    w = p.shape[sax] // N_CHIPS
    slab_shape = list(p.shape)
    slab_shape[sax] = w

    def make(in_refs, out_refs, send_sems, recv_sems, base):
        x, y, c = _coords()
        return [_Xfer(_remote(_sub(in_refs[0], sax, 2 * px + py, w), out_refs[0].at[k], send_sems.at[base + k],
                              recv_sems.at[base + k], (px, py, c)))
                for k, (px, py) in enumerate(_other_chips(x, y))]

    return _Plan([p], [_sds((3, *slab_shape), p.dtype)], {}, 3, make)


def _add_slabs(place, g, pair, recv, sax, blk, name):
    hax = 3 - sax
    _, L, a, b = recv.shape
    tr, tc = blk
    nr, nc = a // tr, b // tc

    def g_map(l, i, jc, p):
        return (l, p[0] * nr + i, p[1] * nc + jc) if hax == 1 else (l, p[1] * nr + i, p[0] * nc + jc)

    def pair_map(l, i, jc, p):
        return (l, i, p[1] * nc + jc) if hax == 1 else (l, p[1] * nr + i, jc)

    def out_map(l, i, jc, p):
        return (l, p[0] * nr + i, jc) if hax == 1 else (l, i, p[0] * nc + jc)

    def body(p_ref, g_ref, pair_ref, r0_ref, r1_ref, r2_ref, o_ref):
        own = g_ref[...] + pair_ref[...]
        o_ref[...] = ((own + r0_ref[...].astype(F32)) + r1_ref[...].astype(F32)) + r2_ref[...].astype(F32)

    def rk(k):
        return pl.BlockSpec((None, None, tr, tc), lambda l, i, jc, p, k=k: (k, l, i, jc))

    out_shape = [L, a, b]
    out_shape[hax] *= 2
    blk3 = (None, tr, tc)
    return pl.pallas_call(
        body,
        name=name,
        grid_spec=pltpu.PrefetchScalarGridSpec(
            num_scalar_prefetch=1, grid=(L, nr, nc),
            in_specs=[pl.BlockSpec(blk3, g_map), pl.BlockSpec(blk3, pair_map), rk(0), rk(1), rk(2)],
            out_specs=pl.BlockSpec(blk3, out_map)),
        out_shape=_sds(tuple(out_shape), F32),
        compiler_params=_cp(("parallel", "parallel", "parallel")),
    )(place, g, pair, recv, recv, recv)


def _pair_assemble_plan(r, hax):
    hw = r.shape[hax] // 2

    def make(in_refs, out_refs, send_sems, recv_sems, base):
        x, y, c = _coords()
        mine, other = _sub(out_refs[0], hax, c, hw), _sub(out_refs[0], hax, 1 - c, hw)
        sems = send_sems.at[base], recv_sems.at[base]
        return [_Xfer(_remote(mine, mine, *sems, (x, y, 1 - c)), _remote(other, other, *sems, (x, y, c)))]

    return _Plan([r], [_sds(r.shape, r.dtype)], {0: 0}, 1, make)


CLASSES = dict(w_in=(2, (128, 4224), (128, 4224)), w_proj=(2, (256, 2048), (512, 512)), w_out=(1, (256, 1024), (256, 1024)))


class _Exchanges:
    def __init__(self, place, shards):
        self.place, self.shards = place, shards
        self.weights, self.pending, self.pair, self.reduced = {}, {}, {}, {}

    def _pass_on(self, bufs, tag):
        plans = [_gather_pass_plan(b, CLASSES[k][0]) for k, b in zip(CLASSES, bufs)]
        return dict(zip(CLASSES, _run_plan(_merge_plans(plans), "gather_pass_" + tag)))

    def gather_send_plan(self, layer):
        return _merge_plans([_gather_send_plan(self.shards[k], layer, CLASSES[k][0]) for k in CLASSES])

    def first_weights(self):
        self.weights[0] = self._pass_on(_run_plan(self.gather_send_plan(0), "gather_send_0"), "0")

    def fwd_plan(self, layer):
        return self.gather_send_plan(layer + 1) if layer + 1 < DEPTH else None

    def fwd_landed(self, layer, bufs):
        self.weights[layer + 1] = self._pass_on(bufs, str(layer + 1))

    def grads_ready(self, layer, grads):
        self.pending[layer] = grads

    def pair_plan(self, layer):
        if layer + 1 not in self.pending:
            return None
        g = self.pending[layer + 1]
        return _merge_plans([_pair_exchange_plan(g[k], 3 - CLASSES[k][0]) for k in CLASSES])

    def pair_landed(self, layer, bufs):
        self.pair[layer + 1] = dict(zip(CLASSES, bufs))

    def slab_plan(self, layer):
        src = layer + 1
        if src not in self.pair:
            return None
        g, pair = self.pending[src], self.pair[src]
        sums = [_add_own_half(self.place, g[k], pair[k], 3 - CLASSES[k][0], CLASSES[k][1], f"rs_pair_add_{k}_{src}") for k in CLASSES]
        return _merge_plans([_slab_exchange_plan(p, CLASSES[k][0]) for k, p in zip(CLASSES, sums)])

    def slab_landed(self, layer, bufs):
        src = layer + 1
        g, pair = self.pending.pop(src), self.pair.pop(src)
        halves = [_add_slabs(self.place, g[k], pair[k], r, CLASSES[k][0], CLASSES[k][2], f"rs_slab_add_{k}_{src}")
                  for k, r in zip(CLASSES, bufs)]
        plans = [_pair_assemble_plan(h, 3 - CLASSES[k][0]) for k, h in zip(CLASSES, halves)]
        self.reduced[src] = dict(zip(CLASSES, _run_plan(_merge_plans(plans), f"rs_assemble_{src}")))

    def finish(self):
        self.pair_landed(-1, _run_plan(self.pair_plan(-1), "rs_pair_0"))
        self.slab_landed(-1, _run_plan(self.slab_plan(-1), "rs_slab_0"))
        return [self.reduced[l] for l in range(DEPTH)]


N_DEV = 8


def _all_reduce_small(v, name):
    rows = v.shape[0]

    def body(v_ref, gath_ref, sum_ref, send_sems, recv_sems, local_sem):
        x, y, c = _coords()
        me, sib = (x, y, c), (x, y, 1 - c)
        chips = _other_chips(x, y)

        def slot(px, py, pc):
            return gath_ref.at[pl.ds(pl.multiple_of((4 * px + 2 * py + pc) * rows, rows), rows), :]

        def copy(k, block, to, src=None):
            return _remote(slot(*block) if src is None else src, slot(*block), send_sems.at[k], recv_sems.at[k], to)

        mine = pltpu.make_async_copy(v_ref, slot(*me), local_sem)
        mine.start()
        first = [copy(0, me, sib, src=v_ref)] + [copy(1 + k, me, (*chip, c), src=v_ref) for k, chip in enumerate(chips)]
        for cp in first:
            cp.start()
        passed = [copy(4 + k, (*chip, c), sib) for k, chip in enumerate(chips)]
        for k, chip in enumerate(chips):
            copy(1 + k, (*chip, c), me).wait_recv()
            passed[k].start()
        copy(0, sib, me).wait_recv()
        for k, chip in enumerate(chips):
            copy(4 + k, (*chip, 1 - c), me).wait_recv()
        for cp in first + passed:
            cp.wait_send()
        mine.wait()
        acc = gath_ref[0:rows, :]
        for d in range(1, N_DEV):
            acc = acc + gath_ref[d * rows:(d + 1) * rows, :]
        sum_ref[...] = acc

    vm = pl.BlockSpec(memory_space=pltpu.VMEM)
    return pl.pallas_call(
        body, name=name, in_specs=[vm], out_specs=[vm, vm],
        out_shape=[_sds((N_DEV * rows, 128), F32), _sds((rows, 128), F32)],
        scratch_shapes=[DMA_SEM((7,)), DMA_SEM((7,)), DMA_SEM(())],
    )(v)[1]


def _pad_rows(a):
    flat = a.reshape(-1).astype(F32)
    rows = -(-flat.shape[0] // (8 * 128)) * 8
    return jnp.pad(flat, (0, rows * 128 - flat.shape[0])).reshape(rows, 128)


def _sum_over_devices(parts, name):
    blocks = [_pad_rows(a) for a in parts.values()]
    total = _all_reduce_small(jnp.concatenate(blocks, axis=0), name)
    out, r0 = {}, 0
    for (key, a), blk in zip(parts.items(), blocks):
        out[key] = total[r0:r0 + blk.shape[0]].reshape(-1)[:a.size].reshape(a.shape)
        r0 += blk.shape[0]
    return out


def kernel(x, w_in, w_proj_hgrn, w_proj_attn, w_proj_conv, w_out, lb_param, hgrn_norm_g, attn_sinks, conv_w, rel_bias, ln_g, ln_b, loss_target, m_w_in, m_w_proj_hgrn, m_w_proj_attn, m_w_proj_conv, m_w_out, m_lb_param, m_hgrn_norm_g, m_attn_sinks, m_conv_w, m_rel_bias, m_ln_g, m_ln_b, v_w_in, v_w_proj_hgrn, v_w_proj_attn, v_w_proj_conv, v_w_out, v_lb_param, v_hgrn_norm_g, v_attn_sinks, v_conv_w, v_rel_bias, v_ln_g, v_ln_b):
    xi, yi, ci = _coords()
    slab = 2 * xi + yi
    place = jnp.stack([ci, slab]).astype(jnp.int32)
    conv_cols = conv_w.shape[-1]

    w_in_b = w_in.astype(BF16)[:, None]
    w_proj_b = jnp.stack([w_proj_hgrn, w_proj_attn, w_proj_conv], axis=1).astype(BF16)
    w_out_b = w_out.astype(BF16)[:, None]
    ex = _Exchanges(place, dict(w_in=w_in_b, w_proj=w_proj_b, w_out=w_out_b))
    ex.first_weights()
    conv_spread = lax.dynamic_update_slice(jnp.zeros((DEPTH, CONV_K, CONV_WIDTH), F32), conv_w, (0, 0, slab * conv_cols))
    conv_full = 0.5 * _sum_over_devices({"conv_w": conv_spread}, "gather_conv_w")["conv_w"]

    loss_blk, dx, small = _local_step(x[0], loss_target[0], ex, lb_param, hgrn_norm_g, attn_sinks, conv_full, rel_bias, ln_g, ln_b)

    reduced = ex.finish()
    g_w_in = jnp.stack([r["w_in"][0] for r in reduced])
    g_w_proj = jnp.stack([r["w_proj"] for r in reduced])
    g_w_out = jnp.stack([r["w_out"][0] for r in reduced])
    small = dict(small, loss=loss_blk[0:1, 0:1])
    small = _sum_over_devices(small, "sum_small")
    loss = small["loss"][0, 0]
    g_conv = lax.dynamic_slice(small["conv_w"], (0, 0, slab * conv_cols), (DEPTH, CONV_K, conv_cols))

    grads = [g_w_in, g_w_proj[:, 0], g_w_proj[:, 1], g_w_proj[:, 2], g_w_out, small["lb_param"], small["hgrn_norm_g"],
             small["attn_sinks"], g_conv, small["rel_bias"], small["ln_g"], small["ln_b"]]
    names = ["w_in", "w_proj_hgrn", "w_proj_attn", "w_proj_conv", "w_out", "lb_param", "hgrn_norm_g", "attn_sinks",
             "conv_w", "rel_bias", "ln_g", "ln_b"]
    ws = [w_in, w_proj_hgrn, w_proj_attn, w_proj_conv, w_out, lb_param, hgrn_norm_g, attn_sinks, conv_w, rel_bias, ln_g, ln_b]
    ms = [m_w_in, m_w_proj_hgrn, m_w_proj_attn, m_w_proj_conv, m_w_out, m_lb_param, m_hgrn_norm_g, m_attn_sinks, m_conv_w,
          m_rel_bias, m_ln_g, m_ln_b]
    vs = [v_w_in, v_w_proj_hgrn, v_w_proj_attn, v_w_proj_conv, v_w_out, v_lb_param, v_hgrn_norm_g, v_attn_sinks, v_conv_w,
          v_rel_bias, v_ln_g, v_ln_b]
    upd = [_adamw(w, g, m, v, "adamw_" + n) for n, w, g, m, v in zip(names, ws, grads, ms, vs)]
    deltas, new_ms, new_vs = zip(*upd)
    return (loss, dx[None], *grads, *deltas, *new_ms, *new_vs)
```

```python
import functools
import math

import jax
import jax.numpy as jnp
from jax import lax
from jax.experimental import pallas as pl
from jax.experimental.pallas import tpu as pltpu

F32 = jnp.float32
BF16 = jnp.bfloat16
MXU_DTYPE = BF16

D_MODEL = 2048
DEPTH = 4
HGRN_WIDTH = 1024
HGRN_HEAD_DIM = 128
HGRN_HEADS = 8
HGRN_CHUNK = 64
ATTN_HEAD_DIM = 64
ATTN_HEADS = 16
ATTN_KV_HEADS = 4
ATTN_GROUP = ATTN_HEADS // ATTN_KV_HEADS
ATTN_WIDTH = 1024
KV_WIDTH = 256
WINDOW = 128
CONV_WIDTH = 1024
CONV_K = 3
N_BUCKETS = 32
MAX_DISTANCE = 128
ALPHA = (2.0 * DEPTH) ** 0.25
LN_EPS = 1e-5
RMS_EPS = 1e-6
N_IN = 16896
OFF_AQ, OFF_AF, OFF_AI, OFF_AG = 0, 1024, 2048, 3072
OFF_BQ, OFF_BK, OFF_BV, OFF_BG = 4096, 5120, 5376, 5632
OFF_CB, OFF_CC, OFF_CX, OFF_CG = 6656, 7680, 8704, 9728
OFF_MA, OFF_MB, OFF_MC = 10752, 12800, 14848

ADAM_LR = 0.001
ADAM_B1 = 0.9
ADAM_B2 = 0.999
ADAM_EPS = 1e-08
ADAM_WD = 0.01
ADAM_STEP = 10

N_CHIPS = 4
VMEM_LIMIT_BYTES = 48 * 1024 * 1024
EXP_CLAMP = 80.0
MASK_VALUE = -1e30
MESH = pl.DeviceIdType.MESH


def _cp(sem=None):
    return pltpu.CompilerParams(dimension_semantics=sem, vmem_limit_bytes=VMEM_LIMIT_BYTES)


def _tile(dim, pref):
    return pref if dim % pref == 0 else dim


def _sds(shape, dtype):
    return jax.ShapeDtypeStruct(shape, dtype)


def _call(body, *, name, grid, in_specs, out_specs, out_shape, args, scratch_shapes=(), sem=None, plan=None):
    in_specs, out_specs, out_shape = list(in_specs), list(out_specs), list(out_shape)
    if plan is None:
        outs = pl.pallas_call(body, name=name, grid=grid, in_specs=in_specs, out_specs=out_specs, out_shape=out_shape,
                              scratch_shapes=list(scratch_shapes), compiler_params=_cp(sem))(*args)
        return list(outs), []
    ni, no, ns = len(in_specs), len(out_specs), len(scratch_shapes)
    pi, po = len(plan.inputs), len(plan.out_shapes)

    def carrier(*refs):
        c_in, p_in = refs[:ni], refs[ni:ni + pi]
        c_out, p_out = refs[ni + pi:ni + pi + no], refs[ni + pi + no:ni + pi + no + po]
        c_scr = refs[ni + pi + no + po:ni + pi + no + po + ns]
        send_sems, recv_sems = refs[-2], refs[-1]
        ids = [pl.program_id(a) for a in range(len(grid))]
        first = functools.reduce(jnp.logical_and, [i == 0 for i in ids])
        last = functools.reduce(jnp.logical_and, [i == n - 1 for i, n in zip(ids, grid)])

        @pl.when(first)
        def _():
            for cp in plan.make(p_in, p_out, send_sems, recv_sems, 0):
                cp.start()

        body(*c_in, *c_out, *c_scr)

        @pl.when(last)
        def _():
            for cp in plan.make(p_in, p_out, send_sems, recv_sems, 0):
                cp.wait()

    any_spec = pl.BlockSpec(memory_space=pl.ANY)
    outs = pl.pallas_call(
        carrier, name=name, grid=grid,
        in_specs=in_specs + [any_spec] * pi, out_specs=out_specs + [any_spec] * po,
        out_shape=out_shape + list(plan.out_shapes),
        scratch_shapes=list(scratch_shapes) + [pltpu.SemaphoreType.DMA((plan.n,)), pltpu.SemaphoreType.DMA((plan.n,))],
        input_output_aliases={ni + k: no + v for k, v in plan.aliases.items()},
        compiler_params=_cp(tuple("arbitrary" for _ in grid)),
    )(*args, *plan.inputs)
    return list(outs[:no]), list(outs[no:])


_DIMS = {
    "nn": (((1,), (0,)), ((), ())),
    "nt": (((1,), (1,)), ((), ())),
    "tn": (((0,), (0,)), ((), ())),
}


def _dot_raw(a, b, mode):
    return lax.dot_general(a.astype(MXU_DTYPE), b.astype(MXU_DTYPE), _DIMS[mode], preferred_element_type=F32)


@functools.partial(jax.custom_vjp, nondiff_argnums=(2,))
def _dot(a, b, mode):
    return _dot_raw(a, b, mode)


def _dot_fwd(a, b, mode):
    return _dot_raw(a, b, mode), (a, b)


def _dot_bwd(mode, res, g):
    a, b = res
    if mode == "nn":
        return _dot_raw(g, b, "nt"), _dot_raw(a, g, "tn")
    if mode == "nt":
        return _dot_raw(g, b, "nn"), _dot_raw(g, a, "tn")
    return _dot_raw(b, g, "nt"), _dot_raw(a, g, "nn")


_dot.defvjp(_dot_fwd, _dot_bwd)


def _matmul(a, b, *, mode, tm, tn, tk, name, a_idx=None, b_idx=None, out_dtype=F32, add=None, add_scale=1.0, plan=None):
    a2, b2 = a.shape[-2:], b.shape[-2:]
    if mode == "nn":
        (M, K), (K2, N) = a2, b2
    elif mode == "nt":
        (M, K), (N, K2) = a2, b2
    else:
        (K, M), (K2, N) = a2, b2
    assert K == K2, (a.shape, b.shape, mode)
    tm, tn, tk = _tile(M, tm), _tile(N, tn), _tile(K, tk)
    nk = K // tk

    a_blk = (tk, tm) if mode == "tn" else (tm, tk)
    b_blk = (tn, tk) if mode == "nt" else (tk, tn)

    def a_map(i, j, k):
        ij = (k, i) if mode == "tn" else (i, k)
        return ij if a_idx is None else (a_idx,) + ij

    def b_map(i, j, k):
        ij = (j, k) if mode == "nt" else (k, j)
        return ij if b_idx is None else (b_idx,) + ij

    in_specs = [
        pl.BlockSpec(a_blk if a_idx is None else (None,) + a_blk, a_map),
        pl.BlockSpec(b_blk if b_idx is None else (None,) + b_blk, b_map),
    ]
    args = [a, b]
    if add is not None:
        in_specs.append(pl.BlockSpec((tm, tn), lambda i, j, k: (i, j)))
        args.append(add)
    n_in = len(args)

    def body(*refs):
        a_ref, b_ref = refs[0], refs[1]
        o_ref = refs[n_in]
        p = _dot_raw(a_ref[...], b_ref[...], mode)

        def finish(val):
            if add is not None:
                val = val + add_scale * refs[2][...]
            o_ref[...] = val.astype(out_dtype)

        if nk == 1:
            finish(p)
        else:
            acc_ref = refs[n_in + 1]
            k = pl.program_id(2)

            @pl.when(k == 0)
            def _():
                acc_ref[...] = p

            @pl.when(k > 0)
            def _():
                acc_ref[...] += p

            @pl.when(k == nk - 1)
            def _():
                finish(acc_ref[...])

    (out,), plan_outs = _call(
        body,
        name=name,
        grid=(M // tm, N // tn, nk),
        in_specs=in_specs,
        out_specs=[pl.BlockSpec((tm, tn), lambda i, j, k: (i, j))],
        out_shape=[_sds((M, N), out_dtype)],
        scratch_shapes=[pltpu.VMEM((tm, tn), F32)] if nk > 1 else [],
        sem=("parallel", "parallel", "arbitrary"),
        args=args,
        plan=plan,
    )
    return out if plan is None else (out, plan_outs)


def _scan_rows(x, reverse):
    n = x.shape[0]
    row = lax.broadcasted_iota(jnp.int32, x.shape, 0)
    s = 1
    while s < n:
        if reverse:
            x = x + jnp.where(row < n - s, pltpu.roll(x, n - s, 0), 0.0)
        else:
            x = x + jnp.where(row >= s, pltpu.roll(x, s, 0), 0.0)
        s *= 2
    return x


@jax.custom_vjp
def _cumsum_rows(x):
    return _scan_rows(x, False)


_cumsum_rows.defvjp(lambda x: (_scan_rows(x, False), None), lambda _, g: (_scan_rows(g, True),))


def _hgrn_chunk(state_t, qraw, fraw, v, lb):
    c = HGRN_CHUNK
    q = qraw * jax.nn.sigmoid(qraw) * (HGRN_HEAD_DIM ** -0.5)
    f = lb + (1.0 - lb) * jax.nn.sigmoid(fraw)
    k = 1.0 - f
    g = jnp.log(f)
    b = _cumsum_rows(g)
    row = lax.broadcasted_iota(jnp.int32, (c, HGRN_HEAD_DIM), 0)
    b_end = jnp.sum(g, axis=0, keepdims=True)
    b_mid = jnp.sum(jnp.where(row < c // 2, g, 0.0), axis=0, keepdims=True)
    inter = _dot(q * jnp.exp(b), state_t, "nt")
    qt = q * jnp.exp(jnp.minimum(b - b_mid, EXP_CLAMP))
    kt = k * jnp.exp(jnp.minimum(b_mid - b, EXP_CLAMP))
    s = _dot(qt, kt, "nt")
    ti = lax.broadcasted_iota(jnp.int32, (c, c), 0)
    si = lax.broadcasted_iota(jnp.int32, (c, c), 1)
    s = jnp.where(si <= ti, s, 0.0)
    intra = _dot(s, v, "nn")
    k_end = k * jnp.exp(b_end - b)
    new_state_t = state_t * jnp.exp(b_end) + _dot(v, k_end, "tn")
    return new_state_t, inter + intra


def _hgrn_specs(T):
    rows = _tile(T, 512)
    return rows, T // rows, rows // HGRN_CHUNK


def _hgrn_fwd(u, lower_l, plan=None):
    T = u.shape[0]
    rows, nblk, ncr = _hgrn_specs(T)
    hb = HGRN_WIDTH // HGRN_HEAD_DIM

    def body(q_ref, f_ref, i_ref, lb_ref, o_ref, st_ref, state):
        @pl.when(pl.program_id(1) == 0)
        def _():
            state[...] = jnp.zeros_like(state)

        lb = lb_ref[...]
        for c in range(ncr):
            rs = pl.ds(c * HGRN_CHUNK, HGRN_CHUNK)
            st = state[...]
            st_ref[c] = st
            new, out = _hgrn_chunk(st, q_ref[rs, :], f_ref[rs, :], i_ref[rs, :], lb)
            state[...] = new
            o_ref[rs, :] = out

    blk = (rows, HGRN_HEAD_DIM)
    return _call(
        body,
        name="hgrn_fwd",
        grid=(HGRN_HEADS, nblk),
        in_specs=[
            pl.BlockSpec(blk, lambda h, r: (r, OFF_AQ // 128 + h)),
            pl.BlockSpec(blk, lambda h, r: (r, OFF_AF // 128 + h)),
            pl.BlockSpec(blk, lambda h, r: (r, OFF_AI // 128 + h)),
            pl.BlockSpec((1, HGRN_HEAD_DIM), lambda h, r: (0, h)),
        ],
        out_specs=[
            pl.BlockSpec(blk, lambda h, r: (r, h)),
            pl.BlockSpec((ncr, None, HGRN_HEAD_DIM, HGRN_HEAD_DIM), lambda h, r: (r, h, 0, 0)),
        ],
        out_shape=[_sds((T, HGRN_WIDTH), F32), _sds((T // HGRN_CHUNK, hb, HGRN_HEAD_DIM, HGRN_HEAD_DIM), F32)],
        scratch_shapes=[pltpu.VMEM((HGRN_HEAD_DIM, HGRN_HEAD_DIM), F32)],
        sem=("parallel", "arbitrary"),
        args=(u, u, u, lower_l),
        plan=plan,
    )


def _hgrn_bwd(u, lower_l, states, do_raw):
    T = u.shape[0]
    rows, nblk, ncr = _hgrn_specs(T)

    def body(q_ref, f_ref, i_ref, lb_ref, st_ref, do_ref, dq_ref, df_ref, di_ref, dlb_ref, dstate):
        @pl.when(pl.program_id(1) == 0)
        def _():
            dstate[...] = jnp.zeros_like(dstate)
            dlb_ref[...] = jnp.zeros_like(dlb_ref)

        lb = lb_ref[...]
        for c in reversed(range(ncr)):
            rs = pl.ds(c * HGRN_CHUNK, HGRN_CHUNK)
            _, vjp = jax.vjp(_hgrn_chunk, st_ref[c], q_ref[rs, :], f_ref[rs, :], i_ref[rs, :], lb)
            dst, dq, df, dv, dlb = vjp((dstate[...], do_ref[rs, :]))
            dstate[...] = dst
            dq_ref[rs, :] = dq.astype(BF16)
            df_ref[rs, :] = df.astype(BF16)
            di_ref[rs, :] = dv.astype(BF16)
            dlb_ref[...] += dlb

    blk = (rows, HGRN_HEAD_DIM)
    last = nblk - 1
    out_blk = pl.BlockSpec(blk, lambda h, r: (last - r, h))
    return pl.pallas_call(
        body,
        name="hgrn_bwd",
        grid=(HGRN_HEADS, nblk),
        in_specs=[
            pl.BlockSpec(blk, lambda h, r: (last - r, OFF_AQ // 128 + h)),
            pl.BlockSpec(blk, lambda h, r: (last - r, OFF_AF // 128 + h)),
            pl.BlockSpec(blk, lambda h, r: (last - r, OFF_AI // 128 + h)),
            pl.BlockSpec((1, HGRN_HEAD_DIM), lambda h, r: (0, h)),
            pl.BlockSpec((ncr, None, HGRN_HEAD_DIM, HGRN_HEAD_DIM), lambda h, r: (last - r, h, 0, 0)),
            out_blk,
        ],
        out_specs=[out_blk, out_blk, out_blk, pl.BlockSpec((1, HGRN_HEAD_DIM), lambda h, r: (0, h))],
        out_shape=[_sds((T, HGRN_WIDTH), BF16)] * 3 + [_sds((1, HGRN_WIDTH), F32)],
        scratch_shapes=[pltpu.VMEM((HGRN_HEAD_DIM, HGRN_HEAD_DIM), F32)],
        compiler_params=_cp(("parallel", "arbitrary")),
    )(u, u, u, lower_l, states, do_raw)


QROWS = ATTN_GROUP * WINDOW


def _attn_block(q, kp, kc, vp, vc, bp, bc, sink, mp, mc):
    scale = ATTN_HEAD_DIM ** -0.5
    sp = jnp.where(mp, _dot(q, kp, "nt") * scale + bp, MASK_VALUE)
    sc = jnp.where(mc, _dot(q, kc, "nt") * scale + bc, MASK_VALUE)
    m = jnp.maximum(jnp.maximum(jnp.max(sp, axis=-1, keepdims=True), jnp.max(sc, axis=-1, keepdims=True)), sink)
    m = lax.stop_gradient(m)
    pp = jnp.exp(sp - m)
    pc = jnp.exp(sc - m)
    den = jnp.sum(pp, axis=-1, keepdims=True) + jnp.sum(pc, axis=-1, keepdims=True) + jnp.exp(sink - m)
    inv = 1.0 / den
    return _dot(pp * inv, vp, "nn") + _dot(pc * inv, vc, "nn")


def _attn_masks(first_block):
    i = lax.broadcasted_iota(jnp.int32, (QROWS, WINDOW), 0) % WINDOW
    j = lax.broadcasted_iota(jnp.int32, (QROWS, WINDOW), 1)
    return (j > i) & jnp.logical_not(first_block), j <= i


def _attn_in_specs():
    q_spec = pl.BlockSpec((WINDOW, ATTN_WIDTH), lambda n: (n, OFF_BQ // ATTN_WIDTH))
    k_cur = pl.BlockSpec((WINDOW, KV_WIDTH), lambda n: (n, OFF_BK // KV_WIDTH))
    k_prev = pl.BlockSpec((WINDOW, KV_WIDTH), lambda n: (jnp.maximum(n - 1, 0), OFF_BK // KV_WIDTH))
    v_cur = pl.BlockSpec((WINDOW, KV_WIDTH), lambda n: (n, OFF_BV // KV_WIDTH))
    v_prev = pl.BlockSpec((WINDOW, KV_WIDTH), lambda n: (jnp.maximum(n - 1, 0), OFF_BV // KV_WIDTH))
    bias = pl.BlockSpec((ATTN_KV_HEADS, QROWS, 2 * WINDOW), lambda n: (0, 0, 0))
    sink = pl.BlockSpec((ATTN_KV_HEADS, QROWS, 1), lambda n: (0, 0, 0))
    return [q_spec, k_prev, k_cur, v_prev, v_cur, bias, sink]


def _head_cols(a):
    return slice(a * ATTN_HEAD_DIM, (a + 1) * ATTN_HEAD_DIM)


def _group_rows(ref, h):
    return jnp.concatenate([ref[:, _head_cols(ATTN_GROUP * h + g)] for g in range(ATTN_GROUP)], axis=0)


def _attn_fwd(u, bias4, sink4):
    T = u.shape[0]

    def body(q_ref, kp_ref, kc_ref, vp_ref, vc_ref, b_ref, s_ref, o_ref):
        mp, mc = _attn_masks(pl.program_id(0) == 0)
        for h in range(ATTN_KV_HEADS):
            hs = _head_cols(h)
            o = _attn_block(_group_rows(q_ref, h), kp_ref[:, hs], kc_ref[:, hs], vp_ref[:, hs], vc_ref[:, hs],
                            b_ref[h, :, :WINDOW], b_ref[h, :, WINDOW:], s_ref[h], mp, mc)
            for g in range(ATTN_GROUP):
                o_ref[:, _head_cols(ATTN_GROUP * h + g)] = o[g * WINDOW:(g + 1) * WINDOW]

    return pl.pallas_call(
        body,
        name="attn_fwd",
        grid=(T // WINDOW,),
        in_specs=_attn_in_specs(),
        out_specs=pl.BlockSpec((WINDOW, ATTN_WIDTH), lambda n: (n, 0)),
        out_shape=_sds((T, ATTN_WIDTH), F32),
        compiler_params=_cp(("parallel",)),
    )(u, u, u, u, u, bias4, sink4)


def _attn_bwd(u, bias4, sink4, do, plan=None):
    T = u.shape[0]
    act = pl.BlockSpec((WINDOW, ATTN_WIDTH), lambda n: (n, 0))
    kv = pl.BlockSpec((WINDOW, KV_WIDTH), lambda n: (n, 0))
    in_specs = _attn_in_specs()
    bias, sink = in_specs[5], in_specs[6]

    def body(q_ref, kp_ref, kc_ref, vp_ref, vc_ref, b_ref, s_ref, do_ref,
             dq_ref, dkc_ref, dkp_ref, dvc_ref, dvp_ref, db_ref, ds_ref):
        n = pl.program_id(0)
        mp, mc = _attn_masks(n == 0)

        @pl.when(n == 0)
        def _():
            db_ref[...] = jnp.zeros_like(db_ref)
            ds_ref[...] = jnp.zeros_like(ds_ref)

        dqs, dkps, dkcs, dvps, dvcs = [], [], [], [], []
        for h in range(ATTN_KV_HEADS):
            hs = _head_cols(h)
            _, vjp = jax.vjp(
                functools.partial(_attn_block, mp=mp, mc=mc),
                _group_rows(q_ref, h), kp_ref[:, hs], kc_ref[:, hs], vp_ref[:, hs], vc_ref[:, hs],
                b_ref[h, :, :WINDOW], b_ref[h, :, WINDOW:], s_ref[h])
            dq, dkp, dkc, dvp, dvc, dbp, dbc, dsink = vjp(_group_rows(do_ref, h))
            dqs += [dq[g * WINDOW:(g + 1) * WINDOW] for g in range(ATTN_GROUP)]
            dkps.append(dkp)
            dkcs.append(dkc)
            dvps.append(dvp)
            dvcs.append(dvc)
            db_ref[h, :, :WINDOW] += dbp
            db_ref[h, :, WINDOW:] += dbc
            ds_ref[h] += dsink
        dq_ref[...] = jnp.concatenate(dqs, axis=1).astype(BF16)
        dkc_ref[...] = jnp.concatenate(dkcs, axis=1)
        dkp_ref[...] = jnp.concatenate(dkps, axis=1)
        dvc_ref[...] = jnp.concatenate(dvcs, axis=1)
        dvp_ref[...] = jnp.concatenate(dvps, axis=1)

    kv_sds = _sds((T, KV_WIDTH), F32)
    return _call(
        body,
        name="attn_bwd",
        grid=(T // WINDOW,),
        in_specs=in_specs + [act],
        out_specs=[act, kv, kv, kv, kv, bias, sink],
        out_shape=[_sds((T, ATTN_WIDTH), BF16), kv_sds, kv_sds, kv_sds, kv_sds,
                   _sds((ATTN_KV_HEADS, QROWS, 2 * WINDOW), F32), _sds((ATTN_KV_HEADS, QROWS, 1), F32)],
        sem=("arbitrary",),
        args=(u, u, u, u, u, bias4, sink4, do),
        plan=plan,
    )


def _with_next_block_part(cur, nxt):
    pad = jnp.zeros_like(nxt[:WINDOW])
    return (cur + jnp.concatenate([nxt[WINDOW:], pad], axis=0)).astype(BF16)


MIX_COLS = 512


def _silu(x):
    return x * jax.nn.sigmoid(x)


def _silu_grad(x):
    s = jax.nn.sigmoid(x)
    return s * (1.0 + x * (1.0 - s))


def _shift_rows_down(h, first, second):
    n = h.shape[0]
    row = lax.broadcasted_iota(jnp.int32, h.shape, 0)
    s1 = jnp.where(row == 0, first, pltpu.roll(h, 1, 0))
    s2 = jnp.where(row == 0, second, jnp.where(row == 1, first, pltpu.roll(h, 2, 0)))
    del n
    return s1, s2


def _shift_rows_up(h, first, second):
    n = h.shape[0]
    row = lax.broadcasted_iota(jnp.int32, h.shape, 0)
    s1 = jnp.where(row == n - 1, first, pltpu.roll(h, n - 1, 0))
    s2 = jnp.where(row == n - 1, second, jnp.where(row == n - 2, first, pltpu.roll(h, n - 2, 0)))
    return s1, s2


def _mix_rows(T):
    return _tile(T, 256)


def _mix_fwd(u, o_raw, o_b, gn_l, cw_l):
    T = u.shape[0]
    tr = _mix_rows(T)
    nrow = T // tr
    hr = tr // 8

    def ucol(off):
        return pl.BlockSpec((tr, MIX_COLS), lambda i, j, off=off: (i, off // MIX_COLS + j))

    def uprev(off):
        return pl.BlockSpec((8, MIX_COLS), lambda i, j, off=off: (jnp.maximum(i * hr - 1, 0), off // MIX_COLS + j))

    act = pl.BlockSpec((tr, MIX_COLS), lambda i, j: (i, j))
    par = lambda rows: pl.BlockSpec((rows, MIX_COLS), lambda i, j: (0, j))

    def body(oraw_ref, ag_ref, ob_ref, bg_ref, cb_ref, cc_ref, cx_ref, cg_ref, ccp_ref, cxp_ref, gn_ref, cw_ref,
             ha_ref, hb_ref, hc_ref, hat_ref, hbt_ref, hct_ref):
        ag = _silu(ag_ref[...])
        for h in range(MIX_COLS // HGRN_HEAD_DIM):
            cs = slice(h * HGRN_HEAD_DIM, (h + 1) * HGRN_HEAD_DIM)
            o = oraw_ref[:, cs]
            nrm = o * lax.rsqrt(jnp.mean(o * o, axis=-1, keepdims=True) + RMS_EPS)
            ha = nrm * gn_ref[:, cs] * ag[:, cs]
            ha_ref[:, cs] = ha.astype(BF16)
            hat_ref[cs, :] = ha.T.astype(BF16)
        hb = ob_ref[...] * _silu(bg_ref[...])
        hb_ref[...] = hb.astype(BF16)
        hbt_ref[...] = hb.T.astype(BF16)
        keep = (pl.program_id(0) > 0).astype(F32)
        hcur = cc_ref[...] * cx_ref[...]
        p1 = ccp_ref[7:8, :] * cxp_ref[7:8, :] * keep
        p2 = ccp_ref[6:7, :] * cxp_ref[6:7, :] * keep
        s1, s2 = _shift_rows_down(hcur, p1, p2)
        y = cw_ref[0:1, :] * s2 + cw_ref[1:2, :] * s1 + cw_ref[2:3, :] * hcur
        hc = cb_ref[...] * y * _silu(cg_ref[...])
        hc_ref[...] = hc.astype(BF16)
        hct_ref[...] = hc.T.astype(BF16)

    out = _sds((T, HGRN_WIDTH), BF16)
    out_t = _sds((HGRN_WIDTH, T), BF16)
    act_t = pl.BlockSpec((MIX_COLS, tr), lambda i, j: (j, i))
    return pl.pallas_call(
        body,
        name="mix_fwd",
        grid=(nrow, HGRN_WIDTH // MIX_COLS),
        in_specs=[act, ucol(OFF_AG), act, ucol(OFF_BG), ucol(OFF_CB), ucol(OFF_CC), ucol(OFF_CX), ucol(OFF_CG),
                  uprev(OFF_CC), uprev(OFF_CX), par(1), par(CONV_K)],
        out_specs=[act, act, act, act_t, act_t, act_t],
        out_shape=[out, out, out, out_t, out_t, out_t],
        compiler_params=_cp(("parallel", "parallel")),
    )(o_raw, u, o_b, u, u, u, u, u, u, u, gn_l, cw_l)


def _mix_bwd(u, o_raw, o_b, gn_l, cw_l, dha, dhb, dhc):
    T = u.shape[0]
    tr = _mix_rows(T)
    nrow = T // tr
    hr = tr // 8
    last_halo = T // 8 - 1

    def ucol(off):
        return pl.BlockSpec((tr, MIX_COLS), lambda j, i, off=off: (i, off // MIX_COLS + j))

    def uprev(off):
        return pl.BlockSpec((8, MIX_COLS), lambda j, i, off=off: (jnp.maximum(i * hr - 1, 0), off // MIX_COLS + j))

    def unext(off):
        return pl.BlockSpec((8, MIX_COLS), lambda j, i, off=off: (jnp.minimum((i + 1) * hr, last_halo), off // MIX_COLS + j))

    act = pl.BlockSpec((tr, MIX_COLS), lambda j, i: (i, j))
    act_next = pl.BlockSpec((8, MIX_COLS), lambda j, i: (jnp.minimum((i + 1) * hr, last_halo), j))
    par = lambda rows: pl.BlockSpec((rows, MIX_COLS), lambda j, i: (0, j))

    def body(oraw_ref, ag_ref, ob_ref, bg_ref, cb_ref, cc_ref, cx_ref, cg_ref, ccp_ref, cxp_ref,
             cbn_ref, cgn_ref, dhcn_ref, gn_ref, cw_ref, dha_ref, dhb_ref, dhc_ref,
             doraw_ref, dob_ref, dag_ref, dbg_ref, dcb_ref, dcc_ref, dcx_ref, dcg_ref, dgn_ref, dcw_ref):
        i = pl.program_id(1)

        @pl.when(i == 0)
        def _():
            dgn_ref[...] = jnp.zeros_like(dgn_ref)
            dcw_ref[...] = jnp.zeros_like(dcw_ref)

        ag = ag_ref[...]
        sag = _silu(ag)
        dha = dha_ref[...]
        for h in range(MIX_COLS // HGRN_HEAD_DIM):
            cs = slice(h * HGRN_HEAD_DIM, (h + 1) * HGRN_HEAD_DIM)
            o = oraw_ref[:, cs]
            rs = lax.rsqrt(jnp.mean(o * o, axis=-1, keepdims=True) + RMS_EPS)
            nrm = o * rs
            gn = gn_ref[:, cs]
            d = dha[:, cs]
            dag_ref[:, cs] = (d * nrm * gn * _silu_grad(ag[:, cs])).astype(BF16)
            dgn_ref[:, cs] += jnp.sum(d * sag[:, cs] * nrm, axis=0, keepdims=True)
            dn = d * sag[:, cs] * gn
            doraw_ref[:, cs] = rs * (dn - nrm * jnp.mean(dn * nrm, axis=-1, keepdims=True))
        bg = bg_ref[...]
        dhb = dhb_ref[...]
        dob_ref[...] = dhb * _silu(bg)
        dbg_ref[...] = (dhb * ob_ref[...] * _silu_grad(bg)).astype(BF16)
        keep_prev = (i > 0).astype(F32)
        keep_next = (i < nrow - 1).astype(F32)
        cc, cx, cb, cg = cc_ref[...], cx_ref[...], cb_ref[...], cg_ref[...]
        hcur = cc * cx
        p1 = ccp_ref[7:8, :] * cxp_ref[7:8, :] * keep_prev
        p2 = ccp_ref[6:7, :] * cxp_ref[6:7, :] * keep_prev
        s1, s2 = _shift_rows_down(hcur, p1, p2)
        w0, w1, w2 = cw_ref[0:1, :], cw_ref[1:2, :], cw_ref[2:3, :]
        y = w0 * s2 + w1 * s1 + w2 * hcur
        dhc = dhc_ref[...]
        scg = _silu(cg)
        doc = dhc * scg
        dcg_ref[...] = (dhc * cb * y * _silu_grad(cg)).astype(BF16)
        dcb_ref[...] = (doc * y).astype(BF16)
        dy = doc * cb
        n1 = dhcn_ref[0:1, :] * _silu(cgn_ref[0:1, :]) * cbn_ref[0:1, :] * keep_next
        n2 = dhcn_ref[1:2, :] * _silu(cgn_ref[1:2, :]) * cbn_ref[1:2, :] * keep_next
        u1, u2 = _shift_rows_up(dy, n1, n2)
        dh = w2 * dy + w1 * u1 + w0 * u2
        dcc_ref[...] = (dh * cx).astype(BF16)
        dcx_ref[...] = (dh * cc).astype(BF16)
        dcw_ref[0:1, :] += jnp.sum(dy * s2, axis=0, keepdims=True)
        dcw_ref[1:2, :] += jnp.sum(dy * s1, axis=0, keepdims=True)
        dcw_ref[2:3, :] += jnp.sum(dy * hcur, axis=0, keepdims=True)

    f32o, bf = _sds((T, HGRN_WIDTH), F32), _sds((T, HGRN_WIDTH), BF16)
    return pl.pallas_call(
        body,
        name="mix_bwd",
        grid=(HGRN_WIDTH // MIX_COLS, nrow),
        in_specs=[act, ucol(OFF_AG), act, ucol(OFF_BG), ucol(OFF_CB), ucol(OFF_CC), ucol(OFF_CX), ucol(OFF_CG),
                  uprev(OFF_CC), uprev(OFF_CX), unext(OFF_CB), unext(OFF_CG), act_next, par(1), par(CONV_K),
                  act, act, act],
        out_specs=[act, act, act, act, act, act, act, act, par(1), par(8)],
        out_shape=[f32o, f32o, bf, bf, bf, bf, bf, bf, _sds((1, HGRN_WIDTH), F32), _sds((8, HGRN_WIDTH), F32)],
        compiler_params=_cp(("parallel", "arbitrary")),
    )(o_raw, u, o_b, u, u, u, u, u, u, u, u, u, dhc, gn_l, cw_l, dha, dhb, dhc)


def _merge_specs(T, order):
    tr = _tile(T, 256)

    def ucol(off):
        if order == "ij":
            return pl.BlockSpec((tr, MIX_COLS), lambda i, j, off=off: (i, off // MIX_COLS + j))
        return pl.BlockSpec((tr, MIX_COLS), lambda j, i, off=off: (i, off // MIX_COLS + j))

    act = pl.BlockSpec((tr, MIX_COLS), (lambda i, j: (i, j)) if order == "ij" else (lambda j, i: (i, j)))
    return tr, ucol, act


def _merge_fwd(u, ya, yb, yc):
    T = u.shape[0]
    tr, ucol, act = _merge_specs(T, "ij")

    def body(ma_ref, mb_ref, mc_ref, ya_ref, yb_ref, yc_ref, o_ref, ot_ref):
        merged = (jax.nn.sigmoid(ma_ref[...]) * ya_ref[...] + jax.nn.sigmoid(mb_ref[...]) * yb_ref[...]
                  + jax.nn.sigmoid(mc_ref[...]) * yc_ref[...])
        o_ref[...] = merged.astype(BF16)
        ot_ref[...] = merged.T.astype(BF16)

    return pl.pallas_call(
        body,
        name="merge_fwd",
        grid=(T // tr, D_MODEL // MIX_COLS),
        in_specs=[ucol(OFF_MA), ucol(OFF_MB), ucol(OFF_MC), act, act, act],
        out_specs=[act, pl.BlockSpec((MIX_COLS, tr), lambda i, j: (j, i))],
        out_shape=[_sds((T, D_MODEL), BF16), _sds((D_MODEL, T), BF16)],
        compiler_params=_cp(("parallel", "parallel")),
    )(u, u, u, ya, yb, yc)


def _merge_bwd(u, ya, yb, yc, dmerged, plan=None):
    T = u.shape[0]
    tr, ucol, act = _merge_specs(T, "ij")

    def body(ma_ref, mb_ref, mc_ref, ya_ref, yb_ref, yc_ref, dm_ref, dya_ref, dyb_ref, dyc_ref, dma_ref, dmb_ref, dmc_ref):
        dm = dm_ref[...]
        for m_ref, y_ref, dy_ref, dg_ref in ((ma_ref, ya_ref, dya_ref, dma_ref), (mb_ref, yb_ref, dyb_ref, dmb_ref),
                                             (mc_ref, yc_ref, dyc_ref, dmc_ref)):
            s = jax.nn.sigmoid(m_ref[...])
            dy_ref[...] = (dm * s).astype(BF16)
            dg_ref[...] = (dm * y_ref[...] * s * (1.0 - s)).astype(BF16)

    out = _sds((T, D_MODEL), BF16)
    return _call(
        body,
        name="merge_bwd",
        grid=(T // tr, D_MODEL // MIX_COLS),
        in_specs=[ucol(OFF_MA), ucol(OFF_MB), ucol(OFF_MC), act, act, act, act],
        out_specs=[act] * 6,
        out_shape=[out] * 6,
        sem=("parallel", "parallel"),
        args=(u, u, u, ya, yb, yc, dmerged),
        plan=plan,
    )


def _ln_fwd(x, y, g_l, b_l):
    T = x.shape[0]
    tr = _tile(T, 256)
    row = pl.BlockSpec((tr, D_MODEL), lambda i: (i, 0))
    col = pl.BlockSpec((D_MODEL, tr), lambda i: (0, i))
    par = pl.BlockSpec((1, D_MODEL), lambda i: (0, 0))

    def body(x_ref, y_ref, g_ref, b_ref, o_ref, z_ref, ob_ref, ot_ref):
        z = ALPHA * x_ref[...] + y_ref[...]
        z_ref[...] = z
        mu = jnp.mean(z, axis=-1, keepdims=True)
        zc = z - mu
        var = jnp.mean(zc * zc, axis=-1, keepdims=True)
        o = zc * lax.rsqrt(var + LN_EPS) * g_ref[...] + b_ref[...]
        o_ref[...] = o
        ob_ref[...] = o.astype(BF16)
        ot_ref[...] = o.T.astype(BF16)

    return pl.pallas_call(
        body,
        name="ln_fwd",
        grid=(T // tr,),
        in_specs=[row, row, par, par],
        out_specs=[row, row, row, col],
        out_shape=[_sds((T, D_MODEL), F32)] * 2 + [_sds((T, D_MODEL), BF16), _sds((D_MODEL, T), BF16)],
        compiler_params=_cp(("parallel",)),
    )(x, y, g_l, b_l)


def _operand_forms(x):
    T = x.shape[0]
    tr = _tile(T, 256)
    row = pl.BlockSpec((tr, D_MODEL), lambda i: (i, 0))
    col = pl.BlockSpec((D_MODEL, tr), lambda i: (0, i))

    def body(x_ref, xb_ref, xt_ref):
        xv = x_ref[...]
        xb_ref[...] = xv.astype(BF16)
        xt_ref[...] = xv.T.astype(BF16)

    return pl.pallas_call(
        body,
        name="operand_forms",
        grid=(T // tr,),
        in_specs=[row],
        out_specs=[row, col],
        out_shape=[_sds((T, D_MODEL), BF16), _sds((D_MODEL, T), BF16)],
        compiler_params=_cp(("parallel",)),
    )(x)


def _ln_bwd(z, dxn, g_l):
    T = z.shape[0]
    tr = _tile(T, 256)
    row = pl.BlockSpec((tr, D_MODEL), lambda i: (i, 0))
    par = pl.BlockSpec((1, D_MODEL), lambda i: (0, 0))

    def body(z_ref, d_ref, g_ref, dz_ref, dzb_ref, dg_ref, db_ref):
        @pl.when(pl.program_id(0) == 0)
        def _():
            dg_ref[...] = jnp.zeros_like(dg_ref)
            db_ref[...] = jnp.zeros_like(db_ref)

        z = z_ref[...]
        d = d_ref[...]
        mu = jnp.mean(z, axis=-1, keepdims=True)
        zc = z - mu
        rstd = lax.rsqrt(jnp.mean(zc * zc, axis=-1, keepdims=True) + LN_EPS)
        zh = zc * rstd
        dg_ref[...] += jnp.sum(d * zh, axis=0, keepdims=True)
        db_ref[...] += jnp.sum(d, axis=0, keepdims=True)
        dh = d * g_ref[...]
        dz = rstd * (dh - jnp.mean(dh, axis=-1, keepdims=True) - zh * jnp.mean(dh * zh, axis=-1, keepdims=True))
        dz_ref[...] = dz
        dzb_ref[...] = dz.astype(BF16)

    return pl.pallas_call(
        body,
        name="ln_bwd",
        grid=(T // tr,),
        in_specs=[row, row, par],
        out_specs=[row, row, par, par],
        out_shape=[_sds((T, D_MODEL), F32), _sds((T, D_MODEL), BF16), _sds((1, D_MODEL), F32), _sds((1, D_MODEL), F32)],
        compiler_params=_cp(("arbitrary",)),
    )(z, dxn, g_l)


def _loss_head(y, target):
    T = y.shape[0]
    tr = _tile(T, 256)
    row = pl.BlockSpec((tr, D_MODEL), lambda i: (i, 0))
    acc = pl.BlockSpec((8, 128), lambda i: (0, 0))

    def body(y_ref, t_ref, l_ref, d_ref):
        @pl.when(pl.program_id(0) == 0)
        def _():
            l_ref[...] = jnp.zeros_like(l_ref)

        err = y_ref[...] - t_ref[...]
        d_ref[...] = err * (1.0 / D_MODEL)
        part = 0.5 * jnp.sum(jnp.sum(err * err, axis=-1, keepdims=True) * (1.0 / D_MODEL), axis=0, keepdims=True)
        r = lax.broadcasted_iota(jnp.int32, (8, 128), 0)
        c = lax.broadcasted_iota(jnp.int32, (8, 128), 1)
        l_ref[...] += jnp.where((r == 0) & (c == 0), part, 0.0)

    return pl.pallas_call(
        body,
        name="loss_head",
        grid=(T // tr,),
        in_specs=[row, row],
        out_specs=[acc, row],
        out_shape=[_sds((8, 128), F32), _sds((T, D_MODEL), F32)],
        compiler_params=_cp(("arbitrary",)),
    )(y, target)


ADAMW_BLOCK_ELEMS = 256 * 1024


def _adamw(w, g, m, v, name):
    shape = w.shape
    cols = shape[-1]
    rows = math.prod(shape[:-1])
    flat = lambda a: a.reshape(rows, cols)
    if rows * cols <= ADAMW_BLOCK_ELEMS or rows % 8:
        tr = rows
    else:
        tr = 8
        while rows % (tr * 2) == 0 and tr * 2 * cols <= ADAMW_BLOCK_ELEMS:
            tr *= 2
    blk = pl.BlockSpec((tr, cols), lambda i: (i, 0))
    c1 = 1.0 - ADAM_B1 ** ADAM_STEP
    c2 = 1.0 - ADAM_B2 ** ADAM_STEP

    def body(w_ref, g_ref, m_ref, v_ref, d_ref, nm_ref, nv_ref):
        gg = g_ref[...]
        nm = ADAM_B1 * m_ref[...] + (1.0 - ADAM_B1) * gg
        nv = ADAM_B2 * v_ref[...] + (1.0 - ADAM_B2) * (gg * gg)
        nm_ref[...] = nm
        nv_ref[...] = nv
        d_ref[...] = -ADAM_LR * ((nm / c1) / (jnp.sqrt(nv / c2) + ADAM_EPS) + ADAM_WD * w_ref[...])

    outs = pl.pallas_call(
        body,
        name=name,
        grid=(rows // tr,),
        in_specs=[blk] * 4,
        out_specs=[blk] * 3,
        out_shape=[_sds((rows, cols), F32)] * 3,
        compiler_params=_cp(("parallel",)),
    )(flat(w), flat(g), flat(m), flat(v))
    return tuple(o.reshape(shape) for o in outs)


def _t5_bucket(dist):
    max_exact = N_BUCKETS // 2
    logd = jnp.log(jnp.maximum(dist, 1).astype(F32) / max_exact) / math.log(MAX_DISTANCE / max_exact)
    large = jnp.minimum(max_exact + (logd * (N_BUCKETS - max_exact)).astype(jnp.int32), N_BUCKETS - 1)
    return jnp.where(dist < max_exact, dist, large)


def _band_bias(rel_bias):
    i = jnp.arange(WINDOW)[:, None]
    j = jnp.arange(2 * WINDOW)[None, :]
    bucket = _t5_bucket(jnp.clip(WINDOW + i - j, 0, WINDOW - 1))
    onehot = (bucket[:, :, None] == jnp.arange(N_BUCKETS)[None, None, :]).astype(F32)
    return jnp.einsum("ijb,bh->hij", onehot, rel_bias.astype(F32), precision=lax.Precision.HIGHEST)


def _lower_bounds(lb_param):
    soft = jax.nn.softmax(lb_param.astype(F32), axis=0)
    return jnp.cumsum(soft, axis=0) - soft[0:1]


def _mm_rows(T):
    return _tile(T, 1024)


def _layer_fwd(xs, ex, layer, lower_l, bias4, sink4_l, gn_l, cw_l, lng_l, lnb_l):
    x, xb, xt = xs
    T = x.shape[0]
    tm = _mm_rows(T)
    w = ex.weights[layer]
    w_in_l, w_proj_l, w_out_l = w["w_in"][0], w["w_proj"], w["w_out"][0]
    plan = ex.fwd_plan(layer)
    u = _matmul(xb, w_in_l, mode="nn", tm=_tile(T, 2048), tn=768, tk=D_MODEL, name="mm_u", plan=plan)
    if plan is not None:
        u, landed = u
        plan = ex.fwd_pass_plan(layer, landed)
    (o_raw, states), landed = _hgrn_fwd(u, lower_l, plan=plan)
    if plan is not None:
        ex.fwd_landed(layer, landed)
    o_b = _attn_fwd(u, bias4, sink4_l)
    ha, hb, hc, hat, hbt, hct = _mix_fwd(u, o_raw, o_b, gn_l, cw_l)
    ys = [_matmul(h, w_proj_l, mode="nn", tm=tm, tn=1024, tk=HGRN_WIDTH, b_idx=i, name="mm_proj", out_dtype=BF16)
          for i, h in enumerate((ha, hb, hc))]
    merged, merged_t = _merge_fwd(u, *ys)
    y = _matmul(merged, w_out_l, mode="nn", tm=tm, tn=1024, tk=D_MODEL, name="mm_out")
    xn, z, xnb, xnt = _ln_fwd(x, y, lng_l, lnb_l)
    saved = dict(xt=xt, u=u, o_raw=o_raw, states=states, o_b=o_b, hts=(hat, hbt, hct), ys=ys,
                 merged_t=merged_t, z=z)
    return (xn, xnb, xnt), saved


def _layer_bwd(dxn, s, ex, layer, lower_l, bias4, sink4_l, gn_l, cw_l, lng_l):
    T = dxn.shape[0]
    tm = _mm_rows(T)
    u = s["u"]
    w = ex.weights[layer]
    w_in_l, w_proj_l, w_out_l = w["w_in"][0], w["w_proj"], w["w_out"][0]
    dz, dzb, d_lng, d_lnb = _ln_bwd(s["z"], dxn, lng_l)
    dmerged = _matmul(dzb, w_out_l, mode="nt", tm=tm, tn=1024, tk=D_MODEL, name="mm_dmerged")
    g_w_out = _matmul(s["merged_t"], dzb, mode="nn", tm=1024, tn=1024, tk=2048, name="mm_gw_out")
    plan = ex.pair_plan(layer)
    (*dys, dma, dmb, dmc), landed = _merge_bwd(u, *s["ys"], dmerged, plan=plan)
    if plan is not None:
        ex.pair_landed(layer, landed)
    dhs =[_matmul(dy, w_proj_l, mode="nt", tm=tm, tn=1024, tk=D_MODEL, b_idx=i, name="mm_dh") for i, dy in enumerate(dys)]
    g_w_proj = jnp.stack([_matmul(ht, dy, mode="nn", tm=1024, tn=1024, tk=2048, name="mm_gw_proj")
                          for ht, dy in zip(s["hts"], dys)])
    do_raw, do_b, dag, dbg, dcb, dcc, dcx, dcg, d_gn, d_cw = _mix_bwd(u, s["o_raw"], s["o_b"], gn_l, cw_l, *dhs)
    daq, daf, dai, d_lower = _hgrn_bwd(u, lower_l, s["states"], do_raw)
    plan = ex.slab_plan(layer)
    (dbq, dkc, dkp, dvc, dvp, d_bias4, d_sink4), landed = _attn_bwd(u, bias4, sink4_l, do_b, plan=plan)
    if plan is not None:
        ex.slab_landed(layer, landed)
    dbk = _with_next_block_part(dkc, dkp)
    dbv = _with_next_block_part(dvc, dvp)
    du = jnp.concatenate([daq, daf, dai, dag, dbq, dbk, dbv, dbg, dcb, dcc, dcx, dcg, dma, dmb, dmc], axis=1)
    g_w_in = _matmul(s["xt"], du, mode="nn", tm=1024, tn=768, tk=4096, name="mm_gw_in")
    ex.grads_ready(layer, dict(w_in=g_w_in[None], w_proj=g_w_proj, w_out=g_w_out[None]))
    assemble = ex.assemble_plan(layer)
    tail = ex.tail_plan() if layer == 0 else None
    carried = [p for p in (assemble, tail) if p is not None]
    plan = _merge_plans(carried) if carried else None
    dx = _matmul(du, w_in_l, mode="nt", tm=_tile(T, 512), tn=1024, tk=5632, name="mm_dx", add=dz, add_scale=ALPHA, plan=plan)
    if plan is not None:
        dx, landed = dx
        n_assemble = 0 if assemble is None else len(assemble.out_shapes)
        if assemble is not None:
            ex.assembled(layer, landed[:n_assemble])
        if tail is not None:
            ex.slab_landed(-1, landed[n_assemble:])
    d_sinks = jnp.sum(d_sink4.reshape(ATTN_HEADS, WINDOW), axis=-1)
    small = dict(lower=d_lower[0], gn=d_gn[0], sinks=d_sinks, cw=d_cw[:CONV_K], bias=d_bias4.reshape(ATTN_HEADS, WINDOW, 2 * WINDOW),
                 lng=d_lng[0], lnb=d_lnb[0])
    return dx, small


def _local_step(x, target, ex, lb_param, hgrn_norm_g, attn_sinks, conv_w_full, rel_bias, ln_g, ln_b):
    lower, lower_vjp = jax.vjp(_lower_bounds, lb_param)
    bias, bias_vjp = jax.vjp(_band_bias, rel_bias)
    bias4 = bias.reshape(ATTN_KV_HEADS, QROWS, 2 * WINDOW)
    sink4 = jnp.broadcast_to(attn_sinks.reshape(DEPTH, ATTN_HEADS, 1, 1), (DEPTH, ATTN_HEADS, WINDOW, 1)).reshape(
        DEPTH, ATTN_KV_HEADS, QROWS, 1)
    row = lambda a, l: a[l:l + 1]
    saved = []
    hs = (x, *_operand_forms(x))
    for l in range(DEPTH):
        hs, s = _layer_fwd(hs, ex, l, row(lower, l), bias4, sink4[l], row(hgrn_norm_g, l), conv_w_full[l], row(ln_g, l), row(ln_b, l))
        saved.append(s)
    loss_blk, dh = _loss_head(hs[0], target)
    smalls = [None] * DEPTH
    for l in reversed(range(DEPTH)):
        dh, smalls[l] = _layer_bwd(dh, saved[l], ex, l, row(lower, l), bias4, sink4[l], row(hgrn_norm_g, l), conv_w_full[l], row(ln_g, l))
    stack = lambda k: jnp.stack([sm[k] for sm in smalls])
    d_bias = smalls[0]["bias"] + smalls[1]["bias"] + smalls[2]["bias"] + smalls[3]["bias"]
    small = dict(
        lb_param=lower_vjp(stack("lower"))[0], hgrn_norm_g=stack("gn"), attn_sinks=stack("sinks"), conv_w=stack("cw"),
        rel_bias=bias_vjp(d_bias)[0], ln_g=stack("lng"), ln_b=stack("lnb"))
    return loss_blk, dh, small


ANY = pl.BlockSpec(memory_space=pl.ANY)
DMA_SEM = pltpu.SemaphoreType.DMA


def _coords():
    return lax.axis_index("x"), lax.axis_index("y"), lax.axis_index("c")


def _other_chips(x, y):
    return [(1 - x, y), (x, 1 - y), (1 - x, 1 - y)]


def _remote(src, dst, send_sem, recv_sem, device):
    return pltpu.make_async_remote_copy(src_ref=src, dst_ref=dst, send_sem=send_sem, recv_sem=recv_sem,
                                        device_id=device, device_id_type=MESH)


def _sub(ref, axis, index, size):
    idx = [slice(None)] * len(ref.shape)
    idx[axis] = pl.ds(pl.multiple_of(index * size, size), size)
    return ref.at[tuple(idx)]


class _Plan:
    def __init__(self, inputs, out_shapes, aliases, n, make):
        self.inputs, self.out_shapes, self.aliases, self.n, self.make = tuple(inputs), tuple(out_shapes), dict(aliases), n, make


class _Xfer:
    def __init__(self, send, recv=None):
        self.send, self.recv = send, send if recv is None else recv

    def start(self):
        self.send.start()

    def wait(self):
        self.send.wait_send()
        self.recv.wait_recv()


def _merge_plans(plans):
    def make(in_refs, out_refs, send_sems, recv_sems, base):
        out, i0, o0, b0 = [], 0, 0, base
        for p in plans:
            out += p.make(in_refs[i0:i0 + len(p.inputs)], out_refs[o0:o0 + len(p.out_shapes)], send_sems, recv_sems, b0)
            i0, o0, b0 = i0 + len(p.inputs), o0 + len(p.out_shapes), b0 + p.n
        return out

    inputs, out_shapes, aliases = [], [], {}
    for p in plans:
        aliases.update({len(inputs) + k: len(out_shapes) + v for k, v in p.aliases.items()})
        inputs += p.inputs
        out_shapes += p.out_shapes
    return _Plan(inputs, out_shapes, aliases, sum(p.n for p in plans), make)


def _run_plan(plan, name):
    ni, no = len(plan.inputs), len(plan.out_shapes)

    def body(*refs):
        transfers = plan.make(refs[:ni], refs[ni:ni + no], refs[ni + no], refs[ni + no + 1], 0)
        for t in transfers:
            t.start()
        for t in transfers:
            t.wait()

    outs = pl.pallas_call(
        body, name=name, in_specs=[ANY] * ni, out_specs=[ANY] * no, out_shape=list(plan.out_shapes),
        input_output_aliases=plan.aliases, scratch_shapes=[DMA_SEM((plan.n,)), DMA_SEM((plan.n,))],
    )(*plan.inputs)
    return list(outs)


def _gather_send_plan(shards, layer, sax):
    shp = shards.shape[1:]
    hax = 3 - sax
    w, hw = shp[sax], shp[hax] // 2
    out_shape = list(shp)
    out_shape[sax] = w * N_CHIPS

    def make(in_refs, out_refs, send_sems, recv_sems, base):
        (src_ref,), (out_ref,) = in_refs, out_refs
        x, y, c = _coords()
        j = 2 * x + y
        src = src_ref.at[layer]
        own = pltpu.make_async_copy(src, _sub(out_ref, sax, j, w), send_sems.at[base])
        dst = _sub(_sub(out_ref, sax, j, w), hax, c, hw)
        return [own] + [
            _Xfer(_remote(_sub(src, hax, c, hw), dst, send_sems.at[base + 1 + k], recv_sems.at[base + 1 + k], (px, py, c)))
            for k, (px, py) in enumerate(_other_chips(x, y))]

    return _Plan([shards], [_sds(tuple(out_shape), shards.dtype)], {}, 4, make)


def _gather_pass_plan(full, sax):
    hax = 3 - sax
    w, hw = full.shape[sax] // N_CHIPS, full.shape[hax] // 2

    def make(in_refs, out_refs, send_sems, recv_sems, base):
        (out_ref,) = out_refs
        x, y, c = _coords()
        region = lambda slab, half: _sub(_sub(out_ref, sax, slab, w), hax, half, hw)
        out = []
        for k, (px, py) in enumerate(_other_chips(x, y)):
            mine, theirs = region(2 * px + py, c), region(2 * px + py, 1 - c)
            sems = send_sems.at[base + k], recv_sems.at[base + k]
            out.append(_Xfer(_remote(mine, mine, *sems, (x, y, 1 - c)), _remote(theirs, theirs, *sems, (x, y, c))))
        return out

    return _Plan([full], [_sds(full.shape, full.dtype)], {0: 0}, 3, make)


def _pair_exchange_plan(g, hax):
    hw = g.shape[hax] // 2
    out_shape = list(g.shape)
    out_shape[hax] = hw

    def make(in_refs, out_refs, send_sems, recv_sems, base):
        x, y, c = _coords()
        return [_Xfer(_remote(_sub(in_refs[0], hax, 1 - c, hw), out_refs[0], send_sems.at[base], recv_sems.at[base], (x, y, 1 - c)))]

    return _Plan([g], [_sds(tuple(out_shape), g.dtype)], {}, 1, make)


def _add_own_half(place, g, recv, hax, blk, name):
    L, ah, bh = recv.shape
    tr, tc = blk
    nr, nc = ah // tr, bh // tc

    def g_map(l, i, jc, p):
        return (l, i + p[0] * nr, jc) if hax == 1 else (l, i, jc + p[0] * nc)

    def body(p_ref, g_ref, r_ref, o_ref):
        o_ref[...] = (g_ref[...] + r_ref[...]).astype(BF16)

    same = pl.BlockSpec((None, tr, tc), lambda l, i, jc, p: (l, i, jc))
    return pl.pallas_call(
        body,
        name=name,
        grid_spec=pltpu.PrefetchScalarGridSpec(
            num_scalar_prefetch=1, grid=(L, nr, nc),
            in_specs=[pl.BlockSpec((None, tr, tc), g_map), same], out_specs=same),
        out_shape=_sds(recv.shape, BF16),
        compiler_params=_cp(("parallel", "parallel", "parallel")),
    )(place, g, recv)


def _slab_exchange_plan(p, sax):
    w = p.shape[sax] // N_CHIPS
    slab_shape = list(p.shape)
    slab_shape[sax] = w

    def make(in_refs, out_refs, send_sems, recv_sems, base):
        x, y, c = _coords()
        return [_Xfer(_remote(_sub(in_refs[0], sax, 2 * px + py, w), out_refs[0].at[k], send_sems.at[base + k],
                              recv_sems.at[base + k], (px, py, c)))
                for k, (px, py) in enumerate(_other_chips(x, y))]

    return _Plan([p], [_sds((3, *slab_shape), p.dtype)], {}, 3, make)


def _add_slabs(place, g, pair, recv, sax, blk, name):
    hax = 3 - sax
    _, L, a, b = recv.shape
    tr, tc = blk
    nr, nc = a // tr, b // tc

    def g_map(l, i, jc, p):
        return (l, p[0] * nr + i, p[1] * nc + jc) if hax == 1 else (l, p[1] * nr + i, p[0] * nc + jc)

    def pair_map(l, i, jc, p):
        return (l, i, p[1] * nc + jc) if hax == 1 else (l, p[1] * nr + i, jc)

    def out_map(l, i, jc, p):
        return (l, p[0] * nr + i, jc) if hax == 1 else (l, i, p[0] * nc + jc)

    def body(p_ref, g_ref, pair_ref, r0_ref, r1_ref, r2_ref, o_ref):
        own = g_ref[...] + pair_ref[...]
        o_ref[...] = ((own + r0_ref[...].astype(F32)) + r1_ref[...].astype(F32)) + r2_ref[...].astype(F32)

    def rk(k):
        return pl.BlockSpec((None, None, tr, tc), lambda l, i, jc, p, k=k: (k, l, i, jc))

    out_shape = [L, a, b]
    out_shape[hax] *= 2
    blk3 = (None, tr, tc)
    return pl.pallas_call(
        body,
        name=name,
        grid_spec=pltpu.PrefetchScalarGridSpec(
            num_scalar_prefetch=1, grid=(L, nr, nc),
            in_specs=[pl.BlockSpec(blk3, g_map), pl.BlockSpec(blk3, pair_map), rk(0), rk(1), rk(2)],
            out_specs=pl.BlockSpec(blk3, out_map)),
        out_shape=_sds(tuple(out_shape), F32),
        compiler_params=_cp(("parallel", "parallel", "parallel")),
    )(place, g, pair, recv, recv, recv)


def _pair_assemble_plan(r, hax):
    hw = r.shape[hax] // 2

    def make(in_refs, out_refs, send_sems, recv_sems, base):
        x, y, c = _coords()
        mine, other = _sub(out_refs[0], hax, c, hw), _sub(out_refs[0], hax, 1 - c, hw)
        sems = send_sems.at[base], recv_sems.at[base]
        return [_Xfer(_remote(mine, mine, *sems, (x, y, 1 - c)), _remote(other, other, *sems, (x, y, c)))]

    return _Plan([r], [_sds(r.shape, r.dtype)], {0: 0}, 1, make)


CLASSES = dict(w_in=(2, (128, 4224), (128, 4224)), w_proj=(2, (256, 2048), (512, 512)), w_out=(1, (256, 1024), (256, 1024)))


class _Exchanges:
    def __init__(self, place, shards):
        self.place, self.shards = place, shards
        self.weights, self.pending, self.pair, self.halves, self.reduced = {}, {}, {}, {}, {}

    def _pass_plan(self, bufs):
        return _merge_plans([_gather_pass_plan(b, CLASSES[k][0]) for k, b in zip(CLASSES, bufs)])

    def gather_send_plan(self, layer):
        return _merge_plans([_gather_send_plan(self.shards[k], layer, CLASSES[k][0]) for k in CLASSES])

    def first_weights(self):
        sent = _run_plan(self.gather_send_plan(0), "gather_send_0")
        self.weights[0] = dict(zip(CLASSES, _run_plan(self._pass_plan(sent), "gather_pass_0")))

    def fwd_plan(self, layer):
        return self.gather_send_plan(layer + 1) if layer + 1 < DEPTH else None

    def fwd_pass_plan(self, layer, bufs):
        return self._pass_plan(bufs)

    def fwd_landed(self, layer, bufs):
        self.weights[layer + 1] = dict(zip(CLASSES, bufs))

    def grads_ready(self, layer, grads):
        self.pending[layer] = grads

    def pair_plan(self, layer):
        if layer + 1 not in self.pending:
            return None
        g = self.pending[layer + 1]
        return _merge_plans([_pair_exchange_plan(g[k], 3 - CLASSES[k][0]) for k in CLASSES])

    def pair_landed(self, layer, bufs):
        self.pair[layer + 1] = dict(zip(CLASSES, bufs))

    def slab_plan(self, layer):
        src = layer + 1
        if src not in self.pair:
            return None
        g, pair = self.pending[src], self.pair[src]
        sums = [_add_own_half(self.place, g[k], pair[k], 3 - CLASSES[k][0], CLASSES[k][1], f"rs_pair_add_{k}_{src}") for k in CLASSES]
        return _merge_plans([_slab_exchange_plan(p, CLASSES[k][0]) for k, p in zip(CLASSES, sums)])

    def slab_landed(self, layer, bufs):
        src = layer + 1
        g, pair = self.pending.pop(src), self.pair.pop(src)
        self.halves[src] = [_add_slabs(self.place, g[k], pair[k], r, CLASSES[k][0], CLASSES[k][2], f"rs_slab_add_{k}_{src}")
                            for k, r in zip(CLASSES, bufs)]

    def assemble_plan(self, layer):
        if layer + 1 not in self.halves:
            return None
        return _merge_plans([_pair_assemble_plan(h, 3 - CLASSES[k][0]) for k, h in zip(CLASSES, self.halves[layer + 1])])

    def assembled(self, layer, bufs):
        del self.halves[layer + 1]
        self.reduced[layer + 1] = dict(zip(CLASSES, bufs))

    def tail_plan(self):
        self.pair_landed(-1, _run_plan(self.pair_plan(-1), "rs_pair_0"))
        return self.slab_plan(-1)

    def finish(self):
        self.assembled(-1, _run_plan(self.assemble_plan(-1), "rs_assemble_0"))
        return [self.reduced[l] for l in range(DEPTH)]


N_DEV = 8


def _all_reduce_small(v, name):
    rows = v.shape[0]

    def body(v_ref, gath_ref, sum_ref, send_sems, recv_sems, local_sem):
        x, y, c = _coords()
        me, sib = (x, y, c), (x, y, 1 - c)
        chips = _other_chips(x, y)

        def slot(px, py, pc):
            return gath_ref.at[pl.ds(pl.multiple_of((4 * px + 2 * py + pc) * rows, rows), rows), :]

        def copy(k, block, to, src=None):
            return _remote(slot(*block) if src is None else src, slot(*block), send_sems.at[k], recv_sems.at[k], to)

        mine = pltpu.make_async_copy(v_ref, slot(*me), local_sem)
        mine.start()
        first = [copy(0, me, sib, src=v_ref)] + [copy(1 + k, me, (*chip, c), src=v_ref) for k, chip in enumerate(chips)]
        for cp in first:
            cp.start()
        passed = [copy(4 + k, (*chip, c), sib) for k, chip in enumerate(chips)]
        for k, chip in enumerate(chips):
            copy(1 + k, (*chip, c), me).wait_recv()
            passed[k].start()
        copy(0, sib, me).wait_recv()
        for k, chip in enumerate(chips):
            copy(4 + k, (*chip, 1 - c), me).wait_recv()
        for cp in first + passed:
            cp.wait_send()
        mine.wait()
        acc = gath_ref[0:rows, :]
        for d in range(1, N_DEV):
            acc = acc + gath_ref[d * rows:(d + 1) * rows, :]
        sum_ref[...] = acc

    vm = pl.BlockSpec(memory_space=pltpu.VMEM)
    return pl.pallas_call(
        body, name=name, in_specs=[vm], out_specs=[vm, vm],
        out_shape=[_sds((N_DEV * rows, 128), F32), _sds((rows, 128), F32)],
        scratch_shapes=[DMA_SEM((7,)), DMA_SEM((7,)), DMA_SEM(())],
    )(v)[1]


def _pad_rows(a):
    flat = a.reshape(-1).astype(F32)
    rows = -(-flat.shape[0] // (8 * 128)) * 8
    return jnp.pad(flat, (0, rows * 128 - flat.shape[0])).reshape(rows, 128)


def _sum_over_devices(parts, name):
    blocks = [_pad_rows(a) for a in parts.values()]
    total = _all_reduce_small(jnp.concatenate(blocks, axis=0), name)
    out, r0 = {}, 0
    for (key, a), blk in zip(parts.items(), blocks):
        out[key] = total[r0:r0 + blk.shape[0]].reshape(-1)[:a.size].reshape(a.shape)
        r0 += blk.shape[0]
    return out


def kernel(x, w_in, w_proj_hgrn, w_proj_attn, w_proj_conv, w_out, lb_param, hgrn_norm_g, attn_sinks, conv_w, rel_bias, ln_g, ln_b, loss_target, m_w_in, m_w_proj_hgrn, m_w_proj_attn, m_w_proj_conv, m_w_out, m_lb_param, m_hgrn_norm_g, m_attn_sinks, m_conv_w, m_rel_bias, m_ln_g, m_ln_b, v_w_in, v_w_proj_hgrn, v_w_proj_attn, v_w_proj_conv, v_w_out, v_lb_param, v_hgrn_norm_g, v_attn_sinks, v_conv_w, v_rel_bias, v_ln_g, v_ln_b):
    xi, yi, ci = _coords()
    slab = 2 * xi + yi
    place = jnp.stack([ci, slab]).astype(jnp.int32)
    conv_cols = conv_w.shape[-1]

    w_in_b = w_in.astype(BF16)[:, None]
    w_proj_b = jnp.stack([w_proj_hgrn, w_proj_attn, w_proj_conv], axis=1).astype(BF16)
    w_out_b = w_out.astype(BF16)[:, None]
    ex = _Exchanges(place, dict(w_in=w_in_b, w_proj=w_proj_b, w_out=w_out_b))
    ex.first_weights()
    conv_spread = lax.dynamic_update_slice(jnp.zeros((DEPTH, CONV_K, CONV_WIDTH), F32), conv_w, (0, 0, slab * conv_cols))
    conv_full = 0.5 * _sum_over_devices({"conv_w": conv_spread}, "gather_conv_w")["conv_w"]

    loss_blk, dx, small = _local_step(x[0], loss_target[0], ex, lb_param, hgrn_norm_g, attn_sinks, conv_full, rel_bias, ln_g, ln_b)

    reduced = ex.finish()
    g_w_in = jnp.stack([r["w_in"][0] for r in reduced])
    g_w_proj = jnp.stack([r["w_proj"] for r in reduced])
    g_w_out = jnp.stack([r["w_out"][0] for r in reduced])
    small = dict(small, loss=loss_blk[0:1, 0:1])
    small = _sum_over_devices(small, "sum_small")
    loss = small["loss"][0, 0]
    g_conv = lax.dynamic_slice(small["conv_w"], (0, 0, slab * conv_cols), (DEPTH, CONV_K, conv_cols))

    grads = [g_w_in, g_w_proj[:, 0], g_w_proj[:, 1], g_w_proj[:, 2], g_w_out, small["lb_param"], small["hgrn_norm_g"],
             small["attn_sinks"], g_conv, small["rel_bias"], small["ln_g"], small["ln_b"]]
    names = ["w_in", "w_proj_hgrn", "w_proj_attn", "w_proj_conv", "w_out", "lb_param", "hgrn_norm_g", "attn_sinks",
             "conv_w", "rel_bias", "ln_g", "ln_b"]
    ws = [w_in, w_proj_hgrn, w_proj_attn, w_proj_conv, w_out, lb_param, hgrn_norm_g, attn_sinks, conv_w, rel_bias, ln_g, ln_b]
    ms = [m_w_in, m_w_proj_hgrn, m_w_proj_attn, m_w_proj_conv, m_w_out, m_lb_param, m_hgrn_norm_g, m_attn_sinks, m_conv_w,
          m_rel_bias, m_ln_g, m_ln_b]
    vs = [v_w_in, v_w_proj_hgrn, v_w_proj_attn, v_w_proj_conv, v_w_out, v_lb_param, v_hgrn_norm_g, v_attn_sinks, v_conv_w,
          v_rel_bias, v_ln_g, v_ln_b]
    upd = [_adamw(w, g, m, v, "adamw_" + n) for n, w, g, m, v in zip(names, ws, grads, ms, vs)]
    deltas, new_ms, new_vs = zip(*upd)
    return (loss, dx[None], *grads, *deltas, *new_ms, *new_vs)
```

```python
import functools
import math

import jax
import jax.numpy as jnp
from jax import lax
from jax.experimental import pallas as pl
from jax.experimental.pallas import tpu as pltpu

F32 = jnp.float32
BF16 = jnp.bfloat16
MXU_DTYPE = BF16

D_MODEL = 2048
DEPTH = 4
HGRN_WIDTH = 1024
HGRN_HEAD_DIM = 128
HGRN_HEADS = 8
HGRN_CHUNK = 64
ATTN_HEAD_DIM = 64
ATTN_HEADS = 16
ATTN_KV_HEADS = 4
ATTN_GROUP = ATTN_HEADS // ATTN_KV_HEADS
ATTN_WIDTH = 1024
KV_WIDTH = 256
WINDOW = 128
CONV_WIDTH = 1024
CONV_K = 3
N_BUCKETS = 32
MAX_DISTANCE = 128
ALPHA = (2.0 * DEPTH) ** 0.25
LN_EPS = 1e-5
RMS_EPS = 1e-6
N_IN = 16896
OFF_AQ, OFF_AF, OFF_AI, OFF_AG = 0, 1024, 2048, 3072
OFF_BQ, OFF_BK, OFF_BV, OFF_BG = 4096, 5120, 5376, 5632
OFF_CB, OFF_CC, OFF_CX, OFF_CG = 6656, 7680, 8704, 9728
OFF_MA, OFF_MB, OFF_MC = 10752, 12800, 14848

ADAM_LR = 0.001
ADAM_B1 = 0.9
ADAM_B2 = 0.999
ADAM_EPS = 1e-08
ADAM_WD = 0.01
ADAM_STEP = 10

N_CHIPS = 4
VMEM_LIMIT_BYTES = 48 * 1024 * 1024
EXP_CLAMP = 80.0
MASK_VALUE = -1e30
MESH = pl.DeviceIdType.MESH


def _cp(sem=None):
    return pltpu.CompilerParams(dimension_semantics=sem, vmem_limit_bytes=VMEM_LIMIT_BYTES)


def _tile(dim, pref):
    return pref if dim % pref == 0 else dim


def _sds(shape, dtype):
    return jax.ShapeDtypeStruct(shape, dtype)


def _call(body, *, name, grid, in_specs, out_specs, out_shape, args, scratch_shapes=(), sem=None, plan=None):
    in_specs, out_specs, out_shape = list(in_specs), list(out_specs), list(out_shape)
    if plan is None:
        outs = pl.pallas_call(body, name=name, grid=grid, in_specs=in_specs, out_specs=out_specs, out_shape=out_shape,
                              scratch_shapes=list(scratch_shapes), compiler_params=_cp(sem))(*args)
        return list(outs), []
    ni, no, ns = len(in_specs), len(out_specs), len(scratch_shapes)
    pi, po = len(plan.inputs), len(plan.out_shapes)

    def carrier(*refs):
        c_in, p_in = refs[:ni], refs[ni:ni + pi]
        c_out, p_out = refs[ni + pi:ni + pi + no], refs[ni + pi + no:ni + pi + no + po]
        c_scr = refs[ni + pi + no + po:ni + pi + no + po + ns]
        send_sems, recv_sems = refs[-2], refs[-1]
        ids = [pl.program_id(a) for a in range(len(grid))]
        first = functools.reduce(jnp.logical_and, [i == 0 for i in ids])
        last = functools.reduce(jnp.logical_and, [i == n - 1 for i, n in zip(ids, grid)])

        @pl.when(first)
        def _():
            for cp in plan.make(p_in, p_out, send_sems, recv_sems, 0):
                cp.start()

        body(*c_in, *c_out, *c_scr)

        @pl.when(last)
        def _():
            for cp in plan.make(p_in, p_out, send_sems, recv_sems, 0):
                cp.wait()

    any_spec = pl.BlockSpec(memory_space=pl.ANY)
    outs = pl.pallas_call(
        carrier, name=name, grid=grid,
        in_specs=in_specs + [any_spec] * pi, out_specs=out_specs + [any_spec] * po,
        out_shape=out_shape + list(plan.out_shapes),
        scratch_shapes=list(scratch_shapes) + [pltpu.SemaphoreType.DMA((plan.n,)), pltpu.SemaphoreType.DMA((plan.n,))],
        input_output_aliases={ni + k: no + v for k, v in plan.aliases.items()},
        compiler_params=_cp(tuple("arbitrary" for _ in grid)),
    )(*args, *plan.inputs)
    return list(outs[:no]), list(outs[no:])


_DIMS = {
    "nn": (((1,), (0,)), ((), ())),
    "nt": (((1,), (1,)), ((), ())),
    "tn": (((0,), (0,)), ((), ())),
}


def _dot_raw(a, b, mode):
    return lax.dot_general(a.astype(MXU_DTYPE), b.astype(MXU_DTYPE), _DIMS[mode], preferred_element_type=F32)


@functools.partial(jax.custom_vjp, nondiff_argnums=(2,))
def _dot(a, b, mode):
    return _dot_raw(a, b, mode)


def _dot_fwd(a, b, mode):
    return _dot_raw(a, b, mode), (a, b)


def _dot_bwd(mode, res, g):
    a, b = res
    if mode == "nn":
        return _dot_raw(g, b, "nt"), _dot_raw(a, g, "tn")
    if mode == "nt":
        return _dot_raw(g, b, "nn"), _dot_raw(g, a, "tn")
    return _dot_raw(b, g, "nt"), _dot_raw(a, g, "nn")


_dot.defvjp(_dot_fwd, _dot_bwd)


def _matmul(a, b, *, mode, tm, tn, tk, name, a_idx=None, b_idx=None, out_dtype=F32, add=None, add_scale=1.0, plan=None):
    a2, b2 = a.shape[-2:], b.shape[-2:]
    if mode == "nn":
        (M, K), (K2, N) = a2, b2
    elif mode == "nt":
        (M, K), (N, K2) = a2, b2
    else:
        (K, M), (K2, N) = a2, b2
    assert K == K2, (a.shape, b.shape, mode)
    tm, tn, tk = _tile(M, tm), _tile(N, tn), _tile(K, tk)
    nk = K // tk

    a_blk = (tk, tm) if mode == "tn" else (tm, tk)
    b_blk = (tn, tk) if mode == "nt" else (tk, tn)

    def a_map(i, j, k):
        ij = (k, i) if mode == "tn" else (i, k)
        return ij if a_idx is None else (a_idx,) + ij

    def b_map(i, j, k):
        ij = (j, k) if mode == "nt" else (k, j)
        return ij if b_idx is None else (b_idx,) + ij

    in_specs = [
        pl.BlockSpec(a_blk if a_idx is None else (None,) + a_blk, a_map),
        pl.BlockSpec(b_blk if b_idx is None else (None,) + b_blk, b_map),
    ]
    args = [a, b]
    if add is not None:
        in_specs.append(pl.BlockSpec((tm, tn), lambda i, j, k: (i, j)))
        args.append(add)
    n_in = len(args)

    def body(*refs):
        a_ref, b_ref = refs[0], refs[1]
        o_ref = refs[n_in]
        p = _dot_raw(a_ref[...], b_ref[...], mode)

        def finish(val):
            if add is not None:
                val = val + add_scale * refs[2][...]
            o_ref[...] = val.astype(out_dtype)

        if nk == 1:
            finish(p)
        else:
            acc_ref = refs[n_in + 1]
            k = pl.program_id(2)

            @pl.when(k == 0)
            def _():
                acc_ref[...] = p

            @pl.when(k > 0)
            def _():
                acc_ref[...] += p

            @pl.when(k == nk - 1)
            def _():
                finish(acc_ref[...])

    (out,), plan_outs = _call(
        body,
        name=name,
        grid=(M // tm, N // tn, nk),
        in_specs=in_specs,
        out_specs=[pl.BlockSpec((tm, tn), lambda i, j, k: (i, j))],
        out_shape=[_sds((M, N), out_dtype)],
        scratch_shapes=[pltpu.VMEM((tm, tn), F32)] if nk > 1 else [],
        sem=("parallel", "parallel", "arbitrary"),
        args=args,
        plan=plan,
    )
    return out if plan is None else (out, plan_outs)


def _scan_rows(x, reverse):
    n = x.shape[0]
    row = lax.broadcasted_iota(jnp.int32, x.shape, 0)
    s = 1
    while s < n:
        if reverse:
            x = x + jnp.where(row < n - s, pltpu.roll(x, n - s, 0), 0.0)
        else:
            x = x + jnp.where(row >= s, pltpu.roll(x, s, 0), 0.0)
        s *= 2
    return x


@jax.custom_vjp
def _cumsum_rows(x):
    return _scan_rows(x, False)


_cumsum_rows.defvjp(lambda x: (_scan_rows(x, False), None), lambda _, g: (_scan_rows(g, True),))


def _hgrn_chunk(state_t, qraw, fraw, v, lb):
    c = HGRN_CHUNK
    q = qraw * jax.nn.sigmoid(qraw) * (HGRN_HEAD_DIM ** -0.5)
    f = lb + (1.0 - lb) * jax.nn.sigmoid(fraw)
    k = 1.0 - f
    g = jnp.log(f)
    b = _cumsum_rows(g)
    row = lax.broadcasted_iota(jnp.int32, (c, HGRN_HEAD_DIM), 0)
    b_end = jnp.sum(g, axis=0, keepdims=True)
    b_mid = jnp.sum(jnp.where(row < c // 2, g, 0.0), axis=0, keepdims=True)
    inter = _dot(q * jnp.exp(b), state_t, "nt")
    qt = q * jnp.exp(jnp.minimum(b - b_mid, EXP_CLAMP))
    kt = k * jnp.exp(jnp.minimum(b_mid - b, EXP_CLAMP))
    s = _dot(qt, kt, "nt")
    ti = lax.broadcasted_iota(jnp.int32, (c, c), 0)
    si = lax.broadcasted_iota(jnp.int32, (c, c), 1)
    s = jnp.where(si <= ti, s, 0.0)
    intra = _dot(s, v, "nn")
    k_end = k * jnp.exp(b_end - b)
    new_state_t = state_t * jnp.exp(b_end) + _dot(v, k_end, "tn")
    return new_state_t, inter + intra


def _hgrn_specs(T):
    rows = _tile(T, 1024)
    return rows, T // rows, rows // HGRN_CHUNK


def _hgrn_fwd(u, lower_l, plan=None):
    T = u.shape[0]
    rows, nblk, ncr = _hgrn_specs(T)
    hb = HGRN_WIDTH // HGRN_HEAD_DIM

    def body(q_ref, f_ref, i_ref, lb_ref, o_ref, st_ref, state):
        @pl.when(pl.program_id(1) == 0)
        def _():
            state[...] = jnp.zeros_like(state)

        lb = lb_ref[...]
        for c in range(ncr):
            rs = pl.ds(c * HGRN_CHUNK, HGRN_CHUNK)
            st = state[...]
            st_ref[c] = st
            new, out = _hgrn_chunk(st, q_ref[rs, :], f_ref[rs, :], i_ref[rs, :], lb)
            state[...] = new
            o_ref[rs, :] = out

    blk = (rows, HGRN_HEAD_DIM)
    return _call(
        body,
        name="hgrn_fwd",
        grid=(HGRN_HEADS, nblk),
        in_specs=[
            pl.BlockSpec(blk, lambda h, r: (r, OFF_AQ // 128 + h)),
            pl.BlockSpec(blk, lambda h, r: (r, OFF_AF // 128 + h)),
            pl.BlockSpec(blk, lambda h, r: (r, OFF_AI // 128 + h)),
            pl.BlockSpec((1, HGRN_HEAD_DIM), lambda h, r: (0, h)),
        ],
        out_specs=[
            pl.BlockSpec(blk, lambda h, r: (r, h)),
            pl.BlockSpec((ncr, None, HGRN_HEAD_DIM, HGRN_HEAD_DIM), lambda h, r: (r, h, 0, 0)),
        ],
        out_shape=[_sds((T, HGRN_WIDTH), F32), _sds((T // HGRN_CHUNK, hb, HGRN_HEAD_DIM, HGRN_HEAD_DIM), F32)],
        scratch_shapes=[pltpu.VMEM((HGRN_HEAD_DIM, HGRN_HEAD_DIM), F32)],
        sem=("parallel", "arbitrary"),
        args=(u, u, u, lower_l),
        plan=plan,
    )


def _hgrn_bwd(u, lower_l, states, do_raw):
    T = u.shape[0]
    rows, nblk, ncr = _hgrn_specs(T)

    def body(q_ref, f_ref, i_ref, lb_ref, st_ref, do_ref, dq_ref, df_ref, di_ref, dlb_ref, dstate):
        @pl.when(pl.program_id(1) == 0)
        def _():
            dstate[...] = jnp.zeros_like(dstate)
            dlb_ref[...] = jnp.zeros_like(dlb_ref)

        lb = lb_ref[...]
        for c in reversed(range(ncr)):
            rs = pl.ds(c * HGRN_CHUNK, HGRN_CHUNK)
            _, vjp = jax.vjp(_hgrn_chunk, st_ref[c], q_ref[rs, :], f_ref[rs, :], i_ref[rs, :], lb)
            dst, dq, df, dv, dlb = vjp((dstate[...], do_ref[rs, :]))
            dstate[...] = dst
            dq_ref[rs, :] = dq.astype(BF16)
            df_ref[rs, :] = df.astype(BF16)
            di_ref[rs, :] = dv.astype(BF16)
            dlb_ref[...] += dlb

    blk = (rows, HGRN_HEAD_DIM)
    last = nblk - 1
    out_blk = pl.BlockSpec(blk, lambda h, r: (last - r, h))
    return pl.pallas_call(
        body,
        name="hgrn_bwd",
        grid=(HGRN_HEADS, nblk),
        in_specs=[
            pl.BlockSpec(blk, lambda h, r: (last - r, OFF_AQ // 128 + h)),
            pl.BlockSpec(blk, lambda h, r: (last - r, OFF_AF // 128 + h)),
            pl.BlockSpec(blk, lambda h, r: (last - r, OFF_AI // 128 + h)),
            pl.BlockSpec((1, HGRN_HEAD_DIM), lambda h, r: (0, h)),
            pl.BlockSpec((ncr, None, HGRN_HEAD_DIM, HGRN_HEAD_DIM), lambda h, r: (last - r, h, 0, 0)),
            out_blk,
        ],
        out_specs=[out_blk, out_blk, out_blk, pl.BlockSpec((1, HGRN_HEAD_DIM), lambda h, r: (0, h))],
        out_shape=[_sds((T, HGRN_WIDTH), BF16)] * 3 + [_sds((1, HGRN_WIDTH), F32)],
        scratch_shapes=[pltpu.VMEM((HGRN_HEAD_DIM, HGRN_HEAD_DIM), F32)],
        compiler_params=_cp(("parallel", "arbitrary")),
    )(u, u, u, lower_l, states, do_raw)


QROWS = ATTN_GROUP * WINDOW


def _attn_block(q, kp, kc, vp, vc, bp, bc, sink, first):
    qs = q * (ATTN_HEAD_DIM ** -0.5)
    sc = _dot(qs, kc, "nt") + bc
    m = jnp.maximum(jnp.max(sc, axis=-1, keepdims=True), sink)
    if not first:
        sp = _dot(qs, kp, "nt") + bp
        m = jnp.maximum(m, jnp.max(sp, axis=-1, keepdims=True))
    m = lax.stop_gradient(m)
    pc = jnp.exp(sc - m)
    den = jnp.sum(pc, axis=-1, keepdims=True) + jnp.exp(sink - m)
    o = _dot(pc, vc, "nn")
    if not first:
        pp = jnp.exp(sp - m)
        den = den + jnp.sum(pp, axis=-1, keepdims=True)
        o = o + _dot(pp, vp, "nn")
    return o * (1.0 / den)


def _first_or_later(n, run):
    @pl.when(n == 0)
    def _():
        run(True)

    @pl.when(n > 0)
    def _():
        run(False)


def _attn_in_specs():
    q_spec = pl.BlockSpec((WINDOW, ATTN_WIDTH), lambda n: (n, OFF_BQ // ATTN_WIDTH))
    k_cur = pl.BlockSpec((WINDOW, KV_WIDTH), lambda n: (n, OFF_BK // KV_WIDTH))
    k_prev = pl.BlockSpec((WINDOW, KV_WIDTH), lambda n: (jnp.maximum(n - 1, 0), OFF_BK // KV_WIDTH))
    v_cur = pl.BlockSpec((WINDOW, KV_WIDTH), lambda n: (n, OFF_BV // KV_WIDTH))
    v_prev = pl.BlockSpec((WINDOW, KV_WIDTH), lambda n: (jnp.maximum(n - 1, 0), OFF_BV // KV_WIDTH))
    bias = pl.BlockSpec((ATTN_KV_HEADS, QROWS, 2 * WINDOW), lambda n: (0, 0, 0))
    sink = pl.BlockSpec((ATTN_KV_HEADS, QROWS, 1), lambda n: (0, 0, 0))
    return [q_spec, k_prev, k_cur, v_prev, v_cur, bias, sink]


def _head_cols(a):
    return slice(a * ATTN_HEAD_DIM, (a + 1) * ATTN_HEAD_DIM)


def _group_rows(ref, h):
    return jnp.concatenate([ref[:, _head_cols(ATTN_GROUP * h + g)] for g in range(ATTN_GROUP)], axis=0)


def _attn_fwd(u, bias4, sink4):
    T = u.shape[0]

    def body(q_ref, kp_ref, kc_ref, vp_ref, vc_ref, b_ref, s_ref, o_ref):
        def run(first):
            for h in range(ATTN_KV_HEADS):
                hs = _head_cols(h)
                o = _attn_block(_group_rows(q_ref, h), kp_ref[:, hs], kc_ref[:, hs], vp_ref[:, hs], vc_ref[:, hs],
                                b_ref[h, :, :WINDOW], b_ref[h, :, WINDOW:], s_ref[h], first)
                for g in range(ATTN_GROUP):
                    o_ref[:, _head_cols(ATTN_GROUP * h + g)] = o[g * WINDOW:(g + 1) * WINDOW]

        _first_or_later(pl.program_id(0), run)

    return pl.pallas_call(
        body,
        name="attn_fwd",
        grid=(T // WINDOW,),
        in_specs=_attn_in_specs(),
        out_specs=pl.BlockSpec((WINDOW, ATTN_WIDTH), lambda n: (n, 0)),
        out_shape=_sds((T, ATTN_WIDTH), F32),
        compiler_params=_cp(("parallel",)),
    )(u, u, u, u, u, bias4, sink4)


def _attn_bwd(u, bias4, sink4, do, plan=None):
    T = u.shape[0]
    act = pl.BlockSpec((WINDOW, ATTN_WIDTH), lambda n: (n, 0))
    kv = pl.BlockSpec((WINDOW, KV_WIDTH), lambda n: (n, 0))
    in_specs = _attn_in_specs()
    bias, sink = in_specs[5], in_specs[6]

    def body(q_ref, kp_ref, kc_ref, vp_ref, vc_ref, b_ref, s_ref, do_ref,
             dq_ref, dkc_ref, dkp_ref, dvc_ref, dvp_ref, db_ref, ds_ref):
        n = pl.program_id(0)

        @pl.when(n == 0)
        def _():
            db_ref[...] = jnp.zeros_like(db_ref)
            ds_ref[...] = jnp.zeros_like(ds_ref)

        def run(first):
            dqs, dkps, dkcs, dvps, dvcs = [], [], [], [], []
            for h in range(ATTN_KV_HEADS):
                hs = _head_cols(h)
                _, vjp = jax.vjp(
                    functools.partial(_attn_block, first=first),
                    _group_rows(q_ref, h), kp_ref[:, hs], kc_ref[:, hs], vp_ref[:, hs], vc_ref[:, hs],
                    b_ref[h, :, :WINDOW], b_ref[h, :, WINDOW:], s_ref[h])
                dq, dkp, dkc, dvp, dvc, dbp, dbc, dsink = vjp(_group_rows(do_ref, h))
                dqs += [dq[g * WINDOW:(g + 1) * WINDOW] for g in range(ATTN_GROUP)]
                dkps.append(dkp)
                dkcs.append(dkc)
                dvps.append(dvp)
                dvcs.append(dvc)
                if not first:
                    db_ref[h, :, :WINDOW] += dbp
                db_ref[h, :, WINDOW:] += dbc
                ds_ref[h] += dsink
            dq_ref[...] = jnp.concatenate(dqs, axis=1).astype(BF16)
            dkc_ref[...] = jnp.concatenate(dkcs, axis=1)
            dkp_ref[...] = jnp.concatenate(dkps, axis=1)
            dvc_ref[...] = jnp.concatenate(dvcs, axis=1)
            dvp_ref[...] = jnp.concatenate(dvps, axis=1)

        _first_or_later(n, run)

    kv_sds = _sds((T, KV_WIDTH), F32)
    return _call(
        body,
        name="attn_bwd",
        grid=(T // WINDOW,),
        in_specs=in_specs + [act],
        out_specs=[act, kv, kv, kv, kv, bias, sink],
        out_shape=[_sds((T, ATTN_WIDTH), BF16), kv_sds, kv_sds, kv_sds, kv_sds,
                   _sds((ATTN_KV_HEADS, QROWS, 2 * WINDOW), F32), _sds((ATTN_KV_HEADS, QROWS, 1), F32)],
        sem=("arbitrary",),
        args=(u, u, u, u, u, bias4, sink4, do),
        plan=plan,
    )


def _with_next_block_part(cur, nxt):
    pad = jnp.zeros_like(nxt[:WINDOW])
    return (cur + jnp.concatenate([nxt[WINDOW:], pad], axis=0)).astype(BF16)


MIX_COLS = 512


def _silu(x):
    return x * jax.nn.sigmoid(x)


def _silu_grad(x):
    s = jax.nn.sigmoid(x)
    return s * (1.0 + x * (1.0 - s))


def _shift_rows_down(h, first, second):
    n = h.shape[0]
    row = lax.broadcasted_iota(jnp.int32, h.shape, 0)
    s1 = jnp.where(row == 0, first, pltpu.roll(h, 1, 0))
    s2 = jnp.where(row == 0, second, jnp.where(row == 1, first, pltpu.roll(h, 2, 0)))
    del n
    return s1, s2


def _shift_rows_up(h, first, second):
    n = h.shape[0]
    row = lax.broadcasted_iota(jnp.int32, h.shape, 0)
    s1 = jnp.where(row == n - 1, first, pltpu.roll(h, n - 1, 0))
    s2 = jnp.where(row == n - 1, second, jnp.where(row == n - 2, first, pltpu.roll(h, n - 2, 0)))
    return s1, s2


def _mix_rows(T):
    return _tile(T, 256)


def _mix_fwd(u, o_raw, o_b, gn_l, cw_l):
    T = u.shape[0]
    tr = _mix_rows(T)
    nrow = T // tr
    hr = tr // 8

    def ucol(off):
        return pl.BlockSpec((tr, MIX_COLS), lambda i, j, off=off: (i, off // MIX_COLS + j))

    def uprev(off):
        return pl.BlockSpec((8, MIX_COLS), lambda i, j, off=off: (jnp.maximum(i * hr - 1, 0), off // MIX_COLS + j))

    act = pl.BlockSpec((tr, MIX_COLS), lambda i, j: (i, j))
    par = lambda rows: pl.BlockSpec((rows, MIX_COLS), lambda i, j: (0, j))

    def body(oraw_ref, ag_ref, ob_ref, bg_ref, cb_ref, cc_ref, cx_ref, cg_ref, ccp_ref, cxp_ref, gn_ref, cw_ref,
             ha_ref, hb_ref, hc_ref, hat_ref, hbt_ref, hct_ref):
        ag = _silu(ag_ref[...])
        for h in range(MIX_COLS // HGRN_HEAD_DIM):
            cs = slice(h * HGRN_HEAD_DIM, (h + 1) * HGRN_HEAD_DIM)
            o = oraw_ref[:, cs]
            nrm = o * lax.rsqrt(jnp.mean(o * o, axis=-1, keepdims=True) + RMS_EPS)
            ha = nrm * gn_ref[:, cs] * ag[:, cs]
            ha_ref[:, cs] = ha.astype(BF16)
            hat_ref[cs, :] = ha.T.astype(BF16)
        hb = ob_ref[...] * _silu(bg_ref[...])
        hb_ref[...] = hb.astype(BF16)
        hbt_ref[...] = hb.T.astype(BF16)
        keep = (pl.program_id(0) > 0).astype(F32)
        hcur = cc_ref[...] * cx_ref[...]
        p1 = ccp_ref[7:8, :] * cxp_ref[7:8, :] * keep
        p2 = ccp_ref[6:7, :] * cxp_ref[6:7, :] * keep
        s1, s2 = _shift_rows_down(hcur, p1, p2)
        y = cw_ref[0:1, :] * s2 + cw_ref[1:2, :] * s1 + cw_ref[2:3, :] * hcur
        hc = cb_ref[...] * y * _silu(cg_ref[...])
        hc_ref[...] = hc.astype(BF16)
        hct_ref[...] = hc.T.astype(BF16)

    out = _sds((T, HGRN_WIDTH), BF16)
    out_t = _sds((HGRN_WIDTH, T), BF16)
    act_t = pl.BlockSpec((MIX_COLS, tr), lambda i, j: (j, i))
    return pl.pallas_call(
        body,
        name="mix_fwd",
        grid=(nrow, HGRN_WIDTH // MIX_COLS),
        in_specs=[act, ucol(OFF_AG), act, ucol(OFF_BG), ucol(OFF_CB), ucol(OFF_CC), ucol(OFF_CX), ucol(OFF_CG),
                  uprev(OFF_CC), uprev(OFF_CX), par(1), par(CONV_K)],
        out_specs=[act, act, act, act_t, act_t, act_t],
        out_shape=[out, out, out, out_t, out_t, out_t],
        compiler_params=_cp(("parallel", "parallel")),
    )(o_raw, u, o_b, u, u, u, u, u, u, u, gn_l, cw_l)


def _mix_bwd(u, o_raw, o_b, gn_l, cw_l, dha, dhb, dhc):
    T = u.shape[0]
    tr = _mix_rows(T)
    nrow = T // tr
    hr = tr // 8
    last_halo = T // 8 - 1

    def ucol(off):
        return pl.BlockSpec((tr, MIX_COLS), lambda j, i, off=off: (i, off // MIX_COLS + j))

    def uprev(off):
        return pl.BlockSpec((8, MIX_COLS), lambda j, i, off=off: (jnp.maximum(i * hr - 1, 0), off // MIX_COLS + j))

    def unext(off):
        return pl.BlockSpec((8, MIX_COLS), lambda j, i, off=off: (jnp.minimum((i + 1) * hr, last_halo), off // MIX_COLS + j))

    act = pl.BlockSpec((tr, MIX_COLS), lambda j, i: (i, j))
    act_next = pl.BlockSpec((8, MIX_COLS), lambda j, i: (jnp.minimum((i + 1) * hr, last_halo), j))
    par = lambda rows: pl.BlockSpec((rows, MIX_COLS), lambda j, i: (0, j))

    def body(oraw_ref, ag_ref, ob_ref, bg_ref, cb_ref, cc_ref, cx_ref, cg_ref, ccp_ref, cxp_ref,
             cbn_ref, cgn_ref, dhcn_ref, gn_ref, cw_ref, dha_ref, dhb_ref, dhc_ref,
             doraw_ref, dob_ref, dag_ref, dbg_ref, dcb_ref, dcc_ref, dcx_ref, dcg_ref, dgn_ref, dcw_ref):
        i = pl.program_id(1)

        @pl.when(i == 0)
        def _():
            dgn_ref[...] = jnp.zeros_like(dgn_ref)
            dcw_ref[...] = jnp.zeros_like(dcw_ref)

        ag = ag_ref[...]
        sag = _silu(ag)
        dha = dha_ref[...]
        for h in range(MIX_COLS // HGRN_HEAD_DIM):
            cs = slice(h * HGRN_HEAD_DIM, (h + 1) * HGRN_HEAD_DIM)
            o = oraw_ref[:, cs]
            rs = lax.rsqrt(jnp.mean(o * o, axis=-1, keepdims=True) + RMS_EPS)
            nrm = o * rs
            gn = gn_ref[:, cs]
            d = dha[:, cs]
            dag_ref[:, cs] = (d * nrm * gn * _silu_grad(ag[:, cs])).astype(BF16)
            dgn_ref[:, cs] += jnp.sum(d * sag[:, cs] * nrm, axis=0, keepdims=True)
            dn = d * sag[:, cs] * gn
            doraw_ref[:, cs] = rs * (dn - nrm * jnp.mean(dn * nrm, axis=-1, keepdims=True))
        bg = bg_ref[...]
        dhb = dhb_ref[...]
        dob_ref[...] = dhb * _silu(bg)
        dbg_ref[...] = (dhb * ob_ref[...] * _silu_grad(bg)).astype(BF16)
        keep_prev = (i > 0).astype(F32)
        keep_next = (i < nrow - 1).astype(F32)
        cc, cx, cb, cg = cc_ref[...], cx_ref[...], cb_ref[...], cg_ref[...]
        hcur = cc * cx
        p1 = ccp_ref[7:8, :] * cxp_ref[7:8, :] * keep_prev
        p2 = ccp_ref[6:7, :] * cxp_ref[6:7, :] * keep_prev
        s1, s2 = _shift_rows_down(hcur, p1, p2)
        w0, w1, w2 = cw_ref[0:1, :], cw_ref[1:2, :], cw_ref[2:3, :]
        y = w0 * s2 + w1 * s1 + w2 * hcur
        dhc = dhc_ref[...]
        scg = _silu(cg)
        doc = dhc * scg
        dcg_ref[...] = (dhc * cb * y * _silu_grad(cg)).astype(BF16)
        dcb_ref[...] = (doc * y).astype(BF16)
        dy = doc * cb
        n1 = dhcn_ref[0:1, :] * _silu(cgn_ref[0:1, :]) * cbn_ref[0:1, :] * keep_next
        n2 = dhcn_ref[1:2, :] * _silu(cgn_ref[1:2, :]) * cbn_ref[1:2, :] * keep_next
        u1, u2 = _shift_rows_up(dy, n1, n2)
        dh = w2 * dy + w1 * u1 + w0 * u2
        dcc_ref[...] = (dh * cx).astype(BF16)
        dcx_ref[...] = (dh * cc).astype(BF16)
        dcw_ref[0:1, :] += jnp.sum(dy * s2, axis=0, keepdims=True)
        dcw_ref[1:2, :] += jnp.sum(dy * s1, axis=0, keepdims=True)
        dcw_ref[2:3, :] += jnp.sum(dy * hcur, axis=0, keepdims=True)

    f32o, bf = _sds((T, HGRN_WIDTH), F32), _sds((T, HGRN_WIDTH), BF16)
    return pl.pallas_call(
        body,
        name="mix_bwd",
        grid=(HGRN_WIDTH // MIX_COLS, nrow),
        in_specs=[act, ucol(OFF_AG), act, ucol(OFF_BG), ucol(OFF_CB), ucol(OFF_CC), ucol(OFF_CX), ucol(OFF_CG),
                  uprev(OFF_CC), uprev(OFF_CX), unext(OFF_CB), unext(OFF_CG), act_next, par(1), par(CONV_K),
                  act, act, act],
        out_specs=[act, act, act, act, act, act, act, act, par(1), par(8)],
        out_shape=[f32o, f32o, bf, bf, bf, bf, bf, bf, _sds((1, HGRN_WIDTH), F32), _sds((8, HGRN_WIDTH), F32)],
        compiler_params=_cp(("parallel", "arbitrary")),
    )(o_raw, u, o_b, u, u, u, u, u, u, u, u, u, dhc, gn_l, cw_l, dha, dhb, dhc)


def _merge_specs(T, order):
    tr = _tile(T, 256)

    def ucol(off):
        if order == "ij":
            return pl.BlockSpec((tr, MIX_COLS), lambda i, j, off=off: (i, off // MIX_COLS + j))
        return pl.BlockSpec((tr, MIX_COLS), lambda j, i, off=off: (i, off // MIX_COLS + j))

    act = pl.BlockSpec((tr, MIX_COLS), (lambda i, j: (i, j)) if order == "ij" else (lambda j, i: (i, j)))
    return tr, ucol, act


def _merge_fwd(u, ya, yb, yc):
    T = u.shape[0]
    tr, ucol, act = _merge_specs(T, "ij")

    def body(ma_ref, mb_ref, mc_ref, ya_ref, yb_ref, yc_ref, o_ref, ot_ref):
        merged = (jax.nn.sigmoid(ma_ref[...]) * ya_ref[...] + jax.nn.sigmoid(mb_ref[...]) * yb_ref[...]
                  + jax.nn.sigmoid(mc_ref[...]) * yc_ref[...])
        o_ref[...] = merged.astype(BF16)
        ot_ref[...] = merged.T.astype(BF16)

    return pl.pallas_call(
        body,
        name="merge_fwd",
        grid=(T // tr, D_MODEL // MIX_COLS),
        in_specs=[ucol(OFF_MA), ucol(OFF_MB), ucol(OFF_MC), act, act, act],
        out_specs=[act, pl.BlockSpec((MIX_COLS, tr), lambda i, j: (j, i))],
        out_shape=[_sds((T, D_MODEL), BF16), _sds((D_MODEL, T), BF16)],
        compiler_params=_cp(("parallel", "parallel")),
    )(u, u, u, ya, yb, yc)


def _merge_bwd(u, ya, yb, yc, dmerged, plan=None):
    T = u.shape[0]
    tr, ucol, act = _merge_specs(T, "ij")

    def body(ma_ref, mb_ref, mc_ref, ya_ref, yb_ref, yc_ref, dm_ref, dya_ref, dyb_ref, dyc_ref, dma_ref, dmb_ref, dmc_ref):
        dm = dm_ref[...]
        for m_ref, y_ref, dy_ref, dg_ref in ((ma_ref, ya_ref, dya_ref, dma_ref), (mb_ref, yb_ref, dyb_ref, dmb_ref),
                                             (mc_ref, yc_ref, dyc_ref, dmc_ref)):
            s = jax.nn.sigmoid(m_ref[...])
            dy_ref[...] = (dm * s).astype(BF16)
            dg_ref[...] = (dm * y_ref[...] * s * (1.0 - s)).astype(BF16)

    out = _sds((T, D_MODEL), BF16)
    return _call(
        body,
        name="merge_bwd",
        grid=(T // tr, D_MODEL // MIX_COLS),
        in_specs=[ucol(OFF_MA), ucol(OFF_MB), ucol(OFF_MC), act, act, act, act],
        out_specs=[act] * 6,
        out_shape=[out] * 6,
        sem=("parallel", "parallel"),
        args=(u, u, u, ya, yb, yc, dmerged),
        plan=plan,
    )


def _ln_fwd(x, y, g_l, b_l):
    T = x.shape[0]
    tr = _tile(T, 256)
    row = pl.BlockSpec((tr, D_MODEL), lambda i: (i, 0))
    col = pl.BlockSpec((D_MODEL, tr), lambda i: (0, i))
    par = pl.BlockSpec((1, D_MODEL), lambda i: (0, 0))

    def body(x_ref, y_ref, g_ref, b_ref, o_ref, z_ref, ob_ref, ot_ref):
        z = ALPHA * x_ref[...] + y_ref[...]
        z_ref[...] = z
        mu = jnp.mean(z, axis=-1, keepdims=True)
        zc = z - mu
        var = jnp.mean(zc * zc, axis=-1, keepdims=True)
        o = zc * lax.rsqrt(var + LN_EPS) * g_ref[...] + b_ref[...]
        o_ref[...] = o
        ob_ref[...] = o.astype(BF16)
        ot_ref[...] = o.T.astype(BF16)

    return pl.pallas_call(
        body,
        name="ln_fwd",
        grid=(T // tr,),
        in_specs=[row, row, par, par],
        out_specs=[row, row, row, col],
        out_shape=[_sds((T, D_MODEL), F32)] * 2 + [_sds((T, D_MODEL), BF16), _sds((D_MODEL, T), BF16)],
        compiler_params=_cp(("parallel",)),
    )(x, y, g_l, b_l)


def _operand_forms(x):
    T = x.shape[0]
    tr = _tile(T, 256)
    row = pl.BlockSpec((tr, D_MODEL), lambda i: (i, 0))
    col = pl.BlockSpec((D_MODEL, tr), lambda i: (0, i))

    def body(x_ref, xb_ref, xt_ref):
        xv = x_ref[...]
        xb_ref[...] = xv.astype(BF16)
        xt_ref[...] = xv.T.astype(BF16)

    return pl.pallas_call(
        body,
        name="operand_forms",
        grid=(T // tr,),
        in_specs=[row],
        out_specs=[row, col],
        out_shape=[_sds((T, D_MODEL), BF16), _sds((D_MODEL, T), BF16)],
        compiler_params=_cp(("parallel",)),
    )(x)


def _ln_bwd(z, dxn, g_l):
    T = z.shape[0]
    tr = _tile(T, 256)
    row = pl.BlockSpec((tr, D_MODEL), lambda i: (i, 0))
    par = pl.BlockSpec((1, D_MODEL), lambda i: (0, 0))

    def body(z_ref, d_ref, g_ref, dz_ref, dzb_ref, dg_ref, db_ref):
        @pl.when(pl.program_id(0) == 0)
        def _():
            dg_ref[...] = jnp.zeros_like(dg_ref)
            db_ref[...] = jnp.zeros_like(db_ref)

        z = z_ref[...]
        d = d_ref[...]
        mu = jnp.mean(z, axis=-1, keepdims=True)
        zc = z - mu
        rstd = lax.rsqrt(jnp.mean(zc * zc, axis=-1, keepdims=True) + LN_EPS)
        zh = zc * rstd
        dg_ref[...] += jnp.sum(d * zh, axis=0, keepdims=True)
        db_ref[...] += jnp.sum(d, axis=0, keepdims=True)
        dh = d * g_ref[...]
        dz = rstd * (dh - jnp.mean(dh, axis=-1, keepdims=True) - zh * jnp.mean(dh * zh, axis=-1, keepdims=True))
        dz_ref[...] = dz
        dzb_ref[...] = dz.astype(BF16)

    return pl.pallas_call(
        body,
        name="ln_bwd",
        grid=(T // tr,),
        in_specs=[row, row, par],
        out_specs=[row, row, par, par],
        out_shape=[_sds((T, D_MODEL), F32), _sds((T, D_MODEL), BF16), _sds((1, D_MODEL), F32), _sds((1, D_MODEL), F32)],
        compiler_params=_cp(("arbitrary",)),
    )(z, dxn, g_l)


def _loss_head(y, target):
    T = y.shape[0]
    tr = _tile(T, 256)
    row = pl.BlockSpec((tr, D_MODEL), lambda i: (i, 0))
    acc = pl.BlockSpec((8, 128), lambda i: (0, 0))

    def body(y_ref, t_ref, l_ref, d_ref):
        @pl.when(pl.program_id(0) == 0)
        def _():
            l_ref[...] = jnp.zeros_like(l_ref)

        err = y_ref[...] - t_ref[...]
        d_ref[...] = err * (1.0 / D_MODEL)
        part = 0.5 * jnp.sum(jnp.sum(err * err, axis=-1, keepdims=True) * (1.0 / D_MODEL), axis=0, keepdims=True)
        r = lax.broadcasted_iota(jnp.int32, (8, 128), 0)
        c = lax.broadcasted_iota(jnp.int32, (8, 128), 1)
        l_ref[...] += jnp.where((r == 0) & (c == 0), part, 0.0)

    return pl.pallas_call(
        body,
        name="loss_head",
        grid=(T // tr,),
        in_specs=[row, row],
        out_specs=[acc, row],
        out_shape=[_sds((8, 128), F32), _sds((T, D_MODEL), F32)],
        compiler_params=_cp(("arbitrary",)),
    )(y, target)


ADAMW_BLOCK_ELEMS = 512 * 1024


def _adamw(w, g, m, v, name):
    shape = w.shape
    cols = shape[-1]
    rows = math.prod(shape[:-1])
    flat = lambda a: a.reshape(rows, cols)
    if rows * cols <= ADAMW_BLOCK_ELEMS or rows % 8:
        tr = rows
    else:
        tr = 8
        while rows % (tr * 2) == 0 and tr * 2 * cols <= ADAMW_BLOCK_ELEMS:
            tr *= 2
    blk = pl.BlockSpec((tr, cols), lambda i: (i, 0))
    c1 = 1.0 - ADAM_B1 ** ADAM_STEP
    c2 = 1.0 - ADAM_B2 ** ADAM_STEP

    def body(w_ref, g_ref, m_ref, v_ref, d_ref, nm_ref, nv_ref):
        gg = g_ref[...]
        nm = ADAM_B1 * m_ref[...] + (1.0 - ADAM_B1) * gg
        nv = ADAM_B2 * v_ref[...] + (1.0 - ADAM_B2) * (gg * gg)
        nm_ref[...] = nm
        nv_ref[...] = nv
        d_ref[...] = -ADAM_LR * ((nm / c1) / (jnp.sqrt(nv / c2) + ADAM_EPS) + ADAM_WD * w_ref[...])

    outs = pl.pallas_call(
        body,
        name=name,
        grid=(rows // tr,),
        in_specs=[blk] * 4,
        out_specs=[blk] * 3,
        out_shape=[_sds((rows, cols), F32)] * 3,
        compiler_params=_cp(("parallel",)),
    )(flat(w), flat(g), flat(m), flat(v))
    return tuple(o.reshape(shape) for o in outs)


def _t5_bucket(dist):
    max_exact = N_BUCKETS // 2
    logd = jnp.log(jnp.maximum(dist, 1).astype(F32) / max_exact) / math.log(MAX_DISTANCE / max_exact)
    large = jnp.minimum(max_exact + (logd * (N_BUCKETS - max_exact)).astype(jnp.int32), N_BUCKETS - 1)
    return jnp.where(dist < max_exact, dist, large)


def _band_bias(rel_bias):
    i = jnp.arange(WINDOW)[:, None]
    j = jnp.arange(2 * WINDOW)[None, :]
    bucket = _t5_bucket(jnp.clip(WINDOW + i - j, 0, WINDOW - 1))
    onehot = (bucket[:, :, None] == jnp.arange(N_BUCKETS)[None, None, :]).astype(F32)
    bias = jnp.einsum("ijb,bh->hij", onehot, rel_bias.astype(F32), precision=lax.Precision.HIGHEST)
    rel = WINDOW + i - j
    return jnp.where(((rel >= 0) & (rel < WINDOW))[None], bias, MASK_VALUE)


def _lower_bounds(lb_param):
    soft = jax.nn.softmax(lb_param.astype(F32), axis=0)
    return jnp.cumsum(soft, axis=0) - soft[0:1]


def _mm_rows(T):
    return _tile(T, 1024)


def _layer_fwd(xs, ex, layer, lower_l, bias4, sink4_l, gn_l, cw_l, lng_l, lnb_l):
    x, xb, xt = xs
    T = x.shape[0]
    tm = _mm_rows(T)
    plan = ex.fwd_plan(layer)
    u = _matmul(xb, ex.weights[layer]["w_in"][0], mode="nn", tm=_tile(T, 2048), tn=768, tk=D_MODEL, name="mm_u", plan=plan)
    if plan is not None:
        u, landed = u
        plan = ex.fwd_pass_plan(layer, landed)
    (o_raw, states), landed = _hgrn_fwd(u, lower_l, plan=plan)
    if plan is not None:
        ex.fwd_landed(layer, landed)
    w_proj_l, w_out_l = ex.weights[layer]["w_proj"], ex.weights[layer]["w_out"][0]
    o_b = _attn_fwd(u, bias4, sink4_l)
    ha, hb, hc, hat, hbt, hct = _mix_fwd(u, o_raw, o_b, gn_l, cw_l)
    ys = [_matmul(h, w_proj_l, mode="nn", tm=tm, tn=1024, tk=HGRN_WIDTH, b_idx=i, name="mm_proj", out_dtype=BF16)
          for i, h in enumerate((ha, hb, hc))]
    merged, merged_t = _merge_fwd(u, *ys)
    y = _matmul(merged, w_out_l, mode="nn", tm=tm, tn=1024, tk=D_MODEL, name="mm_out")
    xn, z, xnb, xnt = _ln_fwd(x, y, lng_l, lnb_l)
    saved = dict(xt=xt, u=u, o_raw=o_raw, states=states, o_b=o_b, hts=(hat, hbt, hct), ys=ys,
                 merged_t=merged_t, z=z)
    return (xn, xnb, xnt), saved


def _layer_bwd(dxn, s, ex, layer, lower_l, bias4, sink4_l, gn_l, cw_l, lng_l):
    T = dxn.shape[0]
    tm = _mm_rows(T)
    u = s["u"]
    w = ex.weights[layer]
    w_in_l, w_proj_l, w_out_l = w["w_in"][0], w["w_proj"], w["w_out"][0]
    dz, dzb, d_lng, d_lnb = _ln_bwd(s["z"], dxn, lng_l)
    dmerged = _matmul(dzb, w_out_l, mode="nt", tm=tm, tn=1024, tk=D_MODEL, name="mm_dmerged")
    g_w_out = _matmul(s["merged_t"], dzb, mode="nn", tm=1024, tn=1024, tk=2048, name="mm_gw_out")
    plan = ex.pair_plan(layer)
    (*dys, dma, dmb, dmc), landed = _merge_bwd(u, *s["ys"], dmerged, plan=plan)
    if plan is not None:
        ex.pair_landed(layer, landed)
    dhs =[_matmul(dy, w_proj_l, mode="nt", tm=tm, tn=1024, tk=D_MODEL, b_idx=i, name="mm_dh") for i, dy in enumerate(dys)]
    g_w_proj = jnp.stack([_matmul(ht, dy, mode="nn", tm=1024, tn=1024, tk=2048, name="mm_gw_proj")
                          for ht, dy in zip(s["hts"], dys)])
    do_raw, do_b, dag, dbg, dcb, dcc, dcx, dcg, d_gn, d_cw = _mix_bwd(u, s["o_raw"], s["o_b"], gn_l, cw_l, *dhs)
    daq, daf, dai, d_lower = _hgrn_bwd(u, lower_l, s["states"], do_raw)
    plan = ex.slab_plan(layer)
    (dbq, dkc, dkp, dvc, dvp, d_bias4, d_sink4), landed = _attn_bwd(u, bias4, sink4_l, do_b, plan=plan)
    if plan is not None:
        ex.slab_landed(layer, landed)
    dbk = _with_next_block_part(dkc, dkp)
    dbv = _with_next_block_part(dvc, dvp)
    du = jnp.concatenate([daq, daf, dai, dag, dbq, dbk, dbv, dbg, dcb, dcc, dcx, dcg, dma, dmb, dmc], axis=1)
    g_w_in = _matmul(s["xt"], du, mode="nn", tm=1024, tn=768, tk=4096, name="mm_gw_in")
    ex.grads_ready(layer, dict(w_in=g_w_in[None], w_proj=g_w_proj, w_out=g_w_out[None]))
    assemble = ex.assemble_plan(layer)
    tail = ex.tail_plan() if layer == 0 else None
    carried = [p for p in (assemble, tail) if p is not None]
    plan = _merge_plans(carried) if carried else None
    dx = _matmul(du, w_in_l, mode="nt", tm=_tile(T, 512), tn=1024, tk=5632, name="mm_dx", add=dz, add_scale=ALPHA, plan=plan)
    if plan is not None:
        dx, landed = dx
        n_assemble = 0 if assemble is None else len(assemble.out_shapes)
        if assemble is not None:
            ex.assembled(layer, landed[:n_assemble])
        if tail is not None:
            ex.slab_landed(-1, landed[n_assemble:])
    d_sinks = jnp.sum(d_sink4.reshape(ATTN_HEADS, WINDOW), axis=-1)
    small = dict(lower=d_lower[0], gn=d_gn[0], sinks=d_sinks, cw=d_cw[:CONV_K], bias=d_bias4.reshape(ATTN_HEADS, WINDOW, 2 * WINDOW),
                 lng=d_lng[0], lnb=d_lnb[0])
    return dx, small


def _local_step(x, target, ex, lb_param, hgrn_norm_g, attn_sinks, conv_w_full, rel_bias, ln_g, ln_b):
    lower, lower_vjp = jax.vjp(_lower_bounds, lb_param)
    bias, bias_vjp = jax.vjp(_band_bias, rel_bias)
    bias4 = bias.reshape(ATTN_KV_HEADS, QROWS, 2 * WINDOW)
    sink4 = jnp.broadcast_to(attn_sinks.reshape(DEPTH, ATTN_HEADS, 1, 1), (DEPTH, ATTN_HEADS, WINDOW, 1)).reshape(
        DEPTH, ATTN_KV_HEADS, QROWS, 1)
    row = lambda a, l: a[l:l + 1]
    saved = []
    hs = (x, *_operand_forms(x))
    for l in range(DEPTH):
        hs, s = _layer_fwd(hs, ex, l, row(lower, l), bias4, sink4[l], row(hgrn_norm_g, l), conv_w_full[l], row(ln_g, l), row(ln_b, l))
        saved.append(s)
    loss_blk, dh = _loss_head(hs[0], target)
    smalls = [None] * DEPTH
    for l in reversed(range(DEPTH)):
        dh, smalls[l] = _layer_bwd(dh, saved[l], ex, l, row(lower, l), bias4, sink4[l], row(hgrn_norm_g, l), conv_w_full[l], row(ln_g, l))
    stack = lambda k: jnp.stack([sm[k] for sm in smalls])
    d_bias = smalls[0]["bias"] + smalls[1]["bias"] + smalls[2]["bias"] + smalls[3]["bias"]
    small = dict(
        lb_param=lower_vjp(stack("lower"))[0], hgrn_norm_g=stack("gn"), attn_sinks=stack("sinks"), conv_w=stack("cw"),
        rel_bias=bias_vjp(d_bias)[0], ln_g=stack("lng"), ln_b=stack("lnb"))
    return loss_blk, dh, small


ANY = pl.BlockSpec(memory_space=pl.ANY)
DMA_SEM = pltpu.SemaphoreType.DMA


def _coords():
    return lax.axis_index("x"), lax.axis_index("y"), lax.axis_index("c")


def _other_chips(x, y):
    return [(1 - x, y), (x, 1 - y), (1 - x, 1 - y)]


def _remote(src, dst, send_sem, recv_sem, device):
    return pltpu.make_async_remote_copy(src_ref=src, dst_ref=dst, send_sem=send_sem, recv_sem=recv_sem,
                                        device_id=device, device_id_type=MESH)


def _sub(ref, axis, index, size):
    idx = [slice(None)] * len(ref.shape)
    idx[axis] = pl.ds(pl.multiple_of(index * size, size), size)
    return ref.at[tuple(idx)]


class _Plan:
    def __init__(self, inputs, out_shapes, aliases, n, make):
        self.inputs, self.out_shapes, self.aliases, self.n, self.make = tuple(inputs), tuple(out_shapes), dict(aliases), n, make


class _Xfer:
    def __init__(self, send, recv=None):
        self.send, self.recv = send, send if recv is None else recv

    def start(self):
        self.send.start()

    def wait(self):
        self.send.wait_send()
        self.recv.wait_recv()


def _merge_plans(plans):
    def make(in_refs, out_refs, send_sems, recv_sems, base):
        out, i0, o0, b0 = [], 0, 0, base
        for p in plans:
            out += p.make(in_refs[i0:i0 + len(p.inputs)], out_refs[o0:o0 + len(p.out_shapes)], send_sems, recv_sems, b0)
            i0, o0, b0 = i0 + len(p.inputs), o0 + len(p.out_shapes), b0 + p.n
        return out

    inputs, out_shapes, aliases = [], [], {}
    for p in plans:
        aliases.update({len(inputs) + k: len(out_shapes) + v for k, v in p.aliases.items()})
        inputs += p.inputs
        out_shapes += p.out_shapes
    return _Plan(inputs, out_shapes, aliases, sum(p.n for p in plans), make)


def _run_plan(plan, name):
    ni, no = len(plan.inputs), len(plan.out_shapes)

    def body(*refs):
        transfers = plan.make(refs[:ni], refs[ni:ni + no], refs[ni + no], refs[ni + no + 1], 0)
        for t in transfers:
            t.start()
        for t in transfers:
            t.wait()

    outs = pl.pallas_call(
        body, name=name, in_specs=[ANY] * ni, out_specs=[ANY] * no, out_shape=list(plan.out_shapes),
        input_output_aliases=plan.aliases, scratch_shapes=[DMA_SEM((plan.n,)), DMA_SEM((plan.n,))],
    )(*plan.inputs)
    return list(outs)


def _gather_send_plan(shards, layer, sax):
    shp = shards.shape[1:]
    hax = 3 - sax
    w, hw = shp[sax], shp[hax] // 2
    out_shape = list(shp)
    out_shape[sax] = w * N_CHIPS

    def make(in_refs, out_refs, send_sems, recv_sems, base):
        (src_ref,), (out_ref,) = in_refs, out_refs
        x, y, c = _coords()
        j = 2 * x + y
        src = src_ref.at[layer]
        own = pltpu.make_async_copy(src, _sub(out_ref, sax, j, w), send_sems.at[base])
        dst = _sub(_sub(out_ref, sax, j, w), hax, c, hw)
        return [own] + [
            _Xfer(_remote(_sub(src, hax, c, hw), dst, send_sems.at[base + 1 + k], recv_sems.at[base + 1 + k], (px, py, c)))
            for k, (px, py) in enumerate(_other_chips(x, y))]

    return _Plan([shards], [_sds(tuple(out_shape), shards.dtype)], {}, 4, make)


def _gather_pass_plan(full, sax):
    hax = 3 - sax
    w, hw = full.shape[sax] // N_CHIPS, full.shape[hax] // 2

    def make(in_refs, out_refs, send_sems, recv_sems, base):
        (out_ref,) = out_refs
        x, y, c = _coords()
        region = lambda slab, half: _sub(_sub(out_ref, sax, slab, w), hax, half, hw)
        out = []
        for k, (px, py) in enumerate(_other_chips(x, y)):
            mine, theirs = region(2 * px + py, c), region(2 * px + py, 1 - c)
            sems = send_sems.at[base + k], recv_sems.at[base + k]
            out.append(_Xfer(_remote(mine, mine, *sems, (x, y, 1 - c)), _remote(theirs, theirs, *sems, (x, y, c))))
        return out

    return _Plan([full], [_sds(full.shape, full.dtype)], {0: 0}, 3, make)


def _pair_exchange_plan(g, hax):
    hw = g.shape[hax] // 2
    out_shape = list(g.shape)
    out_shape[hax] = hw

    def make(in_refs, out_refs, send_sems, recv_sems, base):
        x, y, c = _coords()
        return [_Xfer(_remote(_sub(in_refs[0], hax, 1 - c, hw), out_refs[0], send_sems.at[base], recv_sems.at[base], (x, y, 1 - c)))]

    return _Plan([g], [_sds(tuple(out_shape), g.dtype)], {}, 1, make)


def _add_own_half(place, g, recv, hax, blk, name):
    L, ah, bh = recv.shape
    tr, tc = blk
    nr, nc = ah // tr, bh // tc

    def g_map(l, i, jc, p):
        return (l, i + p[0] * nr, jc) if hax == 1 else (l, i, jc + p[0] * nc)

    def body(p_ref, g_ref, r_ref, o_ref):
        o_ref[...] = (g_ref[...] + r_ref[...]).astype(BF16)

    same = pl.BlockSpec((None, tr, tc), lambda l, i, jc, p: (l, i, jc))
    return pl.pallas_call(
        body,
        name=name,
        grid_spec=pltpu.PrefetchScalarGridSpec(
            num_scalar_prefetch=1, grid=(L, nr, nc),
            in_specs=[pl.BlockSpec((None, tr, tc), g_map), same], out_specs=same),
        out_shape=_sds(recv.shape, BF16),
        compiler_params=_cp(("parallel", "parallel", "parallel")),
    )(place, g, recv)


def _slab_exchange_plan(p, sax):
    w = p.shape[sax] // N_CHIPS
    slab_shape = list(p.shape)
    slab_shape[sax] = w

    def make(in_refs, out_refs, send_sems, recv_sems, base):
        x, y, c = _coords()
        return [_Xfer(_remote(_sub(in_refs[0], sax, 2 * px + py, w), out_refs[0].at[k], send_sems.at[base + k],
                              recv_sems.at[base + k], (px, py, c)))
                for k, (px, py) in enumerate(_other_chips(x, y))]

    return _Plan([p], [_sds((3, *slab_shape), p.dtype)], {}, 3, make)


def _add_slabs(place, g, pair, recv, sax, blk, name):
    hax = 3 - sax
    _, L, a, b = recv.shape
    tr, tc = blk
    nr, nc = a // tr, b // tc

    def g_map(l, i, jc, p):
        return (l, p[0] * nr + i, p[1] * nc + jc) if hax == 1 else (l, p[1] * nr + i, p[0] * nc + jc)

    def pair_map(l, i, jc, p):
        return (l, i, p[1] * nc + jc) if hax == 1 else (l, p[1] * nr + i, jc)

    def out_map(l, i, jc, p):
        return (l, p[0] * nr + i, jc) if hax == 1 else (l, i, p[0] * nc + jc)

    def body(p_ref, g_ref, pair_ref, r0_ref, r1_ref, r2_ref, o_ref):
        own = g_ref[...] + pair_ref[...]
        o_ref[...] = ((own + r0_ref[...].astype(F32)) + r1_ref[...].astype(F32)) + r2_ref[...].astype(F32)

    def rk(k):
        return pl.BlockSpec((None, None, tr, tc), lambda l, i, jc, p, k=k: (k, l, i, jc))

    out_shape = [L, a, b]
    out_shape[hax] *= 2
    blk3 = (None, tr, tc)
    return pl.pallas_call(
        body,
        name=name,
        grid_spec=pltpu.PrefetchScalarGridSpec(
            num_scalar_prefetch=1, grid=(L, nr, nc),
            in_specs=[pl.BlockSpec(blk3, g_map), pl.BlockSpec(blk3, pair_map), rk(0), rk(1), rk(2)],
            out_specs=pl.BlockSpec(blk3, out_map)),
        out_shape=_sds(tuple(out_shape), F32),
        compiler_params=_cp(("parallel", "parallel", "parallel")),
    )(place, g, pair, recv, recv, recv)


def _pair_assemble_plan(r, hax):
    hw = r.shape[hax] // 2

    def make(in_refs, out_refs, send_sems, recv_sems, base):
        x, y, c = _coords()
        mine, other = _sub(out_refs[0], hax, c, hw), _sub(out_refs[0], hax, 1 - c, hw)
        sems = send_sems.at[base], recv_sems.at[base]
        return [_Xfer(_remote(mine, mine, *sems, (x, y, 1 - c)), _remote(other, other, *sems, (x, y, c)))]

    return _Plan([r], [_sds(r.shape, r.dtype)], {0: 0}, 1, make)


CLASSES = dict(w_in=(2, (128, 4224), (128, 4224)), w_proj=(2, (256, 2048), (512, 512)), w_out=(1, (256, 1024), (256, 1024)))


class _Exchanges:
    def __init__(self, place, shards):
        self.place, self.shards = place, shards
        self.weights, self.pending, self.pair, self.halves, self.reduced = {}, {}, {}, {}, {}

    FIRST = ("w_in",)

    def _carried(self, layer):
        keys = [(0, k) for k in CLASSES if k not in self.FIRST] if layer == 0 else []
        return keys + ([(layer + 1, k) for k in CLASSES] if layer + 1 < DEPTH else [])

    def _send_plan(self, keys):
        return _merge_plans([_gather_send_plan(self.shards[k], l, CLASSES[k][0]) for l, k in keys])

    def _pass_plan(self, keys, bufs):
        return _merge_plans([_gather_pass_plan(b, CLASSES[k][0]) for (_, k), b in zip(keys, bufs)])

    def _landed(self, keys, bufs):
        for (l, k), b in zip(keys, bufs):
            self.weights.setdefault(l, {})[k] = b

    def first_weights(self):
        keys = [(0, k) for k in self.FIRST]
        sent = _run_plan(self._send_plan(keys), "gather_send_0")
        self._landed(keys, _run_plan(self._pass_plan(keys, sent), "gather_pass_0"))

    def fwd_plan(self, layer):
        keys = self._carried(layer)
        return self._send_plan(keys) if keys else None

    def fwd_pass_plan(self, layer, bufs):
        return self._pass_plan(self._carried(layer), bufs)

    def fwd_landed(self, layer, bufs):
        self._landed(self._carried(layer), bufs)

    def grads_ready(self, layer, grads):
        self.pending[layer] = grads

    def pair_plan(self, layer):
        if layer + 1 not in self.pending:
            return None
        g = self.pending[layer + 1]
        return _merge_plans([_pair_exchange_plan(g[k], 3 - CLASSES[k][0]) for k in CLASSES])

    def pair_landed(self, layer, bufs):
        self.pair[layer + 1] = dict(zip(CLASSES, bufs))

    def slab_plan(self, layer):
        src = layer + 1
        if src not in self.pair:
            return None
        g, pair = self.pending[src], self.pair[src]
        sums = [_add_own_half(self.place, g[k], pair[k], 3 - CLASSES[k][0], CLASSES[k][1], f"rs_pair_add_{k}_{src}") for k in CLASSES]
        return _merge_plans([_slab_exchange_plan(p, CLASSES[k][0]) for k, p in zip(CLASSES, sums)])

    def slab_landed(self, layer, bufs):
        src = layer + 1
        g, pair = self.pending.pop(src), self.pair.pop(src)
        self.halves[src] = [_add_slabs(self.place, g[k], pair[k], r, CLASSES[k][0], CLASSES[k][2], f"rs_slab_add_{k}_{src}")
                            for k, r in zip(CLASSES, bufs)]

    def assemble_plan(self, layer):
        if layer + 1 not in self.halves:
            return None
        return _merge_plans([_pair_assemble_plan(h, 3 - CLASSES[k][0]) for k, h in zip(CLASSES, self.halves[layer + 1])])

    def assembled(self, layer, bufs):
        del self.halves[layer + 1]
        self.reduced[layer + 1] = dict(zip(CLASSES, bufs))

    def tail_plan(self):
        self.pair_landed(-1, _run_plan(self.pair_plan(-1), "rs_pair_0"))
        return self.slab_plan(-1)

    def finish(self):
        self.assembled(-1, _run_plan(self.assemble_plan(-1), "rs_assemble_0"))
        return [self.reduced[l] for l in range(DEPTH)]


N_DEV = 8


def _all_reduce_small(v, name):
    rows = v.shape[0]

    def body(v_ref, gath_ref, sum_ref, send_sems, recv_sems, local_sem):
        x, y, c = _coords()
        me, sib = (x, y, c), (x, y, 1 - c)
        chips = _other_chips(x, y)

        def slot(px, py, pc):
            return gath_ref.at[pl.ds(pl.multiple_of((4 * px + 2 * py + pc) * rows, rows), rows), :]

        def copy(k, block, to, src=None):
            return _remote(slot(*block) if src is None else src, slot(*block), send_sems.at[k], recv_sems.at[k], to)

        mine = pltpu.make_async_copy(v_ref, slot(*me), local_sem)
        mine.start()
        first = [copy(0, me, sib, src=v_ref)] + [copy(1 + k, me, (*chip, c), src=v_ref) for k, chip in enumerate(chips)]
        for cp in first:
            cp.start()
        passed = [copy(4 + k, (*chip, c), sib) for k, chip in enumerate(chips)]
        for k, chip in enumerate(chips):
            copy(1 + k, (*chip, c), me).wait_recv()
            passed[k].start()
        copy(0, sib, me).wait_recv()
        for k, chip in enumerate(chips):
            copy(4 + k, (*chip, 1 - c), me).wait_recv()
        for cp in first + passed:
            cp.wait_send()
        mine.wait()
        acc = gath_ref[0:rows, :]
        for d in range(1, N_DEV):
            acc = acc + gath_ref[d * rows:(d + 1) * rows, :]
        sum_ref[...] = acc

    vm = pl.BlockSpec(memory_space=pltpu.VMEM)
    return pl.pallas_call(
        body, name=name, in_specs=[vm], out_specs=[vm, vm],
        out_shape=[_sds((N_DEV * rows, 128), F32), _sds((rows, 128), F32)],
        scratch_shapes=[DMA_SEM((7,)), DMA_SEM((7,)), DMA_SEM(())],
    )(v)[1]


def _pad_rows(a):
    flat = a.reshape(-1).astype(F32)
    rows = -(-flat.shape[0] // (8 * 128)) * 8
    return jnp.pad(flat, (0, rows * 128 - flat.shape[0])).reshape(rows, 128)


def _sum_over_devices(parts, name):
    blocks = [_pad_rows(a) for a in parts.values()]
    total = _all_reduce_small(jnp.concatenate(blocks, axis=0), name)
    out, r0 = {}, 0
    for (key, a), blk in zip(parts.items(), blocks):
        out[key] = total[r0:r0 + blk.shape[0]].reshape(-1)[:a.size].reshape(a.shape)
        r0 += blk.shape[0]
    return out


def kernel(x, w_in, w_proj_hgrn, w_proj_attn, w_proj_conv, w_out, lb_param, hgrn_norm_g, attn_sinks, conv_w, rel_bias, ln_g, ln_b, loss_target, m_w_in, m_w_proj_hgrn, m_w_proj_attn, m_w_proj_conv, m_w_out, m_lb_param, m_hgrn_norm_g, m_attn_sinks, m_conv_w, m_rel_bias, m_ln_g, m_ln_b, v_w_in, v_w_proj_hgrn, v_w_proj_attn, v_w_proj_conv, v_w_out, v_lb_param, v_hgrn_norm_g, v_attn_sinks, v_conv_w, v_rel_bias, v_ln_g, v_ln_b):
    xi, yi, ci = _coords()
    slab = 2 * xi + yi
    place = jnp.stack([ci, slab]).astype(jnp.int32)
    conv_cols = conv_w.shape[-1]

    w_in_b = w_in.astype(BF16)[:, None]
    w_proj_b = jnp.stack([w_proj_hgrn, w_proj_attn, w_proj_conv], axis=1).astype(BF16)
    w_out_b = w_out.astype(BF16)[:, None]
    ex = _Exchanges(place, dict(w_in=w_in_b, w_proj=w_proj_b, w_out=w_out_b))
    ex.first_weights()
    conv_spread = lax.dynamic_update_slice(jnp.zeros((DEPTH, CONV_K, CONV_WIDTH), F32), conv_w, (0, 0, slab * conv_cols))
    conv_full = 0.5 * _sum_over_devices({"conv_w": conv_spread}, "gather_conv_w")["conv_w"]

    loss_blk, dx, small = _local_step(x[0], loss_target[0], ex, lb_param, hgrn_norm_g, attn_sinks, conv_full, rel_bias, ln_g, ln_b)

    reduced = ex.finish()
    g_w_in = jnp.stack([r["w_in"][0] for r in reduced])
    g_w_proj = jnp.stack([r["w_proj"] for r in reduced])
    g_w_out = jnp.stack([r["w_out"][0] for r in reduced])
    small = dict(small, loss=loss_blk[0:1, 0:1])
    small = _sum_over_devices(small, "sum_small")
    loss = small["loss"][0, 0]
    g_conv = lax.dynamic_slice(small["conv_w"], (0, 0, slab * conv_cols), (DEPTH, CONV_K, conv_cols))

    grads = [g_w_in, g_w_proj[:, 0], g_w_proj[:, 1], g_w_proj[:, 2], g_w_out, small["lb_param"], small["hgrn_norm_g"],
             small["attn_sinks"], g_conv, small["rel_bias"], small["ln_g"], small["ln_b"]]
    names = ["w_in", "w_proj_hgrn", "w_proj_attn", "w_proj_conv", "w_out", "lb_param", "hgrn_norm_g", "attn_sinks",
             "conv_w", "rel_bias", "ln_g", "ln_b"]
    ws = [w_in, w_proj_hgrn, w_proj_attn, w_proj_conv, w_out, lb_param, hgrn_norm_g, attn_sinks, conv_w, rel_bias, ln_g, ln_b]
    ms = [m_w_in, m_w_proj_hgrn, m_w_proj_attn, m_w_proj_conv, m_w_out, m_lb_param, m_hgrn_norm_g, m_attn_sinks, m_conv_w,
          m_rel_bias, m_ln_g, m_ln_b]
    vs = [v_w_in, v_w_proj_hgrn, v_w_proj_attn, v_w_proj_conv, v_w_out, v_lb_param, v_hgrn_norm_g, v_attn_sinks, v_conv_w,
          v_rel_bias, v_ln_g, v_ln_b]
    upd = [_adamw(w, g, m, v, "adamw_" + n) for n, w, g, m, v in zip(names, ws, grads, ms, vs)]
    deltas, new_ms, new_vs = zip(*upd)
    return (loss, dx[None], *grads, *deltas, *new_ms, *new_vs)
```

```python
import functools
import math

import jax
import jax.numpy as jnp
from jax import lax
from jax.experimental import pallas as pl
from jax.experimental.pallas import tpu as pltpu

F32 = jnp.float32
BF16 = jnp.bfloat16
MXU_DTYPE = BF16

D_MODEL = 2048
DEPTH = 4
HGRN_WIDTH = 1024
HGRN_HEAD_DIM = 128
HGRN_HEADS = 8
HGRN_CHUNK = 64
ATTN_HEAD_DIM = 64
ATTN_HEADS = 16
ATTN_KV_HEADS = 4
ATTN_GROUP = ATTN_HEADS // ATTN_KV_HEADS
ATTN_WIDTH = 1024
KV_WIDTH = 256
WINDOW = 128
CONV_WIDTH = 1024
CONV_K = 3
N_BUCKETS = 32
MAX_DISTANCE = 128
ALPHA = (2.0 * DEPTH) ** 0.25
LN_EPS = 1e-5
RMS_EPS = 1e-6
N_IN = 16896
OFF_AQ, OFF_AF, OFF_AI, OFF_AG = 0, 1024, 2048, 3072
OFF_BQ, OFF_BK, OFF_BV, OFF_BG = 4096, 5120, 5376, 5632
OFF_CB, OFF_CC, OFF_CX, OFF_CG = 6656, 7680, 8704, 9728
OFF_MA, OFF_MB, OFF_MC = 10752, 12800, 14848

ADAM_LR = 0.001
ADAM_B1 = 0.9
ADAM_B2 = 0.999
ADAM_EPS = 1e-08
ADAM_WD = 0.01
ADAM_STEP = 10

N_CHIPS = 4
VMEM_LIMIT_BYTES = 48 * 1024 * 1024
EXP_CLAMP = 80.0
MASK_VALUE = -1e30
MESH = pl.DeviceIdType.MESH


def _cp(sem=None):
    return pltpu.CompilerParams(dimension_semantics=sem, vmem_limit_bytes=VMEM_LIMIT_BYTES)


def _tile(dim, pref):
    return pref if dim % pref == 0 else dim


def _sds(shape, dtype):
    return jax.ShapeDtypeStruct(shape, dtype)


def _call(body, *, name, grid, in_specs, out_specs, out_shape, args, scratch_shapes=(), sem=None, plan=None):
    in_specs, out_specs, out_shape = list(in_specs), list(out_specs), list(out_shape)
    if plan is None:
        outs = pl.pallas_call(body, name=name, grid=grid, in_specs=in_specs, out_specs=out_specs, out_shape=out_shape,
                              scratch_shapes=list(scratch_shapes), compiler_params=_cp(sem))(*args)
        return list(outs), []
    ni, no, ns = len(in_specs), len(out_specs), len(scratch_shapes)
    pi, po = len(plan.inputs), len(plan.out_shapes)

    def carrier(*refs):
        c_in, p_in = refs[:ni], refs[ni:ni + pi]
        c_out, p_out = refs[ni + pi:ni + pi + no], refs[ni + pi + no:ni + pi + no + po]
        c_scr = refs[ni + pi + no + po:ni + pi + no + po + ns]
        send_sems, recv_sems = refs[-2], refs[-1]
        ids = [pl.program_id(a) for a in range(len(grid))]
        first = functools.reduce(jnp.logical_and, [i == 0 for i in ids])
        last = functools.reduce(jnp.logical_and, [i == n - 1 for i, n in zip(ids, grid)])

        @pl.when(first)
        def _():
            for cp in plan.make(p_in, p_out, send_sems, recv_sems, 0):
                cp.start()

        body(*c_in, *c_out, *c_scr)

        @pl.when(last)
        def _():
            for cp in plan.make(p_in, p_out, send_sems, recv_sems, 0):
                cp.wait()

    any_spec = pl.BlockSpec(memory_space=pl.ANY)
    outs = pl.pallas_call(
        carrier, name=name, grid=grid,
        in_specs=in_specs + [any_spec] * pi, out_specs=out_specs + [any_spec] * po,
        out_shape=out_shape + list(plan.out_shapes),
        scratch_shapes=list(scratch_shapes) + [pltpu.SemaphoreType.DMA((plan.n,)), pltpu.SemaphoreType.DMA((plan.n,))],
        input_output_aliases={ni + k: no + v for k, v in plan.aliases.items()},
        compiler_params=_cp(tuple("arbitrary" for _ in grid)),
    )(*args, *plan.inputs)
    return list(outs[:no]), list(outs[no:])


_DIMS = {
    "nn": (((1,), (0,)), ((), ())),
    "nt": (((1,), (1,)), ((), ())),
    "tn": (((0,), (0,)), ((), ())),
}


def _dot_raw(a, b, mode):
    return lax.dot_general(a.astype(MXU_DTYPE), b.astype(MXU_DTYPE), _DIMS[mode], preferred_element_type=F32)


@functools.partial(jax.custom_vjp, nondiff_argnums=(2,))
def _dot(a, b, mode):
    return _dot_raw(a, b, mode)


def _dot_fwd(a, b, mode):
    return _dot_raw(a, b, mode), (a, b)


def _dot_bwd(mode, res, g):
    a, b = res
    if mode == "nn":
        return _dot_raw(g, b, "nt"), _dot_raw(a, g, "tn")
    if mode == "nt":
        return _dot_raw(g, b, "nn"), _dot_raw(g, a, "tn")
    return _dot_raw(b, g, "nt"), _dot_raw(a, g, "nn")


_dot.defvjp(_dot_fwd, _dot_bwd)


def _matmul(a, b, *, mode, tm, tn, tk, name, a_idx=None, b_idx=None, out_dtype=F32, add=None, add_scale=1.0, plan=None):
    a2, b2 = a.shape[-2:], b.shape[-2:]
    if mode == "nn":
        (M, K), (K2, N) = a2, b2
    elif mode == "nt":
        (M, K), (N, K2) = a2, b2
    else:
        (K, M), (K2, N) = a2, b2
    assert K == K2, (a.shape, b.shape, mode)
    tm, tn, tk = _tile(M, tm), _tile(N, tn), _tile(K, tk)
    nk = K // tk

    a_blk = (tk, tm) if mode == "tn" else (tm, tk)
    b_blk = (tn, tk) if mode == "nt" else (tk, tn)

    def a_map(i, j, k):
        ij = (k, i) if mode == "tn" else (i, k)
        return ij if a_idx is None else (a_idx,) + ij

    def b_map(i, j, k):
        ij = (j, k) if mode == "nt" else (k, j)
        return ij if b_idx is None else (b_idx,) + ij

    in_specs = [
        pl.BlockSpec(a_blk if a_idx is None else (None,) + a_blk, a_map),
        pl.BlockSpec(b_blk if b_idx is None else (None,) + b_blk, b_map),
    ]
    args = [a, b]
    if add is not None:
        in_specs.append(pl.BlockSpec((tm, tn), lambda i, j, k: (i, j)))
        args.append(add)
    n_in = len(args)

    def body(*refs):
        a_ref, b_ref = refs[0], refs[1]
        o_ref = refs[n_in]
        p = _dot_raw(a_ref[...], b_ref[...], mode)

        def finish(val):
            if add is not None:
                val = val + add_scale * refs[2][...]
            o_ref[...] = val.astype(out_dtype)

        if nk == 1:
            finish(p)
        else:
            acc_ref = refs[n_in + 1]
            k = pl.program_id(2)

            @pl.when(k == 0)
            def _():
                acc_ref[...] = p

            @pl.when(k > 0)
            def _():
                acc_ref[...] += p

            @pl.when(k == nk - 1)
            def _():
                finish(acc_ref[...])

    (out,), plan_outs = _call(
        body,
        name=name,
        grid=(M // tm, N // tn, nk),
        in_specs=in_specs,
        out_specs=[pl.BlockSpec((tm, tn), lambda i, j, k: (i, j))],
        out_shape=[_sds((M, N), out_dtype)],
        scratch_shapes=[pltpu.VMEM((tm, tn), F32)] if nk > 1 else [],
        sem=("parallel", "parallel", "arbitrary"),
        args=args,
        plan=plan,
    )
    return out if plan is None else (out, plan_outs)


def _scan_rows(x, reverse):
    n = x.shape[0]
    row = lax.broadcasted_iota(jnp.int32, x.shape, 0)
    s = 1
    while s < n:
        if reverse:
            x = x + jnp.where(row < n - s, pltpu.roll(x, n - s, 0), 0.0)
        else:
            x = x + jnp.where(row >= s, pltpu.roll(x, s, 0), 0.0)
        s *= 2
    return x


@jax.custom_vjp
def _cumsum_rows(x):
    return _scan_rows(x, False)


_cumsum_rows.defvjp(lambda x: (_scan_rows(x, False), None), lambda _, g: (_scan_rows(g, True),))


def _hgrn_chunk(state_t, qraw, fraw, v, lb):
    c = HGRN_CHUNK
    q = qraw * jax.nn.sigmoid(qraw) * (HGRN_HEAD_DIM ** -0.5)
    f = lb + (1.0 - lb) * jax.nn.sigmoid(fraw)
    k = 1.0 - f
    g = jnp.log(f)
    b = _cumsum_rows(g)
    row = lax.broadcasted_iota(jnp.int32, (c, HGRN_HEAD_DIM), 0)
    b_end = jnp.sum(g, axis=0, keepdims=True)
    b_mid = jnp.sum(jnp.where(row < c // 2, g, 0.0), axis=0, keepdims=True)
    inter = _dot(q * jnp.exp(b), state_t, "nt")
    qt = q * jnp.exp(jnp.minimum(b - b_mid, EXP_CLAMP))
    kt = k * jnp.exp(jnp.minimum(b_mid - b, EXP_CLAMP))
    s = _dot(qt, kt, "nt")
    ti = lax.broadcasted_iota(jnp.int32, (c, c), 0)
    si = lax.broadcasted_iota(jnp.int32, (c, c), 1)
    s = jnp.where(si <= ti, s, 0.0)
    intra = _dot(s, v, "nn")
    k_end = k * jnp.exp(b_end - b)
    new_state_t = state_t * jnp.exp(b_end) + _dot(v, k_end, "tn")
    return new_state_t, inter + intra


def _hgrn_specs(T):
    rows = _tile(T, 1024)
    return rows, T // rows, rows // HGRN_CHUNK


def _hgrn_fwd(u, lower_l, plan=None):
    T = u.shape[0]
    rows, nblk, ncr = _hgrn_specs(T)
    hb = HGRN_WIDTH // HGRN_HEAD_DIM

    def body(q_ref, f_ref, i_ref, lb_ref, o_ref, st_ref, state):
        @pl.when(pl.program_id(1) == 0)
        def _():
            state[...] = jnp.zeros_like(state)

        lb = lb_ref[...]
        for c in range(ncr):
            rs = pl.ds(c * HGRN_CHUNK, HGRN_CHUNK)
            st = state[...]
            st_ref[c] = st
            new, out = _hgrn_chunk(st, q_ref[rs, :], f_ref[rs, :], i_ref[rs, :], lb)
            state[...] = new
            o_ref[rs, :] = out

    blk = (rows, HGRN_HEAD_DIM)
    return _call(
        body,
        name="hgrn_fwd",
        grid=(HGRN_HEADS, nblk),
        in_specs=[
            pl.BlockSpec(blk, lambda h, r: (r, OFF_AQ // 128 + h)),
            pl.BlockSpec(blk, lambda h, r: (r, OFF_AF // 128 + h)),
            pl.BlockSpec(blk, lambda h, r: (r, OFF_AI // 128 + h)),
            pl.BlockSpec((1, HGRN_HEAD_DIM), lambda h, r: (0, h)),
        ],
        out_specs=[
            pl.BlockSpec(blk, lambda h, r: (r, h)),
            pl.BlockSpec((ncr, None, HGRN_HEAD_DIM, HGRN_HEAD_DIM), lambda h, r: (r, h, 0, 0)),
        ],
        out_shape=[_sds((T, HGRN_WIDTH), F32), _sds((T // HGRN_CHUNK, hb, HGRN_HEAD_DIM, HGRN_HEAD_DIM), F32)],
        scratch_shapes=[pltpu.VMEM((HGRN_HEAD_DIM, HGRN_HEAD_DIM), F32)],
        sem=("parallel", "arbitrary"),
        args=(u, u, u, lower_l),
        plan=plan,
    )


def _hgrn_bwd(u, lower_l, states, do_raw):
    T = u.shape[0]
    rows, nblk, ncr = _hgrn_specs(T)

    def body(q_ref, f_ref, i_ref, lb_ref, st_ref, do_ref, dq_ref, df_ref, di_ref, dlb_ref, dstate):
        @pl.when(pl.program_id(1) == 0)
        def _():
            dstate[...] = jnp.zeros_like(dstate)
            dlb_ref[...] = jnp.zeros_like(dlb_ref)

        lb = lb_ref[...]
        for c in reversed(range(ncr)):
            rs = pl.ds(c * HGRN_CHUNK, HGRN_CHUNK)
            _, vjp = jax.vjp(_hgrn_chunk, st_ref[c], q_ref[rs, :], f_ref[rs, :], i_ref[rs, :], lb)
            dst, dq, df, dv, dlb = vjp((dstate[...], do_ref[rs, :]))
            dstate[...] = dst
            dq_ref[rs, :] = dq.astype(BF16)
            df_ref[rs, :] = df.astype(BF16)
            di_ref[rs, :] = dv.astype(BF16)
            dlb_ref[...] += dlb

    blk = (rows, HGRN_HEAD_DIM)
    last = nblk - 1
    out_blk = pl.BlockSpec(blk, lambda h, r: (last - r, h))
    return pl.pallas_call(
        body,
        name="hgrn_bwd",
        grid=(HGRN_HEADS, nblk),
        in_specs=[
            pl.BlockSpec(blk, lambda h, r: (last - r, OFF_AQ // 128 + h)),
            pl.BlockSpec(blk, lambda h, r: (last - r, OFF_AF // 128 + h)),
            pl.BlockSpec(blk, lambda h, r: (last - r, OFF_AI // 128 + h)),
            pl.BlockSpec((1, HGRN_HEAD_DIM), lambda h, r: (0, h)),
            pl.BlockSpec((ncr, None, HGRN_HEAD_DIM, HGRN_HEAD_DIM), lambda h, r: (last - r, h, 0, 0)),
            out_blk,
        ],
        out_specs=[out_blk, out_blk, out_blk, pl.BlockSpec((1, HGRN_HEAD_DIM), lambda h, r: (0, h))],
        out_shape=[_sds((T, HGRN_WIDTH), BF16)] * 3 + [_sds((1, HGRN_WIDTH), F32)],
        scratch_shapes=[pltpu.VMEM((HGRN_HEAD_DIM, HGRN_HEAD_DIM), F32)],
        compiler_params=_cp(("parallel", "arbitrary")),
    )(u, u, u, lower_l, states, do_raw)


QROWS = ATTN_GROUP * WINDOW


def _attn_block(q, kp, kc, vp, vc, bp, bc, sink, first):
    qs = q * (ATTN_HEAD_DIM ** -0.5)
    sc = _dot(qs, kc, "nt") + bc
    m = jnp.maximum(jnp.max(sc, axis=-1, keepdims=True), sink)
    if not first:
        sp = _dot(qs, kp, "nt") + bp
        m = jnp.maximum(m, jnp.max(sp, axis=-1, keepdims=True))
    m = lax.stop_gradient(m)
    pc = jnp.exp(sc - m)
    den = jnp.sum(pc, axis=-1, keepdims=True) + jnp.exp(sink - m)
    o = _dot(pc, vc, "nn")
    if not first:
        pp = jnp.exp(sp - m)
        den = den + jnp.sum(pp, axis=-1, keepdims=True)
        o = o + _dot(pp, vp, "nn")
    return o * (1.0 / den)


def _first_or_later(n, run):
    @pl.when(n == 0)
    def _():
        run(True)

    @pl.when(n > 0)
    def _():
        run(False)


def _attn_in_specs():
    q_spec = pl.BlockSpec((WINDOW, ATTN_WIDTH), lambda n: (n, OFF_BQ // ATTN_WIDTH))
    k_cur = pl.BlockSpec((WINDOW, KV_WIDTH), lambda n: (n, OFF_BK // KV_WIDTH))
    k_prev = pl.BlockSpec((WINDOW, KV_WIDTH), lambda n: (jnp.maximum(n - 1, 0), OFF_BK // KV_WIDTH))
    v_cur = pl.BlockSpec((WINDOW, KV_WIDTH), lambda n: (n, OFF_BV // KV_WIDTH))
    v_prev = pl.BlockSpec((WINDOW, KV_WIDTH), lambda n: (jnp.maximum(n - 1, 0), OFF_BV // KV_WIDTH))
    bias = pl.BlockSpec((ATTN_KV_HEADS, QROWS, 2 * WINDOW), lambda n: (0, 0, 0))
    sink = pl.BlockSpec((ATTN_KV_HEADS, QROWS, 1), lambda n: (0, 0, 0))
    return [q_spec, k_prev, k_cur, v_prev, v_cur, bias, sink]


def _head_cols(a):
    return slice(a * ATTN_HEAD_DIM, (a + 1) * ATTN_HEAD_DIM)


def _group_rows(ref, h):
    return jnp.concatenate([ref[:, _head_cols(ATTN_GROUP * h + g)] for g in range(ATTN_GROUP)], axis=0)


def _attn_fwd(u, bias4, sink4):
    T = u.shape[0]

    def body(q_ref, kp_ref, kc_ref, vp_ref, vc_ref, b_ref, s_ref, o_ref):
        def run(first):
            for h in range(ATTN_KV_HEADS):
                hs = _head_cols(h)
                o = _attn_block(_group_rows(q_ref, h), kp_ref[:, hs], kc_ref[:, hs], vp_ref[:, hs], vc_ref[:, hs],
                                b_ref[h, :, :WINDOW], b_ref[h, :, WINDOW:], s_ref[h], first)
                for g in range(ATTN_GROUP):
                    o_ref[:, _head_cols(ATTN_GROUP * h + g)] = o[g * WINDOW:(g + 1) * WINDOW]

        _first_or_later(pl.program_id(0), run)

    return pl.pallas_call(
        body,
        name="attn_fwd",
        grid=(T // WINDOW,),
        in_specs=_attn_in_specs(),
        out_specs=pl.BlockSpec((WINDOW, ATTN_WIDTH), lambda n: (n, 0)),
        out_shape=_sds((T, ATTN_WIDTH), F32),
        compiler_params=_cp(("parallel",)),
    )(u, u, u, u, u, bias4, sink4)


def _attn_bwd(u, bias4, sink4, do, plan=None):
    T = u.shape[0]
    act = pl.BlockSpec((WINDOW, ATTN_WIDTH), lambda n: (n, 0))
    kv = pl.BlockSpec((WINDOW, KV_WIDTH), lambda n: (n, 0))
    in_specs = _attn_in_specs()
    bias, sink = in_specs[5], in_specs[6]

    def body(q_ref, kp_ref, kc_ref, vp_ref, vc_ref, b_ref, s_ref, do_ref,
             dq_ref, dkc_ref, dkp_ref, dvc_ref, dvp_ref, db_ref, ds_ref):
        n = pl.program_id(0)

        @pl.when(n == 0)
        def _():
            db_ref[...] = jnp.zeros_like(db_ref)
            ds_ref[...] = jnp.zeros_like(ds_ref)

        def run(first):
            dqs, dkps, dkcs, dvps, dvcs = [], [], [], [], []
            for h in range(ATTN_KV_HEADS):
                hs = _head_cols(h)
                _, vjp = jax.vjp(
                    functools.partial(_attn_block, first=first),
                    _group_rows(q_ref, h), kp_ref[:, hs], kc_ref[:, hs], vp_ref[:, hs], vc_ref[:, hs],
                    b_ref[h, :, :WINDOW], b_ref[h, :, WINDOW:], s_ref[h])
                dq, dkp, dkc, dvp, dvc, dbp, dbc, dsink = vjp(_group_rows(do_ref, h))
                dqs += [dq[g * WINDOW:(g + 1) * WINDOW] for g in range(ATTN_GROUP)]
                dkps.append(dkp)
                dkcs.append(dkc)
                dvps.append(dvp)
                dvcs.append(dvc)
                if not first:
                    db_ref[h, :, :WINDOW] += dbp
                db_ref[h, :, WINDOW:] += dbc
                ds_ref[h] += dsink
            dq_ref[...] = jnp.concatenate(dqs, axis=1).astype(BF16)
            dkc_ref[...] = jnp.concatenate(dkcs, axis=1)
            dkp_ref[...] = jnp.concatenate(dkps, axis=1)
            dvc_ref[...] = jnp.concatenate(dvcs, axis=1)
            dvp_ref[...] = jnp.concatenate(dvps, axis=1)

        _first_or_later(n, run)

    kv_sds = _sds((T, KV_WIDTH), F32)
    return _call(
        body,
        name="attn_bwd",
        grid=(T // WINDOW,),
        in_specs=in_specs + [act],
        out_specs=[act, kv, kv, kv, kv, bias, sink],
        out_shape=[_sds((T, ATTN_WIDTH), BF16), kv_sds, kv_sds, kv_sds, kv_sds,
                   _sds((ATTN_KV_HEADS, QROWS, 2 * WINDOW), F32), _sds((ATTN_KV_HEADS, QROWS, 1), F32)],
        sem=("arbitrary",),
        args=(u, u, u, u, u, bias4, sink4, do),
        plan=plan,
    )


def _with_next_block_part(cur, nxt):
    pad = jnp.zeros_like(nxt[:WINDOW])
    return (cur + jnp.concatenate([nxt[WINDOW:], pad], axis=0)).astype(BF16)


MIX_COLS = 512


def _silu(x):
    return x * jax.nn.sigmoid(x)


def _silu_grad(x):
    s = jax.nn.sigmoid(x)
    return s * (1.0 + x * (1.0 - s))


def _shift_rows_down(h, first, second):
    n = h.shape[0]
    row = lax.broadcasted_iota(jnp.int32, h.shape, 0)
    s1 = jnp.where(row == 0, first, pltpu.roll(h, 1, 0))
    s2 = jnp.where(row == 0, second, jnp.where(row == 1, first, pltpu.roll(h, 2, 0)))
    del n
    return s1, s2


def _shift_rows_up(h, first, second):
    n = h.shape[0]
    row = lax.broadcasted_iota(jnp.int32, h.shape, 0)
    s1 = jnp.where(row == n - 1, first, pltpu.roll(h, n - 1, 0))
    s2 = jnp.where(row == n - 1, second, jnp.where(row == n - 2, first, pltpu.roll(h, n - 2, 0)))
    return s1, s2


def _mix_rows(T):
    return _tile(T, 512)


def _mix_fwd(u, o_raw, o_b, gn_l, cw_l):
    T = u.shape[0]
    tr = _mix_rows(T)
    nrow = T // tr
    hr = tr // 8

    def ucol(off):
        return pl.BlockSpec((tr, MIX_COLS), lambda i, j, off=off: (i, off // MIX_COLS + j))

    def uprev(off):
        return pl.BlockSpec((8, MIX_COLS), lambda i, j, off=off: (jnp.maximum(i * hr - 1, 0), off // MIX_COLS + j))

    act = pl.BlockSpec((tr, MIX_COLS), lambda i, j: (i, j))
    par = lambda rows: pl.BlockSpec((rows, MIX_COLS), lambda i, j: (0, j))

    def body(oraw_ref, ag_ref, ob_ref, bg_ref, cb_ref, cc_ref, cx_ref, cg_ref, ccp_ref, cxp_ref, gn_ref, cw_ref,
             ha_ref, hb_ref, hc_ref, hat_ref, hbt_ref, hct_ref):
        ag = _silu(ag_ref[...])
        for h in range(MIX_COLS // HGRN_HEAD_DIM):
            cs = slice(h * HGRN_HEAD_DIM, (h + 1) * HGRN_HEAD_DIM)
            o = oraw_ref[:, cs]
            nrm = o * lax.rsqrt(jnp.mean(o * o, axis=-1, keepdims=True) + RMS_EPS)
            ha = nrm * gn_ref[:, cs] * ag[:, cs]
            ha_ref[:, cs] = ha.astype(BF16)
            hat_ref[cs, :] = ha.T.astype(BF16)
        hb = ob_ref[...] * _silu(bg_ref[...])
        hb_ref[...] = hb.astype(BF16)
        hbt_ref[...] = hb.T.astype(BF16)
        keep = (pl.program_id(0) > 0).astype(F32)
        hcur = cc_ref[...] * cx_ref[...]
        p1 = ccp_ref[7:8, :] * cxp_ref[7:8, :] * keep
        p2 = ccp_ref[6:7, :] * cxp_ref[6:7, :] * keep
        s1, s2 = _shift_rows_down(hcur, p1, p2)
        y = cw_ref[0:1, :] * s2 + cw_ref[1:2, :] * s1 + cw_ref[2:3, :] * hcur
        hc = cb_ref[...] * y * _silu(cg_ref[...])
        hc_ref[...] = hc.astype(BF16)
        hct_ref[...] = hc.T.astype(BF16)

    out = _sds((T, HGRN_WIDTH), BF16)
    out_t = _sds((HGRN_WIDTH, T), BF16)
    act_t = pl.BlockSpec((MIX_COLS, tr), lambda i, j: (j, i))
    return pl.pallas_call(
        body,
        name="mix_fwd",
        grid=(nrow, HGRN_WIDTH // MIX_COLS),
        in_specs=[act, ucol(OFF_AG), act, ucol(OFF_BG), ucol(OFF_CB), ucol(OFF_CC), ucol(OFF_CX), ucol(OFF_CG),
                  uprev(OFF_CC), uprev(OFF_CX), par(1), par(CONV_K)],
        out_specs=[act, act, act, act_t, act_t, act_t],
        out_shape=[out, out, out, out_t, out_t, out_t],
        compiler_params=_cp(("parallel", "parallel")),
    )(o_raw, u, o_b, u, u, u, u, u, u, u, gn_l, cw_l)


def _mix_bwd(u, o_raw, o_b, gn_l, cw_l, dha, dhb, dhc):
    T = u.shape[0]
    tr = _mix_rows(T)
    nrow = T // tr
    hr = tr // 8
    last_halo = T // 8 - 1

    def ucol(off):
        return pl.BlockSpec((tr, MIX_COLS), lambda j, i, off=off: (i, off // MIX_COLS + j))

    def uprev(off):
        return pl.BlockSpec((8, MIX_COLS), lambda j, i, off=off: (jnp.maximum(i * hr - 1, 0), off // MIX_COLS + j))

    def unext(off):
        return pl.BlockSpec((8, MIX_COLS), lambda j, i, off=off: (jnp.minimum((i + 1) * hr, last_halo), off // MIX_COLS + j))

    act = pl.BlockSpec((tr, MIX_COLS), lambda j, i: (i, j))
    act_next = pl.BlockSpec((8, MIX_COLS), lambda j, i: (jnp.minimum((i + 1) * hr, last_halo), j))
    par = lambda rows: pl.BlockSpec((rows, MIX_COLS), lambda j, i: (0, j))

    def body(oraw_ref, ag_ref, ob_ref, bg_ref, cb_ref, cc_ref, cx_ref, cg_ref, ccp_ref, cxp_ref,
             cbn_ref, cgn_ref, dhcn_ref, gn_ref, cw_ref, dha_ref, dhb_ref, dhc_ref,
             doraw_ref, dob_ref, dag_ref, dbg_ref, dcb_ref, dcc_ref, dcx_ref, dcg_ref, dgn_ref, dcw_ref):
        i = pl.program_id(1)

        @pl.when(i == 0)
        def _():
            dgn_ref[...] = jnp.zeros_like(dgn_ref)
            dcw_ref[...] = jnp.zeros_like(dcw_ref)

        ag = ag_ref[...]
        sag = _silu(ag)
        dha = dha_ref[...]
        for h in range(MIX_COLS // HGRN_HEAD_DIM):
            cs = slice(h * HGRN_HEAD_DIM, (h + 1) * HGRN_HEAD_DIM)
            o = oraw_ref[:, cs]
            rs = lax.rsqrt(jnp.mean(o * o, axis=-1, keepdims=True) + RMS_EPS)
            nrm = o * rs
            gn = gn_ref[:, cs]
            d = dha[:, cs]
            dag_ref[:, cs] = (d * nrm * gn * _silu_grad(ag[:, cs])).astype(BF16)
            dgn_ref[:, cs] += jnp.sum(d * sag[:, cs] * nrm, axis=0, keepdims=True)
            dn = d * sag[:, cs] * gn
            doraw_ref[:, cs] = rs * (dn - nrm * jnp.mean(dn * nrm, axis=-1, keepdims=True))
        bg = bg_ref[...]
        dhb = dhb_ref[...]
        dob_ref[...] = dhb * _silu(bg)
        dbg_ref[...] = (dhb * ob_ref[...] * _silu_grad(bg)).astype(BF16)
        keep_prev = (i > 0).astype(F32)
        keep_next = (i < nrow - 1).astype(F32)
        cc, cx, cb, cg = cc_ref[...], cx_ref[...], cb_ref[...], cg_ref[...]
        hcur = cc * cx
        p1 = ccp_ref[7:8, :] * cxp_ref[7:8, :] * keep_prev
        p2 = ccp_ref[6:7, :] * cxp_ref[6:7, :] * keep_prev
        s1, s2 = _shift_rows_down(hcur, p1, p2)
        w0, w1, w2 = cw_ref[0:1, :], cw_ref[1:2, :], cw_ref[2:3, :]
        y = w0 * s2 + w1 * s1 + w2 * hcur
        dhc = dhc_ref[...]
        scg = _silu(cg)
        doc = dhc * scg
        dcg_ref[...] = (dhc * cb * y * _silu_grad(cg)).astype(BF16)
        dcb_ref[...] = (doc * y).astype(BF16)
        dy = doc * cb
        n1 = dhcn_ref[0:1, :] * _silu(cgn_ref[0:1, :]) * cbn_ref[0:1, :] * keep_next
        n2 = dhcn_ref[1:2, :] * _silu(cgn_ref[1:2, :]) * cbn_ref[1:2, :] * keep_next
        u1, u2 = _shift_rows_up(dy, n1, n2)
        dh = w2 * dy + w1 * u1 + w0 * u2
        dcc_ref[...] = (dh * cx).astype(BF16)
        dcx_ref[...] = (dh * cc).astype(BF16)
        dcw_ref[0:1, :] += jnp.sum(dy * s2, axis=0, keepdims=True)
        dcw_ref[1:2, :] += jnp.sum(dy * s1, axis=0, keepdims=True)
        dcw_ref[2:3, :] += jnp.sum(dy * hcur, axis=0, keepdims=True)

    f32o, bf = _sds((T, HGRN_WIDTH), F32), _sds((T, HGRN_WIDTH), BF16)
    return pl.pallas_call(
        body,
        name="mix_bwd",
        grid=(HGRN_WIDTH // MIX_COLS, nrow),
        in_specs=[act, ucol(OFF_AG), act, ucol(OFF_BG), ucol(OFF_CB), ucol(OFF_CC), ucol(OFF_CX), ucol(OFF_CG),
                  uprev(OFF_CC), uprev(OFF_CX), unext(OFF_CB), unext(OFF_CG), act_next, par(1), par(CONV_K),
                  act, act, act],
        out_specs=[act, act, act, act, act, act, act, act, par(1), par(8)],
        out_shape=[f32o, f32o, bf, bf, bf, bf, bf, bf, _sds((1, HGRN_WIDTH), F32), _sds((8, HGRN_WIDTH), F32)],
        compiler_params=_cp(("parallel", "arbitrary")),
    )(o_raw, u, o_b, u, u, u, u, u, u, u, u, u, dhc, gn_l, cw_l, dha, dhb, dhc)


def _merge_specs(T, order):
    tr = _tile(T, 512)

    def ucol(off):
        if order == "ij":
            return pl.BlockSpec((tr, MIX_COLS), lambda i, j, off=off: (i, off // MIX_COLS + j))
        return pl.BlockSpec((tr, MIX_COLS), lambda j, i, off=off: (i, off // MIX_COLS + j))

    act = pl.BlockSpec((tr, MIX_COLS), (lambda i, j: (i, j)) if order == "ij" else (lambda j, i: (i, j)))
    return tr, ucol, act


def _merge_fwd(u, ya, yb, yc):
    T = u.shape[0]
    tr, ucol, act = _merge_specs(T, "ij")

    def body(ma_ref, mb_ref, mc_ref, ya_ref, yb_ref, yc_ref, o_ref, ot_ref):
        merged = (jax.nn.sigmoid(ma_ref[...]) * ya_ref[...] + jax.nn.sigmoid(mb_ref[...]) * yb_ref[...]
                  + jax.nn.sigmoid(mc_ref[...]) * yc_ref[...])
        o_ref[...] = merged.astype(BF16)
        ot_ref[...] = merged.T.astype(BF16)

    return pl.pallas_call(
        body,
        name="merge_fwd",
        grid=(T // tr, D_MODEL // MIX_COLS),
        in_specs=[ucol(OFF_MA), ucol(OFF_MB), ucol(OFF_MC), act, act, act],
        out_specs=[act, pl.BlockSpec((MIX_COLS, tr), lambda i, j: (j, i))],
        out_shape=[_sds((T, D_MODEL), BF16), _sds((D_MODEL, T), BF16)],
        compiler_params=_cp(("parallel", "parallel")),
    )(u, u, u, ya, yb, yc)


def _merge_bwd(u, ya, yb, yc, dmerged, plan=None):
    T = u.shape[0]
    tr, ucol, act = _merge_specs(T, "ij")

    def body(ma_ref, mb_ref, mc_ref, ya_ref, yb_ref, yc_ref, dm_ref, dya_ref, dyb_ref, dyc_ref, dma_ref, dmb_ref, dmc_ref):
        dm = dm_ref[...]
        for m_ref, y_ref, dy_ref, dg_ref in ((ma_ref, ya_ref, dya_ref, dma_ref), (mb_ref, yb_ref, dyb_ref, dmb_ref),
                                             (mc_ref, yc_ref, dyc_ref, dmc_ref)):
            s = jax.nn.sigmoid(m_ref[...])
            dy_ref[...] = (dm * s).astype(BF16)
            dg_ref[...] = (dm * y_ref[...] * s * (1.0 - s)).astype(BF16)

    out = _sds((T, D_MODEL), BF16)
    return _call(
        body,
        name="merge_bwd",
        grid=(T // tr, D_MODEL // MIX_COLS),
        in_specs=[ucol(OFF_MA), ucol(OFF_MB), ucol(OFF_MC), act, act, act, act],
        out_specs=[act] * 6,
        out_shape=[out] * 6,
        sem=("parallel", "parallel"),
        args=(u, u, u, ya, yb, yc, dmerged),
        plan=plan,
    )


def _ln_fwd(x, y, g_l, b_l):
    T = x.shape[0]
    tr = _tile(T, 256)
    row = pl.BlockSpec((tr, D_MODEL), lambda i: (i, 0))
    col = pl.BlockSpec((D_MODEL, tr), lambda i: (0, i))
    par = pl.BlockSpec((1, D_MODEL), lambda i: (0, 0))

    def body(x_ref, y_ref, g_ref, b_ref, o_ref, z_ref, ob_ref, ot_ref):
        z = ALPHA * x_ref[...] + y_ref[...]
        z_ref[...] = z
        mu = jnp.mean(z, axis=-1, keepdims=True)
        zc = z - mu
        var = jnp.mean(zc * zc, axis=-1, keepdims=True)
        o = zc * lax.rsqrt(var + LN_EPS) * g_ref[...] + b_ref[...]
        o_ref[...] = o
        ob_ref[...] = o.astype(BF16)
        ot_ref[...] = o.T.astype(BF16)

    return pl.pallas_call(
        body,
        name="ln_fwd",
        grid=(T // tr,),
        in_specs=[row, row, par, par],
        out_specs=[row, row, row, col],
        out_shape=[_sds((T, D_MODEL), F32)] * 2 + [_sds((T, D_MODEL), BF16), _sds((D_MODEL, T), BF16)],
        compiler_params=_cp(("parallel",)),
    )(x, y, g_l, b_l)


def _operand_forms(x):
    T = x.shape[0]
    tr = _tile(T, 256)
    row = pl.BlockSpec((tr, D_MODEL), lambda i: (i, 0))
    col = pl.BlockSpec((D_MODEL, tr), lambda i: (0, i))

    def body(x_ref, xb_ref, xt_ref):
        xv = x_ref[...]
        xb_ref[...] = xv.astype(BF16)
        xt_ref[...] = xv.T.astype(BF16)

    return pl.pallas_call(
        body,
        name="operand_forms",
        grid=(T // tr,),
        in_specs=[row],
        out_specs=[row, col],
        out_shape=[_sds((T, D_MODEL), BF16), _sds((D_MODEL, T), BF16)],
        compiler_params=_cp(("parallel",)),
    )(x)


def _ln_bwd(z, dxn, g_l):
    T = z.shape[0]
    tr = _tile(T, 256)
    row = pl.BlockSpec((tr, D_MODEL), lambda i: (i, 0))
    par = pl.BlockSpec((1, D_MODEL), lambda i: (0, 0))

    def body(z_ref, d_ref, g_ref, dz_ref, dzb_ref, dg_ref, db_ref):
        @pl.when(pl.program_id(0) == 0)
        def _():
            dg_ref[...] = jnp.zeros_like(dg_ref)
            db_ref[...] = jnp.zeros_like(db_ref)

        z = z_ref[...]
        d = d_ref[...]
        mu = jnp.mean(z, axis=-1, keepdims=True)
        zc = z - mu
        rstd = lax.rsqrt(jnp.mean(zc * zc, axis=-1, keepdims=True) + LN_EPS)
        zh = zc * rstd
        dg_ref[...] += jnp.sum(d * zh, axis=0, keepdims=True)
        db_ref[...] += jnp.sum(d, axis=0, keepdims=True)
        dh = d * g_ref[...]
        dz = rstd * (dh - jnp.mean(dh, axis=-1, keepdims=True) - zh * jnp.mean(dh * zh, axis=-1, keepdims=True))
        dz_ref[...] = dz
        dzb_ref[...] = dz.astype(BF16)

    return pl.pallas_call(
        body,
        name="ln_bwd",
        grid=(T // tr,),
        in_specs=[row, row, par],
        out_specs=[row, row, par, par],
        out_shape=[_sds((T, D_MODEL), F32), _sds((T, D_MODEL), BF16), _sds((1, D_MODEL), F32), _sds((1, D_MODEL), F32)],
        compiler_params=_cp(("arbitrary",)),
    )(z, dxn, g_l)


def _loss_head(y, target):
    T = y.shape[0]
    tr = _tile(T, 256)
    row = pl.BlockSpec((tr, D_MODEL), lambda i: (i, 0))
    acc = pl.BlockSpec((8, 128), lambda i: (0, 0))

    def body(y_ref, t_ref, l_ref, d_ref):
        @pl.when(pl.program_id(0) == 0)
        def _():
            l_ref[...] = jnp.zeros_like(l_ref)

        err = y_ref[...] - t_ref[...]
        d_ref[...] = err * (1.0 / D_MODEL)
        part = 0.5 * jnp.sum(jnp.sum(err * err, axis=-1, keepdims=True) * (1.0 / D_MODEL), axis=0, keepdims=True)
        r = lax.broadcasted_iota(jnp.int32, (8, 128), 0)
        c = lax.broadcasted_iota(jnp.int32, (8, 128), 1)
        l_ref[...] += jnp.where((r == 0) & (c == 0), part, 0.0)

    return pl.pallas_call(
        body,
        name="loss_head",
        grid=(T // tr,),
        in_specs=[row, row],
        out_specs=[acc, row],
        out_shape=[_sds((8, 128), F32), _sds((T, D_MODEL), F32)],
        compiler_params=_cp(("arbitrary",)),
    )(y, target)


ADAMW_BLOCK_ELEMS = 512 * 1024


def _adamw(w, g, m, v, name):
    shape = w.shape
    cols = shape[-1]
    rows = math.prod(shape[:-1])
    flat = lambda a: a.reshape(rows, cols)
    if rows * cols <= ADAMW_BLOCK_ELEMS or rows % 8:
        tr = rows
    else:
        tr = 8
        while rows % (tr * 2) == 0 and tr * 2 * cols <= ADAMW_BLOCK_ELEMS:
            tr *= 2
    blk = pl.BlockSpec((tr, cols), lambda i: (i, 0))
    c1 = 1.0 - ADAM_B1 ** ADAM_STEP
    c2 = 1.0 - ADAM_B2 ** ADAM_STEP

    def body(w_ref, g_ref, m_ref, v_ref, d_ref, nm_ref, nv_ref):
        gg = g_ref[...]
        nm = ADAM_B1 * m_ref[...] + (1.0 - ADAM_B1) * gg
        nv = ADAM_B2 * v_ref[...] + (1.0 - ADAM_B2) * (gg * gg)
        nm_ref[...] = nm
        nv_ref[...] = nv
        d_ref[...] = -ADAM_LR * ((nm / c1) / (jnp.sqrt(nv / c2) + ADAM_EPS) + ADAM_WD * w_ref[...])

    outs = pl.pallas_call(
        body,
        name=name,
        grid=(rows // tr,),
        in_specs=[blk] * 4,
        out_specs=[blk] * 3,
        out_shape=[_sds((rows, cols), F32)] * 3,
        compiler_params=_cp(("parallel",)),
    )(flat(w), flat(g), flat(m), flat(v))
    return tuple(o.reshape(shape) for o in outs)


def _t5_bucket(dist):
    max_exact = N_BUCKETS // 2
    logd = jnp.log(jnp.maximum(dist, 1).astype(F32) / max_exact) / math.log(MAX_DISTANCE / max_exact)
    large = jnp.minimum(max_exact + (logd * (N_BUCKETS - max_exact)).astype(jnp.int32), N_BUCKETS - 1)
    return jnp.where(dist < max_exact, dist, large)


def _band_bias(rel_bias):
    i = jnp.arange(WINDOW)[:, None]
    j = jnp.arange(2 * WINDOW)[None, :]
    bucket = _t5_bucket(jnp.clip(WINDOW + i - j, 0, WINDOW - 1))
    onehot = (bucket[:, :, None] == jnp.arange(N_BUCKETS)[None, None, :]).astype(F32)
    bias = jnp.einsum("ijb,bh->hij", onehot, rel_bias.astype(F32), precision=lax.Precision.HIGHEST)
    rel = WINDOW + i - j
    return jnp.where(((rel >= 0) & (rel < WINDOW))[None], bias, MASK_VALUE)


def _lower_bounds(lb_param):
    soft = jax.nn.softmax(lb_param.astype(F32), axis=0)
    return jnp.cumsum(soft, axis=0) - soft[0:1]


def _mm_rows(T):
    return _tile(T, 1024)


def _layer_fwd(xs, ex, layer, lower_l, bias4, sink4_l, gn_l, cw_l, lng_l, lnb_l):
    x, xb, xt = xs
    T = x.shape[0]
    tm = _mm_rows(T)
    plan = ex.fwd_plan(layer)
    u = _matmul(xb, ex.weights[layer]["w_in"][0], mode="nn", tm=_tile(T, 2048), tn=768, tk=D_MODEL, name="mm_u", plan=plan)
    if plan is not None:
        u, landed = u
        plan = ex.fwd_pass_plan(layer, landed)
    (o_raw, states), landed = _hgrn_fwd(u, lower_l, plan=plan)
    if plan is not None:
        ex.fwd_landed(layer, landed)
    w_proj_l, w_out_l = ex.weights[layer]["w_proj"], ex.weights[layer]["w_out"][0]
    o_b = _attn_fwd(u, bias4, sink4_l)
    ha, hb, hc, hat, hbt, hct = _mix_fwd(u, o_raw, o_b, gn_l, cw_l)
    ys = [_matmul(h, w_proj_l, mode="nn", tm=tm, tn=1024, tk=HGRN_WIDTH, b_idx=i, name="mm_proj", out_dtype=BF16)
          for i, h in enumerate((ha, hb, hc))]
    merged, merged_t = _merge_fwd(u, *ys)
    y = _matmul(merged, w_out_l, mode="nn", tm=tm, tn=1024, tk=D_MODEL, name="mm_out")
    xn, z, xnb, xnt = _ln_fwd(x, y, lng_l, lnb_l)
    saved = dict(xt=xt, u=u, o_raw=o_raw, states=states, o_b=o_b, hts=(hat, hbt, hct), ys=ys,
                 merged_t=merged_t, z=z)
    return (xn, xnb, xnt), saved


def _layer_bwd(dxn, s, ex, layer, lower_l, bias4, sink4_l, gn_l, cw_l, lng_l):
    T = dxn.shape[0]
    tm = _mm_rows(T)
    u = s["u"]
    w = ex.weights[layer]
    w_in_l, w_proj_l, w_out_l = w["w_in"][0], w["w_proj"], w["w_out"][0]
    dz, dzb, d_lng, d_lnb = _ln_bwd(s["z"], dxn, lng_l)
    dmerged = _matmul(dzb, w_out_l, mode="nt", tm=tm, tn=1024, tk=D_MODEL, name="mm_dmerged")
    g_w_out = _matmul(s["merged_t"], dzb, mode="nn", tm=1024, tn=1024, tk=2048, name="mm_gw_out")
    plan = ex.pair_plan(layer)
    (*dys, dma, dmb, dmc), landed = _merge_bwd(u, *s["ys"], dmerged, plan=plan)
    if plan is not None:
        ex.pair_landed(layer, landed)
    dhs =[_matmul(dy, w_proj_l, mode="nt", tm=tm, tn=1024, tk=D_MODEL, b_idx=i, name="mm_dh") for i, dy in enumerate(dys)]
    g_w_proj = jnp.stack([_matmul(ht, dy, mode="nn", tm=1024, tn=1024, tk=2048, name="mm_gw_proj")
                          for ht, dy in zip(s["hts"], dys)])
    do_raw, do_b, dag, dbg, dcb, dcc, dcx, dcg, d_gn, d_cw = _mix_bwd(u, s["o_raw"], s["o_b"], gn_l, cw_l, *dhs)
    daq, daf, dai, d_lower = _hgrn_bwd(u, lower_l, s["states"], do_raw)
    plan = ex.slab_plan(layer)
    (dbq, dkc, dkp, dvc, dvp, d_bias4, d_sink4), landed = _attn_bwd(u, bias4, sink4_l, do_b, plan=plan)
    if plan is not None:
        ex.slab_landed(layer, landed)
    dbk = _with_next_block_part(dkc, dkp)
    dbv = _with_next_block_part(dvc, dvp)
    du = jnp.concatenate([daq, daf, dai, dag, dbq, dbk, dbv, dbg, dcb, dcc, dcx, dcg, dma, dmb, dmc], axis=1)
    g_w_in = _matmul(s["xt"], du, mode="nn", tm=1024, tn=768, tk=4096, name="mm_gw_in")
    ex.grads_ready(layer, dict(w_in=g_w_in[None], w_proj=g_w_proj, w_out=g_w_out[None]))
    assemble = ex.assemble_plan(layer)
    tail = ex.tail_plan() if layer == 0 else None
    carried = [p for p in (assemble, tail) if p is not None]
    plan = _merge_plans(carried) if carried else None
    dx = _matmul(du, w_in_l, mode="nt", tm=_tile(T, 512), tn=1024, tk=5632, name="mm_dx", add=dz, add_scale=ALPHA, plan=plan)
    if plan is not None:
        dx, landed = dx
        n_assemble = 0 if assemble is None else len(assemble.out_shapes)
        if assemble is not None:
            ex.assembled(layer, landed[:n_assemble])
        if tail is not None:
            ex.slab_landed(-1, landed[n_assemble:])
    d_sinks = jnp.sum(d_sink4.reshape(ATTN_HEADS, WINDOW), axis=-1)
    small = dict(lower=d_lower[0], gn=d_gn[0], sinks=d_sinks, cw=d_cw[:CONV_K], bias=d_bias4.reshape(ATTN_HEADS, WINDOW, 2 * WINDOW),
                 lng=d_lng[0], lnb=d_lnb[0])
    return dx, small


def _local_step(x, target, ex, lb_param, hgrn_norm_g, attn_sinks, conv_w_full, rel_bias, ln_g, ln_b):
    lower, lower_vjp = jax.vjp(_lower_bounds, lb_param)
    bias, bias_vjp = jax.vjp(_band_bias, rel_bias)
    bias4 = bias.reshape(ATTN_KV_HEADS, QROWS, 2 * WINDOW)
    sink4 = jnp.broadcast_to(attn_sinks.reshape(DEPTH, ATTN_HEADS, 1, 1), (DEPTH, ATTN_HEADS, WINDOW, 1)).reshape(
        DEPTH, ATTN_KV_HEADS, QROWS, 1)
    row = lambda a, l: a[l:l + 1]
    saved = []
    hs = (x, *_operand_forms(x))
    for l in range(DEPTH):
        hs, s = _layer_fwd(hs, ex, l, row(lower, l), bias4, sink4[l], row(hgrn_norm_g, l), conv_w_full[l], row(ln_g, l), row(ln_b, l))
        saved.append(s)
    loss_blk, dh = _loss_head(hs[0], target)
    smalls = [None] * DEPTH
    for l in reversed(range(DEPTH)):
        dh, smalls[l] = _layer_bwd(dh, saved[l], ex, l, row(lower, l), bias4, sink4[l], row(hgrn_norm_g, l), conv_w_full[l], row(ln_g, l))
    stack = lambda k: jnp.stack([sm[k] for sm in smalls])
    d_bias = smalls[0]["bias"] + smalls[1]["bias"] + smalls[2]["bias"] + smalls[3]["bias"]
    small = dict(
        lb_param=lower_vjp(stack("lower"))[0], hgrn_norm_g=stack("gn"), attn_sinks=stack("sinks"), conv_w=stack("cw"),
        rel_bias=bias_vjp(d_bias)[0], ln_g=stack("lng"), ln_b=stack("lnb"))
    return loss_blk, dh, small


ANY = pl.BlockSpec(memory_space=pl.ANY)
DMA_SEM = pltpu.SemaphoreType.DMA


def _coords():
    return lax.axis_index("x"), lax.axis_index("y"), lax.axis_index("c")


def _other_chips(x, y):
    return [(1 - x, y), (x, 1 - y), (1 - x, 1 - y)]


def _remote(src, dst, send_sem, recv_sem, device):
    return pltpu.make_async_remote_copy(src_ref=src, dst_ref=dst, send_sem=send_sem, recv_sem=recv_sem,
                                        device_id=device, device_id_type=MESH)


def _sub(ref, axis, index, size):
    idx = [slice(None)] * len(ref.shape)
    idx[axis] = pl.ds(pl.multiple_of(index * size, size), size)
    return ref.at[tuple(idx)]


class _Plan:
    def __init__(self, inputs, out_shapes, aliases, n, make):
        self.inputs, self.out_shapes, self.aliases, self.n, self.make = tuple(inputs), tuple(out_shapes), dict(aliases), n, make


class _Xfer:
    def __init__(self, send, recv=None):
        self.send, self.recv = send, send if recv is None else recv

    def start(self):
        self.send.start()

    def wait(self):
        self.send.wait_send()
        self.recv.wait_recv()


def _merge_plans(plans):
    def make(in_refs, out_refs, send_sems, recv_sems, base):
        out, i0, o0, b0 = [], 0, 0, base
        for p in plans:
            out += p.make(in_refs[i0:i0 + len(p.inputs)], out_refs[o0:o0 + len(p.out_shapes)], send_sems, recv_sems, b0)
            i0, o0, b0 = i0 + len(p.inputs), o0 + len(p.out_shapes), b0 + p.n
        return out

    inputs, out_shapes, aliases = [], [], {}
    for p in plans:
        aliases.update({len(inputs) + k: len(out_shapes) + v for k, v in p.aliases.items()})
        inputs += p.inputs
        out_shapes += p.out_shapes
    return _Plan(inputs, out_shapes, aliases, sum(p.n for p in plans), make)


def _run_plan(plan, name):
    ni, no = len(plan.inputs), len(plan.out_shapes)

    def body(*refs):
        transfers = plan.make(refs[:ni], refs[ni:ni + no], refs[ni + no], refs[ni + no + 1], 0)
        for t in transfers:
            t.start()
        for t in transfers:
            t.wait()

    outs = pl.pallas_call(
        body, name=name, in_specs=[ANY] * ni, out_specs=[ANY] * no, out_shape=list(plan.out_shapes),
        input_output_aliases=plan.aliases, scratch_shapes=[DMA_SEM((plan.n,)), DMA_SEM((plan.n,))],
    )(*plan.inputs)
    return list(outs)


def _gather_send_plan(shards, layer, sax):
    shp = shards.shape[1:]
    hax = 3 - sax
    w, hw = shp[sax], shp[hax] // 2
    out_shape = list(shp)
    out_shape[sax] = w * N_CHIPS

    def make(in_refs, out_refs, send_sems, recv_sems, base):
        (src_ref,), (out_ref,) = in_refs, out_refs
        x, y, c = _coords()
        j = 2 * x + y
        src = src_ref.at[layer]
        own = pltpu.make_async_copy(src, _sub(out_ref, sax, j, w), send_sems.at[base])
        dst = _sub(_sub(out_ref, sax, j, w), hax, c, hw)
        return [own] + [
            _Xfer(_remote(_sub(src, hax, c, hw), dst, send_sems.at[base + 1 + k], recv_sems.at[base + 1 + k], (px, py, c)))
            for k, (px, py) in enumerate(_other_chips(x, y))]

    return _Plan([shards], [_sds(tuple(out_shape), shards.dtype)], {}, 4, make)


def _gather_pass_plan(full, sax):
    hax = 3 - sax
    w, hw = full.shape[sax] // N_CHIPS, full.shape[hax] // 2

    def make(in_refs, out_refs, send_sems, recv_sems, base):
        (out_ref,) = out_refs
        x, y, c = _coords()
        region = lambda slab, half: _sub(_sub(out_ref, sax, slab, w), hax, half, hw)
        out = []
        for k, (px, py) in enumerate(_other_chips(x, y)):
            mine, theirs = region(2 * px + py, c), region(2 * px + py, 1 - c)
            sems = send_sems.at[base + k], recv_sems.at[base + k]
            out.append(_Xfer(_remote(mine, mine, *sems, (x, y, 1 - c)), _remote(theirs, theirs, *sems, (x, y, c))))
        return out

    return _Plan([full], [_sds(full.shape, full.dtype)], {0: 0}, 3, make)


def _pair_exchange_plan(g, hax):
    hw = g.shape[hax] // 2
    out_shape = list(g.shape)
    out_shape[hax] = hw

    def make(in_refs, out_refs, send_sems, recv_sems, base):
        x, y, c = _coords()
        return [_Xfer(_remote(_sub(in_refs[0], hax, 1 - c, hw), out_refs[0], send_sems.at[base], recv_sems.at[base], (x, y, 1 - c)))]

    return _Plan([g], [_sds(tuple(out_shape), g.dtype)], {}, 1, make)


def _add_own_half(place, g, recv, hax, blk, name):
    L, ah, bh = recv.shape
    tr, tc = blk
    nr, nc = ah // tr, bh // tc

    def g_map(l, i, jc, p):
        return (l, i + p[0] * nr, jc) if hax == 1 else (l, i, jc + p[0] * nc)

    def body(p_ref, g_ref, r_ref, o_ref):
        o_ref[...] = (g_ref[...] + r_ref[...]).astype(BF16)

    same = pl.BlockSpec((None, tr, tc), lambda l, i, jc, p: (l, i, jc))
    return pl.pallas_call(
        body,
        name=name,
        grid_spec=pltpu.PrefetchScalarGridSpec(
            num_scalar_prefetch=1, grid=(L, nr, nc),
            in_specs=[pl.BlockSpec((None, tr, tc), g_map), same], out_specs=same),
        out_shape=_sds(recv.shape, BF16),
        compiler_params=_cp(("parallel", "parallel", "parallel")),
    )(place, g, recv)


def _slab_exchange_plan(p, sax):
    w = p.shape[sax] // N_CHIPS
    slab_shape = list(p.shape)
    slab_shape[sax] = w

    def make(in_refs, out_refs, send_sems, recv_sems, base):
        x, y, c = _coords()
        return [_Xfer(_remote(_sub(in_refs[0], sax, 2 * px + py, w), out_refs[0].at[k], send_sems.at[base + k],
                              recv_sems.at[base + k], (px, py, c)))
                for k, (px, py) in enumerate(_other_chips(x, y))]

    return _Plan([p], [_sds((3, *slab_shape), p.dtype)], {}, 3, make)


def _add_slabs(place, g, pair, recv, sax, blk, name):
    hax = 3 - sax
    _, L, a, b = recv.shape
    tr, tc = blk
    nr, nc = a // tr, b // tc

    def g_map(l, i, jc, p):
        return (l, p[0] * nr + i, p[1] * nc + jc) if hax == 1 else (l, p[1] * nr + i, p[0] * nc + jc)

    def pair_map(l, i, jc, p):
        return (l, i, p[1] * nc + jc) if hax == 1 else (l, p[1] * nr + i, jc)

    def out_map(l, i, jc, p):
        return (l, p[0] * nr + i, jc) if hax == 1 else (l, i, p[0] * nc + jc)

    def body(p_ref, g_ref, pair_ref, r0_ref, r1_ref, r2_ref, o_ref):
        own = g_ref[...] + pair_ref[...]
        o_ref[...] = ((own + r0_ref[...].astype(F32)) + r1_ref[...].astype(F32)) + r2_ref[...].astype(F32)

    def rk(k):
        return pl.BlockSpec((None, None, tr, tc), lambda l, i, jc, p, k=k: (k, l, i, jc))

    out_shape = [L, a, b]
    out_shape[hax] *= 2
    blk3 = (None, tr, tc)
    return pl.pallas_call(
        body,
        name=name,
        grid_spec=pltpu.PrefetchScalarGridSpec(
            num_scalar_prefetch=1, grid=(L, nr, nc),
            in_specs=[pl.BlockSpec(blk3, g_map), pl.BlockSpec(blk3, pair_map), rk(0), rk(1), rk(2)],
            out_specs=pl.BlockSpec(blk3, out_map)),
        out_shape=_sds(tuple(out_shape), F32),
        compiler_params=_cp(("parallel", "parallel", "parallel")),
    )(place, g, pair, recv, recv, recv)


def _pair_assemble_plan(r, hax):
    hw = r.shape[hax] // 2

    def make(in_refs, out_refs, send_sems, recv_sems, base):
        x, y, c = _coords()
        mine, other = _sub(out_refs[0], hax, c, hw), _sub(out_refs[0], hax, 1 - c, hw)
        sems = send_sems.at[base], recv_sems.at[base]
        return [_Xfer(_remote(mine, mine, *sems, (x, y, 1 - c)), _remote(other, other, *sems, (x, y, c)))]

    return _Plan([r], [_sds(r.shape, r.dtype)], {0: 0}, 1, make)


CLASSES = dict(w_in=(2, (128, 4224), (128, 4224)), w_proj=(2, (256, 2048), (512, 512)), w_out=(1, (256, 1024), (256, 1024)))


class _Exchanges:
    def __init__(self, place, shards):
        self.place, self.shards = place, shards
        self.weights, self.pending, self.pair, self.halves, self.reduced = {}, {}, {}, {}, {}

    FIRST = ("w_in",)

    def _carried(self, layer):
        keys = [(0, k) for k in CLASSES if k not in self.FIRST] if layer == 0 else []
        return keys + ([(layer + 1, k) for k in CLASSES] if layer + 1 < DEPTH else [])

    def _send_plan(self, keys):
        return _merge_plans([_gather_send_plan(self.shards[k], l, CLASSES[k][0]) for l, k in keys])

    def _pass_plan(self, keys, bufs):
        return _merge_plans([_gather_pass_plan(b, CLASSES[k][0]) for (_, k), b in zip(keys, bufs)])

    def _landed(self, keys, bufs):
        for (l, k), b in zip(keys, bufs):
            self.weights.setdefault(l, {})[k] = b

    def first_weights(self):
        keys = [(0, k) for k in self.FIRST]
        sent = _run_plan(self._send_plan(keys), "gather_send_0")
        self._landed(keys, _run_plan(self._pass_plan(keys, sent), "gather_pass_0"))

    def fwd_plan(self, layer):
        keys = self._carried(layer)
        return self._send_plan(keys) if keys else None

    def fwd_pass_plan(self, layer, bufs):
        return self._pass_plan(self._carried(layer), bufs)

    def fwd_landed(self, layer, bufs):
        self._landed(self._carried(layer), bufs)

    def grads_ready(self, layer, grads):
        self.pending[layer] = grads

    def pair_plan(self, layer):
        if layer + 1 not in self.pending:
            return None
        g = self.pending[layer + 1]
        return _merge_plans([_pair_exchange_plan(g[k], 3 - CLASSES[k][0]) for k in CLASSES])

    def pair_landed(self, layer, bufs):
        self.pair[layer + 1] = dict(zip(CLASSES, bufs))

    def slab_plan(self, layer):
        src = layer + 1
        if src not in self.pair:
            return None
        g, pair = self.pending[src], self.pair[src]
        sums = [_add_own_half(self.place, g[k], pair[k], 3 - CLASSES[k][0], CLASSES[k][1], f"rs_pair_add_{k}_{src}") for k in CLASSES]
        return _merge_plans([_slab_exchange_plan(p, CLASSES[k][0]) for k, p in zip(CLASSES, sums)])

    def slab_landed(self, layer, bufs):
        src = layer + 1
        g, pair = self.pending.pop(src), self.pair.pop(src)
        self.halves[src] = [_add_slabs(self.place, g[k], pair[k], r, CLASSES[k][0], CLASSES[k][2], f"rs_slab_add_{k}_{src}")
                            for k, r in zip(CLASSES, bufs)]

    def assemble_plan(self, layer):
        if layer + 1 not in self.halves:
            return None
        return _merge_plans([_pair_assemble_plan(h, 3 - CLASSES[k][0]) for k, h in zip(CLASSES, self.halves[layer + 1])])

    def assembled(self, layer, bufs):
        del self.halves[layer + 1]
        self.reduced[layer + 1] = dict(zip(CLASSES, bufs))

    def tail_plan(self):
        self.pair_landed(-1, _run_plan(self.pair_plan(-1), "rs_pair_0"))
        return self.slab_plan(-1)

    def finish(self):
        self.assembled(-1, _run_plan(self.assemble_plan(-1), "rs_assemble_0"))
        return [self.reduced[l] for l in range(DEPTH)]


N_DEV = 8


def _all_reduce_small(v, name):
    rows = v.shape[0]

    def body(v_ref, gath_ref, sum_ref, send_sems, recv_sems, local_sem):
        x, y, c = _coords()
        me, sib = (x, y, c), (x, y, 1 - c)
        chips = _other_chips(x, y)

        def slot(px, py, pc):
            return gath_ref.at[pl.ds(pl.multiple_of((4 * px + 2 * py + pc) * rows, rows), rows), :]

        def copy(k, block, to, src=None):
            return _remote(slot(*block) if src is None else src, slot(*block), send_sems.at[k], recv_sems.at[k], to)

        mine = pltpu.make_async_copy(v_ref, slot(*me), local_sem)
        mine.start()
        first = [copy(0, me, sib, src=v_ref)] + [copy(1 + k, me, (*chip, c), src=v_ref) for k, chip in enumerate(chips)]
        for cp in first:
            cp.start()
        passed = [copy(4 + k, (*chip, c), sib) for k, chip in enumerate(chips)]
        for k, chip in enumerate(chips):
            copy(1 + k, (*chip, c), me).wait_recv()
            passed[k].start()
        copy(0, sib, me).wait_recv()
        for k, chip in enumerate(chips):
            copy(4 + k, (*chip, 1 - c), me).wait_recv()
        for cp in first + passed:
            cp.wait_send()
        mine.wait()
        acc = gath_ref[0:rows, :]
        for d in range(1, N_DEV):
            acc = acc + gath_ref[d * rows:(d + 1) * rows, :]
        sum_ref[...] = acc

    vm = pl.BlockSpec(memory_space=pltpu.VMEM)
    return pl.pallas_call(
        body, name=name, in_specs=[vm], out_specs=[vm, vm],
        out_shape=[_sds((N_DEV * rows, 128), F32), _sds((rows, 128), F32)],
        scratch_shapes=[DMA_SEM((7,)), DMA_SEM((7,)), DMA_SEM(())],
    )(v)[1]


def _pad_rows(a):
    flat = a.reshape(-1).astype(F32)
    rows = -(-flat.shape[0] // (8 * 128)) * 8
    return jnp.pad(flat, (0, rows * 128 - flat.shape[0])).reshape(rows, 128)


def _sum_over_devices(parts, name):
    blocks = [_pad_rows(a) for a in parts.values()]
    total = _all_reduce_small(jnp.concatenate(blocks, axis=0), name)
    out, r0 = {}, 0
    for (key, a), blk in zip(parts.items(), blocks):
        out[key] = total[r0:r0 + blk.shape[0]].reshape(-1)[:a.size].reshape(a.shape)
        r0 += blk.shape[0]
    return out


def kernel(x, w_in, w_proj_hgrn, w_proj_attn, w_proj_conv, w_out, lb_param, hgrn_norm_g, attn_sinks, conv_w, rel_bias, ln_g, ln_b, loss_target, m_w_in, m_w_proj_hgrn, m_w_proj_attn, m_w_proj_conv, m_w_out, m_lb_param, m_hgrn_norm_g, m_attn_sinks, m_conv_w, m_rel_bias, m_ln_g, m_ln_b, v_w_in, v_w_proj_hgrn, v_w_proj_attn, v_w_proj_conv, v_w_out, v_lb_param, v_hgrn_norm_g, v_attn_sinks, v_conv_w, v_rel_bias, v_ln_g, v_ln_b):
    xi, yi, ci = _coords()
    slab = 2 * xi + yi
    place = jnp.stack([ci, slab]).astype(jnp.int32)
    conv_cols = conv_w.shape[-1]

    w_in_b = w_in.astype(BF16)[:, None]
    w_proj_b = jnp.stack([w_proj_hgrn, w_proj_attn, w_proj_conv], axis=1).astype(BF16)
    w_out_b = w_out.astype(BF16)[:, None]
    ex = _Exchanges(place, dict(w_in=w_in_b, w_proj=w_proj_b, w_out=w_out_b))
    ex.first_weights()
    conv_spread = lax.dynamic_update_slice(jnp.zeros((DEPTH, CONV_K, CONV_WIDTH), F32), conv_w, (0, 0, slab * conv_cols))
    conv_full = 0.5 * _sum_over_devices({"conv_w": conv_spread}, "gather_conv_w")["conv_w"]

    loss_blk, dx, small = _local_step(x[0], loss_target[0], ex, lb_param, hgrn_norm_g, attn_sinks, conv_full, rel_bias, ln_g, ln_b)

    reduced = ex.finish()
    g_w_in = jnp.stack([r["w_in"][0] for r in reduced])
    g_w_proj = jnp.stack([r["w_proj"] for r in reduced])
    g_w_out = jnp.stack([r["w_out"][0] for r in reduced])
    small = dict(small, loss=loss_blk[0:1, 0:1])
    small = _sum_over_devices(small, "sum_small")
    loss = small["loss"][0, 0]
    g_conv = lax.dynamic_slice(small["conv_w"], (0, 0, slab * conv_cols), (DEPTH, CONV_K, conv_cols))

    grads = [g_w_in, g_w_proj[:, 0], g_w_proj[:, 1], g_w_proj[:, 2], g_w_out, small["lb_param"], small["hgrn_norm_g"],
             small["attn_sinks"], g_conv, small["rel_bias"], small["ln_g"], small["ln_b"]]
    names = ["w_in", "w_proj_hgrn", "w_proj_attn", "w_proj_conv", "w_out", "lb_param", "hgrn_norm_g", "attn_sinks",
             "conv_w", "rel_bias", "ln_g", "ln_b"]
    ws = [w_in, w_proj_hgrn, w_proj_attn, w_proj_conv, w_out, lb_param, hgrn_norm_g, attn_sinks, conv_w, rel_bias, ln_g, ln_b]
    ms = [m_w_in, m_w_proj_hgrn, m_w_proj_attn, m_w_proj_conv, m_w_out, m_lb_param, m_hgrn_norm_g, m_attn_sinks, m_conv_w,
          m_rel_bias, m_ln_g, m_ln_b]
    vs = [v_w_in, v_w_proj_hgrn, v_w_proj_attn, v_w_proj_conv, v_w_out, v_lb_param, v_hgrn_norm_g, v_attn_sinks, v_conv_w,
          v_rel_bias, v_ln_g, v_ln_b]
    upd = [_adamw(w, g, m, v, "adamw_" + n) for n, w, g, m, v in zip(names, ws, grads, ms, vs)]
    deltas, new_ms, new_vs = zip(*upd)
    return (loss, dx[None], *grads, *deltas, *new_ms, *new_vs)
```

```python
import functools
import math

import jax
import jax.numpy as jnp
from jax import lax
from jax.experimental import pallas as pl
from jax.experimental.pallas import tpu as pltpu

F32 = jnp.float32
BF16 = jnp.bfloat16
MXU_DTYPE = BF16

D_MODEL = 2048
DEPTH = 4
HGRN_WIDTH = 1024
HGRN_HEAD_DIM = 128
HGRN_HEADS = 8
HGRN_CHUNK = 64
ATTN_HEAD_DIM = 64
ATTN_HEADS = 16
ATTN_KV_HEADS = 4
ATTN_GROUP = ATTN_HEADS // ATTN_KV_HEADS
ATTN_WIDTH = 1024
KV_WIDTH = 256
WINDOW = 128
CONV_WIDTH = 1024
CONV_K = 3
N_BUCKETS = 32
MAX_DISTANCE = 128
ALPHA = (2.0 * DEPTH) ** 0.25
LN_EPS = 1e-5
RMS_EPS = 1e-6
N_IN = 16896
OFF_AQ, OFF_AF, OFF_AI, OFF_AG = 0, 1024, 2048, 3072
OFF_BQ, OFF_BK, OFF_BV, OFF_BG = 4096, 5120, 5376, 5632
OFF_CB, OFF_CC, OFF_CX, OFF_CG = 6656, 7680, 8704, 9728
OFF_MA, OFF_MB, OFF_MC = 10752, 12800, 14848

ADAM_LR = 0.001
ADAM_B1 = 0.9
ADAM_B2 = 0.999
ADAM_EPS = 1e-08
ADAM_WD = 0.01
ADAM_STEP = 10

N_CHIPS = 4
VMEM_LIMIT_BYTES = 48 * 1024 * 1024
EXP_CLAMP = 80.0
MASK_VALUE = -1e30
MESH = pl.DeviceIdType.MESH


def _cp(sem=None):
    return pltpu.CompilerParams(dimension_semantics=sem, vmem_limit_bytes=VMEM_LIMIT_BYTES)


def _tile(dim, pref):
    return pref if dim % pref == 0 else dim


def _sds(shape, dtype):
    return jax.ShapeDtypeStruct(shape, dtype)


def _call(body, *, name, grid, in_specs, out_specs, out_shape, args, scratch_shapes=(), sem=None, plan=None):
    in_specs, out_specs, out_shape = list(in_specs), list(out_specs), list(out_shape)
    if plan is None:
        outs = pl.pallas_call(body, name=name, grid=grid, in_specs=in_specs, out_specs=out_specs, out_shape=out_shape,
                              scratch_shapes=list(scratch_shapes), compiler_params=_cp(sem))(*args)
        return list(outs), []
    ni, no, ns = len(in_specs), len(out_specs), len(scratch_shapes)
    pi, po = len(plan.inputs), len(plan.out_shapes)

    def carrier(*refs):
        c_in, p_in = refs[:ni], refs[ni:ni + pi]
        c_out, p_out = refs[ni + pi:ni + pi + no], refs[ni + pi + no:ni + pi + no + po]
        c_scr = refs[ni + pi + no + po:ni + pi + no + po + ns]
        send_sems, recv_sems = refs[-2], refs[-1]
        ids = [pl.program_id(a) for a in range(len(grid))]
        first = functools.reduce(jnp.logical_and, [i == 0 for i in ids])
        last = functools.reduce(jnp.logical_and, [i == n - 1 for i, n in zip(ids, grid)])

        @pl.when(first)
        def _():
            for cp in plan.make(p_in, p_out, send_sems, recv_sems, 0):
                cp.start()

        body(*c_in, *c_out, *c_scr)

        @pl.when(last)
        def _():
            for cp in plan.make(p_in, p_out, send_sems, recv_sems, 0):
                cp.wait()

    any_spec = pl.BlockSpec(memory_space=pl.ANY)
    outs = pl.pallas_call(
        carrier, name=name, grid=grid,
        in_specs=in_specs + [any_spec] * pi, out_specs=out_specs + [any_spec] * po,
        out_shape=out_shape + list(plan.out_shapes),
        scratch_shapes=list(scratch_shapes) + [pltpu.SemaphoreType.DMA((plan.n,)), pltpu.SemaphoreType.DMA((plan.n,))],
        input_output_aliases={ni + k: no + v for k, v in plan.aliases.items()},
        compiler_params=_cp(tuple("arbitrary" for _ in grid)),
    )(*args, *plan.inputs)
    return list(outs[:no]), list(outs[no:])


_DIMS = {
    "nn": (((1,), (0,)), ((), ())),
    "nt": (((1,), (1,)), ((), ())),
    "tn": (((0,), (0,)), ((), ())),
}


def _dot_raw(a, b, mode):
    return lax.dot_general(a.astype(MXU_DTYPE), b.astype(MXU_DTYPE), _DIMS[mode], preferred_element_type=F32)


@functools.partial(jax.custom_vjp, nondiff_argnums=(2,))
def _dot(a, b, mode):
    return _dot_raw(a, b, mode)


def _dot_fwd(a, b, mode):
    return _dot_raw(a, b, mode), (a, b)


def _dot_bwd(mode, res, g):
    a, b = res
    if mode == "nn":
        return _dot_raw(g, b, "nt"), _dot_raw(a, g, "tn")
    if mode == "nt":
        return _dot_raw(g, b, "nn"), _dot_raw(g, a, "tn")
    return _dot_raw(b, g, "nt"), _dot_raw(a, g, "nn")


_dot.defvjp(_dot_fwd, _dot_bwd)


def _matmul(a, b, *, mode, tm, tn, tk, name, a_idx=None, b_idx=None, out_dtype=F32, add=None, add_scale=1.0, plan=None):
    a2, b2 = a.shape[-2:], b.shape[-2:]
    if mode == "nn":
        (M, K), (K2, N) = a2, b2
    elif mode == "nt":
        (M, K), (N, K2) = a2, b2
    else:
        (K, M), (K2, N) = a2, b2
    assert K == K2, (a.shape, b.shape, mode)
    tm, tn, tk = _tile(M, tm), _tile(N, tn), _tile(K, tk)
    nk = K // tk

    a_blk = (tk, tm) if mode == "tn" else (tm, tk)
    b_blk = (tn, tk) if mode == "nt" else (tk, tn)

    def a_map(i, j, k):
        ij = (k, i) if mode == "tn" else (i, k)
        return ij if a_idx is None else (a_idx,) + ij

    def b_map(i, j, k):
        ij = (j, k) if mode == "nt" else (k, j)
        return ij if b_idx is None else (b_idx,) + ij

    in_specs = [
        pl.BlockSpec(a_blk if a_idx is None else (None,) + a_blk, a_map),
        pl.BlockSpec(b_blk if b_idx is None else (None,) + b_blk, b_map),
    ]
    args = [a, b]
    if add is not None:
        in_specs.append(pl.BlockSpec((tm, tn), lambda i, j, k: (i, j)))
        args.append(add)
    n_in = len(args)

    def body(*refs):
        a_ref, b_ref = refs[0], refs[1]
        o_ref = refs[n_in]
        p = _dot_raw(a_ref[...], b_ref[...], mode)

        def finish(val):
            if add is not None:
                val = val + add_scale * refs[2][...]
            o_ref[...] = val.astype(out_dtype)

        if nk == 1:
            finish(p)
        else:
            acc_ref = refs[n_in + 1]
            k = pl.program_id(2)

            @pl.when(k == 0)
            def _():
                acc_ref[...] = p

            @pl.when(k > 0)
            def _():
                acc_ref[...] += p

            @pl.when(k == nk - 1)
            def _():
                finish(acc_ref[...])

    (out,), plan_outs = _call(
        body,
        name=name,
        grid=(M // tm, N // tn, nk),
        in_specs=in_specs,
        out_specs=[pl.BlockSpec((tm, tn), lambda i, j, k: (i, j))],
        out_shape=[_sds((M, N), out_dtype)],
        scratch_shapes=[pltpu.VMEM((tm, tn), F32)] if nk > 1 else [],
        sem=("parallel", "parallel", "arbitrary"),
        args=args,
        plan=plan,
    )
    return out if plan is None else (out, plan_outs)


def _input_projection(xb, w_in_l, plan=None):
    T = xb.shape[0]
    tm, tn = _tile(T, 1024), 768
    n_main = OFF_MA // tn
    assert OFF_MA % tn == 0 and N_IN % tn == 0

    def body(a_ref, b_ref, u_ref, ug_ref):
        p = _dot_raw(a_ref[...], b_ref[...], "nn")
        j = pl.program_id(1)

        @pl.when(j < n_main)
        def _():
            u_ref[...] = p

        @pl.when(j >= n_main)
        def _():
            ug_ref[...] = p.astype(BF16)

    outs, landed = _call(
        body,
        name="mm_u",
        grid=(T // tm, N_IN // tn),
        in_specs=[pl.BlockSpec((tm, D_MODEL), lambda i, j: (i, 0)), pl.BlockSpec((D_MODEL, tn), lambda i, j: (0, j))],
        out_specs=[pl.BlockSpec((tm, tn), lambda i, j: (i, jnp.minimum(j, n_main - 1))),
                   pl.BlockSpec((tm, tn), lambda i, j: (i, jnp.maximum(j - n_main, 0)))],
        out_shape=[_sds((T, OFF_MA), F32), _sds((T, N_IN - OFF_MA), BF16)],
        sem=("parallel", "arbitrary"),
        args=(xb, w_in_l),
        plan=plan,
    )
    return outs[0], outs[1], landed


def _scan_rows(x, reverse):
    n = x.shape[0]
    row = lax.broadcasted_iota(jnp.int32, x.shape, 0)
    s = 1
    while s < n:
        if reverse:
            x = x + jnp.where(row < n - s, pltpu.roll(x, n - s, 0), 0.0)
        else:
            x = x + jnp.where(row >= s, pltpu.roll(x, s, 0), 0.0)
        s *= 2
    return x


@jax.custom_vjp
def _cumsum_rows(x):
    return _scan_rows(x, False)


_cumsum_rows.defvjp(lambda x: (_scan_rows(x, False), None), lambda _, g: (_scan_rows(g, True),))


def _hgrn_chunk(state_t, qraw, fraw, v, lb):
    c = HGRN_CHUNK
    q = qraw * jax.nn.sigmoid(qraw) * (HGRN_HEAD_DIM ** -0.5)
    f = lb + (1.0 - lb) * jax.nn.sigmoid(fraw)
    k = 1.0 - f
    g = jnp.log(f)
    b = _cumsum_rows(g)
    row = lax.broadcasted_iota(jnp.int32, (c, HGRN_HEAD_DIM), 0)
    b_end = jnp.sum(g, axis=0, keepdims=True)
    b_mid = jnp.sum(jnp.where(row < c // 2, g, 0.0), axis=0, keepdims=True)
    inter = _dot(q * jnp.exp(b), state_t, "nt")
    qt = q * jnp.exp(jnp.minimum(b - b_mid, EXP_CLAMP))
    kt = k * jnp.exp(jnp.minimum(b_mid - b, EXP_CLAMP))
    s = _dot(qt, kt, "nt")
    ti = lax.broadcasted_iota(jnp.int32, (c, c), 0)
    si = lax.broadcasted_iota(jnp.int32, (c, c), 1)
    s = jnp.where(si <= ti, s, 0.0)
    intra = _dot(s, v, "nn")
    k_end = k * jnp.exp(b_end - b)
    new_state_t = state_t * jnp.exp(b_end) + _dot(v, k_end, "tn")
    return new_state_t, inter + intra


def _hgrn_specs(T):
    rows = _tile(T, 1024)
    return rows, T // rows, rows // HGRN_CHUNK


def _hgrn_fwd(u, lower_l, plan=None):
    T = u.shape[0]
    rows, nblk, ncr = _hgrn_specs(T)
    hb = HGRN_WIDTH // HGRN_HEAD_DIM

    def body(q_ref, f_ref, i_ref, lb_ref, o_ref, st_ref, state):
        @pl.when(pl.program_id(1) == 0)
        def _():
            state[...] = jnp.zeros_like(state)

        lb = lb_ref[...]
        for c in range(ncr):
            rs = pl.ds(c * HGRN_CHUNK, HGRN_CHUNK)
            st = state[...]
            st_ref[c] = st
            new, out = _hgrn_chunk(st, q_ref[rs, :], f_ref[rs, :], i_ref[rs, :], lb)
            state[...] = new
            o_ref[rs, :] = out

    blk = (rows, HGRN_HEAD_DIM)
    return _call(
        body,
        name="hgrn_fwd",
        grid=(HGRN_HEADS, nblk),
        in_specs=[
            pl.BlockSpec(blk, lambda h, r: (r, OFF_AQ // 128 + h)),
            pl.BlockSpec(blk, lambda h, r: (r, OFF_AF // 128 + h)),
            pl.BlockSpec(blk, lambda h, r: (r, OFF_AI // 128 + h)),
            pl.BlockSpec((1, HGRN_HEAD_DIM), lambda h, r: (0, h)),
        ],
        out_specs=[
            pl.BlockSpec(blk, lambda h, r: (r, h)),
            pl.BlockSpec((ncr, None, HGRN_HEAD_DIM, HGRN_HEAD_DIM), lambda h, r: (r, h, 0, 0)),
        ],
        out_shape=[_sds((T, HGRN_WIDTH), F32), _sds((T // HGRN_CHUNK, hb, HGRN_HEAD_DIM, HGRN_HEAD_DIM), F32)],
        scratch_shapes=[pltpu.VMEM((HGRN_HEAD_DIM, HGRN_HEAD_DIM), F32)],
        sem=("parallel", "arbitrary"),
        args=(u, u, u, lower_l),
        plan=plan,
    )


def _hgrn_bwd(u, lower_l, states, do_raw):
    T = u.shape[0]
    rows, nblk, ncr = _hgrn_specs(T)

    def body(q_ref, f_ref, i_ref, lb_ref, st_ref, do_ref, dq_ref, df_ref, di_ref, dlb_ref, dstate):
        @pl.when(pl.program_id(1) == 0)
        def _():
            dstate[...] = jnp.zeros_like(dstate)
            dlb_ref[...] = jnp.zeros_like(dlb_ref)

        lb = lb_ref[...]
        for c in reversed(range(ncr)):
            rs = pl.ds(c * HGRN_CHUNK, HGRN_CHUNK)
            _, vjp = jax.vjp(_hgrn_chunk, st_ref[c], q_ref[rs, :], f_ref[rs, :], i_ref[rs, :], lb)
            dst, dq, df, dv, dlb = vjp((dstate[...], do_ref[rs, :]))
            dstate[...] = dst
            dq_ref[rs, :] = dq.astype(BF16)
            df_ref[rs, :] = df.astype(BF16)
            di_ref[rs, :] = dv.astype(BF16)
            dlb_ref[...] += dlb

    blk = (rows, HGRN_HEAD_DIM)
    last = nblk - 1
    out_blk = pl.BlockSpec(blk, lambda h, r: (last - r, h))
    return pl.pallas_call(
        body,
        name="hgrn_bwd",
        grid=(HGRN_HEADS, nblk),
        in_specs=[
            pl.BlockSpec(blk, lambda h, r: (last - r, OFF_AQ // 128 + h)),
            pl.BlockSpec(blk, lambda h, r: (last - r, OFF_AF // 128 + h)),
            pl.BlockSpec(blk, lambda h, r: (last - r, OFF_AI // 128 + h)),
            pl.BlockSpec((1, HGRN_HEAD_DIM), lambda h, r: (0, h)),
            pl.BlockSpec((ncr, None, HGRN_HEAD_DIM, HGRN_HEAD_DIM), lambda h, r: (last - r, h, 0, 0)),
            out_blk,
        ],
        out_specs=[out_blk, out_blk, out_blk, pl.BlockSpec((1, HGRN_HEAD_DIM), lambda h, r: (0, h))],
        out_shape=[_sds((T, HGRN_WIDTH), BF16)] * 3 + [_sds((1, HGRN_WIDTH), F32)],
        scratch_shapes=[pltpu.VMEM((HGRN_HEAD_DIM, HGRN_HEAD_DIM), F32)],
        compiler_params=_cp(("parallel", "arbitrary")),
    )(u, u, u, lower_l, states, do_raw)


QROWS = ATTN_GROUP * WINDOW


def _attn_block(q, kp, kc, vp, vc, bp, bc, sink, first):
    qs = q * (ATTN_HEAD_DIM ** -0.5)
    sc = _dot(qs, kc, "nt") + bc
    m = jnp.maximum(jnp.max(sc, axis=-1, keepdims=True), sink)
    if not first:
        sp = _dot(qs, kp, "nt") + bp
        m = jnp.maximum(m, jnp.max(sp, axis=-1, keepdims=True))
    m = lax.stop_gradient(m)
    pc = jnp.exp(sc - m)
    den = jnp.sum(pc, axis=-1, keepdims=True) + jnp.exp(sink - m)
    o = _dot(pc, vc, "nn")
    if not first:
        pp = jnp.exp(sp - m)
        den = den + jnp.sum(pp, axis=-1, keepdims=True)
        o = o + _dot(pp, vp, "nn")
    return o * (1.0 / den)


def _first_or_later(n, run):
    @pl.when(n == 0)
    def _():
        run(True)

    @pl.when(n > 0)
    def _():
        run(False)


def _attn_in_specs():
    q_spec = pl.BlockSpec((WINDOW, ATTN_WIDTH), lambda n: (n, OFF_BQ // ATTN_WIDTH))
    k_cur = pl.BlockSpec((WINDOW, KV_WIDTH), lambda n: (n, OFF_BK // KV_WIDTH))
    k_prev = pl.BlockSpec((WINDOW, KV_WIDTH), lambda n: (jnp.maximum(n - 1, 0), OFF_BK // KV_WIDTH))
    v_cur = pl.BlockSpec((WINDOW, KV_WIDTH), lambda n: (n, OFF_BV // KV_WIDTH))
    v_prev = pl.BlockSpec((WINDOW, KV_WIDTH), lambda n: (jnp.maximum(n - 1, 0), OFF_BV // KV_WIDTH))
    bias = pl.BlockSpec((ATTN_KV_HEADS, QROWS, 2 * WINDOW), lambda n: (0, 0, 0))
    sink = pl.BlockSpec((ATTN_KV_HEADS, QROWS, 1), lambda n: (0, 0, 0))
    return [q_spec, k_prev, k_cur, v_prev, v_cur, bias, sink]


def _head_cols(a):
    return slice(a * ATTN_HEAD_DIM, (a + 1) * ATTN_HEAD_DIM)


def _group_rows(ref, h):
    return jnp.concatenate([ref[:, _head_cols(ATTN_GROUP * h + g)] for g in range(ATTN_GROUP)], axis=0)


def _attn_fwd(u, bias4, sink4):
    T = u.shape[0]

    def body(q_ref, kp_ref, kc_ref, vp_ref, vc_ref, b_ref, s_ref, o_ref):
        def run(first):
            for h in range(ATTN_KV_HEADS):
                hs = _head_cols(h)
                o = _attn_block(_group_rows(q_ref, h), kp_ref[:, hs], kc_ref[:, hs], vp_ref[:, hs], vc_ref[:, hs],
                                b_ref[h, :, :WINDOW], b_ref[h, :, WINDOW:], s_ref[h], first)
                for g in range(ATTN_GROUP):
                    o_ref[:, _head_cols(ATTN_GROUP * h + g)] = o[g * WINDOW:(g + 1) * WINDOW]

        _first_or_later(pl.program_id(0), run)

    return pl.pallas_call(
        body,
        name="attn_fwd",
        grid=(T // WINDOW,),
        in_specs=_attn_in_specs(),
        out_specs=pl.BlockSpec((WINDOW, ATTN_WIDTH), lambda n: (n, 0)),
        out_shape=_sds((T, ATTN_WIDTH), F32),
        compiler_params=_cp(("parallel",)),
    )(u, u, u, u, u, bias4, sink4)


def _attn_bwd(u, bias4, sink4, do, plan=None):
    T = u.shape[0]
    act = pl.BlockSpec((WINDOW, ATTN_WIDTH), lambda n: (n, 0))
    kv = pl.BlockSpec((WINDOW, KV_WIDTH), lambda n: (n, 0))
    in_specs = _attn_in_specs()
    bias, sink = in_specs[5], in_specs[6]

    def body(q_ref, kp_ref, kc_ref, vp_ref, vc_ref, b_ref, s_ref, do_ref,
             dq_ref, dkc_ref, dkp_ref, dvc_ref, dvp_ref, db_ref, ds_ref):
        n = pl.program_id(0)

        @pl.when(n == 0)
        def _():
            db_ref[...] = jnp.zeros_like(db_ref)
            ds_ref[...] = jnp.zeros_like(ds_ref)

        def run(first):
            dqs, dkps, dkcs, dvps, dvcs = [], [], [], [], []
            for h in range(ATTN_KV_HEADS):
                hs = _head_cols(h)
                _, vjp = jax.vjp(
                    functools.partial(_attn_block, first=first),
                    _group_rows(q_ref, h), kp_ref[:, hs], kc_ref[:, hs], vp_ref[:, hs], vc_ref[:, hs],
                    b_ref[h, :, :WINDOW], b_ref[h, :, WINDOW:], s_ref[h])
                dq, dkp, dkc, dvp, dvc, dbp, dbc, dsink = vjp(_group_rows(do_ref, h))
                dqs += [dq[g * WINDOW:(g + 1) * WINDOW] for g in range(ATTN_GROUP)]
                dkps.append(dkp)
                dkcs.append(dkc)
                dvps.append(dvp)
                dvcs.append(dvc)
                if not first:
                    db_ref[h, :, :WINDOW] += dbp
                db_ref[h, :, WINDOW:] += dbc
                ds_ref[h] += dsink
            dq_ref[...] = jnp.concatenate(dqs, axis=1).astype(BF16)
            dkc_ref[...] = jnp.concatenate(dkcs, axis=1)
            dkp_ref[...] = jnp.concatenate(dkps, axis=1)
            dvc_ref[...] = jnp.concatenate(dvcs, axis=1)
            dvp_ref[...] = jnp.concatenate(dvps, axis=1)

        _first_or_later(n, run)

    kv_sds = _sds((T, KV_WIDTH), F32)
    return _call(
        body,
        name="attn_bwd",
        grid=(T // WINDOW,),
        in_specs=in_specs + [act],
        out_specs=[act, kv, kv, kv, kv, bias, sink],
        out_shape=[_sds((T, ATTN_WIDTH), BF16), kv_sds, kv_sds, kv_sds, kv_sds,
                   _sds((ATTN_KV_HEADS, QROWS, 2 * WINDOW), F32), _sds((ATTN_KV_HEADS, QROWS, 1), F32)],
        sem=("arbitrary",),
        args=(u, u, u, u, u, bias4, sink4, do),
        plan=plan,
    )


def _with_next_block_part(cur, nxt):
    pad = jnp.zeros_like(nxt[:WINDOW])
    return (cur + jnp.concatenate([nxt[WINDOW:], pad], axis=0)).astype(BF16)


MIX_COLS = 512


def _silu(x):
    return x * jax.nn.sigmoid(x)


def _silu_grad(x):
    s = jax.nn.sigmoid(x)
    return s * (1.0 + x * (1.0 - s))


def _shift_rows_down(h, first, second):
    n = h.shape[0]
    row = lax.broadcasted_iota(jnp.int32, h.shape, 0)
    s1 = jnp.where(row == 0, first, pltpu.roll(h, 1, 0))
    s2 = jnp.where(row == 0, second, jnp.where(row == 1, first, pltpu.roll(h, 2, 0)))
    del n
    return s1, s2


def _shift_rows_up(h, first, second):
    n = h.shape[0]
    row = lax.broadcasted_iota(jnp.int32, h.shape, 0)
    s1 = jnp.where(row == n - 1, first, pltpu.roll(h, n - 1, 0))
    s2 = jnp.where(row == n - 1, second, jnp.where(row == n - 2, first, pltpu.roll(h, n - 2, 0)))
    return s1, s2


def _mix_rows(T):
    return _tile(T, 512)


def _mix_fwd(u, o_raw, o_b, gn_l, cw_l):
    T = u.shape[0]
    tr = _mix_rows(T)
    nrow = T // tr
    hr = tr // 8

    def ucol(off):
        return pl.BlockSpec((tr, MIX_COLS), lambda i, j, off=off: (i, off // MIX_COLS + j))

    def uprev(off):
        return pl.BlockSpec((8, MIX_COLS), lambda i, j, off=off: (jnp.maximum(i * hr - 1, 0), off // MIX_COLS + j))

    act = pl.BlockSpec((tr, MIX_COLS), lambda i, j: (i, j))
    par = lambda rows: pl.BlockSpec((rows, MIX_COLS), lambda i, j: (0, j))

    def body(oraw_ref, ag_ref, ob_ref, bg_ref, cb_ref, cc_ref, cx_ref, cg_ref, ccp_ref, cxp_ref, gn_ref, cw_ref,
             ha_ref, hb_ref, hc_ref, hat_ref, hbt_ref, hct_ref):
        ag = _silu(ag_ref[...])
        for h in range(MIX_COLS // HGRN_HEAD_DIM):
            cs = slice(h * HGRN_HEAD_DIM, (h + 1) * HGRN_HEAD_DIM)
            o = oraw_ref[:, cs]
            nrm = o * lax.rsqrt(jnp.mean(o * o, axis=-1, keepdims=True) + RMS_EPS)
            ha = nrm * gn_ref[:, cs] * ag[:, cs]
            ha_ref[:, cs] = ha.astype(BF16)
            hat_ref[cs, :] = ha.T.astype(BF16)
        hb = ob_ref[...] * _silu(bg_ref[...])
        hb_ref[...] = hb.astype(BF16)
        hbt_ref[...] = hb.T.astype(BF16)
        keep = (pl.program_id(0) > 0).astype(F32)
        hcur = cc_ref[...] * cx_ref[...]
        p1 = ccp_ref[7:8, :] * cxp_ref[7:8, :] * keep
        p2 = ccp_ref[6:7, :] * cxp_ref[6:7, :] * keep
        s1, s2 = _shift_rows_down(hcur, p1, p2)
        y = cw_ref[0:1, :] * s2 + cw_ref[1:2, :] * s1 + cw_ref[2:3, :] * hcur
        hc = cb_ref[...] * y * _silu(cg_ref[...])
        hc_ref[...] = hc.astype(BF16)
        hct_ref[...] = hc.T.astype(BF16)

    out = _sds((T, HGRN_WIDTH), BF16)
    out_t = _sds((HGRN_WIDTH, T), BF16)
    act_t = pl.BlockSpec((MIX_COLS, tr), lambda i, j: (j, i))
    return pl.pallas_call(
        body,
        name="mix_fwd",
        grid=(nrow, HGRN_WIDTH // MIX_COLS),
        in_specs=[act, ucol(OFF_AG), act, ucol(OFF_BG), ucol(OFF_CB), ucol(OFF_CC), ucol(OFF_CX), ucol(OFF_CG),
                  uprev(OFF_CC), uprev(OFF_CX), par(1), par(CONV_K)],
        out_specs=[act, act, act, act_t, act_t, act_t],
        out_shape=[out, out, out, out_t, out_t, out_t],
        compiler_params=_cp(("parallel", "parallel")),
    )(o_raw, u, o_b, u, u, u, u, u, u, u, gn_l, cw_l)


def _mix_bwd(u, o_raw, o_b, gn_l, cw_l, dha, dhb, dhc):
    T = u.shape[0]
    tr = _mix_rows(T)
    nrow = T // tr
    hr = tr // 8
    last_halo = T // 8 - 1

    def ucol(off):
        return pl.BlockSpec((tr, MIX_COLS), lambda j, i, off=off: (i, off // MIX_COLS + j))

    def uprev(off):
        return pl.BlockSpec((8, MIX_COLS), lambda j, i, off=off: (jnp.maximum(i * hr - 1, 0), off // MIX_COLS + j))

    def unext(off):
        return pl.BlockSpec((8, MIX_COLS), lambda j, i, off=off: (jnp.minimum((i + 1) * hr, last_halo), off // MIX_COLS + j))

    act = pl.BlockSpec((tr, MIX_COLS), lambda j, i: (i, j))
    act_next = pl.BlockSpec((8, MIX_COLS), lambda j, i: (jnp.minimum((i + 1) * hr, last_halo), j))
    par = lambda rows: pl.BlockSpec((rows, MIX_COLS), lambda j, i: (0, j))

    def body(oraw_ref, ag_ref, ob_ref, bg_ref, cb_ref, cc_ref, cx_ref, cg_ref, ccp_ref, cxp_ref,
             cbn_ref, cgn_ref, dhcn_ref, gn_ref, cw_ref, dha_ref, dhb_ref, dhc_ref,
             doraw_ref, dob_ref, dag_ref, dbg_ref, dcb_ref, dcc_ref, dcx_ref, dcg_ref, dgn_ref, dcw_ref):
        i = pl.program_id(1)

        @pl.when(i == 0)
        def _():
            dgn_ref[...] = jnp.zeros_like(dgn_ref)
            dcw_ref[...] = jnp.zeros_like(dcw_ref)

        ag = ag_ref[...]
        sag = _silu(ag)
        dha = dha_ref[...]
        for h in range(MIX_COLS // HGRN_HEAD_DIM):
            cs = slice(h * HGRN_HEAD_DIM, (h + 1) * HGRN_HEAD_DIM)
            o = oraw_ref[:, cs]
            rs = lax.rsqrt(jnp.mean(o * o, axis=-1, keepdims=True) + RMS_EPS)
            nrm = o * rs
            gn = gn_ref[:, cs]
            d = dha[:, cs]
            dag_ref[:, cs] = (d * nrm * gn * _silu_grad(ag[:, cs])).astype(BF16)
            dgn_ref[:, cs] += jnp.sum(d * sag[:, cs] * nrm, axis=0, keepdims=True)
            dn = d * sag[:, cs] * gn
            doraw_ref[:, cs] = rs * (dn - nrm * jnp.mean(dn * nrm, axis=-1, keepdims=True))
        bg = bg_ref[...]
        dhb = dhb_ref[...]
        dob_ref[...] = dhb * _silu(bg)
        dbg_ref[...] = (dhb * ob_ref[...] * _silu_grad(bg)).astype(BF16)
        keep_prev = (i > 0).astype(F32)
        keep_next = (i < nrow - 1).astype(F32)
        cc, cx, cb, cg = cc_ref[...], cx_ref[...], cb_ref[...], cg_ref[...]
        hcur = cc * cx
        p1 = ccp_ref[7:8, :] * cxp_ref[7:8, :] * keep_prev
        p2 = ccp_ref[6:7, :] * cxp_ref[6:7, :] * keep_prev
        s1, s2 = _shift_rows_down(hcur, p1, p2)
        w0, w1, w2 = cw_ref[0:1, :], cw_ref[1:2, :], cw_ref[2:3, :]
        y = w0 * s2 + w1 * s1 + w2 * hcur
        dhc = dhc_ref[...]
        scg = _silu(cg)
        doc = dhc * scg
        dcg_ref[...] = (dhc * cb * y * _silu_grad(cg)).astype(BF16)
        dcb_ref[...] = (doc * y).astype(BF16)
        dy = doc * cb
        n1 = dhcn_ref[0:1, :] * _silu(cgn_ref[0:1, :]) * cbn_ref[0:1, :] * keep_next
        n2 = dhcn_ref[1:2, :] * _silu(cgn_ref[1:2, :]) * cbn_ref[1:2, :] * keep_next
        u1, u2 = _shift_rows_up(dy, n1, n2)
        dh = w2 * dy + w1 * u1 + w0 * u2
        dcc_ref[...] = (dh * cx).astype(BF16)
        dcx_ref[...] = (dh * cc).astype(BF16)
        dcw_ref[0:1, :] += jnp.sum(dy * s2, axis=0, keepdims=True)
        dcw_ref[1:2, :] += jnp.sum(dy * s1, axis=0, keepdims=True)
        dcw_ref[2:3, :] += jnp.sum(dy * hcur, axis=0, keepdims=True)

    f32o, bf = _sds((T, HGRN_WIDTH), F32), _sds((T, HGRN_WIDTH), BF16)
    return pl.pallas_call(
        body,
        name="mix_bwd",
        grid=(HGRN_WIDTH // MIX_COLS, nrow),
        in_specs=[act, ucol(OFF_AG), act, ucol(OFF_BG), ucol(OFF_CB), ucol(OFF_CC), ucol(OFF_CX), ucol(OFF_CG),
                  uprev(OFF_CC), uprev(OFF_CX), unext(OFF_CB), unext(OFF_CG), act_next, par(1), par(CONV_K),
                  act, act, act],
        out_specs=[act, act, act, act, act, act, act, act, par(1), par(8)],
        out_shape=[f32o, f32o, bf, bf, bf, bf, bf, bf, _sds((1, HGRN_WIDTH), F32), _sds((8, HGRN_WIDTH), F32)],
        compiler_params=_cp(("parallel", "arbitrary")),
    )(o_raw, u, o_b, u, u, u, u, u, u, u, u, u, dhc, gn_l, cw_l, dha, dhb, dhc)


def _merge_specs(T):
    tr = _tile(T, 1024)

    def gate(off):
        return pl.BlockSpec((tr, MIX_COLS), lambda i, j, off=off: (i, (off - OFF_MA) // MIX_COLS + j))

    act = pl.BlockSpec((tr, MIX_COLS), lambda i, j: (i, j))
    return tr, gate, act


def _gate(m_ref):
    return jax.nn.sigmoid(m_ref[...].astype(F32))


def _merge_fwd(u, ya, yb, yc):
    T = u.shape[0]
    tr, ucol, act = _merge_specs(T)

    def body(ma_ref, mb_ref, mc_ref, ya_ref, yb_ref, yc_ref, o_ref, ot_ref):
        merged = _gate(ma_ref) * ya_ref[...] + _gate(mb_ref) * yb_ref[...] + _gate(mc_ref) * yc_ref[...]
        o_ref[...] = merged.astype(BF16)
        ot_ref[...] = merged.T.astype(BF16)

    return pl.pallas_call(
        body,
        name="merge_fwd",
        grid=(T // tr, D_MODEL // MIX_COLS),
        in_specs=[ucol(OFF_MA), ucol(OFF_MB), ucol(OFF_MC), act, act, act],
        out_specs=[act, pl.BlockSpec((MIX_COLS, tr), lambda i, j: (j, i))],
        out_shape=[_sds((T, D_MODEL), BF16), _sds((D_MODEL, T), BF16)],
        compiler_params=_cp(("parallel", "parallel")),
    )(u, u, u, ya, yb, yc)


def _merge_bwd(u, ya, yb, yc, dmerged, plan=None):
    T = u.shape[0]
    tr, ucol, act = _merge_specs(T)

    def body(ma_ref, mb_ref, mc_ref, ya_ref, yb_ref, yc_ref, dm_ref, dya_ref, dyb_ref, dyc_ref, dma_ref, dmb_ref, dmc_ref):
        dm = dm_ref[...]
        for m_ref, y_ref, dy_ref, dg_ref in ((ma_ref, ya_ref, dya_ref, dma_ref), (mb_ref, yb_ref, dyb_ref, dmb_ref),
                                             (mc_ref, yc_ref, dyc_ref, dmc_ref)):
            s = _gate(m_ref)
            dy_ref[...] = (dm * s).astype(BF16)
            dg_ref[...] = (dm * y_ref[...] * s * (1.0 - s)).astype(BF16)

    out = _sds((T, D_MODEL), BF16)
    return _call(
        body,
        name="merge_bwd",
        grid=(T // tr, D_MODEL // MIX_COLS),
        in_specs=[ucol(OFF_MA), ucol(OFF_MB), ucol(OFF_MC), act, act, act, act],
        out_specs=[act] * 6,
        out_shape=[out] * 6,
        sem=("parallel", "parallel"),
        args=(u, u, u, ya, yb, yc, dmerged),
        plan=plan,
    )


def _ln_fwd(x, y, g_l, b_l):
    T = x.shape[0]
    tr = _tile(T, 256)
    row = pl.BlockSpec((tr, D_MODEL), lambda i: (i, 0))
    col = pl.BlockSpec((D_MODEL, tr), lambda i: (0, i))
    par = pl.BlockSpec((1, D_MODEL), lambda i: (0, 0))

    def body(x_ref, y_ref, g_ref, b_ref, o_ref, z_ref, ob_ref, ot_ref):
        z = ALPHA * x_ref[...] + y_ref[...]
        z_ref[...] = z
        mu = jnp.mean(z, axis=-1, keepdims=True)
        zc = z - mu
        var = jnp.mean(zc * zc, axis=-1, keepdims=True)
        o = zc * lax.rsqrt(var + LN_EPS) * g_ref[...] + b_ref[...]
        o_ref[...] = o
        ob_ref[...] = o.astype(BF16)
        ot_ref[...] = o.T.astype(BF16)

    return pl.pallas_call(
        body,
        name="ln_fwd",
        grid=(T // tr,),
        in_specs=[row, row, par, par],
        out_specs=[row, row, row, col],
        out_shape=[_sds((T, D_MODEL), F32)] * 2 + [_sds((T, D_MODEL), BF16), _sds((D_MODEL, T), BF16)],
        compiler_params=_cp(("parallel",)),
    )(x, y, g_l, b_l)


def _operand_forms(x):
    T = x.shape[0]
    tr = _tile(T, 256)
    row = pl.BlockSpec((tr, D_MODEL), lambda i: (i, 0))
    col = pl.BlockSpec((D_MODEL, tr), lambda i: (0, i))

    def body(x_ref, xb_ref, xt_ref):
        xv = x_ref[...]
        xb_ref[...] = xv.astype(BF16)
        xt_ref[...] = xv.T.astype(BF16)

    return pl.pallas_call(
        body,
        name="operand_forms",
        grid=(T // tr,),
        in_specs=[row],
        out_specs=[row, col],
        out_shape=[_sds((T, D_MODEL), BF16), _sds((D_MODEL, T), BF16)],
        compiler_params=_cp(("parallel",)),
    )(x)


def _ln_bwd(z, dxn, g_l):
    T = z.shape[0]
    tr = _tile(T, 256)
    row = pl.BlockSpec((tr, D_MODEL), lambda i: (i, 0))
    par = pl.BlockSpec((1, D_MODEL), lambda i: (0, 0))

    def body(z_ref, d_ref, g_ref, dz_ref, dzb_ref, dg_ref, db_ref):
        @pl.when(pl.program_id(0) == 0)
        def _():
            dg_ref[...] = jnp.zeros_like(dg_ref)
            db_ref[...] = jnp.zeros_like(db_ref)

        z = z_ref[...]
        d = d_ref[...]
        mu = jnp.mean(z, axis=-1, keepdims=True)
        zc = z - mu
        rstd = lax.rsqrt(jnp.mean(zc * zc, axis=-1, keepdims=True) + LN_EPS)
        zh = zc * rstd
        dg_ref[...] += jnp.sum(d * zh, axis=0, keepdims=True)
        db_ref[...] += jnp.sum(d, axis=0, keepdims=True)
        dh = d * g_ref[...]
        dz = rstd * (dh - jnp.mean(dh, axis=-1, keepdims=True) - zh * jnp.mean(dh * zh, axis=-1, keepdims=True))
        dz_ref[...] = dz
        dzb_ref[...] = dz.astype(BF16)

    return pl.pallas_call(
        body,
        name="ln_bwd",
        grid=(T // tr,),
        in_specs=[row, row, par],
        out_specs=[row, row, par, par],
        out_shape=[_sds((T, D_MODEL), F32), _sds((T, D_MODEL), BF16), _sds((1, D_MODEL), F32), _sds((1, D_MODEL), F32)],
        compiler_params=_cp(("arbitrary",)),
    )(z, dxn, g_l)


def _loss_head(y, target):
    T = y.shape[0]
    tr = _tile(T, 256)
    row = pl.BlockSpec((tr, D_MODEL), lambda i: (i, 0))
    acc = pl.BlockSpec((8, 128), lambda i: (0, 0))

    def body(y_ref, t_ref, l_ref, d_ref):
        @pl.when(pl.program_id(0) == 0)
        def _():
            l_ref[...] = jnp.zeros_like(l_ref)

        err = y_ref[...] - t_ref[...]
        d_ref[...] = err * (1.0 / D_MODEL)
        part = 0.5 * jnp.sum(jnp.sum(err * err, axis=-1, keepdims=True) * (1.0 / D_MODEL), axis=0, keepdims=True)
        r = lax.broadcasted_iota(jnp.int32, (8, 128), 0)
        c = lax.broadcasted_iota(jnp.int32, (8, 128), 1)
        l_ref[...] += jnp.where((r == 0) & (c == 0), part, 0.0)

    return pl.pallas_call(
        body,
        name="loss_head",
        grid=(T // tr,),
        in_specs=[row, row],
        out_specs=[acc, row],
        out_shape=[_sds((8, 128), F32), _sds((T, D_MODEL), F32)],
        compiler_params=_cp(("arbitrary",)),
    )(y, target)


ADAMW_BLOCK_ELEMS = 512 * 1024


def _adamw(w, g, m, v, name):
    shape = w.shape
    cols = shape[-1]
    rows = math.prod(shape[:-1])
    flat = lambda a: a.reshape(rows, cols)
    if rows * cols <= ADAMW_BLOCK_ELEMS or rows % 8:
        tr = rows
    else:
        tr = 8
        while rows % (tr * 2) == 0 and tr * 2 * cols <= ADAMW_BLOCK_ELEMS:
            tr *= 2
    blk = pl.BlockSpec((tr, cols), lambda i: (i, 0))
    c1 = 1.0 - ADAM_B1 ** ADAM_STEP
    c2 = 1.0 - ADAM_B2 ** ADAM_STEP

    def body(w_ref, g_ref, m_ref, v_ref, d_ref, nm_ref, nv_ref):
        gg = g_ref[...]
        nm = ADAM_B1 * m_ref[...] + (1.0 - ADAM_B1) * gg
        nv = ADAM_B2 * v_ref[...] + (1.0 - ADAM_B2) * (gg * gg)
        nm_ref[...] = nm
        nv_ref[...] = nv
        d_ref[...] = -ADAM_LR * ((nm / c1) / (jnp.sqrt(nv / c2) + ADAM_EPS) + ADAM_WD * w_ref[...])

    outs = pl.pallas_call(
        body,
        name=name,
        grid=(rows // tr,),
        in_specs=[blk] * 4,
        out_specs=[blk] * 3,
        out_shape=[_sds((rows, cols), F32)] * 3,
        compiler_params=_cp(("parallel",)),
    )(flat(w), flat(g), flat(m), flat(v))
    return tuple(o.reshape(shape) for o in outs)


def _t5_bucket(dist):
    max_exact = N_BUCKETS // 2
    logd = jnp.log(jnp.maximum(dist, 1).astype(F32) / max_exact) / math.log(MAX_DISTANCE / max_exact)
    large = jnp.minimum(max_exact + (logd * (N_BUCKETS - max_exact)).astype(jnp.int32), N_BUCKETS - 1)
    return jnp.where(dist < max_exact, dist, large)


def _band_bias(rel_bias):
    i = jnp.arange(WINDOW)[:, None]
    j = jnp.arange(2 * WINDOW)[None, :]
    bucket = _t5_bucket(jnp.clip(WINDOW + i - j, 0, WINDOW - 1))
    onehot = (bucket[:, :, None] == jnp.arange(N_BUCKETS)[None, None, :]).astype(F32)
    bias = jnp.einsum("ijb,bh->hij", onehot, rel_bias.astype(F32), precision=lax.Precision.HIGHEST)
    rel = WINDOW + i - j
    return jnp.where(((rel >= 0) & (rel < WINDOW))[None], bias, MASK_VALUE)


def _lower_bounds(lb_param):
    soft = jax.nn.softmax(lb_param.astype(F32), axis=0)
    return jnp.cumsum(soft, axis=0) - soft[0:1]


def _mm_rows(T):
    return _tile(T, 1024)


def _layer_fwd(xs, ex, layer, lower_l, bias4, sink4_l, gn_l, cw_l, lng_l, lnb_l):
    x, xb, xt = xs
    T = x.shape[0]
    tm = _mm_rows(T)
    plan = ex.fwd_plan(layer)
    u, ug, landed = _input_projection(xb, ex.weights[layer]["w_in"][0], plan=plan)
    if plan is not None:
        plan = ex.fwd_pass_plan(layer, landed)
    (o_raw, states), landed = _hgrn_fwd(u, lower_l, plan=plan)
    if plan is not None:
        ex.fwd_landed(layer, landed)
    w_proj_l, w_out_l = ex.weights[layer]["w_proj"], ex.weights[layer]["w_out"][0]
    o_b = _attn_fwd(u, bias4, sink4_l)
    ha, hb, hc, hat, hbt, hct = _mix_fwd(u, o_raw, o_b, gn_l, cw_l)
    ys = [_matmul(h, w_proj_l, mode="nn", tm=tm, tn=1024, tk=HGRN_WIDTH, b_idx=i, name="mm_proj", out_dtype=BF16)
          for i, h in enumerate((ha, hb, hc))]
    merged, merged_t = _merge_fwd(ug, *ys)
    y = _matmul(merged, w_out_l, mode="nn", tm=tm, tn=1024, tk=D_MODEL, name="mm_out")
    xn, z, xnb, xnt = _ln_fwd(x, y, lng_l, lnb_l)
    saved = dict(xt=xt, u=u, ug=ug, o_raw=o_raw, states=states, o_b=o_b, hts=(hat, hbt, hct), ys=ys,
                 merged_t=merged_t, z=z)
    return (xn, xnb, xnt), saved


def _layer_bwd(dxn, s, ex, layer, lower_l, bias4, sink4_l, gn_l, cw_l, lng_l):
    T = dxn.shape[0]
    tm = _mm_rows(T)
    u = s["u"]
    w = ex.weights[layer]
    w_in_l, w_proj_l, w_out_l = w["w_in"][0], w["w_proj"], w["w_out"][0]
    dz, dzb, d_lng, d_lnb = _ln_bwd(s["z"], dxn, lng_l)
    dmerged = _matmul(dzb, w_out_l, mode="nt", tm=tm, tn=1024, tk=D_MODEL, name="mm_dmerged")
    g_w_out = _matmul(s["merged_t"], dzb, mode="nn", tm=1024, tn=1024, tk=2048, name="mm_gw_out")
    plan = ex.pair_plan(layer)
    (*dys, dma, dmb, dmc), landed = _merge_bwd(s["ug"], *s["ys"], dmerged, plan=plan)
    if plan is not None:
        ex.pair_landed(layer, landed)
    dhs =[_matmul(dy, w_proj_l, mode="nt", tm=tm, tn=1024, tk=D_MODEL, b_idx=i, name="mm_dh") for i, dy in enumerate(dys)]
    g_w_proj = jnp.stack([_matmul(ht, dy, mode="nn", tm=1024, tn=1024, tk=2048, name="mm_gw_proj")
                          for ht, dy in zip(s["hts"], dys)])
    do_raw, do_b, dag, dbg, dcb, dcc, dcx, dcg, d_gn, d_cw = _mix_bwd(u, s["o_raw"], s["o_b"], gn_l, cw_l, *dhs)
    daq, daf, dai, d_lower = _hgrn_bwd(u, lower_l, s["states"], do_raw)
    plan = ex.slab_plan(layer)
    (dbq, dkc, dkp, dvc, dvp, d_bias4, d_sink4), landed = _attn_bwd(u, bias4, sink4_l, do_b, plan=plan)
    if plan is not None:
        ex.slab_landed(layer, landed)
    dbk = _with_next_block_part(dkc, dkp)
    dbv = _with_next_block_part(dvc, dvp)
    du = jnp.concatenate([daq, daf, dai, dag, dbq, dbk, dbv, dbg, dcb, dcc, dcx, dcg, dma, dmb, dmc], axis=1)
    g_w_in = _matmul(s["xt"], du, mode="nn", tm=1024, tn=768, tk=4096, name="mm_gw_in")
    ex.grads_ready(layer, dict(w_in=g_w_in[None], w_proj=g_w_proj, w_out=g_w_out[None]))
    assemble = ex.assemble_plan(layer)
    tail = ex.tail_plan() if layer == 0 else None
    carried = [p for p in (assemble, tail) if p is not None]
    plan = _merge_plans(carried) if carried else None
    dx = _matmul(du, w_in_l, mode="nt", tm=_tile(T, 512), tn=1024, tk=5632, name="mm_dx", add=dz, add_scale=ALPHA, plan=plan)
    if plan is not None:
        dx, landed = dx
        n_assemble = 0 if assemble is None else len(assemble.out_shapes)
        if assemble is not None:
            ex.assembled(layer, landed[:n_assemble])
        if tail is not None:
            ex.slab_landed(-1, landed[n_assemble:])
    d_sinks = jnp.sum(d_sink4.reshape(ATTN_HEADS, WINDOW), axis=-1)
    small = dict(lower=d_lower[0], gn=d_gn[0], sinks=d_sinks, cw=d_cw[:CONV_K], bias=d_bias4.reshape(ATTN_HEADS, WINDOW, 2 * WINDOW),
                 lng=d_lng[0], lnb=d_lnb[0])
    return dx, small


def _local_step(x, target, ex, lb_param, hgrn_norm_g, attn_sinks, conv_w_full, rel_bias, ln_g, ln_b):
    lower, lower_vjp = jax.vjp(_lower_bounds, lb_param)
    bias, bias_vjp = jax.vjp(_band_bias, rel_bias)
    bias4 = bias.reshape(ATTN_KV_HEADS, QROWS, 2 * WINDOW)
    sink4 = jnp.broadcast_to(attn_sinks.reshape(DEPTH, ATTN_HEADS, 1, 1), (DEPTH, ATTN_HEADS, WINDOW, 1)).reshape(
        DEPTH, ATTN_KV_HEADS, QROWS, 1)
    row = lambda a, l: a[l:l + 1]
    saved = []
    hs = (x, *_operand_forms(x))
    for l in range(DEPTH):
        hs, s = _layer_fwd(hs, ex, l, row(lower, l), bias4, sink4[l], row(hgrn_norm_g, l), conv_w_full[l], row(ln_g, l), row(ln_b, l))
        saved.append(s)
    loss_blk, dh = _loss_head(hs[0], target)
    smalls = [None] * DEPTH
    for l in reversed(range(DEPTH)):
        dh, smalls[l] = _layer_bwd(dh, saved[l], ex, l, row(lower, l), bias4, sink4[l], row(hgrn_norm_g, l), conv_w_full[l], row(ln_g, l))
    stack = lambda k: jnp.stack([sm[k] for sm in smalls])
    d_bias = smalls[0]["bias"] + smalls[1]["bias"] + smalls[2]["bias"] + smalls[3]["bias"]
    small = dict(
        lb_param=lower_vjp(stack("lower"))[0], hgrn_norm_g=stack("gn"), attn_sinks=stack("sinks"), conv_w=stack("cw"),
        rel_bias=bias_vjp(d_bias)[0], ln_g=stack("lng"), ln_b=stack("lnb"))
    return loss_blk, dh, small


ANY = pl.BlockSpec(memory_space=pl.ANY)
DMA_SEM = pltpu.SemaphoreType.DMA


def _coords():
    return lax.axis_index("x"), lax.axis_index("y"), lax.axis_index("c")


def _other_chips(x, y):
    return [(1 - x, y), (x, 1 - y), (1 - x, 1 - y)]


def _remote(src, dst, send_sem, recv_sem, device):
    return pltpu.make_async_remote_copy(src_ref=src, dst_ref=dst, send_sem=send_sem, recv_sem=recv_sem,
                                        device_id=device, device_id_type=MESH)


def _sub(ref, axis, index, size):
    idx = [slice(None)] * len(ref.shape)
    idx[axis] = pl.ds(pl.multiple_of(index * size, size), size)
    return ref.at[tuple(idx)]


class _Plan:
    def __init__(self, inputs, out_shapes, aliases, n, make):
        self.inputs, self.out_shapes, self.aliases, self.n, self.make = tuple(inputs), tuple(out_shapes), dict(aliases), n, make


class _Xfer:
    def __init__(self, send, recv=None):
        self.send, self.recv = send, send if recv is None else recv

    def start(self):
        self.send.start()

    def wait(self):
        self.send.wait_send()
        self.recv.wait_recv()


def _merge_plans(plans):
    def make(in_refs, out_refs, send_sems, recv_sems, base):
        out, i0, o0, b0 = [], 0, 0, base
        for p in plans:
            out += p.make(in_refs[i0:i0 + len(p.inputs)], out_refs[o0:o0 + len(p.out_shapes)], send_sems, recv_sems, b0)
            i0, o0, b0 = i0 + len(p.inputs), o0 + len(p.out_shapes), b0 + p.n
        return out

    inputs, out_shapes, aliases = [], [], {}
    for p in plans:
        aliases.update({len(inputs) + k: len(out_shapes) + v for k, v in p.aliases.items()})
        inputs += p.inputs
        out_shapes += p.out_shapes
    return _Plan(inputs, out_shapes, aliases, sum(p.n for p in plans), make)


def _run_plan(plan, name):
    ni, no = len(plan.inputs), len(plan.out_shapes)

    def body(*refs):
        transfers = plan.make(refs[:ni], refs[ni:ni + no], refs[ni + no], refs[ni + no + 1], 0)
        for t in transfers:
            t.start()
        for t in transfers:
            t.wait()

    outs = pl.pallas_call(
        body, name=name, in_specs=[ANY] * ni, out_specs=[ANY] * no, out_shape=list(plan.out_shapes),
        input_output_aliases=plan.aliases, scratch_shapes=[DMA_SEM((plan.n,)), DMA_SEM((plan.n,))],
    )(*plan.inputs)
    return list(outs)


def _gather_send_plan(shards, layer, sax):
    shp = shards.shape[1:]
    hax = 3 - sax
    w, hw = shp[sax], shp[hax] // 2
    out_shape = list(shp)
    out_shape[sax] = w * N_CHIPS

    def make(in_refs, out_refs, send_sems, recv_sems, base):
        (src_ref,), (out_ref,) = in_refs, out_refs
        x, y, c = _coords()
        j = 2 * x + y
        src = src_ref.at[layer]
        own = pltpu.make_async_copy(src, _sub(out_ref, sax, j, w), send_sems.at[base])
        dst = _sub(_sub(out_ref, sax, j, w), hax, c, hw)
        return [own] + [
            _Xfer(_remote(_sub(src, hax, c, hw), dst, send_sems.at[base + 1 + k], recv_sems.at[base + 1 + k], (px, py, c)))
            for k, (px, py) in enumerate(_other_chips(x, y))]

    return _Plan([shards], [_sds(tuple(out_shape), shards.dtype)], {}, 4, make)


def _gather_pass_plan(full, sax):
    hax = 3 - sax
    w, hw = full.shape[sax] // N_CHIPS, full.shape[hax] // 2

    def make(in_refs, out_refs, send_sems, recv_sems, base):
        (out_ref,) = out_refs
        x, y, c = _coords()
        region = lambda slab, half: _sub(_sub(out_ref, sax, slab, w), hax, half, hw)
        out = []
        for k, (px, py) in enumerate(_other_chips(x, y)):
            mine, theirs = region(2 * px + py, c), region(2 * px + py, 1 - c)
            sems = send_sems.at[base + k], recv_sems.at[base + k]
            out.append(_Xfer(_remote(mine, mine, *sems, (x, y, 1 - c)), _remote(theirs, theirs, *sems, (x, y, c))))
        return out

    return _Plan([full], [_sds(full.shape, full.dtype)], {0: 0}, 3, make)


def _pair_exchange_plan(g, hax):
    hw = g.shape[hax] // 2
    out_shape = list(g.shape)
    out_shape[hax] = hw

    def make(in_refs, out_refs, send_sems, recv_sems, base):
        x, y, c = _coords()
        return [_Xfer(_remote(_sub(in_refs[0], hax, 1 - c, hw), out_refs[0], send_sems.at[base], recv_sems.at[base], (x, y, 1 - c)))]

    return _Plan([g], [_sds(tuple(out_shape), g.dtype)], {}, 1, make)


def _add_own_half(place, g, recv, hax, blk, name):
    L, ah, bh = recv.shape
    tr, tc = blk
    nr, nc = ah // tr, bh // tc

    def g_map(l, i, jc, p):
        return (l, i + p[0] * nr, jc) if hax == 1 else (l, i, jc + p[0] * nc)

    def body(p_ref, g_ref, r_ref, o_ref):
        o_ref[...] = (g_ref[...] + r_ref[...]).astype(BF16)

    same = pl.BlockSpec((None, tr, tc), lambda l, i, jc, p: (l, i, jc))
    return pl.pallas_call(
        body,
        name=name,
        grid_spec=pltpu.PrefetchScalarGridSpec(
            num_scalar_prefetch=1, grid=(L, nr, nc),
            in_specs=[pl.BlockSpec((None, tr, tc), g_map), same], out_specs=same),
        out_shape=_sds(recv.shape, BF16),
        compiler_params=_cp(("parallel", "parallel", "parallel")),
    )(place, g, recv)


def _slab_exchange_plan(p, sax):
    w = p.shape[sax] // N_CHIPS
    slab_shape = list(p.shape)
    slab_shape[sax] = w

    def make(in_refs, out_refs, send_sems, recv_sems, base):
        x, y, c = _coords()
        return [_Xfer(_remote(_sub(in_refs[0], sax, 2 * px + py, w), out_refs[0].at[k], send_sems.at[base + k],
                              recv_sems.at[base + k], (px, py, c)))
                for k, (px, py) in enumerate(_other_chips(x, y))]

    return _Plan([p], [_sds((3, *slab_shape), p.dtype)], {}, 3, make)


def _add_slabs(place, g, pair, recv, sax, blk, name):
    hax = 3 - sax
    _, L, a, b = recv.shape
    tr, tc = blk
    nr, nc = a // tr, b // tc

    def g_map(l, i, jc, p):
        return (l, p[0] * nr + i, p[1] * nc + jc) if hax == 1 else (l, p[1] * nr + i, p[0] * nc + jc)

    def pair_map(l, i, jc, p):
        return (l, i, p[1] * nc + jc) if hax == 1 else (l, p[1] * nr + i, jc)

    def out_map(l, i, jc, p):
        return (l, p[0] * nr + i, jc) if hax == 1 else (l, i, p[0] * nc + jc)

    def body(p_ref, g_ref, pair_ref, r0_ref, r1_ref, r2_ref, o_ref):
        own = g_ref[...] + pair_ref[...]
        o_ref[...] = ((own + r0_ref[...].astype(F32)) + r1_ref[...].astype(F32)) + r2_ref[...].astype(F32)

    def rk(k):
        return pl.BlockSpec((None, None, tr, tc), lambda l, i, jc, p, k=k: (k, l, i, jc))

    out_shape = [L, a, b]
    out_shape[hax] *= 2
    blk3 = (None, tr, tc)
    return pl.pallas_call(
        body,
        name=name,
        grid_spec=pltpu.PrefetchScalarGridSpec(
            num_scalar_prefetch=1, grid=(L, nr, nc),
            in_specs=[pl.BlockSpec(blk3, g_map), pl.BlockSpec(blk3, pair_map), rk(0), rk(1), rk(2)],
            out_specs=pl.BlockSpec(blk3, out_map)),
        out_shape=_sds(tuple(out_shape), F32),
        compiler_params=_cp(("parallel", "parallel", "parallel")),
    )(place, g, pair, recv, recv, recv)


def _pair_assemble_plan(r, hax):
    hw = r.shape[hax] // 2

    def make(in_refs, out_refs, send_sems, recv_sems, base):
        x, y, c = _coords()
        mine, other = _sub(out_refs[0], hax, c, hw), _sub(out_refs[0], hax, 1 - c, hw)
        sems = send_sems.at[base], recv_sems.at[base]
        return [_Xfer(_remote(mine, mine, *sems, (x, y, 1 - c)), _remote(other, other, *sems, (x, y, c)))]

    return _Plan([r], [_sds(r.shape, r.dtype)], {0: 0}, 1, make)


CLASSES = dict(w_in=(2, (128, 4224), (128, 4224)), w_proj=(2, (256, 2048), (512, 512)), w_out=(1, (256, 1024), (256, 1024)))


class _Exchanges:
    def __init__(self, place, shards):
        self.place, self.shards = place, shards
        self.weights, self.pending, self.pair, self.halves, self.reduced = {}, {}, {}, {}, {}

    FIRST = ("w_in",)

    def _carried(self, layer):
        keys = [(0, k) for k in CLASSES if k not in self.FIRST] if layer == 0 else []
        return keys + ([(layer + 1, k) for k in CLASSES] if layer + 1 < DEPTH else [])

    def _send_plan(self, keys):
        return _merge_plans([_gather_send_plan(self.shards[k], l, CLASSES[k][0]) for l, k in keys])

    def _pass_plan(self, keys, bufs):
        return _merge_plans([_gather_pass_plan(b, CLASSES[k][0]) for (_, k), b in zip(keys, bufs)])

    def _landed(self, keys, bufs):
        for (l, k), b in zip(keys, bufs):
            self.weights.setdefault(l, {})[k] = b

    def first_weights(self):
        keys = [(0, k) for k in self.FIRST]
        sent = _run_plan(self._send_plan(keys), "gather_send_0")
        self._landed(keys, _run_plan(self._pass_plan(keys, sent), "gather_pass_0"))

    def fwd_plan(self, layer):
        keys = self._carried(layer)
        return self._send_plan(keys) if keys else None

    def fwd_pass_plan(self, layer, bufs):
        return self._pass_plan(self._carried(layer), bufs)

    def fwd_landed(self, layer, bufs):
        self._landed(self._carried(layer), bufs)

    def grads_ready(self, layer, grads):
        self.pending[layer] = grads

    def pair_plan(self, layer):
        if layer + 1 not in self.pending:
            return None
        g = self.pending[layer + 1]
        return _merge_plans([_pair_exchange_plan(g[k], 3 - CLASSES[k][0]) for k in CLASSES])

    def pair_landed(self, layer, bufs):
        self.pair[layer + 1] = dict(zip(CLASSES, bufs))

    def slab_plan(self, layer):
        src = layer + 1
        if src not in self.pair:
            return None
        g, pair = self.pending[src], self.pair[src]
        sums = [_add_own_half(self.place, g[k], pair[k], 3 - CLASSES[k][0], CLASSES[k][1], f"rs_pair_add_{k}_{src}") for k in CLASSES]
        return _merge_plans([_slab_exchange_plan(p, CLASSES[k][0]) for k, p in zip(CLASSES, sums)])

    def slab_landed(self, layer, bufs):
        src = layer + 1
        g, pair = self.pending.pop(src), self.pair.pop(src)
        self.halves[src] = [_add_slabs(self.place, g[k], pair[k], r, CLASSES[k][0], CLASSES[k][2], f"rs_slab_add_{k}_{src}")
                            for k, r in zip(CLASSES, bufs)]

    def assemble_plan(self, layer):
        if layer + 1 not in self.halves:
            return None
        return _merge_plans([_pair_assemble_plan(h, 3 - CLASSES[k][0]) for k, h in zip(CLASSES, self.halves[layer + 1])])

    def assembled(self, layer, bufs):
        del self.halves[layer + 1]
        self.reduced[layer + 1] = dict(zip(CLASSES, bufs))

    def tail_plan(self):
        self.pair_landed(-1, _run_plan(self.pair_plan(-1), "rs_pair_0"))
        return self.slab_plan(-1)

    def finish(self):
        self.assembled(-1, _run_plan(self.assemble_plan(-1), "rs_assemble_0"))
        return [self.reduced[l] for l in range(DEPTH)]


N_DEV = 8


def _all_reduce_small(v, name):
    rows = v.shape[0]

    def body(v_ref, gath_ref, sum_ref, send_sems, recv_sems, local_sem):
        x, y, c = _coords()
        me, sib = (x, y, c), (x, y, 1 - c)
        chips = _other_chips(x, y)

        def slot(px, py, pc):
            return gath_ref.at[pl.ds(pl.multiple_of((4 * px + 2 * py + pc) * rows, rows), rows), :]

        def copy(k, block, to, src=None):
            return _remote(slot(*block) if src is None else src, slot(*block), send_sems.at[k], recv_sems.at[k], to)

        mine = pltpu.make_async_copy(v_ref, slot(*me), local_sem)
        mine.start()
        first = [copy(0, me, sib, src=v_ref)] + [copy(1 + k, me, (*chip, c), src=v_ref) for k, chip in enumerate(chips)]
        for cp in first:
            cp.start()
        passed = [copy(4 + k, (*chip, c), sib) for k, chip in enumerate(chips)]
        for k, chip in enumerate(chips):
            copy(1 + k, (*chip, c), me).wait_recv()
            passed[k].start()
        copy(0, sib, me).wait_recv()
        for k, chip in enumerate(chips):
            copy(4 + k, (*chip, 1 - c), me).wait_recv()
        for cp in first + passed:
            cp.wait_send()
        mine.wait()
        acc = gath_ref[0:rows, :]
        for d in range(1, N_DEV):
            acc = acc + gath_ref[d * rows:(d + 1) * rows, :]
        sum_ref[...] = acc

    vm = pl.BlockSpec(memory_space=pltpu.VMEM)
    return pl.pallas_call(
        body, name=name, in_specs=[vm], out_specs=[vm, vm],
        out_shape=[_sds((N_DEV * rows, 128), F32), _sds((rows, 128), F32)],
        scratch_shapes=[DMA_SEM((7,)), DMA_SEM((7,)), DMA_SEM(())],
    )(v)[1]


def _pad_rows(a):
    flat = a.reshape(-1).astype(F32)
    rows = -(-flat.shape[0] // (8 * 128)) * 8
    return jnp.pad(flat, (0, rows * 128 - flat.shape[0])).reshape(rows, 128)


def _sum_over_devices(parts, name):
    blocks = [_pad_rows(a) for a in parts.values()]
    total = _all_reduce_small(jnp.concatenate(blocks, axis=0), name)
    out, r0 = {}, 0
    for (key, a), blk in zip(parts.items(), blocks):
        out[key] = total[r0:r0 + blk.shape[0]].reshape(-1)[:a.size].reshape(a.shape)
        r0 += blk.shape[0]
    return out


def kernel(x, w_in, w_proj_hgrn, w_proj_attn, w_proj_conv, w_out, lb_param, hgrn_norm_g, attn_sinks, conv_w, rel_bias, ln_g, ln_b, loss_target, m_w_in, m_w_proj_hgrn, m_w_proj_attn, m_w_proj_conv, m_w_out, m_lb_param, m_hgrn_norm_g, m_attn_sinks, m_conv_w, m_rel_bias, m_ln_g, m_ln_b, v_w_in, v_w_proj_hgrn, v_w_proj_attn, v_w_proj_conv, v_w_out, v_lb_param, v_hgrn_norm_g, v_attn_sinks, v_conv_w, v_rel_bias, v_ln_g, v_ln_b):
    xi, yi, ci = _coords()
    slab = 2 * xi + yi
    place = jnp.stack([ci, slab]).astype(jnp.int32)
    conv_cols = conv_w.shape[-1]

    w_in_b = w_in.astype(BF16)[:, None]
    w_proj_b = jnp.stack([w_proj_hgrn, w_proj_attn, w_proj_conv], axis=1).astype(BF16)
    w_out_b = w_out.astype(BF16)[:, None]
    ex = _Exchanges(place, dict(w_in=w_in_b, w_proj=w_proj_b, w_out=w_out_b))
    ex.first_weights()
    conv_spread = lax.dynamic_update_slice(jnp.zeros((DEPTH, CONV_K, CONV_WIDTH), F32), conv_w, (0, 0, slab * conv_cols))
    conv_full = 0.5 * _sum_over_devices({"conv_w": conv_spread}, "gather_conv_w")["conv_w"]

    loss_blk, dx, small = _local_step(x[0], loss_target[0], ex, lb_param, hgrn_norm_g, attn_sinks, conv_full, rel_bias, ln_g, ln_b)

    reduced = ex.finish()
    g_w_in = jnp.stack([r["w_in"][0] for r in reduced])
    g_w_proj = jnp.stack([r["w_proj"] for r in reduced])
    g_w_out = jnp.stack([r["w_out"][0] for r in reduced])
    small = dict(small, loss=loss_blk[0:1, 0:1])
    small = _sum_over_devices(small, "sum_small")
    loss = small["loss"][0, 0]
    g_conv = lax.dynamic_slice(small["conv_w"], (0, 0, slab * conv_cols), (DEPTH, CONV_K, conv_cols))

    grads = [g_w_in, g_w_proj[:, 0], g_w_proj[:, 1], g_w_proj[:, 2], g_w_out, small["lb_param"], small["hgrn_norm_g"],
             small["attn_sinks"], g_conv, small["rel_bias"], small["ln_g"], small["ln_b"]]
    names = ["w_in", "w_proj_hgrn", "w_proj_attn", "w_proj_conv", "w_out", "lb_param", "hgrn_norm_g", "attn_sinks",
             "conv_w", "rel_bias", "ln_g", "ln_b"]
    ws = [w_in, w_proj_hgrn, w_proj_attn, w_proj_conv, w_out, lb_param, hgrn_norm_g, attn_sinks, conv_w, rel_bias, ln_g, ln_b]
    ms = [m_w_in, m_w_proj_hgrn, m_w_proj_attn, m_w_proj_conv, m_w_out, m_lb_param, m_hgrn_norm_g, m_attn_sinks, m_conv_w,
          m_rel_bias, m_ln_g, m_ln_b]
    vs = [v_w_in, v_w_proj_hgrn, v_w_proj_attn, v_w_proj_conv, v_w_out, v_lb_param, v_hgrn_norm_g, v_attn_sinks, v_conv_w,
          v_rel_bias, v_ln_g, v_ln_b]
    upd = [_adamw(w, g, m, v, "adamw_" + n) for n, w, g, m, v in zip(names, ws, grads, ms, vs)]
    deltas, new_ms, new_vs = zip(*upd)
    return (loss, dx[None], *grads, *deltas, *new_ms, *new_vs)
```

```python
import functools
import math

import jax
import jax.numpy as jnp
from jax import lax
from jax.experimental import pallas as pl
from jax.experimental.pallas import tpu as pltpu

F32 = jnp.float32
BF16 = jnp.bfloat16
MXU_DTYPE = BF16

D_MODEL = 2048
DEPTH = 4
HGRN_WIDTH = 1024
HGRN_HEAD_DIM = 128
HGRN_HEADS = 8
HGRN_CHUNK = 64
ATTN_HEAD_DIM = 64
ATTN_HEADS = 16
ATTN_KV_HEADS = 4
ATTN_GROUP = ATTN_HEADS // ATTN_KV_HEADS
ATTN_WIDTH = 1024
KV_WIDTH = 256
WINDOW = 128
CONV_WIDTH = 1024
CONV_K = 3
N_BUCKETS = 32
MAX_DISTANCE = 128
ALPHA = (2.0 * DEPTH) ** 0.25
LN_EPS = 1e-5
RMS_EPS = 1e-6
N_IN = 16896
OFF_AQ, OFF_AF, OFF_AI, OFF_AG = 0, 1024, 2048, 3072
OFF_BQ, OFF_BK, OFF_BV, OFF_BG = 4096, 5120, 5376, 5632
OFF_CB, OFF_CC, OFF_CX, OFF_CG = 6656, 7680, 8704, 9728
OFF_MA, OFF_MB, OFF_MC = 10752, 12800, 14848

ADAM_LR = 0.001
ADAM_B1 = 0.9
ADAM_B2 = 0.999
ADAM_EPS = 1e-08
ADAM_WD = 0.01
ADAM_STEP = 10

N_CHIPS = 4
VMEM_LIMIT_BYTES = 48 * 1024 * 1024
EXP_CLAMP = 80.0
MASK_VALUE = -1e30
MESH = pl.DeviceIdType.MESH


def _cp(sem=None):
    return pltpu.CompilerParams(dimension_semantics=sem, vmem_limit_bytes=VMEM_LIMIT_BYTES)


def _tile(dim, pref):
    return pref if dim % pref == 0 else dim


def _sds(shape, dtype):
    return jax.ShapeDtypeStruct(shape, dtype)


def _call(body, *, name, grid, in_specs, out_specs, out_shape, args, scratch_shapes=(), sem=None, plan=None):
    in_specs, out_specs, out_shape = list(in_specs), list(out_specs), list(out_shape)
    if plan is None:
        outs = pl.pallas_call(body, name=name, grid=grid, in_specs=in_specs, out_specs=out_specs, out_shape=out_shape,
                              scratch_shapes=list(scratch_shapes), compiler_params=_cp(sem))(*args)
        return list(outs), []
    ni, no, ns = len(in_specs), len(out_specs), len(scratch_shapes)
    pi, po = len(plan.inputs), len(plan.out_shapes)

    def carrier(*refs):
        c_in, p_in = refs[:ni], refs[ni:ni + pi]
        c_out, p_out = refs[ni + pi:ni + pi + no], refs[ni + pi + no:ni + pi + no + po]
        c_scr = refs[ni + pi + no + po:ni + pi + no + po + ns]
        send_sems, recv_sems = refs[-2], refs[-1]
        ids = [pl.program_id(a) for a in range(len(grid))]
        first = functools.reduce(jnp.logical_and, [i == 0 for i in ids])
        last = functools.reduce(jnp.logical_and, [i == n - 1 for i, n in zip(ids, grid)])

        @pl.when(first)
        def _():
            for cp in plan.make(p_in, p_out, send_sems, recv_sems, 0):
                cp.start()

        body(*c_in, *c_out, *c_scr)

        @pl.when(last)
        def _():
            for cp in plan.make(p_in, p_out, send_sems, recv_sems, 0):
                cp.wait()

    any_spec = pl.BlockSpec(memory_space=pl.ANY)
    outs = pl.pallas_call(
        carrier, name=name, grid=grid,
        in_specs=in_specs + [any_spec] * pi, out_specs=out_specs + [any_spec] * po,
        out_shape=out_shape + list(plan.out_shapes),
        scratch_shapes=list(scratch_shapes) + [pltpu.SemaphoreType.DMA((plan.n,)), pltpu.SemaphoreType.DMA((plan.n,))],
        input_output_aliases={ni + k: no + v for k, v in plan.aliases.items()},
        compiler_params=_cp(tuple("arbitrary" for _ in grid)),
    )(*args, *plan.inputs)
    return list(outs[:no]), list(outs[no:])


_DIMS = {
    "nn": (((1,), (0,)), ((), ())),
    "nt": (((1,), (1,)), ((), ())),
    "tn": (((0,), (0,)), ((), ())),
}


def _dot_raw(a, b, mode):
    return lax.dot_general(a.astype(MXU_DTYPE), b.astype(MXU_DTYPE), _DIMS[mode], preferred_element_type=F32)


@functools.partial(jax.custom_vjp, nondiff_argnums=(2,))
def _dot(a, b, mode):
    return _dot_raw(a, b, mode)


def _dot_fwd(a, b, mode):
    return _dot_raw(a, b, mode), (a, b)


def _dot_bwd(mode, res, g):
    a, b = res
    if mode == "nn":
        return _dot_raw(g, b, "nt"), _dot_raw(a, g, "tn")
    if mode == "nt":
        return _dot_raw(g, b, "nn"), _dot_raw(g, a, "tn")
    return _dot_raw(b, g, "nt"), _dot_raw(a, g, "nn")


_dot.defvjp(_dot_fwd, _dot_bwd)


def _matmul(a, b, *, mode, tm, tn, tk, name, a_idx=None, b_idx=None, out_dtype=F32, add=None, add_scale=1.0, plan=None):
    a2, b2 = a.shape[-2:], b.shape[-2:]
    if mode == "nn":
        (M, K), (K2, N) = a2, b2
    elif mode == "nt":
        (M, K), (N, K2) = a2, b2
    else:
        (K, M), (K2, N) = a2, b2
    assert K == K2, (a.shape, b.shape, mode)
    tm, tn, tk = _tile(M, tm), _tile(N, tn), _tile(K, tk)
    nk = K // tk

    a_blk = (tk, tm) if mode == "tn" else (tm, tk)
    b_blk = (tn, tk) if mode == "nt" else (tk, tn)

    def a_map(i, j, k):
        ij = (k, i) if mode == "tn" else (i, k)
        return ij if a_idx is None else (a_idx,) + ij

    def b_map(i, j, k):
        ij = (j, k) if mode == "nt" else (k, j)
        return ij if b_idx is None else (b_idx,) + ij

    in_specs = [
        pl.BlockSpec(a_blk if a_idx is None else (None,) + a_blk, a_map),
        pl.BlockSpec(b_blk if b_idx is None else (None,) + b_blk, b_map),
    ]
    args = [a, b]
    if add is not None:
        in_specs.append(pl.BlockSpec((tm, tn), lambda i, j, k: (i, j)))
        args.append(add)
    n_in = len(args)

    def body(*refs):
        a_ref, b_ref = refs[0], refs[1]
        o_ref = refs[n_in]
        p = _dot_raw(a_ref[...], b_ref[...], mode)

        def finish(val):
            if add is not None:
                val = val + add_scale * refs[2][...]
            o_ref[...] = val.astype(out_dtype)

        if nk == 1:
            finish(p)
        else:
            acc_ref = refs[n_in + 1]
            k = pl.program_id(2)

            @pl.when(k == 0)
            def _():
                acc_ref[...] = p

            @pl.when(k > 0)
            def _():
                acc_ref[...] += p

            @pl.when(k == nk - 1)
            def _():
                finish(acc_ref[...])

    (out,), plan_outs = _call(
        body,
        name=name,
        grid=(M // tm, N // tn, nk),
        in_specs=in_specs,
        out_specs=[pl.BlockSpec((tm, tn), lambda i, j, k: (i, j))],
        out_shape=[_sds((M, N), out_dtype)],
        scratch_shapes=[pltpu.VMEM((tm, tn), F32)] if nk > 1 else [],
        sem=("parallel", "parallel", "arbitrary"),
        args=args,
        plan=plan,
    )
    return out if plan is None else (out, plan_outs)


def _input_projection(xb, w_in_l, plan=None):
    T = xb.shape[0]
    tm, tn = _tile(T, 1024), 768
    n_main = OFF_MA // tn
    assert OFF_MA % tn == 0 and N_IN % tn == 0

    def body(a_ref, b_ref, u_ref, ug_ref):
        p = _dot_raw(a_ref[...], b_ref[...], "nn")
        j = pl.program_id(1)

        @pl.when(j < n_main)
        def _():
            u_ref[...] = p

        @pl.when(j >= n_main)
        def _():
            ug_ref[...] = p.astype(BF16)

    outs, landed = _call(
        body,
        name="mm_u",
        grid=(T // tm, N_IN // tn),
        in_specs=[pl.BlockSpec((tm, D_MODEL), lambda i, j: (i, 0)), pl.BlockSpec((D_MODEL, tn), lambda i, j: (0, j))],
        out_specs=[pl.BlockSpec((tm, tn), lambda i, j: (i, jnp.minimum(j, n_main - 1))),
                   pl.BlockSpec((tm, tn), lambda i, j: (i, jnp.maximum(j - n_main, 0)))],
        out_shape=[_sds((T, OFF_MA), F32), _sds((T, N_IN - OFF_MA), BF16)],
        sem=("parallel", "arbitrary"),
        args=(xb, w_in_l),
        plan=plan,
    )
    return outs[0], outs[1], landed


def _scan_rows(x, reverse):
    n = x.shape[0]
    row = lax.broadcasted_iota(jnp.int32, x.shape, 0)
    s = 1
    while s < n:
        if reverse:
            x = x + jnp.where(row < n - s, pltpu.roll(x, n - s, 0), 0.0)
        else:
            x = x + jnp.where(row >= s, pltpu.roll(x, s, 0), 0.0)
        s *= 2
    return x


@jax.custom_vjp
def _cumsum_rows(x):
    return _scan_rows(x, False)


_cumsum_rows.defvjp(lambda x: (_scan_rows(x, False), None), lambda _, g: (_scan_rows(g, True),))


def _hgrn_chunk(state_t, qraw, fraw, v, lb):
    c = HGRN_CHUNK
    q = qraw * jax.nn.sigmoid(qraw) * (HGRN_HEAD_DIM ** -0.5)
    f = lb + (1.0 - lb) * jax.nn.sigmoid(fraw)
    k = 1.0 - f
    g = jnp.log(f)
    b = _cumsum_rows(g)
    row = lax.broadcasted_iota(jnp.int32, (c, HGRN_HEAD_DIM), 0)
    b_end = jnp.sum(g, axis=0, keepdims=True)
    b_mid = jnp.sum(jnp.where(row < c // 2, g, 0.0), axis=0, keepdims=True)
    inter = _dot(q * jnp.exp(b), state_t, "nt")
    qt = q * jnp.exp(jnp.minimum(b - b_mid, EXP_CLAMP))
    kt = k * jnp.exp(jnp.minimum(b_mid - b, EXP_CLAMP))
    s = _dot(qt, kt, "nt")
    ti = lax.broadcasted_iota(jnp.int32, (c, c), 0)
    si = lax.broadcasted_iota(jnp.int32, (c, c), 1)
    s = jnp.where(si <= ti, s, 0.0)
    intra = _dot(s, v, "nn")
    k_end = k * jnp.exp(b_end - b)
    new_state_t = state_t * jnp.exp(b_end) + _dot(v, k_end, "tn")
    return new_state_t, inter + intra


def _hgrn_specs(T):
    rows = _tile(T, 1024)
    return rows, T // rows, rows // HGRN_CHUNK


def _hgrn_fwd(u, lower_l, plan=None):
    T = u.shape[0]
    rows, nblk, ncr = _hgrn_specs(T)
    hb = HGRN_WIDTH // HGRN_HEAD_DIM

    def body(q_ref, f_ref, i_ref, lb_ref, o_ref, st_ref, state):
        @pl.when(pl.program_id(1) == 0)
        def _():
            state[...] = jnp.zeros_like(state)

        lb = lb_ref[...]
        for c in range(ncr):
            rs = pl.ds(c * HGRN_CHUNK, HGRN_CHUNK)
            st = state[...]
            st_ref[c] = st
            new, out = _hgrn_chunk(st, q_ref[rs, :], f_ref[rs, :], i_ref[rs, :], lb)
            state[...] = new
            o_ref[rs, :] = out

    blk = (rows, HGRN_HEAD_DIM)
    return _call(
        body,
        name="hgrn_fwd",
        grid=(HGRN_HEADS, nblk),
        in_specs=[
            pl.BlockSpec(blk, lambda h, r: (r, OFF_AQ // 128 + h)),
            pl.BlockSpec(blk, lambda h, r: (r, OFF_AF // 128 + h)),
            pl.BlockSpec(blk, lambda h, r: (r, OFF_AI // 128 + h)),
            pl.BlockSpec((1, HGRN_HEAD_DIM), lambda h, r: (0, h)),
        ],
        out_specs=[
            pl.BlockSpec(blk, lambda h, r: (r, h)),
            pl.BlockSpec((ncr, None, HGRN_HEAD_DIM, HGRN_HEAD_DIM), lambda h, r: (r, h, 0, 0)),
        ],
        out_shape=[_sds((T, HGRN_WIDTH), F32), _sds((T // HGRN_CHUNK, hb, HGRN_HEAD_DIM, HGRN_HEAD_DIM), F32)],
        scratch_shapes=[pltpu.VMEM((HGRN_HEAD_DIM, HGRN_HEAD_DIM), F32)],
        sem=("parallel", "arbitrary"),
        args=(u, u, u, lower_l),
        plan=plan,
    )


def _hgrn_bwd(u, lower_l, states, do_raw):
    T = u.shape[0]
    rows, nblk, ncr = _hgrn_specs(T)

    def body(q_ref, f_ref, i_ref, lb_ref, st_ref, do_ref, dq_ref, df_ref, di_ref, dlb_ref, dstate):
        @pl.when(pl.program_id(1) == 0)
        def _():
            dstate[...] = jnp.zeros_like(dstate)
            dlb_ref[...] = jnp.zeros_like(dlb_ref)

        lb = lb_ref[...]
        for c in reversed(range(ncr)):
            rs = pl.ds(c * HGRN_CHUNK, HGRN_CHUNK)
            _, vjp = jax.vjp(_hgrn_chunk, st_ref[c], q_ref[rs, :], f_ref[rs, :], i_ref[rs, :], lb)
            dst, dq, df, dv, dlb = vjp((dstate[...], do_ref[rs, :]))
            dstate[...] = dst
            dq_ref[rs, :] = dq.astype(BF16)
            df_ref[rs, :] = df.astype(BF16)
            di_ref[rs, :] = dv.astype(BF16)
            dlb_ref[...] += dlb

    blk = (rows, HGRN_HEAD_DIM)
    last = nblk - 1
    out_blk = pl.BlockSpec(blk, lambda h, r: (last - r, h))
    return pl.pallas_call(
        body,
        name="hgrn_bwd",
        grid=(HGRN_HEADS, nblk),
        in_specs=[
            pl.BlockSpec(blk, lambda h, r: (last - r, OFF_AQ // 128 + h)),
            pl.BlockSpec(blk, lambda h, r: (last - r, OFF_AF // 128 + h)),
            pl.BlockSpec(blk, lambda h, r: (last - r, OFF_AI // 128 + h)),
            pl.BlockSpec((1, HGRN_HEAD_DIM), lambda h, r: (0, h)),
            pl.BlockSpec((ncr, None, HGRN_HEAD_DIM, HGRN_HEAD_DIM), lambda h, r: (last - r, h, 0, 0)),
            out_blk,
        ],
        out_specs=[out_blk, out_blk, out_blk, pl.BlockSpec((1, HGRN_HEAD_DIM), lambda h, r: (0, h))],
        out_shape=[_sds((T, HGRN_WIDTH), BF16)] * 3 + [_sds((1, HGRN_WIDTH), F32)],
        scratch_shapes=[pltpu.VMEM((HGRN_HEAD_DIM, HGRN_HEAD_DIM), F32)],
        compiler_params=_cp(("parallel", "arbitrary")),
    )(u, u, u, lower_l, states, do_raw)


QROWS = ATTN_GROUP * WINDOW


def _attn_block(q, kp, kc, vp, vc, bp, bc, sink, first):
    qs = q * (ATTN_HEAD_DIM ** -0.5)
    sc = _dot(qs, kc, "nt") + bc
    m = jnp.maximum(jnp.max(sc, axis=-1, keepdims=True), sink)
    if not first:
        sp = _dot(qs, kp, "nt") + bp
        m = jnp.maximum(m, jnp.max(sp, axis=-1, keepdims=True))
    m = lax.stop_gradient(m)
    pc = jnp.exp(sc - m)
    den = jnp.sum(pc, axis=-1, keepdims=True) + jnp.exp(sink - m)
    o = _dot(pc, vc, "nn")
    if not first:
        pp = jnp.exp(sp - m)
        den = den + jnp.sum(pp, axis=-1, keepdims=True)
        o = o + _dot(pp, vp, "nn")
    return o * (1.0 / den)


def _first_or_later(n, run):
    @pl.when(n == 0)
    def _():
        run(True)

    @pl.when(n > 0)
    def _():
        run(False)


def _attn_in_specs():
    q_spec = pl.BlockSpec((WINDOW, ATTN_WIDTH), lambda n: (n, OFF_BQ // ATTN_WIDTH))
    k_cur = pl.BlockSpec((WINDOW, KV_WIDTH), lambda n: (n, OFF_BK // KV_WIDTH))
    k_prev = pl.BlockSpec((WINDOW, KV_WIDTH), lambda n: (jnp.maximum(n - 1, 0), OFF_BK // KV_WIDTH))
    v_cur = pl.BlockSpec((WINDOW, KV_WIDTH), lambda n: (n, OFF_BV // KV_WIDTH))
    v_prev = pl.BlockSpec((WINDOW, KV_WIDTH), lambda n: (jnp.maximum(n - 1, 0), OFF_BV // KV_WIDTH))
    bias = pl.BlockSpec((ATTN_KV_HEADS, QROWS, 2 * WINDOW), lambda n: (0, 0, 0))
    sink = pl.BlockSpec((ATTN_KV_HEADS, QROWS, 1), lambda n: (0, 0, 0))
    return [q_spec, k_prev, k_cur, v_prev, v_cur, bias, sink]


def _head_cols(a):
    return slice(a * ATTN_HEAD_DIM, (a + 1) * ATTN_HEAD_DIM)


def _group_rows(ref, h):
    return jnp.concatenate([ref[:, _head_cols(ATTN_GROUP * h + g)] for g in range(ATTN_GROUP)], axis=0)


def _attn_fwd(u, bias4, sink4):
    T = u.shape[0]

    def body(q_ref, kp_ref, kc_ref, vp_ref, vc_ref, b_ref, s_ref, o_ref):
        def run(first):
            for h in range(ATTN_KV_HEADS):
                hs = _head_cols(h)
                o = _attn_block(_group_rows(q_ref, h), kp_ref[:, hs], kc_ref[:, hs], vp_ref[:, hs], vc_ref[:, hs],
                                b_ref[h, :, :WINDOW], b_ref[h, :, WINDOW:], s_ref[h], first)
                for g in range(ATTN_GROUP):
                    o_ref[:, _head_cols(ATTN_GROUP * h + g)] = o[g * WINDOW:(g + 1) * WINDOW]

        _first_or_later(pl.program_id(0), run)

    return pl.pallas_call(
        body,
        name="attn_fwd",
        grid=(T // WINDOW,),
        in_specs=_attn_in_specs(),
        out_specs=pl.BlockSpec((WINDOW, ATTN_WIDTH), lambda n: (n, 0)),
        out_shape=_sds((T, ATTN_WIDTH), F32),
        compiler_params=_cp(("parallel",)),
    )(u, u, u, u, u, bias4, sink4)


def _attn_bwd(u, bias4, sink4, do, plan=None):
    T = u.shape[0]
    act = pl.BlockSpec((WINDOW, ATTN_WIDTH), lambda n: (n, 0))
    kv = pl.BlockSpec((WINDOW, KV_WIDTH), lambda n: (n, 0))
    in_specs = _attn_in_specs()
    bias, sink = in_specs[5], in_specs[6]

    def body(q_ref, kp_ref, kc_ref, vp_ref, vc_ref, b_ref, s_ref, do_ref,
             dq_ref, dkc_ref, dkp_ref, dvc_ref, dvp_ref, db_ref, ds_ref):
        n = pl.program_id(0)

        @pl.when(n == 0)
        def _():
            db_ref[...] = jnp.zeros_like(db_ref)
            ds_ref[...] = jnp.zeros_like(ds_ref)

        def run(first):
            dqs, dkps, dkcs, dvps, dvcs = [], [], [], [], []
            for h in range(ATTN_KV_HEADS):
                hs = _head_cols(h)
                _, vjp = jax.vjp(
                    functools.partial(_attn_block, first=first),
                    _group_rows(q_ref, h), kp_ref[:, hs], kc_ref[:, hs], vp_ref[:, hs], vc_ref[:, hs],
                    b_ref[h, :, :WINDOW], b_ref[h, :, WINDOW:], s_ref[h])
                dq, dkp, dkc, dvp, dvc, dbp, dbc, dsink = vjp(_group_rows(do_ref, h))
                dqs += [dq[g * WINDOW:(g + 1) * WINDOW] for g in range(ATTN_GROUP)]
                dkps.append(dkp)
                dkcs.append(dkc)
                dvps.append(dvp)
                dvcs.append(dvc)
                if not first:
                    db_ref[h, :, :WINDOW] += dbp
                db_ref[h, :, WINDOW:] += dbc
                ds_ref[h] += dsink
            dq_ref[...] = jnp.concatenate(dqs, axis=1).astype(BF16)
            dkc_ref[...] = jnp.concatenate(dkcs, axis=1)
            dkp_ref[...] = jnp.concatenate(dkps, axis=1)
            dvc_ref[...] = jnp.concatenate(dvcs, axis=1)
            dvp_ref[...] = jnp.concatenate(dvps, axis=1)

        _first_or_later(n, run)

    kv_sds = _sds((T, KV_WIDTH), F32)
    return _call(
        body,
        name="attn_bwd",
        grid=(T // WINDOW,),
        in_specs=in_specs + [act],
        out_specs=[act, kv, kv, kv, kv, bias, sink],
        out_shape=[_sds((T, ATTN_WIDTH), BF16), kv_sds, kv_sds, kv_sds, kv_sds,
                   _sds((ATTN_KV_HEADS, QROWS, 2 * WINDOW), F32), _sds((ATTN_KV_HEADS, QROWS, 1), F32)],
        sem=("arbitrary",),
        args=(u, u, u, u, u, bias4, sink4, do),
        plan=plan,
    )


def _with_next_block_part(cur, nxt):
    pad = jnp.zeros_like(nxt[:WINDOW])
    return (cur + jnp.concatenate([nxt[WINDOW:], pad], axis=0)).astype(BF16)


MIX_COLS = 512


def _silu(x):
    return x * jax.nn.sigmoid(x)


def _silu_grad(x):
    s = jax.nn.sigmoid(x)
    return s * (1.0 + x * (1.0 - s))


def _shift_rows_down(h, first, second):
    n = h.shape[0]
    row = lax.broadcasted_iota(jnp.int32, h.shape, 0)
    s1 = jnp.where(row == 0, first, pltpu.roll(h, 1, 0))
    s2 = jnp.where(row == 0, second, jnp.where(row == 1, first, pltpu.roll(h, 2, 0)))
    del n
    return s1, s2


def _shift_rows_up(h, first, second):
    n = h.shape[0]
    row = lax.broadcasted_iota(jnp.int32, h.shape, 0)
    s1 = jnp.where(row == n - 1, first, pltpu.roll(h, n - 1, 0))
    s2 = jnp.where(row == n - 1, second, jnp.where(row == n - 2, first, pltpu.roll(h, n - 2, 0)))
    return s1, s2


def _mix_rows(T):
    return _tile(T, 512)


def _mix_fwd(u, o_raw, o_b, gn_l, cw_l):
    T = u.shape[0]
    tr = _mix_rows(T)
    nrow = T // tr
    hr = tr // 8

    def ucol(off):
        return pl.BlockSpec((tr, MIX_COLS), lambda i, j, off=off: (i, off // MIX_COLS + j))

    def uprev(off):
        return pl.BlockSpec((8, MIX_COLS), lambda i, j, off=off: (jnp.maximum(i * hr - 1, 0), off // MIX_COLS + j))

    act = pl.BlockSpec((tr, MIX_COLS), lambda i, j: (i, j))
    par = lambda rows: pl.BlockSpec((rows, MIX_COLS), lambda i, j: (0, j))

    def body(oraw_ref, ag_ref, ob_ref, bg_ref, cb_ref, cc_ref, cx_ref, cg_ref, ccp_ref, cxp_ref, gn_ref, cw_ref,
             ha_ref, hb_ref, hc_ref, hat_ref, hbt_ref, hct_ref):
        ag = _silu(ag_ref[...])
        for h in range(MIX_COLS // HGRN_HEAD_DIM):
            cs = slice(h * HGRN_HEAD_DIM, (h + 1) * HGRN_HEAD_DIM)
            o = oraw_ref[:, cs]
            nrm = o * lax.rsqrt(jnp.mean(o * o, axis=-1, keepdims=True) + RMS_EPS)
            ha = nrm * gn_ref[:, cs] * ag[:, cs]
            ha_ref[:, cs] = ha.astype(BF16)
            hat_ref[cs, :] = ha.T.astype(BF16)
        hb = ob_ref[...] * _silu(bg_ref[...])
        hb_ref[...] = hb.astype(BF16)
        hbt_ref[...] = hb.T.astype(BF16)
        keep = (pl.program_id(0) > 0).astype(F32)
        hcur = cc_ref[...] * cx_ref[...]
        p1 = ccp_ref[7:8, :] * cxp_ref[7:8, :] * keep
        p2 = ccp_ref[6:7, :] * cxp_ref[6:7, :] * keep
        s1, s2 = _shift_rows_down(hcur, p1, p2)
        y = cw_ref[0:1, :] * s2 + cw_ref[1:2, :] * s1 + cw_ref[2:3, :] * hcur
        hc = cb_ref[...] * y * _silu(cg_ref[...])
        hc_ref[...] = hc.astype(BF16)
        hct_ref[...] = hc.T.astype(BF16)

    out = _sds((T, HGRN_WIDTH), BF16)
    out_t = _sds((HGRN_WIDTH, T), BF16)
    act_t = pl.BlockSpec((MIX_COLS, tr), lambda i, j: (j, i))
    return pl.pallas_call(
        body,
        name="mix_fwd",
        grid=(nrow, HGRN_WIDTH // MIX_COLS),
        in_specs=[act, ucol(OFF_AG), act, ucol(OFF_BG), ucol(OFF_CB), ucol(OFF_CC), ucol(OFF_CX), ucol(OFF_CG),
                  uprev(OFF_CC), uprev(OFF_CX), par(1), par(CONV_K)],
        out_specs=[act, act, act, act_t, act_t, act_t],
        out_shape=[out, out, out, out_t, out_t, out_t],
        compiler_params=_cp(("parallel", "parallel")),
    )(o_raw, u, o_b, u, u, u, u, u, u, u, gn_l, cw_l)


def _mix_bwd(u, o_raw, o_b, gn_l, cw_l, dha, dhb, dhc):
    T = u.shape[0]
    tr = _mix_rows(T)
    nrow = T // tr
    hr = tr // 8
    last_halo = T // 8 - 1

    def ucol(off):
        return pl.BlockSpec((tr, MIX_COLS), lambda j, i, off=off: (i, off // MIX_COLS + j))

    def uprev(off):
        return pl.BlockSpec((8, MIX_COLS), lambda j, i, off=off: (jnp.maximum(i * hr - 1, 0), off // MIX_COLS + j))

    def unext(off):
        return pl.BlockSpec((8, MIX_COLS), lambda j, i, off=off: (jnp.minimum((i + 1) * hr, last_halo), off // MIX_COLS + j))

    act = pl.BlockSpec((tr, MIX_COLS), lambda j, i: (i, j))
    act_next = pl.BlockSpec((8, MIX_COLS), lambda j, i: (jnp.minimum((i + 1) * hr, last_halo), j))
    par = lambda rows: pl.BlockSpec((rows, MIX_COLS), lambda j, i: (0, j))

    def body(oraw_ref, ag_ref, ob_ref, bg_ref, cb_ref, cc_ref, cx_ref, cg_ref, ccp_ref, cxp_ref,
             cbn_ref, cgn_ref, dhcn_ref, gn_ref, cw_ref, dha_ref, dhb_ref, dhc_ref,
             doraw_ref, dob_ref, dag_ref, dbg_ref, dcb_ref, dcc_ref, dcx_ref, dcg_ref, dgn_ref, dcw_ref):
        i = pl.program_id(1)

        @pl.when(i == 0)
        def _():
            dgn_ref[...] = jnp.zeros_like(dgn_ref)
            dcw_ref[...] = jnp.zeros_like(dcw_ref)

        ag = ag_ref[...]
        sag = _silu(ag)
        dha = dha_ref[...]
        for h in range(MIX_COLS // HGRN_HEAD_DIM):
            cs = slice(h * HGRN_HEAD_DIM, (h + 1) * HGRN_HEAD_DIM)
            o = oraw_ref[:, cs]
            rs = lax.rsqrt(jnp.mean(o * o, axis=-1, keepdims=True) + RMS_EPS)
            nrm = o * rs
            gn = gn_ref[:, cs]
            d = dha[:, cs]
            dag_ref[:, cs] = (d * nrm * gn * _silu_grad(ag[:, cs])).astype(BF16)
            dgn_ref[:, cs] += jnp.sum(d * sag[:, cs] * nrm, axis=0, keepdims=True)
            dn = d * sag[:, cs] * gn
            doraw_ref[:, cs] = rs * (dn - nrm * jnp.mean(dn * nrm, axis=-1, keepdims=True))
        bg = bg_ref[...]
        dhb = dhb_ref[...]
        dob_ref[...] = dhb * _silu(bg)
        dbg_ref[...] = (dhb * ob_ref[...] * _silu_grad(bg)).astype(BF16)
        keep_prev = (i > 0).astype(F32)
        keep_next = (i < nrow - 1).astype(F32)
        cc, cx, cb, cg = cc_ref[...], cx_ref[...], cb_ref[...], cg_ref[...]
        hcur = cc * cx
        p1 = ccp_ref[7:8, :] * cxp_ref[7:8, :] * keep_prev
        p2 = ccp_ref[6:7, :] * cxp_ref[6:7, :] * keep_prev
        s1, s2 = _shift_rows_down(hcur, p1, p2)
        w0, w1, w2 = cw_ref[0:1, :], cw_ref[1:2, :], cw_ref[2:3, :]
        y = w0 * s2 + w1 * s1 + w2 * hcur
        dhc = dhc_ref[...]
        scg = _silu(cg)
        doc = dhc * scg
        dcg_ref[...] = (dhc * cb * y * _silu_grad(cg)).astype(BF16)
        dcb_ref[...] = (doc * y).astype(BF16)
        dy = doc * cb
        n1 = dhcn_ref[0:1, :] * _silu(cgn_ref[0:1, :]) * cbn_ref[0:1, :] * keep_next
        n2 = dhcn_ref[1:2, :] * _silu(cgn_ref[1:2, :]) * cbn_ref[1:2, :] * keep_next
        u1, u2 = _shift_rows_up(dy, n1, n2)
        dh = w2 * dy + w1 * u1 + w0 * u2
        dcc_ref[...] = (dh * cx).astype(BF16)
        dcx_ref[...] = (dh * cc).astype(BF16)
        dcw_ref[0:1, :] += jnp.sum(dy * s2, axis=0, keepdims=True)
        dcw_ref[1:2, :] += jnp.sum(dy * s1, axis=0, keepdims=True)
        dcw_ref[2:3, :] += jnp.sum(dy * hcur, axis=0, keepdims=True)

    f32o, bf = _sds((T, HGRN_WIDTH), F32), _sds((T, HGRN_WIDTH), BF16)
    return pl.pallas_call(
        body,
        name="mix_bwd",
        grid=(HGRN_WIDTH // MIX_COLS, nrow),
        in_specs=[act, ucol(OFF_AG), act, ucol(OFF_BG), ucol(OFF_CB), ucol(OFF_CC), ucol(OFF_CX), ucol(OFF_CG),
                  uprev(OFF_CC), uprev(OFF_CX), unext(OFF_CB), unext(OFF_CG), act_next, par(1), par(CONV_K),
                  act, act, act],
        out_specs=[act, act, act, act, act, act, act, act, par(1), par(8)],
        out_shape=[f32o, f32o, bf, bf, bf, bf, bf, bf, _sds((1, HGRN_WIDTH), F32), _sds((8, HGRN_WIDTH), F32)],
        compiler_params=_cp(("parallel", "arbitrary")),
    )(o_raw, u, o_b, u, u, u, u, u, u, u, u, u, dhc, gn_l, cw_l, dha, dhb, dhc)


def _merge_specs(T):
    tr = _tile(T, 1024)

    def gate(off):
        return pl.BlockSpec((tr, MIX_COLS), lambda i, j, off=off: (i, (off - OFF_MA) // MIX_COLS + j))

    act = pl.BlockSpec((tr, MIX_COLS), lambda i, j: (i, j))
    return tr, gate, act


def _gate(m_ref):
    return jax.nn.sigmoid(m_ref[...].astype(F32))


def _merge_fwd(u, ya, yb, yc):
    T = u.shape[0]
    tr, ucol, act = _merge_specs(T)

    def body(ma_ref, mb_ref, mc_ref, ya_ref, yb_ref, yc_ref, o_ref, ot_ref):
        merged = _gate(ma_ref) * ya_ref[...] + _gate(mb_ref) * yb_ref[...] + _gate(mc_ref) * yc_ref[...]
        o_ref[...] = merged.astype(BF16)
        ot_ref[...] = merged.T.astype(BF16)

    return pl.pallas_call(
        body,
        name="merge_fwd",
        grid=(T // tr, D_MODEL // MIX_COLS),
        in_specs=[ucol(OFF_MA), ucol(OFF_MB), ucol(OFF_MC), act, act, act],
        out_specs=[act, pl.BlockSpec((MIX_COLS, tr), lambda i, j: (j, i))],
        out_shape=[_sds((T, D_MODEL), BF16), _sds((D_MODEL, T), BF16)],
        compiler_params=_cp(("parallel", "parallel")),
    )(u, u, u, ya, yb, yc)


def _merge_bwd(u, ya, yb, yc, dmerged, plan=None):
    T = u.shape[0]
    tr, ucol, act = _merge_specs(T)

    def body(ma_ref, mb_ref, mc_ref, ya_ref, yb_ref, yc_ref, dm_ref, dya_ref, dyb_ref, dyc_ref, dma_ref, dmb_ref, dmc_ref):
        dm = dm_ref[...]
        for m_ref, y_ref, dy_ref, dg_ref in ((ma_ref, ya_ref, dya_ref, dma_ref), (mb_ref, yb_ref, dyb_ref, dmb_ref),
                                             (mc_ref, yc_ref, dyc_ref, dmc_ref)):
            s = _gate(m_ref)
            dy_ref[...] = (dm * s).astype(BF16)
            dg_ref[...] = (dm * y_ref[...] * s * (1.0 - s)).astype(BF16)

    out = _sds((T, D_MODEL), BF16)
    return _call(
        body,
        name="merge_bwd",
        grid=(T // tr, D_MODEL // MIX_COLS),
        in_specs=[ucol(OFF_MA), ucol(OFF_MB), ucol(OFF_MC), act, act, act, act],
        out_specs=[act] * 6,
        out_shape=[out] * 6,
        sem=("parallel", "parallel"),
        args=(u, u, u, ya, yb, yc, dmerged),
        plan=plan,
    )


def _ln_fwd(x, y, g_l, b_l):
    T = x.shape[0]
    tr = _tile(T, 512)
    row = pl.BlockSpec((tr, D_MODEL), lambda i: (i, 0))
    col = pl.BlockSpec((D_MODEL, tr), lambda i: (0, i))
    par = pl.BlockSpec((1, D_MODEL), lambda i: (0, 0))

    def body(x_ref, y_ref, g_ref, b_ref, o_ref, z_ref, ob_ref, ot_ref):
        z = ALPHA * x_ref[...] + y_ref[...]
        z_ref[...] = z
        mu = jnp.mean(z, axis=-1, keepdims=True)
        zc = z - mu
        var = jnp.mean(zc * zc, axis=-1, keepdims=True)
        o = zc * lax.rsqrt(var + LN_EPS) * g_ref[...] + b_ref[...]
        o_ref[...] = o
        ob_ref[...] = o.astype(BF16)
        ot_ref[...] = o.T.astype(BF16)

    return pl.pallas_call(
        body,
        name="ln_fwd",
        grid=(T // tr,),
        in_specs=[row, row, par, par],
        out_specs=[row, row, row, col],
        out_shape=[_sds((T, D_MODEL), F32)] * 2 + [_sds((T, D_MODEL), BF16), _sds((D_MODEL, T), BF16)],
        compiler_params=_cp(("parallel",)),
    )(x, y, g_l, b_l)


def _operand_forms(x):
    T = x.shape[0]
    tr = _tile(T, 256)
    row = pl.BlockSpec((tr, D_MODEL), lambda i: (i, 0))
    col = pl.BlockSpec((D_MODEL, tr), lambda i: (0, i))

    def body(x_ref, xb_ref, xt_ref):
        xv = x_ref[...]
        xb_ref[...] = xv.astype(BF16)
        xt_ref[...] = xv.T.astype(BF16)

    return pl.pallas_call(
        body,
        name="operand_forms",
        grid=(T // tr,),
        in_specs=[row],
        out_specs=[row, col],
        out_shape=[_sds((T, D_MODEL), BF16), _sds((D_MODEL, T), BF16)],
        compiler_params=_cp(("parallel",)),
    )(x)


def _ln_bwd(z, dxn, g_l):
    T = z.shape[0]
    tr = _tile(T, 512)
    row = pl.BlockSpec((tr, D_MODEL), lambda i: (i, 0))
    par = pl.BlockSpec((1, D_MODEL), lambda i: (0, 0))

    def body(z_ref, d_ref, g_ref, dz_ref, dzb_ref, dg_ref, db_ref):
        @pl.when(pl.program_id(0) == 0)
        def _():
            dg_ref[...] = jnp.zeros_like(dg_ref)
            db_ref[...] = jnp.zeros_like(db_ref)

        z = z_ref[...]
        d = d_ref[...]
        mu = jnp.mean(z, axis=-1, keepdims=True)
        zc = z - mu
        rstd = lax.rsqrt(jnp.mean(zc * zc, axis=-1, keepdims=True) + LN_EPS)
        zh = zc * rstd
        dg_ref[...] += jnp.sum(d * zh, axis=0, keepdims=True)
        db_ref[...] += jnp.sum(d, axis=0, keepdims=True)
        dh = d * g_ref[...]
        dz = rstd * (dh - jnp.mean(dh, axis=-1, keepdims=True) - zh * jnp.mean(dh * zh, axis=-1, keepdims=True))
        dz_ref[...] = dz
        dzb_ref[...] = dz.astype(BF16)

    return pl.pallas_call(
        body,
        name="ln_bwd",
        grid=(T // tr,),
        in_specs=[row, row, par],
        out_specs=[row, row, par, par],
        out_shape=[_sds((T, D_MODEL), F32), _sds((T, D_MODEL), BF16), _sds((1, D_MODEL), F32), _sds((1, D_MODEL), F32)],
        compiler_params=_cp(("arbitrary",)),
    )(z, dxn, g_l)


def _loss_head(y, target):
    T = y.shape[0]
    tr = _tile(T, 256)
    row = pl.BlockSpec((tr, D_MODEL), lambda i: (i, 0))
    acc = pl.BlockSpec((8, 128), lambda i: (0, 0))

    def body(y_ref, t_ref, l_ref, d_ref):
        @pl.when(pl.program_id(0) == 0)
        def _():
            l_ref[...] = jnp.zeros_like(l_ref)

        err = y_ref[...] - t_ref[...]
        d_ref[...] = err * (1.0 / D_MODEL)
        part = 0.5 * jnp.sum(jnp.sum(err * err, axis=-1, keepdims=True) * (1.0 / D_MODEL), axis=0, keepdims=True)
        r = lax.broadcasted_iota(jnp.int32, (8, 128), 0)
        c = lax.broadcasted_iota(jnp.int32, (8, 128), 1)
        l_ref[...] += jnp.where((r == 0) & (c == 0), part, 0.0)

    return pl.pallas_call(
        body,
        name="loss_head",
        grid=(T // tr,),
        in_specs=[row, row],
        out_specs=[acc, row],
        out_shape=[_sds((8, 128), F32), _sds((T, D_MODEL), F32)],
        compiler_params=_cp(("arbitrary",)),
    )(y, target)


ADAMW_BLOCK_ELEMS = 512 * 1024


def _adamw(w, g, m, v, name):
    shape = w.shape
    cols = shape[-1]
    rows = math.prod(shape[:-1])
    flat = lambda a: a.reshape(rows, cols)
    if rows * cols <= ADAMW_BLOCK_ELEMS or rows % 8:
        tr = rows
    else:
        tr = 8
        while rows % (tr * 2) == 0 and tr * 2 * cols <= ADAMW_BLOCK_ELEMS:
            tr *= 2
    blk = pl.BlockSpec((tr, cols), lambda i: (i, 0))
    c1 = 1.0 - ADAM_B1 ** ADAM_STEP
    c2 = 1.0 - ADAM_B2 ** ADAM_STEP

    def body(w_ref, g_ref, m_ref, v_ref, d_ref, nm_ref, nv_ref):
        gg = g_ref[...]
        nm = ADAM_B1 * m_ref[...] + (1.0 - ADAM_B1) * gg
        nv = ADAM_B2 * v_ref[...] + (1.0 - ADAM_B2) * (gg * gg)
        nm_ref[...] = nm
        nv_ref[...] = nv
        d_ref[...] = -ADAM_LR * ((nm / c1) / (jnp.sqrt(nv / c2) + ADAM_EPS) + ADAM_WD * w_ref[...])

    outs = pl.pallas_call(
        body,
        name=name,
        grid=(rows // tr,),
        in_specs=[blk] * 4,
        out_specs=[blk] * 3,
        out_shape=[_sds((rows, cols), F32)] * 3,
        compiler_params=_cp(("parallel",)),
    )(flat(w), flat(g), flat(m), flat(v))
    return tuple(o.reshape(shape) for o in outs)


def _t5_bucket(dist):
    max_exact = N_BUCKETS // 2
    logd = jnp.log(jnp.maximum(dist, 1).astype(F32) / max_exact) / math.log(MAX_DISTANCE / max_exact)
    large = jnp.minimum(max_exact + (logd * (N_BUCKETS - max_exact)).astype(jnp.int32), N_BUCKETS - 1)
    return jnp.where(dist < max_exact, dist, large)


def _band_bias(rel_bias):
    i = jnp.arange(WINDOW)[:, None]
    j = jnp.arange(2 * WINDOW)[None, :]
    bucket = _t5_bucket(jnp.clip(WINDOW + i - j, 0, WINDOW - 1))
    onehot = (bucket[:, :, None] == jnp.arange(N_BUCKETS)[None, None, :]).astype(F32)
    bias = jnp.einsum("ijb,bh->hij", onehot, rel_bias.astype(F32), precision=lax.Precision.HIGHEST)
    rel = WINDOW + i - j
    return jnp.where(((rel >= 0) & (rel < WINDOW))[None], bias, MASK_VALUE)


def _lower_bounds(lb_param):
    soft = jax.nn.softmax(lb_param.astype(F32), axis=0)
    return jnp.cumsum(soft, axis=0) - soft[0:1]


def _mm_rows(T):
    return _tile(T, 1024)


def _layer_fwd(xs, ex, layer, lower_l, bias4, sink4_l, gn_l, cw_l, lng_l, lnb_l):
    x, xb, xt = xs
    T = x.shape[0]
    tm = _mm_rows(T)
    plan = ex.fwd_plan(layer)
    u, ug, landed = _input_projection(xb, ex.weights[layer]["w_in"][0], plan=plan)
    if plan is not None:
        plan = ex.fwd_pass_plan(layer, landed)
    (o_raw, states), landed = _hgrn_fwd(u, lower_l, plan=plan)
    if plan is not None:
        ex.fwd_landed(layer, landed)
    w_proj_l, w_out_l = ex.weights[layer]["w_proj"], ex.weights[layer]["w_out"][0]
    o_b = _attn_fwd(u, bias4, sink4_l)
    ha, hb, hc, hat, hbt, hct = _mix_fwd(u, o_raw, o_b, gn_l, cw_l)
    ys = [_matmul(h, w_proj_l, mode="nn", tm=tm, tn=1024, tk=HGRN_WIDTH, b_idx=i, name="mm_proj", out_dtype=BF16)
          for i, h in enumerate((ha, hb, hc))]
    merged, merged_t = _merge_fwd(ug, *ys)
    y = _matmul(merged, w_out_l, mode="nn", tm=tm, tn=1024, tk=D_MODEL, name="mm_out")
    xn, z, xnb, xnt = _ln_fwd(x, y, lng_l, lnb_l)
    saved = dict(xt=xt, u=u, ug=ug, o_raw=o_raw, states=states, o_b=o_b, hts=(hat, hbt, hct), ys=ys,
                 merged_t=merged_t, z=z)
    return (xn, xnb, xnt), saved


def _layer_bwd(dxn, s, ex, layer, lower_l, bias4, sink4_l, gn_l, cw_l, lng_l):
    T = dxn.shape[0]
    tm = _mm_rows(T)
    u = s["u"]
    w = ex.weights[layer]
    w_in_l, w_proj_l, w_out_l = w["w_in"][0], w["w_proj"], w["w_out"][0]
    dz, dzb, d_lng, d_lnb = _ln_bwd(s["z"], dxn, lng_l)
    dmerged = _matmul(dzb, w_out_l, mode="nt", tm=tm, tn=1024, tk=D_MODEL, name="mm_dmerged")
    g_w_out = _matmul(s["merged_t"], dzb, mode="nn", tm=1024, tn=1024, tk=2048, name="mm_gw_out")
    plan = ex.pair_plan(layer)
    (*dys, dma, dmb, dmc), landed = _merge_bwd(s["ug"], *s["ys"], dmerged, plan=plan)
    if plan is not None:
        ex.pair_landed(layer, landed)
    dhs =[_matmul(dy, w_proj_l, mode="nt", tm=tm, tn=1024, tk=D_MODEL, b_idx=i, name="mm_dh") for i, dy in enumerate(dys)]
    g_w_proj = jnp.stack([_matmul(ht, dy, mode="nn", tm=1024, tn=1024, tk=2048, name="mm_gw_proj")
                          for ht, dy in zip(s["hts"], dys)])
    do_raw, do_b, dag, dbg, dcb, dcc, dcx, dcg, d_gn, d_cw = _mix_bwd(u, s["o_raw"], s["o_b"], gn_l, cw_l, *dhs)
    daq, daf, dai, d_lower = _hgrn_bwd(u, lower_l, s["states"], do_raw)
    plan = ex.slab_plan(layer)
    (dbq, dkc, dkp, dvc, dvp, d_bias4, d_sink4), landed = _attn_bwd(u, bias4, sink4_l, do_b, plan=plan)
    if plan is not None:
        ex.slab_landed(layer, landed)
    dbk = _with_next_block_part(dkc, dkp)
    dbv = _with_next_block_part(dvc, dvp)
    du = jnp.concatenate([daq, daf, dai, dag, dbq, dbk, dbv, dbg, dcb, dcc, dcx, dcg, dma, dmb, dmc], axis=1)
    g_w_in = _matmul(s["xt"], du, mode="nn", tm=1024, tn=256, tk=8192, name="mm_gw_in")
    ex.grads_ready(layer, dict(w_in=g_w_in[None], w_proj=g_w_proj, w_out=g_w_out[None]))
    assemble = ex.assemble_plan(layer)
    tail = ex.tail_plan() if layer == 0 else None
    carried = [p for p in (assemble, tail) if p is not None]
    plan = _merge_plans(carried) if carried else None
    dx = _matmul(du, w_in_l, mode="nt", tm=_tile(T, 512), tn=1024, tk=5632, name="mm_dx", add=dz, add_scale=ALPHA, plan=plan)
    if plan is not None:
        dx, landed = dx
        n_assemble = 0 if assemble is None else len(assemble.out_shapes)
        if assemble is not None:
            ex.assembled(layer, landed[:n_assemble])
        if tail is not None:
            ex.slab_landed(-1, landed[n_assemble:])
    d_sinks = jnp.sum(d_sink4.reshape(ATTN_HEADS, WINDOW), axis=-1)
    small = dict(lower=d_lower[0], gn=d_gn[0], sinks=d_sinks, cw=d_cw[:CONV_K], bias=d_bias4.reshape(ATTN_HEADS, WINDOW, 2 * WINDOW),
                 lng=d_lng[0], lnb=d_lnb[0])
    return dx, small


def _local_step(x, target, ex, lb_param, hgrn_norm_g, attn_sinks, conv_w_full, rel_bias, ln_g, ln_b):
    lower, lower_vjp = jax.vjp(_lower_bounds, lb_param)
    bias, bias_vjp = jax.vjp(_band_bias, rel_bias)
    bias4 = bias.reshape(ATTN_KV_HEADS, QROWS, 2 * WINDOW)
    sink4 = jnp.broadcast_to(attn_sinks.reshape(DEPTH, ATTN_HEADS, 1, 1), (DEPTH, ATTN_HEADS, WINDOW, 1)).reshape(
        DEPTH, ATTN_KV_HEADS, QROWS, 1)
    row = lambda a, l: a[l:l + 1]
    saved = []
    hs = (x, *_operand_forms(x))
    for l in range(DEPTH):
        hs, s = _layer_fwd(hs, ex, l, row(lower, l), bias4, sink4[l], row(hgrn_norm_g, l), conv_w_full[l], row(ln_g, l), row(ln_b, l))
        saved.append(s)
    loss_blk, dh = _loss_head(hs[0], target)
    smalls = [None] * DEPTH
    for l in reversed(range(DEPTH)):
        dh, smalls[l] = _layer_bwd(dh, saved[l], ex, l, row(lower, l), bias4, sink4[l], row(hgrn_norm_g, l), conv_w_full[l], row(ln_g, l))
    stack = lambda k: jnp.stack([sm[k] for sm in smalls])
    d_bias = smalls[0]["bias"] + smalls[1]["bias"] + smalls[2]["bias"] + smalls[3]["bias"]
    small = dict(
        lb_param=lower_vjp(stack("lower"))[0], hgrn_norm_g=stack("gn"), attn_sinks=stack("sinks"), conv_w=stack("cw"),
        rel_bias=bias_vjp(d_bias)[0], ln_g=stack("lng"), ln_b=stack("lnb"))
    return loss_blk, dh, small


ANY = pl.BlockSpec(memory_space=pl.ANY)
DMA_SEM = pltpu.SemaphoreType.DMA


def _coords():
    return lax.axis_index("x"), lax.axis_index("y"), lax.axis_index("c")


def _other_chips(x, y):
    return [(1 - x, y), (x, 1 - y), (1 - x, 1 - y)]


def _remote(src, dst, send_sem, recv_sem, device):
    return pltpu.make_async_remote_copy(src_ref=src, dst_ref=dst, send_sem=send_sem, recv_sem=recv_sem,
                                        device_id=device, device_id_type=MESH)


def _sub(ref, axis, index, size):
    idx = [slice(None)] * len(ref.shape)
    idx[axis] = pl.ds(pl.multiple_of(index * size, size), size)
    return ref.at[tuple(idx)]


class _Plan:
    def __init__(self, inputs, out_shapes, aliases, n, make):
        self.inputs, self.out_shapes, self.aliases, self.n, self.make = tuple(inputs), tuple(out_shapes), dict(aliases), n, make


class _Xfer:
    def __init__(self, send, recv=None):
        self.send, self.recv = send, send if recv is None else recv

    def start(self):
        self.send.start()

    def wait(self):
        self.send.wait_send()
        self.recv.wait_recv()


def _merge_plans(plans):
    def make(in_refs, out_refs, send_sems, recv_sems, base):
        out, i0, o0, b0 = [], 0, 0, base
        for p in plans:
            out += p.make(in_refs[i0:i0 + len(p.inputs)], out_refs[o0:o0 + len(p.out_shapes)], send_sems, recv_sems, b0)
            i0, o0, b0 = i0 + len(p.inputs), o0 + len(p.out_shapes), b0 + p.n
        return out

    inputs, out_shapes, aliases = [], [], {}
    for p in plans:
        aliases.update({len(inputs) + k: len(out_shapes) + v for k, v in p.aliases.items()})
        inputs += p.inputs
        out_shapes += p.out_shapes
    return _Plan(inputs, out_shapes, aliases, sum(p.n for p in plans), make)


def _run_plan(plan, name):
    ni, no = len(plan.inputs), len(plan.out_shapes)

    def body(*refs):
        transfers = plan.make(refs[:ni], refs[ni:ni + no], refs[ni + no], refs[ni + no + 1], 0)
        for t in transfers:
            t.start()
        for t in transfers:
            t.wait()

    outs = pl.pallas_call(
        body, name=name, in_specs=[ANY] * ni, out_specs=[ANY] * no, out_shape=list(plan.out_shapes),
        input_output_aliases=plan.aliases, scratch_shapes=[DMA_SEM((plan.n,)), DMA_SEM((plan.n,))],
    )(*plan.inputs)
    return list(outs)


def _gather_send_plan(shards, layer, sax):
    shp = shards.shape[1:]
    hax = 3 - sax
    w, hw = shp[sax], shp[hax] // 2
    out_shape = list(shp)
    out_shape[sax] = w * N_CHIPS

    def make(in_refs, out_refs, send_sems, recv_sems, base):
        (src_ref,), (out_ref,) = in_refs, out_refs
        x, y, c = _coords()
        j = 2 * x + y
        src = src_ref.at[layer]
        own = pltpu.make_async_copy(src, _sub(out_ref, sax, j, w), send_sems.at[base])
        dst = _sub(_sub(out_ref, sax, j, w), hax, c, hw)
        return [own] + [
            _Xfer(_remote(_sub(src, hax, c, hw), dst, send_sems.at[base + 1 + k], recv_sems.at[base + 1 + k], (px, py, c)))
            for k, (px, py) in enumerate(_other_chips(x, y))]

    return _Plan([shards], [_sds(tuple(out_shape), shards.dtype)], {}, 4, make)


def _gather_pass_plan(full, sax):
    hax = 3 - sax
    w, hw = full.shape[sax] // N_CHIPS, full.shape[hax] // 2

    def make(in_refs, out_refs, send_sems, recv_sems, base):
        (out_ref,) = out_refs
        x, y, c = _coords()
        region = lambda slab, half: _sub(_sub(out_ref, sax, slab, w), hax, half, hw)
        out = []
        for k, (px, py) in enumerate(_other_chips(x, y)):
            mine, theirs = region(2 * px + py, c), region(2 * px + py, 1 - c)
            sems = send_sems.at[base + k], recv_sems.at[base + k]
            out.append(_Xfer(_remote(mine, mine, *sems, (x, y, 1 - c)), _remote(theirs, theirs, *sems, (x, y, c))))
        return out

    return _Plan([full], [_sds(full.shape, full.dtype)], {0: 0}, 3, make)


def _pair_exchange_plan(g, hax):
    hw = g.shape[hax] // 2
    out_shape = list(g.shape)
    out_shape[hax] = hw

    def make(in_refs, out_refs, send_sems, recv_sems, base):
        x, y, c = _coords()
        return [_Xfer(_remote(_sub(in_refs[0], hax, 1 - c, hw), out_refs[0], send_sems.at[base], recv_sems.at[base], (x, y, 1 - c)))]

    return _Plan([g], [_sds(tuple(out_shape), g.dtype)], {}, 1, make)


def _add_own_half(place, g, recv, hax, blk, name):
    L, ah, bh = recv.shape
    tr, tc = blk
    nr, nc = ah // tr, bh // tc

    def g_map(l, i, jc, p):
        return (l, i + p[0] * nr, jc) if hax == 1 else (l, i, jc + p[0] * nc)

    def body(p_ref, g_ref, r_ref, o_ref):
        o_ref[...] = (g_ref[...] + r_ref[...]).astype(BF16)

    same = pl.BlockSpec((None, tr, tc), lambda l, i, jc, p: (l, i, jc))
    return pl.pallas_call(
        body,
        name=name,
        grid_spec=pltpu.PrefetchScalarGridSpec(
            num_scalar_prefetch=1, grid=(L, nr, nc),
            in_specs=[pl.BlockSpec((None, tr, tc), g_map), same], out_specs=same),
        out_shape=_sds(recv.shape, BF16),
        compiler_params=_cp(("parallel", "parallel", "parallel")),
    )(place, g, recv)


def _slab_exchange_plan(p, sax):
    w = p.shape[sax] // N_CHIPS
    slab_shape = list(p.shape)
    slab_shape[sax] = w

    def make(in_refs, out_refs, send_sems, recv_sems, base):
        x, y, c = _coords()
        return [_Xfer(_remote(_sub(in_refs[0], sax, 2 * px + py, w), out_refs[0].at[k], send_sems.at[base + k],
                              recv_sems.at[base + k], (px, py, c)))
                for k, (px, py) in enumerate(_other_chips(x, y))]

    return _Plan([p], [_sds((3, *slab_shape), p.dtype)], {}, 3, make)


def _add_slabs(place, g, pair, recv, sax, blk, name):
    hax = 3 - sax
    _, L, a, b = recv.shape
    tr, tc = blk
    nr, nc = a // tr, b // tc

    def g_map(l, i, jc, p):
        return (l, p[0] * nr + i, p[1] * nc + jc) if hax == 1 else (l, p[1] * nr + i, p[0] * nc + jc)

    def pair_map(l, i, jc, p):
        return (l, i, p[1] * nc + jc) if hax == 1 else (l, p[1] * nr + i, jc)

    def out_map(l, i, jc, p):
        return (l, p[0] * nr + i, jc) if hax == 1 else (l, i, p[0] * nc + jc)

    def body(p_ref, g_ref, pair_ref, r0_ref, r1_ref, r2_ref, o_ref):
        own = g_ref[...] + pair_ref[...]
        o_ref[...] = ((own + r0_ref[...].astype(F32)) + r1_ref[...].astype(F32)) + r2_ref[...].astype(F32)

    def rk(k):
        return pl.BlockSpec((None, None, tr, tc), lambda l, i, jc, p, k=k: (k, l, i, jc))

    out_shape = [L, a, b]
    out_shape[hax] *= 2
    blk3 = (None, tr, tc)
    return pl.pallas_call(
        body,
        name=name,
        grid_spec=pltpu.PrefetchScalarGridSpec(
            num_scalar_prefetch=1, grid=(L, nr, nc),
            in_specs=[pl.BlockSpec(blk3, g_map), pl.BlockSpec(blk3, pair_map), rk(0), rk(1), rk(2)],
            out_specs=pl.BlockSpec(blk3, out_map)),
        out_shape=_sds(tuple(out_shape), F32),
        compiler_params=_cp(("parallel", "parallel", "parallel")),
    )(place, g, pair, recv, recv, recv)


def _pair_assemble_plan(r, hax):
    hw = r.shape[hax] // 2

    def make(in_refs, out_refs, send_sems, recv_sems, base):
        x, y, c = _coords()
        mine, other = _sub(out_refs[0], hax, c, hw), _sub(out_refs[0], hax, 1 - c, hw)
        sems = send_sems.at[base], recv_sems.at[base]
        return [_Xfer(_remote(mine, mine, *sems, (x, y, 1 - c)), _remote(other, other, *sems, (x, y, c)))]

    return _Plan([r], [_sds(r.shape, r.dtype)], {0: 0}, 1, make)


CLASSES = dict(w_in=(2, (128, 4224), (128, 4224)), w_proj=(2, (256, 2048), (512, 512)), w_out=(1, (256, 1024), (256, 1024)))


class _Exchanges:
    def __init__(self, place, shards):
        self.place, self.shards = place, shards
        self.weights, self.pending, self.pair, self.halves, self.reduced = {}, {}, {}, {}, {}

    FIRST = ("w_in",)

    def _carried(self, layer):
        keys = [(0, k) for k in CLASSES if k not in self.FIRST] if layer == 0 else []
        return keys + ([(layer + 1, k) for k in CLASSES] if layer + 1 < DEPTH else [])

    def _send_plan(self, keys):
        return _merge_plans([_gather_send_plan(self.shards[k], l, CLASSES[k][0]) for l, k in keys])

    def _pass_plan(self, keys, bufs):
        return _merge_plans([_gather_pass_plan(b, CLASSES[k][0]) for (_, k), b in zip(keys, bufs)])

    def _landed(self, keys, bufs):
        for (l, k), b in zip(keys, bufs):
            self.weights.setdefault(l, {})[k] = b

    def first_weights(self):
        keys = [(0, k) for k in self.FIRST]
        sent = _run_plan(self._send_plan(keys), "gather_send_0")
        self._landed(keys, _run_plan(self._pass_plan(keys, sent), "gather_pass_0"))

    def fwd_plan(self, layer):
        keys = self._carried(layer)
        return self._send_plan(keys) if keys else None

    def fwd_pass_plan(self, layer, bufs):
        return self._pass_plan(self._carried(layer), bufs)

    def fwd_landed(self, layer, bufs):
        self._landed(self._carried(layer), bufs)

    def grads_ready(self, layer, grads):
        self.pending[layer] = grads

    def pair_plan(self, layer):
        if layer + 1 not in self.pending:
            return None
        g = self.pending[layer + 1]
        return _merge_plans([_pair_exchange_plan(g[k], 3 - CLASSES[k][0]) for k in CLASSES])

    def pair_landed(self, layer, bufs):
        self.pair[layer + 1] = dict(zip(CLASSES, bufs))

    def slab_plan(self, layer):
        src = layer + 1
        if src not in self.pair:
            return None
        g, pair = self.pending[src], self.pair[src]
        sums = [_add_own_half(self.place, g[k], pair[k], 3 - CLASSES[k][0], CLASSES[k][1], f"rs_pair_add_{k}_{src}") for k in CLASSES]
        return _merge_plans([_slab_exchange_plan(p, CLASSES[k][0]) for k, p in zip(CLASSES, sums)])

    def slab_landed(self, layer, bufs):
        src = layer + 1
        g, pair = self.pending.pop(src), self.pair.pop(src)
        self.halves[src] = [_add_slabs(self.place, g[k], pair[k], r, CLASSES[k][0], CLASSES[k][2], f"rs_slab_add_{k}_{src}")
                            for k, r in zip(CLASSES, bufs)]

    def assemble_plan(self, layer):
        if layer + 1 not in self.halves:
            return None
        return _merge_plans([_pair_assemble_plan(h, 3 - CLASSES[k][0]) for k, h in zip(CLASSES, self.halves[layer + 1])])

    def assembled(self, layer, bufs):
        del self.halves[layer + 1]
        self.reduced[layer + 1] = dict(zip(CLASSES, bufs))

    def tail_plan(self):
        self.pair_landed(-1, _run_plan(self.pair_plan(-1), "rs_pair_0"))
        return self.slab_plan(-1)

    def finish(self):
        self.assembled(-1, _run_plan(self.assemble_plan(-1), "rs_assemble_0"))
        return [self.reduced[l] for l in range(DEPTH)]


N_DEV = 8


def _all_reduce_small(v, name):
    rows = v.shape[0]

    def body(v_ref, gath_ref, sum_ref, send_sems, recv_sems, local_sem):
        x, y, c = _coords()
        me, sib = (x, y, c), (x, y, 1 - c)
        chips = _other_chips(x, y)

        def slot(px, py, pc):
            return gath_ref.at[pl.ds(pl.multiple_of((4 * px + 2 * py + pc) * rows, rows), rows), :]

        def copy(k, block, to, src=None):
            return _remote(slot(*block) if src is None else src, slot(*block), send_sems.at[k], recv_sems.at[k], to)

        mine = pltpu.make_async_copy(v_ref, slot(*me), local_sem)
        mine.start()
        first = [copy(0, me, sib, src=v_ref)] + [copy(1 + k, me, (*chip, c), src=v_ref) for k, chip in enumerate(chips)]
        for cp in first:
            cp.start()
        passed = [copy(4 + k, (*chip, c), sib) for k, chip in enumerate(chips)]
        for k, chip in enumerate(chips):
            copy(1 + k, (*chip, c), me).wait_recv()
            passed[k].start()
        copy(0, sib, me).wait_recv()
        for k, chip in enumerate(chips):
            copy(4 + k, (*chip, 1 - c), me).wait_recv()
        for cp in first + passed:
            cp.wait_send()
        mine.wait()
        acc = gath_ref[0:rows, :]
        for d in range(1, N_DEV):
            acc = acc + gath_ref[d * rows:(d + 1) * rows, :]
        sum_ref[...] = acc

    vm = pl.BlockSpec(memory_space=pltpu.VMEM)
    return pl.pallas_call(
        body, name=name, in_specs=[vm], out_specs=[vm, vm],
        out_shape=[_sds((N_DEV * rows, 128), F32), _sds((rows, 128), F32)],
        scratch_shapes=[DMA_SEM((7,)), DMA_SEM((7,)), DMA_SEM(())],
    )(v)[1]


def _pad_rows(a):
    flat = a.reshape(-1).astype(F32)
    rows = -(-flat.shape[0] // (8 * 128)) * 8
    return jnp.pad(flat, (0, rows * 128 - flat.shape[0])).reshape(rows, 128)


def _sum_over_devices(parts, name):
    blocks = [_pad_rows(a) for a in parts.values()]
    total = _all_reduce_small(jnp.concatenate(blocks, axis=0), name)
    out, r0 = {}, 0
    for (key, a), blk in zip(parts.items(), blocks):
        out[key] = total[r0:r0 + blk.shape[0]].reshape(-1)[:a.size].reshape(a.shape)
        r0 += blk.shape[0]
    return out


def kernel(x, w_in, w_proj_hgrn, w_proj_attn, w_proj_conv, w_out, lb_param, hgrn_norm_g, attn_sinks, conv_w, rel_bias, ln_g, ln_b, loss_target, m_w_in, m_w_proj_hgrn, m_w_proj_attn, m_w_proj_conv, m_w_out, m_lb_param, m_hgrn_norm_g, m_attn_sinks, m_conv_w, m_rel_bias, m_ln_g, m_ln_b, v_w_in, v_w_proj_hgrn, v_w_proj_attn, v_w_proj_conv, v_w_out, v_lb_param, v_hgrn_norm_g, v_attn_sinks, v_conv_w, v_rel_bias, v_ln_g, v_ln_b):
    xi, yi, ci = _coords()
    slab = 2 * xi + yi
    place = jnp.stack([ci, slab]).astype(jnp.int32)
    conv_cols = conv_w.shape[-1]

    w_in_b = w_in.astype(BF16)[:, None]
    w_proj_b = jnp.stack([w_proj_hgrn, w_proj_attn, w_proj_conv], axis=1).astype(BF16)
    w_out_b = w_out.astype(BF16)[:, None]
    ex = _Exchanges(place, dict(w_in=w_in_b, w_proj=w_proj_b, w_out=w_out_b))
    ex.first_weights()
    conv_spread = lax.dynamic_update_slice(jnp.zeros((DEPTH, CONV_K, CONV_WIDTH), F32), conv_w, (0, 0, slab * conv_cols))
    conv_full = 0.5 * _sum_over_devices({"conv_w": conv_spread}, "gather_conv_w")["conv_w"]

    loss_blk, dx, small = _local_step(x[0], loss_target[0], ex, lb_param, hgrn_norm_g, attn_sinks, conv_full, rel_bias, ln_g, ln_b)

    reduced = ex.finish()
    g_w_in = jnp.stack([r["w_in"][0] for r in reduced])
    g_w_proj = jnp.stack([r["w_proj"] for r in reduced])
    g_w_out = jnp.stack([r["w_out"][0] for r in reduced])
    small = dict(small, loss=loss_blk[0:1, 0:1])
    small = _sum_over_devices(small, "sum_small")
    loss = small["loss"][0, 0]
    g_conv = lax.dynamic_slice(small["conv_w"], (0, 0, slab * conv_cols), (DEPTH, CONV_K, conv_cols))

    grads = [g_w_in, g_w_proj[:, 0], g_w_proj[:, 1], g_w_proj[:, 2], g_w_out, small["lb_param"], small["hgrn_norm_g"],
             small["attn_sinks"], g_conv, small["rel_bias"], small["ln_g"], small["ln_b"]]
    names = ["w_in", "w_proj_hgrn", "w_proj_attn", "w_proj_conv", "w_out", "lb_param", "hgrn_norm_g", "attn_sinks",
             "conv_w", "rel_bias", "ln_g", "ln_b"]
    ws = [w_in, w_proj_hgrn, w_proj_attn, w_proj_conv, w_out, lb_param, hgrn_norm_g, attn_sinks, conv_w, rel_bias, ln_g, ln_b]
    ms = [m_w_in, m_w_proj_hgrn, m_w_proj_attn, m_w_proj_conv, m_w_out, m_lb_param, m_hgrn_norm_g, m_attn_sinks, m_conv_w,
          m_rel_bias, m_ln_g, m_ln_b]
    vs = [v_w_in, v_w_proj_hgrn, v_w_proj_attn, v_w_proj_conv, v_w_out, v_lb_param, v_hgrn_norm_g, v_attn_sinks, v_conv_w,
          v_rel_bias, v_ln_g, v_ln_b]
    upd = [_adamw(w, g, m, v, "adamw_" + n) for n, w, g, m, v in zip(names, ws, grads, ms, vs)]
    deltas, new_ms, new_vs = zip(*upd)
    return (loss, dx[None], *grads, *deltas, *new_ms, *new_vs)
```

```python
import functools
import math

import jax
import jax.numpy as jnp
from jax import lax
from jax.experimental import pallas as pl
from jax.experimental.pallas import tpu as pltpu

F32 = jnp.float32
BF16 = jnp.bfloat16
MXU_DTYPE = BF16

D_MODEL = 2048
DEPTH = 4
HGRN_WIDTH = 1024
HGRN_HEAD_DIM = 128
HGRN_HEADS = 8
HGRN_CHUNK = 64
ATTN_HEAD_DIM = 64
ATTN_HEADS = 16
ATTN_KV_HEADS = 4
ATTN_GROUP = ATTN_HEADS // ATTN_KV_HEADS
ATTN_WIDTH = 1024
KV_WIDTH = 256
WINDOW = 128
CONV_WIDTH = 1024
CONV_K = 3
N_BUCKETS = 32
MAX_DISTANCE = 128
ALPHA = (2.0 * DEPTH) ** 0.25
LN_EPS = 1e-5
RMS_EPS = 1e-6
N_IN = 16896
OFF_AQ, OFF_AF, OFF_AI, OFF_AG = 0, 1024, 2048, 3072
OFF_BQ, OFF_BK, OFF_BV, OFF_BG = 4096, 5120, 5376, 5632
OFF_CB, OFF_CC, OFF_CX, OFF_CG = 6656, 7680, 8704, 9728
OFF_MA, OFF_MB, OFF_MC = 10752, 12800, 14848

ADAM_LR = 0.001
ADAM_B1 = 0.9
ADAM_B2 = 0.999
ADAM_EPS = 1e-08
ADAM_WD = 0.01
ADAM_STEP = 10

N_CHIPS = 4
VMEM_LIMIT_BYTES = 48 * 1024 * 1024
EXP_CLAMP = 80.0
MASK_VALUE = -1e30
MESH = pl.DeviceIdType.MESH


def _cp(sem=None):
    return pltpu.CompilerParams(dimension_semantics=sem, vmem_limit_bytes=VMEM_LIMIT_BYTES)


def _tile(dim, pref):
    return pref if dim % pref == 0 else dim


def _sds(shape, dtype):
    return jax.ShapeDtypeStruct(shape, dtype)


def _call(body, *, name, grid, in_specs, out_specs, out_shape, args, scratch_shapes=(), sem=None, plan=None):
    in_specs, out_specs, out_shape = list(in_specs), list(out_specs), list(out_shape)
    if plan is None:
        outs = pl.pallas_call(body, name=name, grid=grid, in_specs=in_specs, out_specs=out_specs, out_shape=out_shape,
                              scratch_shapes=list(scratch_shapes), compiler_params=_cp(sem))(*args)
        return list(outs), []
    ni, no, ns = len(in_specs), len(out_specs), len(scratch_shapes)
    pi, po = len(plan.inputs), len(plan.out_shapes)

    def carrier(*refs):
        c_in, p_in = refs[:ni], refs[ni:ni + pi]
        c_out, p_out = refs[ni + pi:ni + pi + no], refs[ni + pi + no:ni + pi + no + po]
        c_scr = refs[ni + pi + no + po:ni + pi + no + po + ns]
        send_sems, recv_sems = refs[-2], refs[-1]
        ids = [pl.program_id(a) for a in range(len(grid))]
        first = functools.reduce(jnp.logical_and, [i == 0 for i in ids])
        last = functools.reduce(jnp.logical_and, [i == n - 1 for i, n in zip(ids, grid)])

        @pl.when(first)
        def _():
            for cp in plan.make(p_in, p_out, send_sems, recv_sems, 0):
                cp.start()

        body(*c_in, *c_out, *c_scr)

        @pl.when(last)
        def _():
            for cp in plan.make(p_in, p_out, send_sems, recv_sems, 0):
                cp.wait()

    any_spec = pl.BlockSpec(memory_space=pl.ANY)
    outs = pl.pallas_call(
        carrier, name=name, grid=grid,
        in_specs=in_specs + [any_spec] * pi, out_specs=out_specs + [any_spec] * po,
        out_shape=out_shape + list(plan.out_shapes),
        scratch_shapes=list(scratch_shapes) + [pltpu.SemaphoreType.DMA((plan.n,)), pltpu.SemaphoreType.DMA((plan.n,))],
        input_output_aliases={ni + k: no + v for k, v in plan.aliases.items()},
        compiler_params=_cp(tuple("arbitrary" for _ in grid)),
    )(*args, *plan.inputs)
    return list(outs[:no]), list(outs[no:])


_DIMS = {
    "nn": (((1,), (0,)), ((), ())),
    "nt": (((1,), (1,)), ((), ())),
    "tn": (((0,), (0,)), ((), ())),
}


def _dot_raw(a, b, mode):
    return lax.dot_general(a.astype(MXU_DTYPE), b.astype(MXU_DTYPE), _DIMS[mode], preferred_element_type=F32)


@functools.partial(jax.custom_vjp, nondiff_argnums=(2,))
def _dot(a, b, mode):
    return _dot_raw(a, b, mode)


def _dot_fwd(a, b, mode):
    return _dot_raw(a, b, mode), (a, b)


def _dot_bwd(mode, res, g):
    a, b = res
    if mode == "nn":
        return _dot_raw(g, b, "nt"), _dot_raw(a, g, "tn")
    if mode == "nt":
        return _dot_raw(g, b, "nn"), _dot_raw(g, a, "tn")
    return _dot_raw(b, g, "nt"), _dot_raw(a, g, "nn")


_dot.defvjp(_dot_fwd, _dot_bwd)


def _matmul(a, b, *, mode, tm, tn, tk, name, a_idx=None, b_idx=None, out_dtype=F32, add=None, add_scale=1.0, plan=None):
    a2, b2 = a.shape[-2:], b.shape[-2:]
    if mode == "nn":
        (M, K), (K2, N) = a2, b2
    elif mode == "nt":
        (M, K), (N, K2) = a2, b2
    else:
        (K, M), (K2, N) = a2, b2
    assert K == K2, (a.shape, b.shape, mode)
    tm, tn, tk = _tile(M, tm), _tile(N, tn), _tile(K, tk)
    nk = K // tk

    a_blk = (tk, tm) if mode == "tn" else (tm, tk)
    b_blk = (tn, tk) if mode == "nt" else (tk, tn)

    def a_map(i, j, k):
        ij = (k, i) if mode == "tn" else (i, k)
        return ij if a_idx is None else (a_idx,) + ij

    def b_map(i, j, k):
        ij = (j, k) if mode == "nt" else (k, j)
        return ij if b_idx is None else (b_idx,) + ij

    in_specs = [
        pl.BlockSpec(a_blk if a_idx is None else (None,) + a_blk, a_map),
        pl.BlockSpec(b_blk if b_idx is None else (None,) + b_blk, b_map),
    ]
    args = [a, b]
    if add is not None:
        in_specs.append(pl.BlockSpec((tm, tn), lambda i, j, k: (i, j)))
        args.append(add)
    n_in = len(args)

    def body(*refs):
        a_ref, b_ref = refs[0], refs[1]
        o_ref = refs[n_in]
        p = _dot_raw(a_ref[...], b_ref[...], mode)

        def finish(val):
            if add is not None:
                val = val + add_scale * refs[2][...]
            o_ref[...] = val.astype(out_dtype)

        if nk == 1:
            finish(p)
        else:
            acc_ref = refs[n_in + 1]
            k = pl.program_id(2)

            @pl.when(k == 0)
            def _():
                acc_ref[...] = p

            @pl.when(k > 0)
            def _():
                acc_ref[...] += p

            @pl.when(k == nk - 1)
            def _():
                finish(acc_ref[...])

    (out,), plan_outs = _call(
        body,
        name=name,
        grid=(M // tm, N // tn, nk),
        in_specs=in_specs,
        out_specs=[pl.BlockSpec((tm, tn), lambda i, j, k: (i, j))],
        out_shape=[_sds((M, N), out_dtype)],
        scratch_shapes=[pltpu.VMEM((tm, tn), F32)] if nk > 1 else [],
        sem=("parallel", "parallel", "arbitrary"),
        args=args,
        plan=plan,
    )
    return out if plan is None else (out, plan_outs)


def _input_projection(xb, w_in_l, plan=None):
    T = xb.shape[0]
    tm, tn = _tile(T, 1024), 768
    n_main = OFF_MA // tn
    assert OFF_MA % tn == 0 and N_IN % tn == 0

    def body(a_ref, b_ref, u_ref, ug_ref):
        p = _dot_raw(a_ref[...], b_ref[...], "nn")
        j = pl.program_id(1)

        @pl.when(j < n_main)
        def _():
            u_ref[...] = p

        @pl.when(j >= n_main)
        def _():
            ug_ref[...] = p.astype(BF16)

    outs, landed = _call(
        body,
        name="mm_u",
        grid=(T // tm, N_IN // tn),
        in_specs=[pl.BlockSpec((tm, D_MODEL), lambda i, j: (i, 0)), pl.BlockSpec((D_MODEL, tn), lambda i, j: (0, j))],
        out_specs=[pl.BlockSpec((tm, tn), lambda i, j: (i, jnp.minimum(j, n_main - 1))),
                   pl.BlockSpec((tm, tn), lambda i, j: (i, jnp.maximum(j - n_main, 0)))],
        out_shape=[_sds((T, OFF_MA), F32), _sds((T, N_IN - OFF_MA), BF16)],
        sem=("parallel", "arbitrary"),
        args=(xb, w_in_l),
        plan=plan,
    )
    return outs[0], outs[1], landed


def _scan_rows(x, reverse):
    n = x.shape[0]
    row = lax.broadcasted_iota(jnp.int32, x.shape, 0)
    s = 1
    while s < n:
        if reverse:
            x = x + jnp.where(row < n - s, pltpu.roll(x, n - s, 0), 0.0)
        else:
            x = x + jnp.where(row >= s, pltpu.roll(x, s, 0), 0.0)
        s *= 2
    return x


@jax.custom_vjp
def _cumsum_rows(x):
    return _scan_rows(x, False)


_cumsum_rows.defvjp(lambda x: (_scan_rows(x, False), None), lambda _, g: (_scan_rows(g, True),))


def _hgrn_chunk(state_t, qraw, fraw, v, lb):
    c = HGRN_CHUNK
    q = qraw * jax.nn.sigmoid(qraw) * (HGRN_HEAD_DIM ** -0.5)
    f = lb + (1.0 - lb) * jax.nn.sigmoid(fraw)
    k = 1.0 - f
    g = jnp.log(f)
    b = _cumsum_rows(g)
    row = lax.broadcasted_iota(jnp.int32, (c, HGRN_HEAD_DIM), 0)
    b_end = jnp.sum(g, axis=0, keepdims=True)
    b_mid = jnp.sum(jnp.where(row < c // 2, g, 0.0), axis=0, keepdims=True)
    inter = _dot(q * jnp.exp(b), state_t, "nt")
    qt = q * jnp.exp(jnp.minimum(b - b_mid, EXP_CLAMP))
    kt = k * jnp.exp(jnp.minimum(b_mid - b, EXP_CLAMP))
    s = _dot(qt, kt, "nt")
    ti = lax.broadcasted_iota(jnp.int32, (c, c), 0)
    si = lax.broadcasted_iota(jnp.int32, (c, c), 1)
    s = jnp.where(si <= ti, s, 0.0)
    intra = _dot(s, v, "nn")
    k_end = k * jnp.exp(b_end - b)
    new_state_t = state_t * jnp.exp(b_end) + _dot(v, k_end, "tn")
    return new_state_t, inter + intra


def _hgrn_specs(T):
    rows = _tile(T, 1024)
    return rows, T // rows, rows // HGRN_CHUNK


def _hgrn_fwd(u, lower_l, plan=None):
    T = u.shape[0]
    rows, nblk, ncr = _hgrn_specs(T)
    hb = HGRN_WIDTH // HGRN_HEAD_DIM

    def body(q_ref, f_ref, i_ref, lb_ref, o_ref, st_ref, state):
        @pl.when(pl.program_id(1) == 0)
        def _():
            state[...] = jnp.zeros_like(state)

        lb = lb_ref[...]
        for c in range(ncr):
            rs = pl.ds(c * HGRN_CHUNK, HGRN_CHUNK)
            st = state[...]
            st_ref[c] = st
            new, out = _hgrn_chunk(st, q_ref[rs, :], f_ref[rs, :], i_ref[rs, :], lb)
            state[...] = new
            o_ref[rs, :] = out

    blk = (rows, HGRN_HEAD_DIM)
    return _call(
        body,
        name="hgrn_fwd",
        grid=(HGRN_HEADS, nblk),
        in_specs=[
            pl.BlockSpec(blk, lambda h, r: (r, OFF_AQ // 128 + h)),
            pl.BlockSpec(blk, lambda h, r: (r, OFF_AF // 128 + h)),
            pl.BlockSpec(blk, lambda h, r: (r, OFF_AI // 128 + h)),
            pl.BlockSpec((1, HGRN_HEAD_DIM), lambda h, r: (0, h)),
        ],
        out_specs=[
            pl.BlockSpec(blk, lambda h, r: (r, h)),
            pl.BlockSpec((ncr, None, HGRN_HEAD_DIM, HGRN_HEAD_DIM), lambda h, r: (r, h, 0, 0)),
        ],
        out_shape=[_sds((T, HGRN_WIDTH), F32), _sds((T // HGRN_CHUNK, hb, HGRN_HEAD_DIM, HGRN_HEAD_DIM), F32)],
        scratch_shapes=[pltpu.VMEM((HGRN_HEAD_DIM, HGRN_HEAD_DIM), F32)],
        sem=("parallel", "arbitrary"),
        args=(u, u, u, lower_l),
        plan=plan,
    )


def _hgrn_bwd(u, lower_l, states, do_raw):
    T = u.shape[0]
    rows, nblk, ncr = _hgrn_specs(T)

    def body(q_ref, f_ref, i_ref, lb_ref, st_ref, do_ref, dq_ref, df_ref, di_ref, dlb_ref, dstate):
        @pl.when(pl.program_id(1) == 0)
        def _():
            dstate[...] = jnp.zeros_like(dstate)
            dlb_ref[...] = jnp.zeros_like(dlb_ref)

        lb = lb_ref[...]
        for c in reversed(range(ncr)):
            rs = pl.ds(c * HGRN_CHUNK, HGRN_CHUNK)
            _, vjp = jax.vjp(_hgrn_chunk, st_ref[c], q_ref[rs, :], f_ref[rs, :], i_ref[rs, :], lb)
            dst, dq, df, dv, dlb = vjp((dstate[...], do_ref[rs, :]))
            dstate[...] = dst
            dq_ref[rs, :] = dq.astype(BF16)
            df_ref[rs, :] = df.astype(BF16)
            di_ref[rs, :] = dv.astype(BF16)
            dlb_ref[...] += dlb

    blk = (rows, HGRN_HEAD_DIM)
    last = nblk - 1
    out_blk = pl.BlockSpec(blk, lambda h, r: (last - r, h))
    return pl.pallas_call(
        body,
        name="hgrn_bwd",
        grid=(HGRN_HEADS, nblk),
        in_specs=[
            pl.BlockSpec(blk, lambda h, r: (last - r, OFF_AQ // 128 + h)),
            pl.BlockSpec(blk, lambda h, r: (last - r, OFF_AF // 128 + h)),
            pl.BlockSpec(blk, lambda h, r: (last - r, OFF_AI // 128 + h)),
            pl.BlockSpec((1, HGRN_HEAD_DIM), lambda h, r: (0, h)),
            pl.BlockSpec((ncr, None, HGRN_HEAD_DIM, HGRN_HEAD_DIM), lambda h, r: (last - r, h, 0, 0)),
            out_blk,
        ],
        out_specs=[out_blk, out_blk, out_blk, pl.BlockSpec((1, HGRN_HEAD_DIM), lambda h, r: (0, h))],
        out_shape=[_sds((T, HGRN_WIDTH), BF16)] * 3 + [_sds((1, HGRN_WIDTH), F32)],
        scratch_shapes=[pltpu.VMEM((HGRN_HEAD_DIM, HGRN_HEAD_DIM), F32)],
        compiler_params=_cp(("parallel", "arbitrary")),
    )(u, u, u, lower_l, states, do_raw)


QROWS = ATTN_GROUP * WINDOW


def _attn_block(q, kp, kc, vp, vc, bp, bc, sink, first):
    qs = q * (ATTN_HEAD_DIM ** -0.5)
    sc = _dot(qs, kc, "nt") + bc
    m = jnp.maximum(jnp.max(sc, axis=-1, keepdims=True), sink)
    if not first:
        sp = _dot(qs, kp, "nt") + bp
        m = jnp.maximum(m, jnp.max(sp, axis=-1, keepdims=True))
    m = lax.stop_gradient(m)
    pc = jnp.exp(sc - m)
    den = jnp.sum(pc, axis=-1, keepdims=True) + jnp.exp(sink - m)
    o = _dot(pc, vc, "nn")
    if not first:
        pp = jnp.exp(sp - m)
        den = den + jnp.sum(pp, axis=-1, keepdims=True)
        o = o + _dot(pp, vp, "nn")
    return o * (1.0 / den)


def _first_or_later(n, run):
    @pl.when(n == 0)
    def _():
        run(True)

    @pl.when(n > 0)
    def _():
        run(False)


def _attn_in_specs():
    q_spec = pl.BlockSpec((WINDOW, ATTN_WIDTH), lambda n: (n, OFF_BQ // ATTN_WIDTH))
    k_cur = pl.BlockSpec((WINDOW, KV_WIDTH), lambda n: (n, OFF_BK // KV_WIDTH))
    k_prev = pl.BlockSpec((WINDOW, KV_WIDTH), lambda n: (jnp.maximum(n - 1, 0), OFF_BK // KV_WIDTH))
    v_cur = pl.BlockSpec((WINDOW, KV_WIDTH), lambda n: (n, OFF_BV // KV_WIDTH))
    v_prev = pl.BlockSpec((WINDOW, KV_WIDTH), lambda n: (jnp.maximum(n - 1, 0), OFF_BV // KV_WIDTH))
    bias = pl.BlockSpec((ATTN_KV_HEADS, QROWS, 2 * WINDOW), lambda n: (0, 0, 0))
    sink = pl.BlockSpec((ATTN_KV_HEADS, QROWS, 1), lambda n: (0, 0, 0))
    return [q_spec, k_prev, k_cur, v_prev, v_cur, bias, sink]


def _head_cols(a):
    return slice(a * ATTN_HEAD_DIM, (a + 1) * ATTN_HEAD_DIM)


def _group_rows(ref, h):
    return jnp.concatenate([ref[:, _head_cols(ATTN_GROUP * h + g)] for g in range(ATTN_GROUP)], axis=0)


def _attn_fwd(u, bias4, sink4):
    T = u.shape[0]

    def body(q_ref, kp_ref, kc_ref, vp_ref, vc_ref, b_ref, s_ref, o_ref):
        def run(first):
            for h in range(ATTN_KV_HEADS):
                hs = _head_cols(h)
                o = _attn_block(_group_rows(q_ref, h), kp_ref[:, hs], kc_ref[:, hs], vp_ref[:, hs], vc_ref[:, hs],
                                b_ref[h, :, :WINDOW], b_ref[h, :, WINDOW:], s_ref[h], first)
                for g in range(ATTN_GROUP):
                    o_ref[:, _head_cols(ATTN_GROUP * h + g)] = o[g * WINDOW:(g + 1) * WINDOW]

        _first_or_later(pl.program_id(0), run)

    return pl.pallas_call(
        body,
        name="attn_fwd",
        grid=(T // WINDOW,),
        in_specs=_attn_in_specs(),
        out_specs=pl.BlockSpec((WINDOW, ATTN_WIDTH), lambda n: (n, 0)),
        out_shape=_sds((T, ATTN_WIDTH), F32),
        compiler_params=_cp(("parallel",)),
    )(u, u, u, u, u, bias4, sink4)


def _attn_bwd(u, bias4, sink4, do, plan=None):
    T = u.shape[0]
    act = pl.BlockSpec((WINDOW, ATTN_WIDTH), lambda n: (n, 0))
    kv = pl.BlockSpec((WINDOW, KV_WIDTH), lambda n: (n, 0))
    in_specs = _attn_in_specs()
    bias, sink = in_specs[5], in_specs[6]

    def body(q_ref, kp_ref, kc_ref, vp_ref, vc_ref, b_ref, s_ref, do_ref,
             dq_ref, dkc_ref, dkp_ref, dvc_ref, dvp_ref, db_ref, ds_ref):
        n = pl.program_id(0)

        @pl.when(n == 0)
        def _():
            db_ref[...] = jnp.zeros_like(db_ref)
            ds_ref[...] = jnp.zeros_like(ds_ref)

        def run(first):
            dqs, dkps, dkcs, dvps, dvcs = [], [], [], [], []
            for h in range(ATTN_KV_HEADS):
                hs = _head_cols(h)
                _, vjp = jax.vjp(
                    functools.partial(_attn_block, first=first),
                    _group_rows(q_ref, h), kp_ref[:, hs], kc_ref[:, hs], vp_ref[:, hs], vc_ref[:, hs],
                    b_ref[h, :, :WINDOW], b_ref[h, :, WINDOW:], s_ref[h])
                dq, dkp, dkc, dvp, dvc, dbp, dbc, dsink = vjp(_group_rows(do_ref, h))
                dqs += [dq[g * WINDOW:(g + 1) * WINDOW] for g in range(ATTN_GROUP)]
                dkps.append(dkp)
                dkcs.append(dkc)
                dvps.append(dvp)
                dvcs.append(dvc)
                if not first:
                    db_ref[h, :, :WINDOW] += dbp
                db_ref[h, :, WINDOW:] += dbc
                ds_ref[h] += dsink
            dq_ref[...] = jnp.concatenate(dqs, axis=1).astype(BF16)
            dkc_ref[...] = jnp.concatenate(dkcs, axis=1)
            dkp_ref[...] = jnp.concatenate(dkps, axis=1)
            dvc_ref[...] = jnp.concatenate(dvcs, axis=1)
            dvp_ref[...] = jnp.concatenate(dvps, axis=1)

        _first_or_later(n, run)

    kv_sds = _sds((T, KV_WIDTH), F32)
    return _call(
        body,
        name="attn_bwd",
        grid=(T // WINDOW,),
        in_specs=in_specs + [act],
        out_specs=[act, kv, kv, kv, kv, bias, sink],
        out_shape=[_sds((T, ATTN_WIDTH), BF16), kv_sds, kv_sds, kv_sds, kv_sds,
                   _sds((ATTN_KV_HEADS, QROWS, 2 * WINDOW), F32), _sds((ATTN_KV_HEADS, QROWS, 1), F32)],
        sem=("arbitrary",),
        args=(u, u, u, u, u, bias4, sink4, do),
        plan=plan,
    )


def _with_next_block_part(cur, nxt):
    pad = jnp.zeros_like(nxt[:WINDOW])
    return (cur + jnp.concatenate([nxt[WINDOW:], pad], axis=0)).astype(BF16)


MIX_COLS = 512


def _silu(x):
    return x * jax.nn.sigmoid(x)


def _silu_grad(x):
    s = jax.nn.sigmoid(x)
    return s * (1.0 + x * (1.0 - s))


def _shift_rows_down(h, first, second):
    n = h.shape[0]
    row = lax.broadcasted_iota(jnp.int32, h.shape, 0)
    s1 = jnp.where(row == 0, first, pltpu.roll(h, 1, 0))
    s2 = jnp.where(row == 0, second, jnp.where(row == 1, first, pltpu.roll(h, 2, 0)))
    del n
    return s1, s2


def _shift_rows_up(h, first, second):
    n = h.shape[0]
    row = lax.broadcasted_iota(jnp.int32, h.shape, 0)
    s1 = jnp.where(row == n - 1, first, pltpu.roll(h, n - 1, 0))
    s2 = jnp.where(row == n - 1, second, jnp.where(row == n - 2, first, pltpu.roll(h, n - 2, 0)))
    return s1, s2


def _mix_rows(T):
    return _tile(T, 512)


def _mix_fwd(u, o_raw, o_b, gn_l, cw_l):
    T = u.shape[0]
    tr = _mix_rows(T)
    nrow = T // tr
    hr = tr // 8

    def ucol(off):
        return pl.BlockSpec((tr, MIX_COLS), lambda i, j, off=off: (i, off // MIX_COLS + j))

    def uprev(off):
        return pl.BlockSpec((8, MIX_COLS), lambda i, j, off=off: (jnp.maximum(i * hr - 1, 0), off // MIX_COLS + j))

    act = pl.BlockSpec((tr, MIX_COLS), lambda i, j: (i, j))
    par = lambda rows: pl.BlockSpec((rows, MIX_COLS), lambda i, j: (0, j))

    def body(oraw_ref, ag_ref, ob_ref, bg_ref, cb_ref, cc_ref, cx_ref, cg_ref, ccp_ref, cxp_ref, gn_ref, cw_ref,
             ha_ref, hb_ref, hc_ref, hat_ref, hbt_ref, hct_ref):
        ag = _silu(ag_ref[...])
        for h in range(MIX_COLS // HGRN_HEAD_DIM):
            cs = slice(h * HGRN_HEAD_DIM, (h + 1) * HGRN_HEAD_DIM)
            o = oraw_ref[:, cs]
            nrm = o * lax.rsqrt(jnp.mean(o * o, axis=-1, keepdims=True) + RMS_EPS)
            ha = nrm * gn_ref[:, cs] * ag[:, cs]
            ha_ref[:, cs] = ha.astype(BF16)
            hat_ref[cs, :] = ha.T.astype(BF16)
        hb = ob_ref[...] * _silu(bg_ref[...])
        hb_ref[...] = hb.astype(BF16)
        hbt_ref[...] = hb.T.astype(BF16)
        keep = (pl.program_id(0) > 0).astype(F32)
        hcur = cc_ref[...] * cx_ref[...]
        p1 = ccp_ref[7:8, :] * cxp_ref[7:8, :] * keep
        p2 = ccp_ref[6:7, :] * cxp_ref[6:7, :] * keep
        s1, s2 = _shift_rows_down(hcur, p1, p2)
        y = cw_ref[0:1, :] * s2 + cw_ref[1:2, :] * s1 + cw_ref[2:3, :] * hcur
        hc = cb_ref[...] * y * _silu(cg_ref[...])
        hc_ref[...] = hc.astype(BF16)
        hct_ref[...] = hc.T.astype(BF16)

    out = _sds((T, HGRN_WIDTH), BF16)
    out_t = _sds((HGRN_WIDTH, T), BF16)
    act_t = pl.BlockSpec((MIX_COLS, tr), lambda i, j: (j, i))
    return pl.pallas_call(
        body,
        name="mix_fwd",
        grid=(nrow, HGRN_WIDTH // MIX_COLS),
        in_specs=[act, ucol(OFF_AG), act, ucol(OFF_BG), ucol(OFF_CB), ucol(OFF_CC), ucol(OFF_CX), ucol(OFF_CG),
                  uprev(OFF_CC), uprev(OFF_CX), par(1), par(CONV_K)],
        out_specs=[act, act, act, act_t, act_t, act_t],
        out_shape=[out, out, out, out_t, out_t, out_t],
        compiler_params=_cp(("parallel", "parallel")),
    )(o_raw, u, o_b, u, u, u, u, u, u, u, gn_l, cw_l)


def _mix_bwd(u, o_raw, o_b, gn_l, cw_l, dha, dhb, dhc):
    T = u.shape[0]
    tr = _mix_rows(T)
    nrow = T // tr
    hr = tr // 8
    last_halo = T // 8 - 1

    def ucol(off):
        return pl.BlockSpec((tr, MIX_COLS), lambda j, i, off=off: (i, off // MIX_COLS + j))

    def uprev(off):
        return pl.BlockSpec((8, MIX_COLS), lambda j, i, off=off: (jnp.maximum(i * hr - 1, 0), off // MIX_COLS + j))

    def unext(off):
        return pl.BlockSpec((8, MIX_COLS), lambda j, i, off=off: (jnp.minimum((i + 1) * hr, last_halo), off // MIX_COLS + j))

    act = pl.BlockSpec((tr, MIX_COLS), lambda j, i: (i, j))
    act_next = pl.BlockSpec((8, MIX_COLS), lambda j, i: (jnp.minimum((i + 1) * hr, last_halo), j))
    par = lambda rows: pl.BlockSpec((rows, MIX_COLS), lambda j, i: (0, j))

    def body(oraw_ref, ag_ref, ob_ref, bg_ref, cb_ref, cc_ref, cx_ref, cg_ref, ccp_ref, cxp_ref,
             cbn_ref, cgn_ref, dhcn_ref, gn_ref, cw_ref, dha_ref, dhb_ref, dhc_ref,
             doraw_ref, dob_ref, dag_ref, dbg_ref, dcb_ref, dcc_ref, dcx_ref, dcg_ref, dgn_ref, dcw_ref):
        i = pl.program_id(1)

        @pl.when(i == 0)
        def _():
            dgn_ref[...] = jnp.zeros_like(dgn_ref)
            dcw_ref[...] = jnp.zeros_like(dcw_ref)

        ag = ag_ref[...]
        sag = _silu(ag)
        dha = dha_ref[...]
        for h in range(MIX_COLS // HGRN_HEAD_DIM):
            cs = slice(h * HGRN_HEAD_DIM, (h + 1) * HGRN_HEAD_DIM)
            o = oraw_ref[:, cs]
            rs = lax.rsqrt(jnp.mean(o * o, axis=-1, keepdims=True) + RMS_EPS)
            nrm = o * rs
            gn = gn_ref[:, cs]
            d = dha[:, cs]
            dag_ref[:, cs] = (d * nrm * gn * _silu_grad(ag[:, cs])).astype(BF16)
            dgn_ref[:, cs] += jnp.sum(d * sag[:, cs] * nrm, axis=0, keepdims=True)
            dn = d * sag[:, cs] * gn
            doraw_ref[:, cs] = rs * (dn - nrm * jnp.mean(dn * nrm, axis=-1, keepdims=True))
        bg = bg_ref[...]
        dhb = dhb_ref[...]
        dob_ref[...] = dhb * _silu(bg)
        dbg_ref[...] = (dhb * ob_ref[...] * _silu_grad(bg)).astype(BF16)
        keep_prev = (i > 0).astype(F32)
        keep_next = (i < nrow - 1).astype(F32)
        cc, cx, cb, cg = cc_ref[...], cx_ref[...], cb_ref[...], cg_ref[...]
        hcur = cc * cx
        p1 = ccp_ref[7:8, :] * cxp_ref[7:8, :] * keep_prev
        p2 = ccp_ref[6:7, :] * cxp_ref[6:7, :] * keep_prev
        s1, s2 = _shift_rows_down(hcur, p1, p2)
        w0, w1, w2 = cw_ref[0:1, :], cw_ref[1:2, :], cw_ref[2:3, :]
        y = w0 * s2 + w1 * s1 + w2 * hcur
        dhc = dhc_ref[...]
        scg = _silu(cg)
        doc = dhc * scg
        dcg_ref[...] = (dhc * cb * y * _silu_grad(cg)).astype(BF16)
        dcb_ref[...] = (doc * y).astype(BF16)
        dy = doc * cb
        n1 = dhcn_ref[0:1, :] * _silu(cgn_ref[0:1, :]) * cbn_ref[0:1, :] * keep_next
        n2 = dhcn_ref[1:2, :] * _silu(cgn_ref[1:2, :]) * cbn_ref[1:2, :] * keep_next
        u1, u2 = _shift_rows_up(dy, n1, n2)
        dh = w2 * dy + w1 * u1 + w0 * u2
        dcc_ref[...] = (dh * cx).astype(BF16)
        dcx_ref[...] = (dh * cc).astype(BF16)
        dcw_ref[0:1, :] += jnp.sum(dy * s2, axis=0, keepdims=True)
        dcw_ref[1:2, :] += jnp.sum(dy * s1, axis=0, keepdims=True)
        dcw_ref[2:3, :] += jnp.sum(dy * hcur, axis=0, keepdims=True)

    f32o, bf = _sds((T, HGRN_WIDTH), F32), _sds((T, HGRN_WIDTH), BF16)
    return pl.pallas_call(
        body,
        name="mix_bwd",
        grid=(HGRN_WIDTH // MIX_COLS, nrow),
        in_specs=[act, ucol(OFF_AG), act, ucol(OFF_BG), ucol(OFF_CB), ucol(OFF_CC), ucol(OFF_CX), ucol(OFF_CG),
                  uprev(OFF_CC), uprev(OFF_CX), unext(OFF_CB), unext(OFF_CG), act_next, par(1), par(CONV_K),
                  act, act, act],
        out_specs=[act, act, act, act, act, act, act, act, par(1), par(8)],
        out_shape=[f32o, f32o, bf, bf, bf, bf, bf, bf, _sds((1, HGRN_WIDTH), F32), _sds((8, HGRN_WIDTH), F32)],
        compiler_params=_cp(("parallel", "arbitrary")),
    )(o_raw, u, o_b, u, u, u, u, u, u, u, u, u, dhc, gn_l, cw_l, dha, dhb, dhc)


def _merge_specs(T):
    tr = _tile(T, 1024)

    def gate(off):
        return pl.BlockSpec((tr, MIX_COLS), lambda i, j, off=off: (i, (off - OFF_MA) // MIX_COLS + j))

    act = pl.BlockSpec((tr, MIX_COLS), lambda i, j: (i, j))
    return tr, gate, act


def _gate(m_ref):
    return jax.nn.sigmoid(m_ref[...].astype(F32))


def _merge_fwd(u, ya, yb, yc):
    T = u.shape[0]
    tr, ucol, act = _merge_specs(T)

    def body(ma_ref, mb_ref, mc_ref, ya_ref, yb_ref, yc_ref, o_ref, ot_ref):
        merged = _gate(ma_ref) * ya_ref[...] + _gate(mb_ref) * yb_ref[...] + _gate(mc_ref) * yc_ref[...]
        o_ref[...] = merged.astype(BF16)
        ot_ref[...] = merged.T.astype(BF16)

    return pl.pallas_call(
        body,
        name="merge_fwd",
        grid=(T // tr, D_MODEL // MIX_COLS),
        in_specs=[ucol(OFF_MA), ucol(OFF_MB), ucol(OFF_MC), act, act, act],
        out_specs=[act, pl.BlockSpec((MIX_COLS, tr), lambda i, j: (j, i))],
        out_shape=[_sds((T, D_MODEL), BF16), _sds((D_MODEL, T), BF16)],
        compiler_params=_cp(("parallel", "parallel")),
    )(u, u, u, ya, yb, yc)


def _merge_bwd(u, ya, yb, yc, dmerged, plan=None):
    T = u.shape[0]
    tr, ucol, act = _merge_specs(T)

    def body(ma_ref, mb_ref, mc_ref, ya_ref, yb_ref, yc_ref, dm_ref, dya_ref, dyb_ref, dyc_ref, dma_ref, dmb_ref, dmc_ref):
        dm = dm_ref[...]
        for m_ref, y_ref, dy_ref, dg_ref in ((ma_ref, ya_ref, dya_ref, dma_ref), (mb_ref, yb_ref, dyb_ref, dmb_ref),
                                             (mc_ref, yc_ref, dyc_ref, dmc_ref)):
            s = _gate(m_ref)
            dy_ref[...] = (dm * s).astype(BF16)
            dg_ref[...] = (dm * y_ref[...] * s * (1.0 - s)).astype(BF16)

    out = _sds((T, D_MODEL), BF16)
    return _call(
        body,
        name="merge_bwd",
        grid=(T // tr, D_MODEL // MIX_COLS),
        in_specs=[ucol(OFF_MA), ucol(OFF_MB), ucol(OFF_MC), act, act, act, act],
        out_specs=[act] * 6,
        out_shape=[out] * 6,
        sem=("parallel", "parallel"),
        args=(u, u, u, ya, yb, yc, dmerged),
        plan=plan,
    )


def _ln_fwd(x, y, g_l, b_l):
    T = x.shape[0]
    tr = _tile(T, 256)
    row = pl.BlockSpec((tr, D_MODEL), lambda i: (i, 0))
    col = pl.BlockSpec((D_MODEL, tr), lambda i: (0, i))
    par = pl.BlockSpec((1, D_MODEL), lambda i: (0, 0))

    def body(x_ref, y_ref, g_ref, b_ref, o_ref, z_ref, ob_ref, ot_ref):
        z = ALPHA * x_ref[...] + y_ref[...]
        z_ref[...] = z
        mu = jnp.mean(z, axis=-1, keepdims=True)
        zc = z - mu
        var = jnp.mean(zc * zc, axis=-1, keepdims=True)
        o = zc * lax.rsqrt(var + LN_EPS) * g_ref[...] + b_ref[...]
        o_ref[...] = o
        ob_ref[...] = o.astype(BF16)
        ot_ref[...] = o.T.astype(BF16)

    return pl.pallas_call(
        body,
        name="ln_fwd",
        grid=(T // tr,),
        in_specs=[row, row, par, par],
        out_specs=[row, row, row, col],
        out_shape=[_sds((T, D_MODEL), F32)] * 2 + [_sds((T, D_MODEL), BF16), _sds((D_MODEL, T), BF16)],
        compiler_params=_cp(("parallel",)),
    )(x, y, g_l, b_l)


def _operand_forms(x):
    T = x.shape[0]
    tr = _tile(T, 256)
    row = pl.BlockSpec((tr, D_MODEL), lambda i: (i, 0))
    col = pl.BlockSpec((D_MODEL, tr), lambda i: (0, i))

    def body(x_ref, xb_ref, xt_ref):
        xv = x_ref[...]
        xb_ref[...] = xv.astype(BF16)
        xt_ref[...] = xv.T.astype(BF16)

    return pl.pallas_call(
        body,
        name="operand_forms",
        grid=(T // tr,),
        in_specs=[row],
        out_specs=[row, col],
        out_shape=[_sds((T, D_MODEL), BF16), _sds((D_MODEL, T), BF16)],
        compiler_params=_cp(("parallel",)),
    )(x)


def _ln_bwd(z, dxn, g_l):
    T = z.shape[0]
    tr = _tile(T, 256)
    row = pl.BlockSpec((tr, D_MODEL), lambda i: (i, 0))
    par = pl.BlockSpec((1, D_MODEL), lambda i: (0, 0))

    def body(z_ref, d_ref, g_ref, dz_ref, dzb_ref, dg_ref, db_ref):
        @pl.when(pl.program_id(0) == 0)
        def _():
            dg_ref[...] = jnp.zeros_like(dg_ref)
            db_ref[...] = jnp.zeros_like(db_ref)

        z = z_ref[...]
        d = d_ref[...]
        mu = jnp.mean(z, axis=-1, keepdims=True)
        zc = z - mu
        rstd = lax.rsqrt(jnp.mean(zc * zc, axis=-1, keepdims=True) + LN_EPS)
        zh = zc * rstd
        dg_ref[...] += jnp.sum(d * zh, axis=0, keepdims=True)
        db_ref[...] += jnp.sum(d, axis=0, keepdims=True)
        dh = d * g_ref[...]
        dz = rstd * (dh - jnp.mean(dh, axis=-1, keepdims=True) - zh * jnp.mean(dh * zh, axis=-1, keepdims=True))
        dz_ref[...] = dz
        dzb_ref[...] = dz.astype(BF16)

    return pl.pallas_call(
        body,
        name="ln_bwd",
        grid=(T // tr,),
        in_specs=[row, row, par],
        out_specs=[row, row, par, par],
        out_shape=[_sds((T, D_MODEL), F32), _sds((T, D_MODEL), BF16), _sds((1, D_MODEL), F32), _sds((1, D_MODEL), F32)],
        compiler_params=_cp(("arbitrary",)),
    )(z, dxn, g_l)


def _loss_head(y, target):
    T = y.shape[0]
    tr = _tile(T, 256)
    row = pl.BlockSpec((tr, D_MODEL), lambda i: (i, 0))
    acc = pl.BlockSpec((8, 128), lambda i: (0, 0))

    def body(y_ref, t_ref, l_ref, d_ref):
        @pl.when(pl.program_id(0) == 0)
        def _():
            l_ref[...] = jnp.zeros_like(l_ref)

        err = y_ref[...] - t_ref[...]
        d_ref[...] = err * (1.0 / D_MODEL)
        part = 0.5 * jnp.sum(jnp.sum(err * err, axis=-1, keepdims=True) * (1.0 / D_MODEL), axis=0, keepdims=True)
        r = lax.broadcasted_iota(jnp.int32, (8, 128), 0)
        c = lax.broadcasted_iota(jnp.int32, (8, 128), 1)
        l_ref[...] += jnp.where((r == 0) & (c == 0), part, 0.0)

    return pl.pallas_call(
        body,
        name="loss_head",
        grid=(T // tr,),
        in_specs=[row, row],
        out_specs=[acc, row],
        out_shape=[_sds((8, 128), F32), _sds((T, D_MODEL), F32)],
        compiler_params=_cp(("arbitrary",)),
    )(y, target)


ADAMW_BLOCK_ELEMS = 512 * 1024


def _adamw(w, g, m, v, name):
    shape = w.shape
    cols = shape[-1]
    rows = math.prod(shape[:-1])
    flat = lambda a: a.reshape(rows, cols)
    if rows * cols <= ADAMW_BLOCK_ELEMS or rows % 8:
        tr = rows
    else:
        tr = 8
        while rows % (tr * 2) == 0 and tr * 2 * cols <= ADAMW_BLOCK_ELEMS:
            tr *= 2
    blk = pl.BlockSpec((tr, cols), lambda i: (i, 0))
    c1 = 1.0 - ADAM_B1 ** ADAM_STEP
    c2 = 1.0 - ADAM_B2 ** ADAM_STEP

    def body(w_ref, g_ref, m_ref, v_ref, d_ref, nm_ref, nv_ref):
        gg = g_ref[...]
        nm = ADAM_B1 * m_ref[...] + (1.0 - ADAM_B1) * gg
        nv = ADAM_B2 * v_ref[...] + (1.0 - ADAM_B2) * (gg * gg)
        nm_ref[...] = nm
        nv_ref[...] = nv
        d_ref[...] = -ADAM_LR * ((nm / c1) / (jnp.sqrt(nv / c2) + ADAM_EPS) + ADAM_WD * w_ref[...])

    outs = pl.pallas_call(
        body,
        name=name,
        grid=(rows // tr,),
        in_specs=[blk] * 4,
        out_specs=[blk] * 3,
        out_shape=[_sds((rows, cols), F32)] * 3,
        compiler_params=_cp(("parallel",)),
    )(flat(w), flat(g), flat(m), flat(v))
    return tuple(o.reshape(shape) for o in outs)


def _t5_bucket(dist):
    max_exact = N_BUCKETS // 2
    logd = jnp.log(jnp.maximum(dist, 1).astype(F32) / max_exact) / math.log(MAX_DISTANCE / max_exact)
    large = jnp.minimum(max_exact + (logd * (N_BUCKETS - max_exact)).astype(jnp.int32), N_BUCKETS - 1)
    return jnp.where(dist < max_exact, dist, large)


def _band_bias(rel_bias):
    i = jnp.arange(WINDOW)[:, None]
    j = jnp.arange(2 * WINDOW)[None, :]
    bucket = _t5_bucket(jnp.clip(WINDOW + i - j, 0, WINDOW - 1))
    onehot = (bucket[:, :, None] == jnp.arange(N_BUCKETS)[None, None, :]).astype(F32)
    bias = jnp.einsum("ijb,bh->hij", onehot, rel_bias.astype(F32), precision=lax.Precision.HIGHEST)
    rel = WINDOW + i - j
    return jnp.where(((rel >= 0) & (rel < WINDOW))[None], bias, MASK_VALUE)


def _lower_bounds(lb_param):
    soft = jax.nn.softmax(lb_param.astype(F32), axis=0)
    return jnp.cumsum(soft, axis=0) - soft[0:1]


def _mm_rows(T):
    return _tile(T, 1024)


def _layer_fwd(xs, ex, layer, lower_l, bias4, sink4_l, gn_l, cw_l, lng_l, lnb_l):
    x, xb, xt = xs
    T = x.shape[0]
    tm = _mm_rows(T)
    plan = ex.fwd_plan(layer)
    u, ug, landed = _input_projection(xb, ex.weights[layer]["w_in"][0], plan=plan)
    if plan is not None:
        plan = ex.fwd_pass_plan(layer, landed)
    (o_raw, states), landed = _hgrn_fwd(u, lower_l, plan=plan)
    if plan is not None:
        ex.fwd_landed(layer, landed)
    w_proj_l, w_out_l = ex.weights[layer]["w_proj"], ex.weights[layer]["w_out"][0]
    o_b = _attn_fwd(u, bias4, sink4_l)
    ha, hb, hc, hat, hbt, hct = _mix_fwd(u, o_raw, o_b, gn_l, cw_l)
    ys = [_matmul(h, w_proj_l, mode="nn", tm=tm, tn=1024, tk=HGRN_WIDTH, b_idx=i, name="mm_proj", out_dtype=BF16)
          for i, h in enumerate((ha, hb, hc))]
    merged, merged_t = _merge_fwd(ug, *ys)
    y = _matmul(merged, w_out_l, mode="nn", tm=tm, tn=1024, tk=D_MODEL, name="mm_out")
    xn, z, xnb, xnt = _ln_fwd(x, y, lng_l, lnb_l)
    saved = dict(xt=xt, u=u, ug=ug, o_raw=o_raw, states=states, o_b=o_b, hts=(hat, hbt, hct), ys=ys,
                 merged_t=merged_t, z=z)
    return (xn, xnb, xnt), saved


def _layer_bwd(dxn, s, ex, layer, lower_l, bias4, sink4_l, gn_l, cw_l, lng_l):
    T = dxn.shape[0]
    tm = _mm_rows(T)
    u = s["u"]
    w = ex.weights[layer]
    w_in_l, w_proj_l, w_out_l = w["w_in"][0], w["w_proj"], w["w_out"][0]
    dz, dzb, d_lng, d_lnb = _ln_bwd(s["z"], dxn, lng_l)
    dmerged = _matmul(dzb, w_out_l, mode="nt", tm=tm, tn=1024, tk=D_MODEL, name="mm_dmerged")
    g_w_out = _matmul(s["merged_t"], dzb, mode="nn", tm=1024, tn=1024, tk=2048, name="mm_gw_out")
    plan = ex.pair_plan(layer)
    (*dys, dma, dmb, dmc), landed = _merge_bwd(s["ug"], *s["ys"], dmerged, plan=plan)
    if plan is not None:
        ex.pair_landed(layer, landed)
    dhs =[_matmul(dy, w_proj_l, mode="nt", tm=tm, tn=1024, tk=D_MODEL, b_idx=i, name="mm_dh") for i, dy in enumerate(dys)]
    g_w_proj = jnp.stack([_matmul(ht, dy, mode="nn", tm=1024, tn=1024, tk=2048, name="mm_gw_proj")
                          for ht, dy in zip(s["hts"], dys)])
    do_raw, do_b, dag, dbg, dcb, dcc, dcx, dcg, d_gn, d_cw = _mix_bwd(u, s["o_raw"], s["o_b"], gn_l, cw_l, *dhs)
    daq, daf, dai, d_lower = _hgrn_bwd(u, lower_l, s["states"], do_raw)
    plan = ex.slab_plan(layer)
    (dbq, dkc, dkp, dvc, dvp, d_bias4, d_sink4), landed = _attn_bwd(u, bias4, sink4_l, do_b, plan=plan)
    if plan is not None:
        ex.slab_landed(layer, landed)
    dbk = _with_next_block_part(dkc, dkp)
    dbv = _with_next_block_part(dvc, dvp)
    du = jnp.concatenate([daq, daf, dai, dag, dbq, dbk, dbv, dbg, dcb, dcc, dcx, dcg, dma, dmb, dmc], axis=1)
    g_w_in = _matmul(s["xt"], du, mode="nn", tm=1024, tn=768, tk=4096, name="mm_gw_in")
    ex.grads_ready(layer, dict(w_in=g_w_in[None], w_proj=g_w_proj, w_out=g_w_out[None]))
    assemble = ex.assemble_plan(layer)
    tail = ex.tail_plan() if layer == 0 else None
    carried = [p for p in (assemble, tail) if p is not None]
    plan = _merge_plans(carried) if carried else None
    dx = _matmul(du, w_in_l, mode="nt", tm=_tile(T, 512), tn=1024, tk=5632, name="mm_dx", add=dz, add_scale=ALPHA, plan=plan)
    if plan is not None:
        dx, landed = dx
        n_assemble = 0 if assemble is None else len(assemble.out_shapes)
        if assemble is not None:
            ex.assembled(layer, landed[:n_assemble])
        if tail is not None:
            ex.slab_landed(-1, landed[n_assemble:])
    d_sinks = jnp.sum(d_sink4.reshape(ATTN_HEADS, WINDOW), axis=-1)
    small = dict(lower=d_lower[0], gn=d_gn[0], sinks=d_sinks, cw=d_cw[:CONV_K], bias=d_bias4.reshape(ATTN_HEADS, WINDOW, 2 * WINDOW),
                 lng=d_lng[0], lnb=d_lnb[0])
    return dx, small


def _local_step(x, target, ex, lb_param, hgrn_norm_g, attn_sinks, conv_w_full, rel_bias, ln_g, ln_b):
    lower, lower_vjp = jax.vjp(_lower_bounds, lb_param)
    bias, bias_vjp = jax.vjp(_band_bias, rel_bias)
    bias4 = bias.reshape(ATTN_KV_HEADS, QROWS, 2 * WINDOW)
    sink4 = jnp.broadcast_to(attn_sinks.reshape(DEPTH, ATTN_HEADS, 1, 1), (DEPTH, ATTN_HEADS, WINDOW, 1)).reshape(
        DEPTH, ATTN_KV_HEADS, QROWS, 1)
    row = lambda a, l: a[l:l + 1]
    saved = []
    hs = (x, *_operand_forms(x))
    for l in range(DEPTH):
        hs, s = _layer_fwd(hs, ex, l, row(lower, l), bias4, sink4[l], row(hgrn_norm_g, l), conv_w_full[l], row(ln_g, l), row(ln_b, l))
        saved.append(s)
    loss_blk, dh = _loss_head(hs[0], target)
    smalls = [None] * DEPTH
    for l in reversed(range(DEPTH)):
        dh, smalls[l] = _layer_bwd(dh, saved[l], ex, l, row(lower, l), bias4, sink4[l], row(hgrn_norm_g, l), conv_w_full[l], row(ln_g, l))
    stack = lambda k: jnp.stack([sm[k] for sm in smalls])
    d_bias = smalls[0]["bias"] + smalls[1]["bias"] + smalls[2]["bias"] + smalls[3]["bias"]
    small = dict(
        lb_param=lower_vjp(stack("lower"))[0], hgrn_norm_g=stack("gn"), attn_sinks=stack("sinks"), conv_w=stack("cw"),
        rel_bias=bias_vjp(d_bias)[0], ln_g=stack("lng"), ln_b=stack("lnb"))
    return loss_blk, dh, small


ANY = pl.BlockSpec(memory_space=pl.ANY)
DMA_SEM = pltpu.SemaphoreType.DMA
GATHER_PIECES = 4


def _coords():
    return lax.axis_index("x"), lax.axis_index("y"), lax.axis_index("c")


def _other_chips(x, y):
    return [(1 - x, y), (x, 1 - y), (1 - x, 1 - y)]


def _remote(src, dst, send_sem, recv_sem, device):
    return pltpu.make_async_remote_copy(src_ref=src, dst_ref=dst, send_sem=send_sem, recv_sem=recv_sem,
                                        device_id=device, device_id_type=MESH)


def _sub(ref, axis, index, size):
    idx = [slice(None)] * len(ref.shape)
    idx[axis] = pl.ds(pl.multiple_of(index * size, size), size)
    return ref.at[tuple(idx)]


class _Plan:
    def __init__(self, inputs, out_shapes, aliases, n, make):
        self.inputs, self.out_shapes, self.aliases, self.n, self.make = tuple(inputs), tuple(out_shapes), dict(aliases), n, make


class _Xfer:
    def __init__(self, send, recv=None):
        self.send, self.recv = send, send if recv is None else recv

    def start(self):
        self.send.start()

    def wait(self):
        self.send.wait_send()
        self.recv.wait_recv()


def _merge_plans(plans):
    def make(in_refs, out_refs, send_sems, recv_sems, base):
        out, i0, o0, b0 = [], 0, 0, base
        for p in plans:
            out += p.make(in_refs[i0:i0 + len(p.inputs)], out_refs[o0:o0 + len(p.out_shapes)], send_sems, recv_sems, b0)
            i0, o0, b0 = i0 + len(p.inputs), o0 + len(p.out_shapes), b0 + p.n
        return out

    inputs, out_shapes, aliases = [], [], {}
    for p in plans:
        aliases.update({len(inputs) + k: len(out_shapes) + v for k, v in p.aliases.items()})
        inputs += p.inputs
        out_shapes += p.out_shapes
    return _Plan(inputs, out_shapes, aliases, sum(p.n for p in plans), make)


def _run_plan(plan, name):
    ni, no = len(plan.inputs), len(plan.out_shapes)

    def body(*refs):
        transfers = plan.make(refs[:ni], refs[ni:ni + no], refs[ni + no], refs[ni + no + 1], 0)
        for t in transfers:
            t.start()
        for t in transfers:
            t.wait()

    outs = pl.pallas_call(
        body, name=name, in_specs=[ANY] * ni, out_specs=[ANY] * no, out_shape=list(plan.out_shapes),
        input_output_aliases=plan.aliases, scratch_shapes=[DMA_SEM((plan.n,)), DMA_SEM((plan.n,))],
    )(*plan.inputs)
    return list(outs)


def _gather_send_plan(shards, layer, sax):
    shp = shards.shape[1:]
    hax = 3 - sax
    w, hw = shp[sax], shp[hax] // 2
    out_shape = list(shp)
    out_shape[sax] = w * N_CHIPS

    pieces = GATHER_PIECES
    ph = hw // pieces

    def make(in_refs, out_refs, send_sems, recv_sems, base):
        (src_ref,), (out_ref,) = in_refs, out_refs
        x, y, c = _coords()
        j = 2 * x + y
        src = src_ref.at[layer]
        slab = _sub(out_ref, sax, j, w)
        out = [pltpu.make_async_copy(src, slab, send_sems.at[base])]
        for k, (px, py) in enumerate(_other_chips(x, y)):
            for q in range(pieces):
                s = base + 1 + k * pieces + q
                part = c * pieces + q
                out.append(_Xfer(_remote(_sub(src, hax, part, ph), _sub(slab, hax, part, ph), send_sems.at[s], recv_sems.at[s],
                                         (px, py, c))))
        return out

    return _Plan([shards], [_sds(tuple(out_shape), shards.dtype)], {}, 1 + 3 * pieces, make)


def _gather_pass_plan(full, sax):
    hax = 3 - sax
    w, hw = full.shape[sax] // N_CHIPS, full.shape[hax] // 2

    def make(in_refs, out_refs, send_sems, recv_sems, base):
        (out_ref,) = out_refs
        x, y, c = _coords()
        region = lambda slab, half: _sub(_sub(out_ref, sax, slab, w), hax, half, hw)
        out = []
        for k, (px, py) in enumerate(_other_chips(x, y)):
            mine, theirs = region(2 * px + py, c), region(2 * px + py, 1 - c)
            sems = send_sems.at[base + k], recv_sems.at[base + k]
            out.append(_Xfer(_remote(mine, mine, *sems, (x, y, 1 - c)), _remote(theirs, theirs, *sems, (x, y, c))))
        return out

    return _Plan([full], [_sds(full.shape, full.dtype)], {0: 0}, 3, make)


def _pair_exchange_plan(g, hax):
    hw = g.shape[hax] // 2
    out_shape = list(g.shape)
    out_shape[hax] = hw

    def make(in_refs, out_refs, send_sems, recv_sems, base):
        x, y, c = _coords()
        return [_Xfer(_remote(_sub(in_refs[0], hax, 1 - c, hw), out_refs[0], send_sems.at[base], recv_sems.at[base], (x, y, 1 - c)))]

    return _Plan([g], [_sds(tuple(out_shape), g.dtype)], {}, 1, make)


def _add_own_half(place, g, recv, hax, blk, name):
    L, ah, bh = recv.shape
    tr, tc = blk
    nr, nc = ah // tr, bh // tc

    def g_map(l, i, jc, p):
        return (l, i + p[0] * nr, jc) if hax == 1 else (l, i, jc + p[0] * nc)

    def body(p_ref, g_ref, r_ref, o_ref):
        o_ref[...] = (g_ref[...] + r_ref[...]).astype(BF16)

    same = pl.BlockSpec((None, tr, tc), lambda l, i, jc, p: (l, i, jc))
    return pl.pallas_call(
        body,
        name=name,
        grid_spec=pltpu.PrefetchScalarGridSpec(
            num_scalar_prefetch=1, grid=(L, nr, nc),
            in_specs=[pl.BlockSpec((None, tr, tc), g_map), same], out_specs=same),
        out_shape=_sds(recv.shape, BF16),
        compiler_params=_cp(("parallel", "parallel", "parallel")),
    )(place, g, recv)


def _slab_exchange_plan(p, sax):
    w = p.shape[sax] // N_CHIPS
    slab_shape = list(p.shape)
    slab_shape[sax] = w

    def make(in_refs, out_refs, send_sems, recv_sems, base):
        x, y, c = _coords()
        return [_Xfer(_remote(_sub(in_refs[0], sax, 2 * px + py, w), out_refs[0].at[k], send_sems.at[base + k],
                              recv_sems.at[base + k], (px, py, c)))
                for k, (px, py) in enumerate(_other_chips(x, y))]

    return _Plan([p], [_sds((3, *slab_shape), p.dtype)], {}, 3, make)


def _add_slabs(place, g, pair, recv, sax, blk, name):
    hax = 3 - sax
    _, L, a, b = recv.shape
    tr, tc = blk
    nr, nc = a // tr, b // tc

    def g_map(l, i, jc, p):
        return (l, p[0] * nr + i, p[1] * nc + jc) if hax == 1 else (l, p[1] * nr + i, p[0] * nc + jc)

    def pair_map(l, i, jc, p):
        return (l, i, p[1] * nc + jc) if hax == 1 else (l, p[1] * nr + i, jc)

    def out_map(l, i, jc, p):
        return (l, p[0] * nr + i, jc) if hax == 1 else (l, i, p[0] * nc + jc)

    def body(p_ref, g_ref, pair_ref, r0_ref, r1_ref, r2_ref, o_ref):
        own = g_ref[...] + pair_ref[...]
        o_ref[...] = ((own + r0_ref[...].astype(F32)) + r1_ref[...].astype(F32)) + r2_ref[...].astype(F32)

    def rk(k):
        return pl.BlockSpec((None, None, tr, tc), lambda l, i, jc, p, k=k: (k, l, i, jc))

    out_shape = [L, a, b]
    out_shape[hax] *= 2
    blk3 = (None, tr, tc)
    return pl.pallas_call(
        body,
        name=name,
        grid_spec=pltpu.PrefetchScalarGridSpec(
            num_scalar_prefetch=1, grid=(L, nr, nc),
            in_specs=[pl.BlockSpec(blk3, g_map), pl.BlockSpec(blk3, pair_map), rk(0), rk(1), rk(2)],
            out_specs=pl.BlockSpec(blk3, out_map)),
        out_shape=_sds(tuple(out_shape), F32),
        compiler_params=_cp(("parallel", "parallel", "parallel")),
    )(place, g, pair, recv, recv, recv)


def _pair_assemble_plan(r, hax):
    hw = r.shape[hax] // 2

    def make(in_refs, out_refs, send_sems, recv_sems, base):
        x, y, c = _coords()
        mine, other = _sub(out_refs[0], hax, c, hw), _sub(out_refs[0], hax, 1 - c, hw)
        sems = send_sems.at[base], recv_sems.at[base]
        return [_Xfer(_remote(mine, mine, *sems, (x, y, 1 - c)), _remote(other, other, *sems, (x, y, c)))]

    return _Plan([r], [_sds(r.shape, r.dtype)], {0: 0}, 1, make)


CLASSES = dict(w_in=(2, (128, 4224), (128, 4224)), w_proj=(2, (256, 2048), (512, 512)), w_out=(1, (256, 1024), (256, 1024)))


class _Exchanges:
    def __init__(self, place, shards):
        self.place, self.shards = place, shards
        self.weights, self.pending, self.pair, self.halves, self.reduced = {}, {}, {}, {}, {}

    FIRST = ("w_in",)

    def _carried(self, layer):
        keys = [(0, k) for k in CLASSES if k not in self.FIRST] if layer == 0 else []
        return keys + ([(layer + 1, k) for k in CLASSES] if layer + 1 < DEPTH else [])

    def _send_plan(self, keys):
        return _merge_plans([_gather_send_plan(self.shards[k], l, CLASSES[k][0]) for l, k in keys])

    def _pass_plan(self, keys, bufs):
        return _merge_plans([_gather_pass_plan(b, CLASSES[k][0]) for (_, k), b in zip(keys, bufs)])

    def _landed(self, keys, bufs):
        for (l, k), b in zip(keys, bufs):
            self.weights.setdefault(l, {})[k] = b

    def first_weights(self):
        keys = [(0, k) for k in self.FIRST]
        sent = _run_plan(self._send_plan(keys), "gather_send_0")
        self._landed(keys, _run_plan(self._pass_plan(keys, sent), "gather_pass_0"))

    def fwd_plan(self, layer):
        keys = self._carried(layer)
        return self._send_plan(keys) if keys else None

    def fwd_pass_plan(self, layer, bufs):
        return self._pass_plan(self._carried(layer), bufs)

    def fwd_landed(self, layer, bufs):
        self._landed(self._carried(layer), bufs)

    def grads_ready(self, layer, grads):
        self.pending[layer] = grads

    def pair_plan(self, layer):
        if layer + 1 not in self.pending:
            return None
        g = self.pending[layer + 1]
        return _merge_plans([_pair_exchange_plan(g[k], 3 - CLASSES[k][0]) for k in CLASSES])

    def pair_landed(self, layer, bufs):
        self.pair[layer + 1] = dict(zip(CLASSES, bufs))

    def slab_plan(self, layer):
        src = layer + 1
        if src not in self.pair:
            return None
        g, pair = self.pending[src], self.pair[src]
        sums = [_add_own_half(self.place, g[k], pair[k], 3 - CLASSES[k][0], CLASSES[k][1], f"rs_pair_add_{k}_{src}") for k in CLASSES]
        return _merge_plans([_slab_exchange_plan(p, CLASSES[k][0]) for k, p in zip(CLASSES, sums)])

    def slab_landed(self, layer, bufs):
        src = layer + 1
        g, pair = self.pending.pop(src), self.pair.pop(src)
        self.halves[src] = [_add_slabs(self.place, g[k], pair[k], r, CLASSES[k][0], CLASSES[k][2], f"rs_slab_add_{k}_{src}")
                            for k, r in zip(CLASSES, bufs)]

    def assemble_plan(self, layer):
        if layer + 1 not in self.halves:
            return None
        return _merge_plans([_pair_assemble_plan(h, 3 - CLASSES[k][0]) for k, h in zip(CLASSES, self.halves[layer + 1])])

    def assembled(self, layer, bufs):
        del self.halves[layer + 1]
        self.reduced[layer + 1] = dict(zip(CLASSES, bufs))

    def tail_plan(self):
        self.pair_landed(-1, _run_plan(self.pair_plan(-1), "rs_pair_0"))
        return self.slab_plan(-1)

    def finish(self):
        self.assembled(-1, _run_plan(self.assemble_plan(-1), "rs_assemble_0"))
        return [self.reduced[l] for l in range(DEPTH)]


N_DEV = 8


def _all_reduce_small(v, name):
    rows = v.shape[0]

    def body(v_ref, gath_ref, sum_ref, send_sems, recv_sems, local_sem):
        x, y, c = _coords()
        me, sib = (x, y, c), (x, y, 1 - c)
        chips = _other_chips(x, y)

        def slot(px, py, pc):
            return gath_ref.at[pl.ds(pl.multiple_of((4 * px + 2 * py + pc) * rows, rows), rows), :]

        def copy(k, block, to, src=None):
            return _remote(slot(*block) if src is None else src, slot(*block), send_sems.at[k], recv_sems.at[k], to)

        mine = pltpu.make_async_copy(v_ref, slot(*me), local_sem)
        mine.start()
        first = [copy(0, me, sib, src=v_ref)] + [copy(1 + k, me, (*chip, c), src=v_ref) for k, chip in enumerate(chips)]
        for cp in first:
            cp.start()
        passed = [copy(4 + k, (*chip, c), sib) for k, chip in enumerate(chips)]
        for k, chip in enumerate(chips):
            copy(1 + k, (*chip, c), me).wait_recv()
            passed[k].start()
        copy(0, sib, me).wait_recv()
        for k, chip in enumerate(chips):
            copy(4 + k, (*chip, 1 - c), me).wait_recv()
        for cp in first + passed:
            cp.wait_send()
        mine.wait()
        acc = gath_ref[0:rows, :]
        for d in range(1, N_DEV):
            acc = acc + gath_ref[d * rows:(d + 1) * rows, :]
        sum_ref[...] = acc

    vm = pl.BlockSpec(memory_space=pltpu.VMEM)
    return pl.pallas_call(
        body, name=name, in_specs=[vm], out_specs=[vm, vm],
        out_shape=[_sds((N_DEV * rows, 128), F32), _sds((rows, 128), F32)],
        scratch_shapes=[DMA_SEM((7,)), DMA_SEM((7,)), DMA_SEM(())],
    )(v)[1]


def _pad_rows(a):
    flat = a.reshape(-1).astype(F32)
    rows = -(-flat.shape[0] // (8 * 128)) * 8
    return jnp.pad(flat, (0, rows * 128 - flat.shape[0])).reshape(rows, 128)


def _sum_over_devices(parts, name):
    blocks = [_pad_rows(a) for a in parts.values()]
    total = _all_reduce_small(jnp.concatenate(blocks, axis=0), name)
    out, r0 = {}, 0
    for (key, a), blk in zip(parts.items(), blocks):
        out[key] = total[r0:r0 + blk.shape[0]].reshape(-1)[:a.size].reshape(a.shape)
        r0 += blk.shape[0]
    return out


def kernel(x, w_in, w_proj_hgrn, w_proj_attn, w_proj_conv, w_out, lb_param, hgrn_norm_g, attn_sinks, conv_w, rel_bias, ln_g, ln_b, loss_target, m_w_in, m_w_proj_hgrn, m_w_proj_attn, m_w_proj_conv, m_w_out, m_lb_param, m_hgrn_norm_g, m_attn_sinks, m_conv_w, m_rel_bias, m_ln_g, m_ln_b, v_w_in, v_w_proj_hgrn, v_w_proj_attn, v_w_proj_conv, v_w_out, v_lb_param, v_hgrn_norm_g, v_attn_sinks, v_conv_w, v_rel_bias, v_ln_g, v_ln_b):
    xi, yi, ci = _coords()
    slab = 2 * xi + yi
    place = jnp.stack([ci, slab]).astype(jnp.int32)
    conv_cols = conv_w.shape[-1]

    w_in_b = w_in.astype(BF16)[:, None]
    w_proj_b = jnp.stack([w_proj_hgrn, w_proj_attn, w_proj_conv], axis=1).astype(BF16)
    w_out_b = w_out.astype(BF16)[:, None]
    ex = _Exchanges(place, dict(w_in=w_in_b, w_proj=w_proj_b, w_out=w_out_b))
    ex.first_weights()
    conv_spread = lax.dynamic_update_slice(jnp.zeros((DEPTH, CONV_K, CONV_WIDTH), F32), conv_w, (0, 0, slab * conv_cols))
    conv_full = 0.5 * _sum_over_devices({"conv_w": conv_spread}, "gather_conv_w")["conv_w"]

    loss_blk, dx, small = _local_step(x[0], loss_target[0], ex, lb_param, hgrn_norm_g, attn_sinks, conv_full, rel_bias, ln_g, ln_b)

    reduced = ex.finish()
    g_w_in = jnp.stack([r["w_in"][0] for r in reduced])
    g_w_proj = jnp.stack([r["w_proj"] for r in reduced])
    g_w_out = jnp.stack([r["w_out"][0] for r in reduced])
    small = dict(small, loss=loss_blk[0:1, 0:1])
    small = _sum_over_devices(small, "sum_small")
    loss = small["loss"][0, 0]
    g_conv = lax.dynamic_slice(small["conv_w"], (0, 0, slab * conv_cols), (DEPTH, CONV_K, conv_cols))

    grads = [g_w_in, g_w_proj[:, 0], g_w_proj[:, 1], g_w_proj[:, 2], g_w_out, small["lb_param"], small["hgrn_norm_g"],
             small["attn_sinks"], g_conv, small["rel_bias"], small["ln_g"], small["ln_b"]]
    names = ["w_in", "w_proj_hgrn", "w_proj_attn", "w_proj_conv", "w_out", "lb_param", "hgrn_norm_g", "attn_sinks",
             "conv_w", "rel_bias", "ln_g", "ln_b"]
    ws = [w_in, w_proj_hgrn, w_proj_attn, w_proj_conv, w_out, lb_param, hgrn_norm_g, attn_sinks, conv_w, rel_bias, ln_g, ln_b]
    ms = [m_w_in, m_w_proj_hgrn, m_w_proj_attn, m_w_proj_conv, m_w_out, m_lb_param, m_hgrn_norm_g, m_attn_sinks, m_conv_w,
          m_rel_bias, m_ln_g, m_ln_b]
    vs = [v_w_in, v_w_proj_hgrn, v_w_proj_attn, v_w_proj_conv, v_w_out, v_lb_param, v_hgrn_norm_g, v_attn_sinks, v_conv_w,
          v_rel_bias, v_ln_g, v_ln_b]
    upd = [_adamw(w, g, m, v, "adamw_" + n) for n, w, g, m, v in zip(names, ws, grads, ms, vs)]
    deltas, new_ms, new_vs = zip(*upd)
    return (loss, dx[None], *grads, *deltas, *new_ms, *new_vs)
```
